```python
import math, functools
import jax, jax.numpy as jnp
from jax import lax
import numpy as np

D_MODEL = 1024
BATCH = 16
SEQ = 256
DEPTH = 4
DEC_BATCH = 4
DEC_SEQ = 2048
PAST_LEN = 256

GRID_W = 64
N_MIXERS = 3
N_HEADS = 16
HEAD_DIM = D_MODEL // N_HEADS
WIN_ROWS = 8
WIN_COLS = 16
Q_BLOCK = 128
SSM_GROUP = 16
SSM_GROUPS = D_MODEL // SSM_GROUP
SSM_STATE = 64
CONV_W = 3
D_FF = 2816
N_EXPERTS = 8
TOP_K = 2
D_FF_EXPERT = 3584
EPS = 1e-6
NEG_INF = -1e30
N_NA_LAYERS = (DEPTH + 2) // 3
N_SSM_LAYERS = (DEPTH + 1) // 3
N_CONV_LAYERS = DEPTH // 3
N_DENSE_LAYERS = (DEPTH + 1) // 2
N_MOE_LAYERS = DEPTH // 2

kernel_name = "hybrid_diffusion_na_s5_conv_step"


def rms_norm(x, g):
    xf = x.astype(jnp.float32)
    y = xf * lax.rsqrt(jnp.mean(xf * xf, axis=-1, keepdims=True) + EPS)
    return (y * g.astype(jnp.float32)).astype(x.dtype)


def ada_modulation(cond, w, b):
    mod = jnp.dot(jax.nn.silu(cond), w) + b
    mod = mod.reshape(cond.shape[:-1] + (6, 1, D_MODEL))
    return [mod[..., i, :, :] for i in range(6)]


def qkv_heads(h, w_qkv, q_g, k_g):
    b, l, _ = h.shape
    qkv = jnp.dot(h, w_qkv).reshape(b, l, 3, N_HEADS, HEAD_DIM)
    qkv = jnp.transpose(qkv, (2, 0, 3, 1, 4))
    return rms_norm(qkv[0], q_g), rms_norm(qkv[1], k_g), qkv[2]


def merge_heads(o, w_o):
    b, nh, l, hd = o.shape
    return jnp.dot(jnp.transpose(o, (0, 2, 1, 3)).reshape(b, l, nh * hd), w_o)


def context_attention(h, w_qkv, w_o, q_g, k_g):
    q, k, v = qkv_heads(h, w_qkv, q_g, k_g)
    b, nh, l, hd = q.shape
    scale = HEAD_DIM ** -0.5

    def block(qb):
        s = jnp.einsum('bhqd,bhkd->bhqk', qb, k).astype(jnp.float32) * scale
        p = jax.nn.softmax(s, axis=-1).astype(v.dtype)
        return jnp.einsum('bhqk,bhkd->bhqd', p, v)

    q_blocks = jnp.moveaxis(q.reshape(b, nh, l // Q_BLOCK, Q_BLOCK, hd), 2, 0)
    o = lax.map(block, q_blocks)
    o = jnp.moveaxis(o, 0, 2).reshape(b, nh, l, hd)
    return merge_heads(o, w_o), (k, v)


def latent_neighbourhood_attention(h, ctx_k, ctx_v, w_qkv, w_o, q_g, k_g, rpb):
    q, k, v = qkv_heads(h, w_qkv, q_g, k_g)
    b, nh, s, hd = q.shape
    rows = s // GRID_W
    kr = min(WIN_ROWS, rows)
    scale = HEAD_DIM ** -0.5
    cols = np.arange(GRID_W)
    col_start = np.clip(cols - WIN_COLS // 2, 0, GRID_W - WIN_COLS)
    col_mask = jnp.asarray((cols[None, :] >= col_start[:, None]) & (cols[None, :] < col_start[:, None] + WIN_COLS))
    col_idx = jnp.asarray(np.clip(cols[None, :] - cols[:, None], -(WIN_COLS - 1), WIN_COLS - 1) + WIN_COLS - 1)
    rpb_cols = rpb[:, :, col_idx]
    k_grid = k.reshape(b, nh, rows, GRID_W, hd)
    v_grid = v.reshape(b, nh, rows, GRID_W, hd)
    q_rows = jnp.moveaxis(q.reshape(b, nh, rows, GRID_W, hd), 2, 0)
    n_loc = kr * GRID_W

    def row_block(args):
        r, q_r = args
        r0 = jnp.clip(r - kr // 2, 0, rows - kr)
        k_band = lax.dynamic_slice_in_dim(k_grid, r0, kr, axis=2).reshape(b, nh, n_loc, hd)
        v_band = lax.dynamic_slice_in_dim(v_grid, r0, kr, axis=2).reshape(b, nh, n_loc, hd)
        row_idx = r0 + jnp.arange(kr) - r + WIN_ROWS - 1
        bias = jnp.transpose(rpb_cols[:, row_idx], (0, 2, 1, 3)).astype(jnp.float32)
        s_loc = jnp.einsum('bhqd,bhkd->bhqk', q_r, k_band).astype(jnp.float32)
        s_loc = s_loc.reshape(b, nh, GRID_W, kr, GRID_W) * scale + bias
        s_loc = jnp.where(col_mask[:, None, :], s_loc, NEG_INF).reshape(b, nh, GRID_W, n_loc)
        s_ctx = jnp.einsum('bhqd,bhcd->bhqc', q_r, ctx_k).astype(jnp.float32) * scale
        p = jax.nn.softmax(jnp.concatenate([s_loc, s_ctx], axis=-1), axis=-1).astype(v.dtype)
        return (jnp.einsum('bhqk,bhkd->bhqd', p[..., :n_loc], v_band)
                + jnp.einsum('bhqc,bhcd->bhqd', p[..., n_loc:], ctx_v))

    o = lax.map(row_block, (jnp.arange(rows), q_rows))
    o = jnp.moveaxis(o, 0, 2).reshape(b, nh, s, hd)
    return merge_heads(o, w_o), None


def _linear_recurrence(e1, e2):
    a1, b1 = e1
    a2, b2 = e2
    return a1 * a2, a2 * b1 + b2


def s5_mix(h, h0_re, h0_im, lam_re, lam_im, log_step, b_re, b_im, c_re, c_im, d_skip, w_glu, b_glu):
    f32 = jnp.float32
    bsz, l, _ = h.shape
    u = h.astype(f32).reshape(bsz, l, SSM_GROUPS, SSM_GROUP)
    uc = u.astype(jnp.complex64)
    y = u * d_skip.astype(f32).reshape(SSM_GROUPS, SSM_GROUP)
    finals = []
    for direction in range(2):
        lam = lax.complex(lam_re[direction].astype(f32), lam_im[direction].astype(f32))
        step = jnp.exp(log_step[direction].astype(f32))[:, None]
        lam_bar = jnp.exp(lam * step)
        b_bar = ((lam_bar - 1.0) / lam)[..., None] * lax.complex(b_re[direction].astype(f32), b_im[direction].astype(f32))
        c_mat = lax.complex(c_re[direction].astype(f32), c_im[direction].astype(f32))
        u_dir = uc if direction == 0 else jnp.flip(uc, axis=1)
        bu = jnp.einsum('blgp,gnp->blgn', u_dir, b_bar)
        h0 = lax.complex(h0_re[:, direction].astype(f32), h0_im[:, direction].astype(f32))
        bu = bu.at[:, 0].add(lam_bar * h0)
        a = jnp.broadcast_to(lam_bar, bu.shape)
        _, states = lax.associative_scan(_linear_recurrence, (a, bu), axis=1)
        finals.append(states[:, -1])
        y_dir = jnp.einsum('blgn,gpn->blgp', states, c_mat).real
        y = y + (y_dir if direction == 0 else jnp.flip(y_dir, axis=1))
    z = jax.nn.gelu(y.reshape(bsz, l, D_MODEL)).astype(h.dtype)
    out = z * jax.nn.sigmoid(jnp.dot(z, w_glu) + b_glu)
    fin = jnp.stack(finals, axis=1)
    return out, (fin.real.astype(h.dtype), fin.imag.astype(h.dtype))


def short_conv_mix(h, w_in, conv_w, conv_b, w_out):
    bg, cg, xi = jnp.split(jnp.dot(h, w_in), 3, axis=-1)
    z = cg * xi
    zc = lax.conv_general_dilated(z, conv_w[:, None, :].astype(z.dtype), (1,),
                                  [(CONV_W // 2, CONV_W // 2)],
                                  dimension_numbers=('NWC', 'WIO', 'NWC'),
                                  feature_group_count=D_MODEL) + conv_b
    return jnp.dot(bg * zc, w_out), None


def swiglu(h, w13, w2):
    g, u = jnp.split(jnp.dot(h, w13), 2, axis=-1)
    return jnp.dot(jax.nn.silu(g) * u, w2)


def moe_swiglu(h, router, w13, w2):
    logits = jnp.dot(h, router).astype(jnp.float32)
    top_v, top_i = lax.top_k(logits, TOP_K)
    top_w = jax.nn.softmax(top_v, axis=-1)
    gates = jnp.einsum('blk,blke->ble', top_w, jax.nn.one_hot(top_i, N_EXPERTS, dtype=jnp.float32)).astype(h.dtype)
    out = jnp.zeros_like(h)
    for e in range(N_EXPERTS):
        out = out + gates[..., e:e + 1] * swiglu(h, w13[e], w2[e])
    return out


def setup_inputs(seed: int = 0) -> dict:
    key = jax.random.key(seed)
    keys = iter(jax.random.split(key, 48))
    f32 = jnp.float32

    def nrm(shape, scale):
        return scale * jax.random.normal(next(keys), shape, f32)

    d = D_MODEL
    n_idx = jnp.arange(SSM_STATE, dtype=f32)
    sshape = (N_SSM_LAYERS, 2, SSM_GROUPS, SSM_STATE)
    return {
        "x_prompt": nrm((BATCH, SEQ, d), 1.0),
        "x_sample": nrm((DEC_BATCH, DEC_SEQ, d), 1.0),
        "c": nrm((DEC_BATCH, d), 1.0),
        "cache_na_k": nrm((DEC_BATCH, N_NA_LAYERS, N_HEADS, PAST_LEN, HEAD_DIM), 1.0),
        "cache_na_v": nrm((DEC_BATCH, N_NA_LAYERS, N_HEADS, PAST_LEN, HEAD_DIM), 1.0),
        "state_ssm_re": nrm((DEC_BATCH,) + sshape, 0.1),
        "state_ssm_im": nrm((DEC_BATCH,) + sshape, 0.1),
        "c_ctx": nrm((d,), 1.0),
        "ln1_g": 1.0 + nrm((DEPTH, d), 0.05),
        "ln2_g": 1.0 + nrm((DEPTH, d), 0.05),
        "ada_w": nrm((DEPTH, d, 6 * d), 0.5 * d ** -0.5),
        "ada_b": nrm((DEPTH, 6 * d), 0.02),
        "na_w_qkv": nrm((N_NA_LAYERS, d, 3 * d), d ** -0.5),
        "na_w_o": nrm((N_NA_LAYERS, d, d), d ** -0.5),
        "na_q_g": 1.0 + nrm((N_NA_LAYERS, HEAD_DIM), 0.05),
        "na_k_g": 1.0 + nrm((N_NA_LAYERS, HEAD_DIM), 0.05),
        "na_rpb": nrm((N_NA_LAYERS, N_HEADS, 2 * WIN_ROWS - 1, 2 * WIN_COLS - 1), 0.1),
        "ssm_lam_re": -0.5 + nrm(sshape, 0.01),
        "ssm_lam_im": math.pi * n_idx + nrm(sshape, 0.01),
        "ssm_log_step": jax.random.uniform(next(keys), (N_SSM_LAYERS, 2, SSM_GROUPS), f32,
                                           minval=math.log(1e-3), maxval=math.log(1e-1)),
        "ssm_b_re": nrm((N_SSM_LAYERS, 2, SSM_GROUPS, SSM_STATE, SSM_GROUP), (0.5 / SSM_GROUP) ** 0.5),
        "ssm_b_im": nrm((N_SSM_LAYERS, 2, SSM_GROUPS, SSM_STATE, SSM_GROUP), (0.5 / SSM_GROUP) ** 0.5),
        "ssm_c_re": nrm((N_SSM_LAYERS, 2, SSM_GROUPS, SSM_GROUP, SSM_STATE), (0.5 / SSM_STATE) ** 0.5),
        "ssm_c_im": nrm((N_SSM_LAYERS, 2, SSM_GROUPS, SSM_GROUP, SSM_STATE), (0.5 / SSM_STATE) ** 0.5),
        "ssm_d": nrm((N_SSM_LAYERS, d), 0.5),
        "ssm_w_glu": nrm((N_SSM_LAYERS, d, d), d ** -0.5),
        "ssm_b_glu": nrm((N_SSM_LAYERS, d), 0.02),
        "cv_w_in": nrm((N_CONV_LAYERS, d, 3 * d), d ** -0.5),
        "cv_conv_w": nrm((N_CONV_LAYERS, CONV_W, d), 0.5),
        "cv_conv_b": nrm((N_CONV_LAYERS, d), 0.02),
        "cv_w_out": nrm((N_CONV_LAYERS, d, d), d ** -0.5),
        "ffn_w13": nrm((N_DENSE_LAYERS, d, 2 * D_FF), d ** -0.5),
        "ffn_w2": nrm((N_DENSE_LAYERS, D_FF, d), D_FF ** -0.5),
        "moe_router": nrm((N_MOE_LAYERS, d, N_EXPERTS), d ** -0.5),
        "moe_w13": nrm((N_MOE_LAYERS, N_EXPERTS, d, 2 * D_FF_EXPERT), d ** -0.5),
        "moe_w2": nrm((N_MOE_LAYERS, N_EXPERTS, D_FF_EXPERT, d), D_FF_EXPERT ** -0.5),
    }


def reference(x_prompt, x_sample, c, cache_na_k, cache_na_v, state_ssm_re, state_ssm_im, c_ctx,
              ln1_g, ln2_g, ada_w, ada_b, na_w_qkv, na_w_o, na_q_g, na_k_g, na_rpb,
              ssm_lam_re, ssm_lam_im, ssm_log_step, ssm_b_re, ssm_b_im, ssm_c_re, ssm_c_im,
              ssm_d, ssm_w_glu, ssm_b_glu, cv_w_in, cv_conv_w, cv_conv_b, cv_w_out,
              ffn_w13, ffn_w2, moe_router, moe_w13, moe_w2):

    def layer(x, cond, l, mixer):
        sh1, sc1, g1, sh2, sc2, g2 = ada_modulation(cond, ada_w[l], ada_b[l])
        h = rms_norm(x, ln1_g[l]) * (1.0 + sc1) + sh1
        mix, aux = mixer(h)
        x = x + g1 * mix
        h = rms_norm(x, ln2_g[l]) * (1.0 + sc2) + sh2
        j = l // 2
        if l % 2 == 0:
            ffn = swiglu(h, ffn_w13[j], ffn_w2[j])
        else:
            ffn = moe_swiglu(h, moe_router[j], moe_w13[j], moe_w2[j])
        return x + g2 * ffn, aux

    def ssm_params(j):
        return dict(lam_re=ssm_lam_re[j], lam_im=ssm_lam_im[j], log_step=ssm_log_step[j],
                    b_re=ssm_b_re[j], b_im=ssm_b_im[j], c_re=ssm_c_re[j], c_im=ssm_c_im[j],
                    d_skip=ssm_d[j], w_glu=ssm_w_glu[j], b_glu=ssm_b_glu[j])

    def conv_mixer(j):
        return functools.partial(short_conv_mix, w_in=cv_w_in[j], conv_w=cv_conv_w[j],
                                 conv_b=cv_conv_b[j], w_out=cv_w_out[j])

    x = x_prompt
    bsz = x_prompt.shape[0]
    zero_state = jnp.zeros((bsz, 2, SSM_GROUPS, SSM_STATE), x_prompt.dtype)
    ks, vs, s_re, s_im = [], [], [], []
    for l in range(DEPTH):
        kind, j = l % N_MIXERS, l // N_MIXERS
        if kind == 0:
            mixer = functools.partial(context_attention, w_qkv=na_w_qkv[j], w_o=na_w_o[j],
                                      q_g=na_q_g[j], k_g=na_k_g[j])
        elif kind == 1:
            mixer = functools.partial(s5_mix, h0_re=zero_state, h0_im=zero_state, **ssm_params(j))
        else:
            mixer = conv_mixer(j)
        x, aux = layer(x, c_ctx, l, mixer)
        if kind == 0:
            ks.append(aux[0])
            vs.append(aux[1])
        elif kind == 1:
            s_re.append(aux[0])
            s_im.append(aux[1])
    y_prompt = x
    new_na_k = jnp.stack(ks, axis=1)
    new_na_v = jnp.stack(vs, axis=1)
    new_ssm_re = jnp.stack(s_re, axis=1)
    new_ssm_im = jnp.stack(s_im, axis=1)

    x = x_sample
    for l in range(DEPTH):
        kind, j = l % N_MIXERS, l // N_MIXERS
        if kind == 0:
            mixer = functools.partial(latent_neighbourhood_attention, ctx_k=cache_na_k[:, j],
                                      ctx_v=cache_na_v[:, j], w_qkv=na_w_qkv[j], w_o=na_w_o[j],
                                      q_g=na_q_g[j], k_g=na_k_g[j], rpb=na_rpb[j])
        elif kind == 1:
            mixer = functools.partial(s5_mix, h0_re=state_ssm_re[:, j], h0_im=state_ssm_im[:, j],
                                      **ssm_params(j))
        else:
            mixer = conv_mixer(j)
        x, _ = layer(x, c, l, mixer)
    y_sample = x

    return (y_prompt, y_sample, new_na_k, new_na_v, new_ssm_re, new_ssm_im)
```

```python
import functools
import math

import numpy as np
import jax
import jax.numpy as jnp
from jax import lax
from jax.experimental import pallas as pl
from jax.experimental.pallas import tpu as pltpu

F32 = jnp.float32
BF16 = jnp.bfloat16

D = 1024
BATCH, SEQ = 16, 256
DEC_BATCH, DEC_SEQ = 4, 2048
GRID_W = 64
ROWS = DEC_SEQ // GRID_W
N_HEADS, HEAD_DIM = 16, 64
WIN_ROWS, WIN_COLS = 8, 16
SSM_GROUP, SSM_GROUPS, SSM_STATE = 16, 64, 64
D_FF = 2816
N_EXPERTS = 8
D_FF_EXPERT = 3584
EPS = 1e-6
NEG_INF = -1e30
SCALE = HEAD_DIM ** -0.5

T_LAT = DEC_BATCH * DEC_SEQ
T_CTX = BATCH * SEQ
T = T_LAT + T_CTX
N_COND = 8
CTX_COND = DEC_BATCH

LANES = 128
VMEM_LIMIT = 56 * 1024 * 1024

TM = 512
TM_CONV = 256
TM_MOE = 512
TF_MOE = 512
TF_FFN = D_FF // 2
HP = N_HEADS // 2
QB_ROWS = 4
S5_SLAB = 128
S5_NSLAB = D // S5_SLAB
S5_LANES = (S5_SLAB // SSM_GROUP) * SSM_STATE
S5_T_LAT = 512


def _cparams(sem):
    return pltpu.CompilerParams(dimension_semantics=sem, vmem_limit_bytes=VMEM_LIMIT)


def _cond_of_row(row):
    return jnp.minimum(row // DEC_SEQ, CTX_COND)


def _modnorm(x, g, sc, sh):
    ms = jnp.mean(x * x, axis=-1, keepdims=True)
    y = x * lax.rsqrt(ms + EPS) * g
    return y * (1.0 + sc) + sh


def _sigmoid(x):
    return 1.0 / (1.0 + jnp.exp(-x))


def _silu(x):
    return x * _sigmoid(x)


def _gelu_tanh(x):
    c = math.sqrt(2.0 / math.pi)
    return 0.5 * x * (1.0 + jnp.tanh(c * (x + 0.044715 * (x * x * x))))


def _ada_kernel(c_ref, w_ref, b_ref, o_ref):
    s = _silu(c_ref[...]).astype(BF16)
    o_ref[0] = jnp.dot(s, w_ref[0].astype(BF16), preferred_element_type=F32) + b_ref[0]


def _ada_call(conds, ada_w, ada_b):
    depth = ada_w.shape[0]
    tn = 1536
    return pl.pallas_call(
        _ada_kernel,
        grid=(depth, 6 * D // tn),
        in_specs=[
            pl.BlockSpec((N_COND, D), lambda l, n: (0, 0)),
            pl.BlockSpec((1, D, tn), lambda l, n: (l, 0, n)),
            pl.BlockSpec((1, 1, tn), lambda l, n: (l, 0, n)),
        ],
        out_specs=pl.BlockSpec((1, N_COND, tn), lambda l, n: (l, 0, n)),
        out_shape=jax.ShapeDtypeStruct((depth, N_COND, 6 * D), F32),
        compiler_params=_cparams(("parallel", "parallel")),
        name="ada_mod",
    )(conds, ada_w, ada_b.reshape(depth, 1, 6 * D))


def _normproj_kernel(x_ref, g_ref, mod_ref, w_ref, o_ref, *, shift, scale):
    mod = mod_ref[0]
    h = _modnorm(x_ref[...], g_ref[...], mod[scale:scale + 1], mod[shift:shift + 1])
    o_ref[...] = jnp.dot(h.astype(BF16), w_ref[...], preferred_element_type=F32)


def _normproj_call(x, ln_g, mod, w, *, shift, scale, name):
    n = w.shape[1]
    return pl.pallas_call(
        functools.partial(_normproj_kernel, shift=shift, scale=scale),
        grid=(T // TM,),
        in_specs=[
            pl.BlockSpec((TM, D), lambda i: (i, 0)),
            pl.BlockSpec((1, D), lambda i: (0, 0)),
            pl.BlockSpec((1, 6, D), lambda i: (_cond_of_row(i * TM), 0, 0)),
            pl.BlockSpec((D, n), lambda i: (0, 0)),
        ],
        out_specs=pl.BlockSpec((TM, n), lambda i: (i, 0)),
        out_shape=jax.ShapeDtypeStruct((T, n), F32),
        compiler_params=_cparams(("parallel",)),
        name=name,
    )(x, ln_g.reshape(1, D), mod, w)


def _norm_kernel(x_ref, g_ref, mod_ref, o_ref, *, shift, scale):
    mod = mod_ref[0]
    o_ref[...] = _modnorm(x_ref[...], g_ref[...], mod[scale:scale + 1], mod[shift:shift + 1])


def _norm_call(x, ln_g, mod, *, shift, scale, name):
    return pl.pallas_call(
        functools.partial(_norm_kernel, shift=shift, scale=scale),
        grid=(T // TM,),
        in_specs=[
            pl.BlockSpec((TM, D), lambda i: (i, 0)),
            pl.BlockSpec((1, D), lambda i: (0, 0)),
            pl.BlockSpec((1, 6, D), lambda i: (_cond_of_row(i * TM), 0, 0)),
        ],
        out_specs=pl.BlockSpec((TM, D), lambda i: (i, 0)),
        out_shape=jax.ShapeDtypeStruct((T, D), F32),
        compiler_params=_cparams(("parallel",)),
        name=name,
    )(x, ln_g.reshape(1, D), mod)


def _proj_res_kernel(a_ref, x_ref, mod_ref, w_ref, o_ref, *, gate):
    y = jnp.dot(a_ref[...].astype(BF16), w_ref[...], preferred_element_type=F32)
    o_ref[...] = x_ref[...] + mod_ref[0][gate:gate + 1] * y


def _proj_res_call(a, x, mod, w, *, gate, name):
    return pl.pallas_call(
        functools.partial(_proj_res_kernel, gate=gate),
        grid=(T // TM,),
        in_specs=[
            pl.BlockSpec((TM, D), lambda i: (i, 0)),
            pl.BlockSpec((TM, D), lambda i: (i, 0)),
            pl.BlockSpec((1, 6, D), lambda i: (_cond_of_row(i * TM), 0, 0)),
            pl.BlockSpec((D, D), lambda i: (0, 0)),
        ],
        out_specs=pl.BlockSpec((TM, D), lambda i: (i, 0)),
        out_shape=jax.ShapeDtypeStruct((T, D), F32),
        compiler_params=_cparams(("parallel",)),
        name=name,
    )(a, x, mod, w)


def _head_norm(x, g, head0):
    x2 = x * x
    s0 = jnp.sum(jnp.where(head0, x2, 0.0), axis=-1, keepdims=True)
    s1 = jnp.sum(jnp.where(head0, 0.0, x2), axis=-1, keepdims=True)
    ms = jnp.where(head0, s0, s1) * (1.0 / HEAD_DIM)
    return x * lax.rsqrt(ms + EPS) * g


def _nt_dot(a, b):
    return lax.dot_general(a, b, (((1,), (1,)), ((), ())), preferred_element_type=F32)


def _ctx_attn_kernel(q_ref, k_ref, v_ref, qg_ref, kg_ref, o_ref, ko_ref, vo_ref):
    head0 = lax.broadcasted_iota(jnp.int32, (SEQ, LANES), 1) < HEAD_DIM
    qn = _head_norm(q_ref[...], qg_ref[...], head0)
    kn = _head_norm(k_ref[...], kg_ref[...], head0)
    v = v_ref[...]
    ko_ref[0, 0] = kn[:, :HEAD_DIM]
    ko_ref[0, 1] = kn[:, HEAD_DIM:]
    vo_ref[0, 0] = v[:, :HEAD_DIM]
    vo_ref[0, 1] = v[:, HEAD_DIM:]
    knb = kn.astype(BF16)
    vb = v.astype(BF16)
    outs = []
    for hh in range(2):
        qm = jnp.where(head0 if hh == 0 else jnp.logical_not(head0), qn, 0.0).astype(BF16)
        s = _nt_dot(qm, knb) * SCALE
        p = jnp.exp(s - jnp.max(s, axis=-1, keepdims=True))
        l = jnp.sum(p, axis=-1, keepdims=True)
        outs.append(jnp.dot(p.astype(BF16), vb, preferred_element_type=F32) / l)
    o_ref[...] = jnp.where(head0, outs[0], outs[1])


def _ctx_attn_call(qkv, qg2, kg2):
    row0 = T_LAT // SEQ
    blk = lambda c: pl.BlockSpec((SEQ, LANES), lambda b, hp, c=c: (row0 + b, c * HP + hp))
    kv_out = pl.BlockSpec((1, 2, SEQ, HEAD_DIM), lambda b, hp: (b, hp, 0, 0))
    return pl.pallas_call(
        _ctx_attn_kernel,
        grid=(BATCH, HP),
        in_specs=[blk(0), blk(1), blk(2),
                  pl.BlockSpec((1, LANES), lambda b, hp: (0, 0)),
                  pl.BlockSpec((1, LANES), lambda b, hp: (0, 0))],
        out_specs=[pl.BlockSpec((SEQ, LANES), lambda b, hp: (b, hp)), kv_out, kv_out],
        out_shape=[jax.ShapeDtypeStruct((T_CTX, D), F32),
                   jax.ShapeDtypeStruct((BATCH, N_HEADS, SEQ, HEAD_DIM), F32),
                   jax.ShapeDtypeStruct((BATCH, N_HEADS, SEQ, HEAD_DIM), F32)],
        compiler_params=_cparams(("parallel", "parallel")),
        name="ctx_attn",
    )(qkv, qkv, qkv, qg2, kg2)


def _band_of_block(p):
    r_lo, r_hi = QB_ROWS * p, QB_ROWS * p + QB_ROWS - 1
    kr = min(WIN_ROWS, ROWS)
    lo = min(max(r_lo - kr // 2, 0), ROWS - kr)
    hi = min(max(r_hi - kr // 2, 0), ROWS - kr) + kr
    lo -= lo % 2
    hi += hi % 2
    return lo, hi - lo


def _lat_attn_kernel(q_ref, k_ref, v_ref, ck_ref, cv_ref, qg_ref, kg_ref, bp_ref, o_ref,
                     qm0_s, qm1_s, kn_s, v_s):
    head0 = lax.broadcasted_iota(jnp.int32, (DEC_SEQ, LANES), 1) < HEAD_DIM
    qn = _head_norm(q_ref[...], qg_ref[...], head0)
    qm0_s[...] = jnp.where(head0, qn, 0.0).astype(BF16)
    qm1_s[...] = jnp.where(head0, 0.0, qn).astype(BF16)
    kn_s[...] = _head_norm(k_ref[...], kg_ref[...], head0).astype(BF16)
    v_s[...] = v_ref[...].astype(BF16)
    ckb = ck_ref[0].astype(BF16)
    cvb = cv_ref[0].astype(BF16)

    qc = lax.broadcasted_iota(jnp.int32, (GRID_W, LANES), 0)
    kl = lax.broadcasted_iota(jnp.int32, (GRID_W, LANES), 1)
    kc = jnp.where(kl < GRID_W, kl, kl - GRID_W)
    cs = jnp.clip(qc - WIN_COLS // 2, 0, GRID_W - WIN_COLS)
    col_ok = jnp.logical_and(kc >= cs, kc < cs + WIN_COLS)
    left = kl < GRID_W
    mask_of = {
        (True, True): col_ok,
        (True, False): jnp.logical_and(col_ok, left),
        (False, True): jnp.logical_and(col_ok, jnp.logical_not(left)),
    }
    h0q = lax.broadcasted_iota(jnp.int32, (QB_ROWS * GRID_W, LANES), 1) < HEAD_DIM
    kr_win = min(WIN_ROWS, ROWS)

    for p in range(ROWS // QB_ROWS):
        u0, nrows = _band_of_block(p)
        q_lo = p * QB_ROWS * GRID_W
        kb = kn_s[u0 * GRID_W:(u0 + nrows) * GRID_W, :]
        vb = v_s[u0 * GRID_W:(u0 + nrows) * GRID_W, :]
        outs = []
        for hh in range(2):
            qm = (qm0_s if hh == 0 else qm1_s)[q_lo:q_lo + QB_ROWS * GRID_W, :]
            s_loc = _nt_dot(qm, kb) * SCALE
            row_blocks = []
            for i in range(QB_ROWS):
                r = p * QB_ROWS + i
                r0 = min(max(r - kr_win // 2, 0), ROWS - kr_win)
                blocks = []
                for m in range(nrows // 2):
                    kr = u0 + 2 * m
                    ok_l = r0 <= kr < r0 + kr_win
                    ok_r = r0 <= kr + 1 < r0 + kr_win
                    sb = s_loc[i * GRID_W:(i + 1) * GRID_W, m * LANES:(m + 1) * LANES]
                    if not (ok_l or ok_r):
                        blocks.append(jnp.full((GRID_W, LANES), NEG_INF, F32))
                    else:
                        bias = bp_ref[hh, kr - r + WIN_ROWS]
                        blocks.append(jnp.where(mask_of[(ok_l, ok_r)], sb + bias, NEG_INF))
                row_blocks.append(jnp.concatenate(blocks, axis=1))
            s_loc = jnp.concatenate(row_blocks, axis=0)
            s_ctx = _nt_dot(qm, ckb) * SCALE
            mx = jnp.maximum(jnp.max(s_loc, axis=-1, keepdims=True),
                             jnp.max(s_ctx, axis=-1, keepdims=True))
            p_loc = jnp.exp(s_loc - mx)
            p_ctx = jnp.exp(s_ctx - mx)
            l = jnp.sum(p_loc, axis=-1, keepdims=True) + jnp.sum(p_ctx, axis=-1, keepdims=True)
            o = (jnp.dot(p_loc.astype(BF16), vb, preferred_element_type=F32)
                 + jnp.dot(p_ctx.astype(BF16), cvb, preferred_element_type=F32))
            outs.append(o / l)
        o_ref[q_lo:q_lo + QB_ROWS * GRID_W, :] = jnp.where(h0q, outs[0], outs[1])


def _lat_attn_call(qkv, ctx_k, ctx_v, qg2, kg2, bias_pairs):
    blk = lambda c: pl.BlockSpec((DEC_SEQ, LANES), lambda hp, b, c=c: (b, c * HP + hp))
    cblk = pl.BlockSpec((1, ctx_k.shape[1], LANES), lambda hp, b: (b, 0, hp))
    return pl.pallas_call(
        _lat_attn_kernel,
        grid=(HP, DEC_BATCH),
        in_specs=[blk(0), blk(1), blk(2), cblk, cblk,
                  pl.BlockSpec((1, LANES), lambda hp, b: (0, 0)),
                  pl.BlockSpec((1, LANES), lambda hp, b: (0, 0)),
                  pl.BlockSpec((2, 2 * WIN_ROWS, GRID_W, LANES), lambda hp, b: (hp, 0, 0, 0))],
        out_specs=pl.BlockSpec((DEC_SEQ, LANES), lambda hp, b: (b, hp)),
        out_shape=jax.ShapeDtypeStruct((T_LAT, D), F32),
        scratch_shapes=[pltpu.VMEM((DEC_SEQ, LANES), BF16)] * 4,
        compiler_params=_cparams(("parallel", "parallel")),
        name="lat_attn",
    )(qkv, qkv, qkv, ctx_k, ctx_v, qg2, kg2, bias_pairs)


def _bias_pairs(rpb):
    cols = np.arange(GRID_W)
    col_idx = np.clip(cols[None, :] - cols[:, None], -(WIN_COLS - 1), WIN_COLS - 1) + WIN_COLS - 1
    rc = rpb[:, :, col_idx]
    left = jnp.pad(rc, ((0, 0), (1, 0), (0, 0), (0, 0)))
    right = jnp.pad(rc, ((0, 0), (0, 1), (0, 0), (0, 0)))
    return jnp.concatenate([left, right], axis=-1)


def _ffn_kernel(x_ref, g_ref, mod_ref, w1_ref, w3_ref, w2_ref, o_ref, h_s, acc_s):
    f = pl.program_id(1)
    mod = mod_ref[0]

    @pl.when(f == 0)
    def _():
        h_s[...] = _modnorm(x_ref[...], g_ref[...], mod[4:5], mod[3:4]).astype(BF16)
        acc_s[...] = jnp.zeros_like(acc_s)

    h = h_s[...]
    a = _silu(jnp.dot(h, w1_ref[...], preferred_element_type=F32)) \
        * jnp.dot(h, w3_ref[...], preferred_element_type=F32)
    acc_s[...] += jnp.dot(a.astype(BF16), w2_ref[...], preferred_element_type=F32)

    @pl.when(f == pl.num_programs(1) - 1)
    def _():
        o_ref[...] = x_ref[...] + mod[5:6] * acc_s[...]


def _ffn_call(x, ln_g, mod, w13, w2):
    nf = D_FF // TF_FFN
    return pl.pallas_call(
        _ffn_kernel,
        grid=(T // TM, nf),
        in_specs=[
            pl.BlockSpec((TM, D), lambda i, f: (i, 0)),
            pl.BlockSpec((1, D), lambda i, f: (0, 0)),
            pl.BlockSpec((1, 6, D), lambda i, f: (_cond_of_row(i * TM), 0, 0)),
            pl.BlockSpec((D, TF_FFN), lambda i, f: (0, f)),
            pl.BlockSpec((D, TF_FFN), lambda i, f: (0, nf + f)),
            pl.BlockSpec((TF_FFN, D), lambda i, f: (f, 0)),
        ],
        out_specs=pl.BlockSpec((TM, D), lambda i, f: (i, 0)),
        out_shape=jax.ShapeDtypeStruct((T, D), F32),
        scratch_shapes=[pltpu.VMEM((TM, D), BF16), pltpu.VMEM((TM, D), F32)],
        compiler_params=_cparams(("parallel", "arbitrary")),
        name="ffn_dense",
    )(x, ln_g.reshape(1, D), mod, w13, w13, w2)


def _s5_kernel(u_ref, bm_ref, cm_ref, ar_ref, ai_ref, hr_ref, hi_ref,
               y_ref, fr_ref, fi_ref, st_s, cr_s, ci_s, *, nb, steps):
    d = pl.program_id(0)
    tb = pl.program_id(3)
    rows = nb * steps
    nch = S5_LANES // LANES
    u = u_ref[...].reshape(rows, S5_SLAB).astype(BF16)
    bu = jnp.dot(u, bm_ref[0, 0], preferred_element_type=F32)
    for c in range(2 * nch):
        st_s[c] = bu[:, c * LANES:(c + 1) * LANES]

    @pl.when(tb == 0)
    def _():
        cr_s[...] = hr_ref[0, 0]
        ci_s[...] = hi_ref[0, 0]

    ar = [jnp.broadcast_to(ar_ref[0, 0][:, c * LANES:(c + 1) * LANES], (nb, LANES)) for c in range(nch)]
    ai = [jnp.broadcast_to(ai_ref[0, 0][:, c * LANES:(c + 1) * LANES], (nb, LANES)) for c in range(nch)]

    def step(i, carry):
        l = jnp.where(d == 0, i, steps - 1 - i)
        idx = pl.ds(l, nb, stride=steps)
        out = []
        for c in range(nch):
            sr, si = carry[2 * c], carry[2 * c + 1]
            nr = ar[c] * sr - ai[c] * si + st_s[c, idx, :]
            ni = ar[c] * si + ai[c] * sr + st_s[nch + c, idx, :]
            st_s[c, idx, :] = nr
            st_s[nch + c, idx, :] = ni
            out += [nr, ni]
        return tuple(out)

    init = []
    for c in range(nch):
        init += [cr_s[:, c * LANES:(c + 1) * LANES], ci_s[:, c * LANES:(c + 1) * LANES]]
    fin = lax.fori_loop(0, steps, step, tuple(init))
    sr = jnp.concatenate([fin[2 * c] for c in range(nch)], axis=1)
    si = jnp.concatenate([fin[2 * c + 1] for c in range(nch)], axis=1)
    cr_s[...] = sr
    ci_s[...] = si
    fr_ref[0, 0] = sr
    fi_ref[0, 0] = si
    states = jnp.concatenate([st_s[c] for c in range(2 * nch)], axis=1).astype(BF16)
    y = jnp.dot(states, cm_ref[0, 0], preferred_element_type=F32)
    y_ref[0] = y.reshape(nb, steps, S5_SLAB)


def _s5_call(u3, bmat, cmat, a_re, a_im, h_re, h_im, *, nb, steps, name):
    n_seq, seq_len, _ = u3.shape
    n_sg, n_tb = n_seq // nb, seq_len // steps

    def tbi(d, tb):
        return jnp.where(d == 0, tb, n_tb - 1 - tb)

    par = lambda last: pl.BlockSpec((1, 1) + last, lambda d, j, sg, tb: (d, j, 0, 0))
    st = pl.BlockSpec((1, 1, nb, S5_LANES), lambda d, j, sg, tb: (d, j, sg, 0))
    return pl.pallas_call(
        functools.partial(_s5_kernel, nb=nb, steps=steps),
        grid=(2, S5_NSLAB, n_sg, n_tb),
        in_specs=[
            pl.BlockSpec((nb, steps, S5_SLAB), lambda d, j, sg, tb: (sg, tbi(d, tb), j)),
            par((S5_SLAB, 2 * S5_LANES)),
            par((2 * S5_LANES, S5_SLAB)),
            par((1, S5_LANES)), par((1, S5_LANES)),
            st, st,
        ],
        out_specs=[
            pl.BlockSpec((1, nb, steps, S5_SLAB), lambda d, j, sg, tb: (d, sg, tbi(d, tb), j)),
            st, st,
        ],
        out_shape=[
            jax.ShapeDtypeStruct((2, n_seq, seq_len, D), F32),
            jax.ShapeDtypeStruct((2, S5_NSLAB, n_seq, S5_LANES), F32),
            jax.ShapeDtypeStruct((2, S5_NSLAB, n_seq, S5_LANES), F32),
        ],
        scratch_shapes=[pltpu.VMEM((2 * S5_LANES // LANES, nb * steps, LANES), F32),
                        pltpu.VMEM((nb, S5_LANES), F32), pltpu.VMEM((nb, S5_LANES), F32)],
        compiler_params=_cparams(("parallel", "parallel", "parallel", "arbitrary")),
        name=name,
    )(u3, bmat, cmat, a_re, a_im, h_re, h_im)


def _s5_params(lam_re, lam_im, log_step, b_re, b_im, c_re, c_im):
    step = jnp.exp(log_step)[..., None]
    zr, zi = lam_re * step, lam_im * step
    mag = jnp.exp(zr)
    a_re, a_im = mag * jnp.cos(zi), mag * jnp.sin(zi)
    nr, ni = a_re - 1.0, a_im
    den = lam_re * lam_re + lam_im * lam_im
    k_re = (nr * lam_re + ni * lam_im) / den
    k_im = (ni * lam_re - nr * lam_im) / den
    bb_re = k_re[..., None] * b_re - k_im[..., None] * b_im
    bb_im = k_re[..., None] * b_im + k_im[..., None] * b_re
    gl = S5_SLAB // SSM_GROUP
    eye = jnp.eye(gl, dtype=F32)

    def bdiag_in(w):
        w = w.reshape(2, S5_NSLAB, gl, SSM_STATE, SSM_GROUP)
        return jnp.einsum('dsgnp,gh->dsgphn', w, eye).reshape(2, S5_NSLAB, S5_SLAB, S5_LANES)

    def bdiag_out(w):
        w = w.reshape(2, S5_NSLAB, gl, SSM_GROUP, SSM_STATE)
        return jnp.einsum('dsgpn,gh->dsgnhp', w, eye).reshape(2, S5_NSLAB, S5_LANES, S5_SLAB)

    bmat = jnp.concatenate([bdiag_in(bb_re), bdiag_in(bb_im)], axis=-1).astype(BF16)
    cmat = jnp.concatenate([bdiag_out(c_re), -bdiag_out(c_im)], axis=-2).astype(BF16)
    slab = lambda a: a.reshape(2, S5_NSLAB, 1, S5_LANES)
    return bmat, cmat, slab(a_re), slab(a_im)


def _state_to_slabs(h):
    b = h.shape[0]
    return jnp.transpose(h.reshape(b, 2, S5_NSLAB, S5_LANES), (1, 2, 0, 3))


def _slabs_to_state(f):
    b = f.shape[2]
    return jnp.transpose(f, (2, 0, 1, 3)).reshape(b, 2, SSM_GROUPS, SSM_STATE)


def _glu_res_kernel(u_ref, y_ref, x_ref, mod_ref, d_ref, w_ref, b_ref, o_ref):
    yt = u_ref[...] * d_ref[...] + y_ref[0] + y_ref[1]
    z = _gelu_tanh(yt)
    gate = _sigmoid(jnp.dot(z.astype(BF16), w_ref[...], preferred_element_type=F32) + b_ref[...])
    o_ref[...] = x_ref[...] + mod_ref[0][2:3] * (z * gate)


def _glu_res_call(u, y, x, mod, d_skip, w_glu, b_glu):
    return pl.pallas_call(
        _glu_res_kernel,
        grid=(T // TM,),
        in_specs=[
            pl.BlockSpec((TM, D), lambda i: (i, 0)),
            pl.BlockSpec((2, TM, D), lambda i: (0, i, 0)),
            pl.BlockSpec((TM, D), lambda i: (i, 0)),
            pl.BlockSpec((1, 6, D), lambda i: (_cond_of_row(i * TM), 0, 0)),
            pl.BlockSpec((1, D), lambda i: (0, 0)),
            pl.BlockSpec((D, D), lambda i: (0, 0)),
            pl.BlockSpec((1, D), lambda i: (0, 0)),
        ],
        out_specs=pl.BlockSpec((TM, D), lambda i: (i, 0)),
        out_shape=jax.ShapeDtypeStruct((T, D), F32),
        compiler_params=_cparams(("parallel",)),
        name="s5_glu_res",
    )(u, y, x, mod, d_skip.reshape(1, D), w_glu, b_glu.reshape(1, D))


def _conv_kernel(bg_ref, cg_ref, xi_ref, cgp_ref, xip_ref, cgn_ref, xin_ref,
                 x_ref, mod_ref, cw_ref, cb_ref, w_ref, o_ref):
    i = pl.program_id(0)
    tiles_per_seq = DEC_SEQ // TM_CONV
    is_lat = i < T_LAT // TM_CONV
    first = jnp.logical_or(jnp.logical_not(is_lat), i % tiles_per_seq == 0)
    last = jnp.logical_or(jnp.logical_not(is_lat), i % tiles_per_seq == tiles_per_seq - 1)
    z = cg_ref[...] * xi_ref[...]
    zp_row = jnp.where(first, 0.0, cgp_ref[7:8, :] * xip_ref[7:8, :])
    zn_row = jnp.where(last, 0.0, cgn_ref[0:1, :] * xin_ref[0:1, :])
    row = lax.broadcasted_iota(jnp.int32, (TM_CONV, D), 0)
    zp = jnp.where(row == 0, zp_row, pltpu.roll(z, 1, axis=0))
    zn = jnp.where(row == TM_CONV - 1, zn_row, pltpu.roll(z, TM_CONV - 1, axis=0))
    cw = cw_ref[...]
    zc = cw[0:1] * zp + cw[1:2] * z + cw[2:3] * zn + cb_ref[...]
    a = (bg_ref[...] * zc).astype(BF16)
    y = jnp.dot(a, w_ref[...], preferred_element_type=F32)
    o_ref[...] = x_ref[...] + mod_ref[0][2:3] * y


def _conv_call(proj, x, mod, conv_w, conv_b, w_out):
    r8 = TM_CONV // 8
    n8 = T // 8
    col = lambda c: pl.BlockSpec((TM_CONV, D), lambda i, c=c: (i, c))
    prev = lambda c: pl.BlockSpec((8, D), lambda i, c=c: (jnp.maximum(i * r8 - 1, 0), c))
    nxt = lambda c: pl.BlockSpec((8, D), lambda i, c=c: (jnp.minimum((i + 1) * r8, n8 - 1), c))
    return pl.pallas_call(
        _conv_kernel,
        grid=(T // TM_CONV,),
        in_specs=[
            col(0), col(1), col(2), prev(1), prev(2), nxt(1), nxt(2),
            pl.BlockSpec((TM_CONV, D), lambda i: (i, 0)),
            pl.BlockSpec((1, 6, D), lambda i: (_cond_of_row(i * TM_CONV), 0, 0)),
            pl.BlockSpec((3, D), lambda i: (0, 0)),
            pl.BlockSpec((1, D), lambda i: (0, 0)),
            pl.BlockSpec((D, D), lambda i: (0, 0)),
        ],
        out_specs=pl.BlockSpec((TM_CONV, D), lambda i: (i, 0)),
        out_shape=jax.ShapeDtypeStruct((T, D), F32),
        compiler_params=_cparams(("parallel",)),
        name="conv_mix",
    )(proj, proj, proj, proj, proj, proj, proj, x, mod, conv_w, conv_b.reshape(1, D), w_out)


def _router_kernel(x_ref, g_ref, mod_ref, r_ref, h_ref, rt_ref):
    mod = mod_ref[0]
    h = _modnorm(x_ref[...], g_ref[...], mod[4:5], mod[3:4])
    h_ref[...] = h
    logits = jnp.dot(h, r_ref[...], preferred_element_type=F32, precision=lax.Precision.HIGHEST)
    lane = lax.broadcasted_iota(jnp.int32, logits.shape, 1)
    logits = jnp.where(lane < N_EXPERTS, logits, -jnp.inf)
    m1 = jnp.max(logits, axis=-1, keepdims=True)
    i1 = jnp.min(jnp.where(logits == m1, lane, LANES), axis=-1, keepdims=True)
    rest = jnp.where(lane == i1, -jnp.inf, logits)
    m2 = jnp.max(rest, axis=-1, keepdims=True)
    i2 = jnp.min(jnp.where(rest == m2, lane, LANES), axis=-1, keepdims=True)
    e2 = jnp.exp(m2 - m1)
    w1 = 1.0 / (1.0 + e2)
    w2 = e2 / (1.0 + e2)
    rt_ref[...] = jnp.where(lane == 0, i1.astype(F32),
                            jnp.where(lane == 1, i2.astype(F32),
                                      jnp.where(lane == 2, w1, jnp.where(lane == 3, w2, 0.0))))


def _router_call(x, ln_g, mod, router_pad):
    return pl.pallas_call(
        _router_kernel,
        grid=(T // TM,),
        in_specs=[
            pl.BlockSpec((TM, D), lambda i: (i, 0)),
            pl.BlockSpec((1, D), lambda i: (0, 0)),
            pl.BlockSpec((1, 6, D), lambda i: (_cond_of_row(i * TM), 0, 0)),
            pl.BlockSpec((D, LANES), lambda i: (0, 0)),
        ],
        out_specs=[pl.BlockSpec((TM, D), lambda i: (i, 0)),
                   pl.BlockSpec((TM, LANES), lambda i: (i, 0))],
        out_shape=[jax.ShapeDtypeStruct((T, D), F32), jax.ShapeDtypeStruct((T, LANES), F32)],
        compiler_params=_cparams(("parallel",)),
        name="moe_router",
    )(x, ln_g.reshape(1, D), mod, router_pad)


R_MAX = 2 * T + N_EXPERTS * TM_MOE
N_TILES = R_MAX // TM_MOE


def _row_copy(src_hbm, src_row, dst_vmem, dst_row, sem):
    return pltpu.make_async_copy(src_hbm.at[pl.ds(src_row, 1), :], dst_vmem.at[pl.ds(dst_row, 1), :], sem)


def _gather_kernel(nv_ref, src_ref, h_hbm, o_ref, buf, sem):
    i = pl.program_id(0)

    @pl.when(i < nv_ref[0])
    def _():
        base = i * TM_MOE

        def issue(r, c):
            _row_copy(h_hbm, src_ref[base + r], buf, r, sem).start()
            return c

        lax.fori_loop(0, TM_MOE, issue, 0)

        def drain(r, c):
            _row_copy(h_hbm, 0, buf, r, sem).wait()
            return c

        lax.fori_loop(0, TM_MOE, drain, 0)
        o_ref[...] = buf[...].astype(BF16)

    @pl.when(i >= nv_ref[0])
    def _():
        o_ref[...] = jnp.zeros_like(o_ref)


def _gather_call(n_valid, src_tok, h):
    return pl.pallas_call(
        _gather_kernel,
        grid_spec=pltpu.PrefetchScalarGridSpec(
            num_scalar_prefetch=2,
            grid=(N_TILES,),
            in_specs=[pl.BlockSpec(memory_space=pl.ANY)],
            out_specs=pl.BlockSpec((TM_MOE, D), lambda i, nv, src: (i, 0)),
            scratch_shapes=[pltpu.VMEM((TM_MOE, D), F32), pltpu.SemaphoreType.DMA(())],
        ),
        out_shape=jax.ShapeDtypeStruct((R_MAX, D), BF16),
        compiler_params=_cparams(("arbitrary",)),
        name="moe_gather",
    )(n_valid, src_tok, h)


def _moe_kernel(nv_ref, te_ref, xs_ref, w1_ref, w3_ref, w2_ref, o_ref, acc_s):
    i = pl.program_id(0)
    f = pl.program_id(1)
    nf = pl.num_programs(1)
    valid = i < nv_ref[0]

    @pl.when(jnp.logical_and(valid, f == 0))
    def _():
        acc_s[...] = jnp.zeros_like(acc_s)

    @pl.when(valid)
    def _():
        xs = xs_ref[...]
        a = _silu(jnp.dot(xs, w1_ref[0], preferred_element_type=F32)) \
            * jnp.dot(xs, w3_ref[0], preferred_element_type=F32)
        acc_s[...] += jnp.dot(a.astype(BF16), w2_ref[0], preferred_element_type=F32)

    @pl.when(f == nf - 1)
    def _():
        o_ref[...] = jnp.where(valid, acc_s[...], 0.0)


def _moe_call(n_valid, tile_expert, xs, w13, w2):
    nf = D_FF_EXPERT // TF_MOE

    def fe(i, f, nv):
        return jnp.where(i < nv[0], f, nf - 1)

    return pl.pallas_call(
        _moe_kernel,
        grid_spec=pltpu.PrefetchScalarGridSpec(
            num_scalar_prefetch=2,
            grid=(N_TILES, nf),
            in_specs=[
                pl.BlockSpec((TM_MOE, D), lambda i, f, nv, te: (i, 0)),
                pl.BlockSpec((1, D, TF_MOE), lambda i, f, nv, te: (te[i], 0, fe(i, f, nv))),
                pl.BlockSpec((1, D, TF_MOE), lambda i, f, nv, te: (te[i], 0, nf + fe(i, f, nv))),
                pl.BlockSpec((1, TF_MOE, D), lambda i, f, nv, te: (te[i], fe(i, f, nv), 0)),
            ],
            out_specs=pl.BlockSpec((TM_MOE, D), lambda i, f, nv, te: (i, 0)),
            scratch_shapes=[pltpu.VMEM((TM_MOE, D), F32)],
        ),
        out_shape=jax.ShapeDtypeStruct((R_MAX, D), F32),
        compiler_params=_cparams(("arbitrary", "arbitrary")),
        name="moe_experts",
    )(n_valid, tile_expert, xs, w13, w13, w2)


TM_COMB = 256


def _combine_kernel(pa_ref, pb_ref, y_hbm, x_ref, rt_ref, mod_ref, o_ref, buf_a, buf_b, sem):
    base = pl.program_id(0) * TM_COMB

    def issue(r, c):
        _row_copy(y_hbm, pa_ref[base + r], buf_a, r, sem).start()
        _row_copy(y_hbm, pb_ref[base + r], buf_b, r, sem).start()
        return c

    lax.fori_loop(0, TM_COMB, issue, 0)

    def drain(r, c):
        _row_copy(y_hbm, 0, buf_a, r, sem).wait()
        _row_copy(y_hbm, 0, buf_b, r, sem).wait()
        return c

    lax.fori_loop(0, TM_COMB, drain, 0)
    rt = rt_ref[...]
    ffn = rt[:, 2:3] * buf_a[...] + rt[:, 3:4] * buf_b[...]
    o_ref[...] = x_ref[...] + mod_ref[0][5:6] * ffn


def _combine_call(pos_a, pos_b, y, x, rt, mod):
    return pl.pallas_call(
        _combine_kernel,
        grid_spec=pltpu.PrefetchScalarGridSpec(
            num_scalar_prefetch=2,
            grid=(T // TM_COMB,),
            in_specs=[
                pl.BlockSpec(memory_space=pl.ANY),
                pl.BlockSpec((TM_COMB, D), lambda i, pa, pb: (i, 0)),
                pl.BlockSpec((TM_COMB, LANES), lambda i, pa, pb: (i, 0)),
                pl.BlockSpec((1, 6, D), lambda i, pa, pb: (_cond_of_row(i * TM_COMB), 0, 0)),
            ],
            out_specs=pl.BlockSpec((TM_COMB, D), lambda i, pa, pb: (i, 0)),
            scratch_shapes=[pltpu.VMEM((TM_COMB, D), F32), pltpu.VMEM((TM_COMB, D), F32),
                            pltpu.SemaphoreType.DMA(())],
        ),
        out_shape=jax.ShapeDtypeStruct((T, D), F32),
        compiler_params=_cparams(("arbitrary",)),
        name="moe_combine",
    )(pos_a, pos_b, y, x, rt, mod)


def _routing_tables(rt):
    ea = rt[:, 0:2].astype(jnp.int32).reshape(-1)
    onehot = (ea[:, None] == jnp.arange(N_EXPERTS, dtype=jnp.int32)[None, :]).astype(jnp.int32)
    csum = jnp.cumsum(onehot, axis=0)
    rank = jnp.take_along_axis(csum, ea[:, None], axis=1)[:, 0] - 1
    counts = csum[-1]
    padded = ((counts + TM_MOE - 1) // TM_MOE) * TM_MOE
    ends = jnp.cumsum(padded)
    pos = (ends - padded)[ea] + rank
    n_valid = (ends[-1] // TM_MOE).astype(jnp.int32).reshape(1)
    tile_start = jnp.arange(N_TILES, dtype=jnp.int32) * TM_MOE
    tile_expert = jnp.minimum(jnp.searchsorted(ends, tile_start, side='right'),
                              jnp.searchsorted(ends, ends[-1] - 1, side='right')).astype(jnp.int32)
    tok = jnp.arange(2 * T, dtype=jnp.int32) // 2
    src_tok = jnp.zeros((R_MAX,), jnp.int32).at[pos].set(tok)
    pos2 = pos.reshape(T, 2).astype(jnp.int32)
    return n_valid, tile_expert, src_tok, pos2[:, 0], pos2[:, 1]


def _moe_layer(x, ln_g, mod, router, w13, w2):
    router_pad = jnp.pad(router, ((0, 0), (0, LANES - N_EXPERTS)))
    h, rt = _router_call(x, ln_g, mod, router_pad)
    n_valid, tile_expert, src_tok, pos_a, pos_b = _routing_tables(rt)
    xs = _gather_call(n_valid, src_tok, h)
    y = _moe_call(n_valid, tile_expert, xs, w13, w2)
    return _combine_call(pos_a, pos_b, y, x, rt, mod)


def kernel(x_prompt, x_sample, c, cache_na_k, cache_na_v, state_ssm_re, state_ssm_im, c_ctx,
           ln1_g, ln2_g, ada_w, ada_b, na_w_qkv, na_w_o, na_q_g, na_k_g, na_rpb,
           ssm_lam_re, ssm_lam_im, ssm_log_step, ssm_b_re, ssm_b_im, ssm_c_re, ssm_c_im,
           ssm_d, ssm_w_glu, ssm_b_glu, cv_w_in, cv_conv_w, cv_conv_b, cv_w_out,
           ffn_w13, ffn_w2, moe_router, moe_w13, moe_w2):
    depth = ada_w.shape[0]
    x = jnp.concatenate([x_sample.reshape(T_LAT, D), x_prompt.reshape(T_CTX, D)], axis=0)
    conds = jnp.concatenate([c, c_ctx[None, :], jnp.zeros((N_COND - DEC_BATCH - 1, D), F32)], axis=0)
    mods = _ada_call(conds, ada_w, ada_b).reshape(depth, N_COND, 6, D)

    ks, vs, s_re, s_im = [], [], [], []
    for l in range(depth):
        kind, j = l % 3, l // 3
        mod = mods[l]
        if kind == 0:
            qkv = _normproj_call(x, ln1_g[l], mod, na_w_qkv[j].astype(BF16), shift=0, scale=1,
                                 name="qkv_proj")
            qg2 = jnp.tile(na_q_g[j], 2).reshape(1, LANES)
            kg2 = jnp.tile(na_k_g[j], 2).reshape(1, LANES)
            o_ctx, k_new, v_new = _ctx_attn_call(qkv, qg2, kg2)
            ks.append(k_new)
            vs.append(v_new)
            past = cache_na_k.shape[3]
            ctx_k = jnp.transpose(cache_na_k[:, j], (0, 2, 1, 3)).reshape(DEC_BATCH, past, D)
            ctx_v = jnp.transpose(cache_na_v[:, j], (0, 2, 1, 3)).reshape(DEC_BATCH, past, D)
            o_lat = _lat_attn_call(qkv, ctx_k, ctx_v, qg2, kg2, _bias_pairs(na_rpb[j]))
            o = jnp.concatenate([o_lat, o_ctx], axis=0)
            x = _proj_res_call(o, x, mod, na_w_o[j].astype(BF16), gate=2, name="attn_out")
        elif kind == 1:
            u = _norm_call(x, ln1_g[l], mod, shift=0, scale=1, name="s5_norm")
            bmat, cmat, a_re, a_im = _s5_params(ssm_lam_re[j], ssm_lam_im[j], ssm_log_step[j],
                                                ssm_b_re[j], ssm_b_im[j], ssm_c_re[j], ssm_c_im[j])
            u_lat = u[:T_LAT].reshape(DEC_BATCH, DEC_SEQ, D)
            u_ctx = u[T_LAT:].reshape(BATCH, SEQ, D)
            y_lat, _, _ = _s5_call(u_lat, bmat, cmat, a_re, a_im,
                                   _state_to_slabs(state_ssm_re[:, j]), _state_to_slabs(state_ssm_im[:, j]),
                                   nb=DEC_BATCH, steps=S5_T_LAT, name="s5_lat")
            zero = jnp.zeros((2, S5_NSLAB, BATCH, S5_LANES), F32)
            y_ctx, f_re, f_im = _s5_call(u_ctx, bmat, cmat, a_re, a_im, zero, zero,
                                         nb=8, steps=SEQ, name="s5_ctx")
            s_re.append(_slabs_to_state(f_re))
            s_im.append(_slabs_to_state(f_im))
            y = jnp.concatenate([y_lat.reshape(2, T_LAT, D), y_ctx.reshape(2, T_CTX, D)], axis=1)
            x = _glu_res_call(u, y, x, mod, ssm_d[j], ssm_w_glu[j].astype(BF16), ssm_b_glu[j])
        else:
            proj = _normproj_call(x, ln1_g[l], mod, cv_w_in[j].astype(BF16), shift=0, scale=1,
                                  name="conv_in")
            x = _conv_call(proj, x, mod, cv_conv_w[j], cv_conv_b[j], cv_w_out[j].astype(BF16))
        jj = l // 2
        if l % 2 == 0:
            x = _ffn_call(x, ln2_g[l], mod, ffn_w13[jj].astype(BF16), ffn_w2[jj].astype(BF16))
        else:
            x = _moe_layer(x, ln2_g[l], mod, moe_router[jj],
                           moe_w13[jj].astype(BF16), moe_w2[jj].astype(BF16))

    y_sample = x[:T_LAT].reshape(DEC_BATCH, DEC_SEQ, D)
    y_prompt = x[T_LAT:].reshape(BATCH, SEQ, D)
    return (y_prompt, y_sample, jnp.stack(ks, axis=1), jnp.stack(vs, axis=1),
            jnp.stack(s_re, axis=1), jnp.stack(s_im, axis=1))
```

```python
import functools
import math

import numpy as np
import jax
import jax.numpy as jnp
from jax import lax
from jax.experimental import pallas as pl
from jax.experimental.pallas import tpu as pltpu

F32 = jnp.float32
BF16 = jnp.bfloat16

D = 1024
BATCH, SEQ = 16, 256
DEC_BATCH, DEC_SEQ = 4, 2048
GRID_W = 64
ROWS = DEC_SEQ // GRID_W
N_HEADS, HEAD_DIM = 16, 64
WIN_ROWS, WIN_COLS = 8, 16
SSM_GROUP, SSM_GROUPS, SSM_STATE = 16, 64, 64
D_FF = 2816
N_EXPERTS = 8
D_FF_EXPERT = 3584
EPS = 1e-6
NEG_INF = -1e30
SCALE = HEAD_DIM ** -0.5

T_LAT = DEC_BATCH * DEC_SEQ
T_CTX = BATCH * SEQ
T = T_LAT + T_CTX
N_COND = 8
CTX_COND = DEC_BATCH

LANES = 128
VMEM_LIMIT = 56 * 1024 * 1024

TM = 512
TM_CONV = 256
TM_MOE = 512
TF_MOE = 512
TF_FFN = D_FF // 2
HP = N_HEADS // 2
QB_ROWS = 4
S5_SLAB = 128
S5_NSLAB = D // S5_SLAB
S5_LANES = (S5_SLAB // SSM_GROUP) * SSM_STATE
S5_T_LAT = 512
S5_ROW_PAD = 4
S5_UNROLL = 4


def _cparams(sem):
    return pltpu.CompilerParams(dimension_semantics=sem, vmem_limit_bytes=VMEM_LIMIT)


def _cond_of_row(row):
    return jnp.minimum(row // DEC_SEQ, CTX_COND)


def _modnorm(x, g, sc, sh):
    ms = jnp.mean(x * x, axis=-1, keepdims=True)
    y = x * lax.rsqrt(ms + EPS) * g
    return y * (1.0 + sc) + sh


def _sigmoid(x):
    return 1.0 / (1.0 + jnp.exp(-x))


def _silu(x):
    return x * _sigmoid(x)


def _gelu_tanh(x):
    c = math.sqrt(2.0 / math.pi)
    return 0.5 * x * (1.0 + jnp.tanh(c * (x + 0.044715 * (x * x * x))))


def _ada_kernel(c_ref, w_ref, b_ref, o_ref):
    s = _silu(c_ref[...]).astype(BF16)
    o_ref[0] = jnp.dot(s, w_ref[0].astype(BF16), preferred_element_type=F32) + b_ref[0]


def _ada_call(conds, ada_w, ada_b):
    depth = ada_w.shape[0]
    tn = 1536
    return pl.pallas_call(
        _ada_kernel,
        grid=(depth, 6 * D // tn),
        in_specs=[
            pl.BlockSpec((N_COND, D), lambda l, n: (0, 0)),
            pl.BlockSpec((1, D, tn), lambda l, n: (l, 0, n)),
            pl.BlockSpec((1, 1, tn), lambda l, n: (l, 0, n)),
        ],
        out_specs=pl.BlockSpec((1, N_COND, tn), lambda l, n: (l, 0, n)),
        out_shape=jax.ShapeDtypeStruct((depth, N_COND, 6 * D), F32),
        compiler_params=_cparams(("parallel", "parallel")),
        name="ada_mod",
    )(conds, ada_w, ada_b.reshape(depth, 1, 6 * D))


def _normproj_kernel(x_ref, g_ref, mod_ref, w_ref, o_ref, *, shift, scale):
    mod = mod_ref[0]
    h = _modnorm(x_ref[...], g_ref[...], mod[scale:scale + 1], mod[shift:shift + 1])
    o_ref[...] = jnp.dot(h.astype(BF16), w_ref[...], preferred_element_type=F32)


def _normproj_call(x, ln_g, mod, w, *, shift, scale, name):
    n = w.shape[1]
    return pl.pallas_call(
        functools.partial(_normproj_kernel, shift=shift, scale=scale),
        grid=(T // TM,),
        in_specs=[
            pl.BlockSpec((TM, D), lambda i: (i, 0)),
            pl.BlockSpec((1, D), lambda i: (0, 0)),
            pl.BlockSpec((1, 6, D), lambda i: (_cond_of_row(i * TM), 0, 0)),
            pl.BlockSpec((D, n), lambda i: (0, 0)),
        ],
        out_specs=pl.BlockSpec((TM, n), lambda i: (i, 0)),
        out_shape=jax.ShapeDtypeStruct((T, n), F32),
        compiler_params=_cparams(("parallel",)),
        name=name,
    )(x, ln_g.reshape(1, D), mod, w)


def _norm_kernel(x_ref, g_ref, mod_ref, o_ref, *, shift, scale):
    mod = mod_ref[0]
    o_ref[...] = _modnorm(x_ref[...], g_ref[...], mod[scale:scale + 1], mod[shift:shift + 1])


def _norm_call(x, ln_g, mod, *, shift, scale, name):
    return pl.pallas_call(
        functools.partial(_norm_kernel, shift=shift, scale=scale),
        grid=(T // TM,),
        in_specs=[
            pl.BlockSpec((TM, D), lambda i: (i, 0)),
            pl.BlockSpec((1, D), lambda i: (0, 0)),
            pl.BlockSpec((1, 6, D), lambda i: (_cond_of_row(i * TM), 0, 0)),
        ],
        out_specs=pl.BlockSpec((TM, D), lambda i: (i, 0)),
        out_shape=jax.ShapeDtypeStruct((T, D), F32),
        compiler_params=_cparams(("parallel",)),
        name=name,
    )(x, ln_g.reshape(1, D), mod)


LAT_TILES = T_LAT // TM


def _lat_tile(i):
    return jnp.minimum(i, LAT_TILES - 1)


def _ctx_tile(i):
    return jnp.maximum(i - LAT_TILES, 0)


def _proj_res_kernel(al_ref, ac_ref, x_ref, mod_ref, w_ref, o_ref, *, gate):
    a = jnp.where(pl.program_id(0) < LAT_TILES, al_ref[...], ac_ref[...])
    y = jnp.dot(a.astype(BF16), w_ref[...], preferred_element_type=F32)
    o_ref[...] = x_ref[...] + mod_ref[0][gate:gate + 1] * y


def _proj_res_call(a_lat, a_ctx, x, mod, w, *, gate, name):
    return pl.pallas_call(
        functools.partial(_proj_res_kernel, gate=gate),
        grid=(T // TM,),
        in_specs=[
            pl.BlockSpec((TM, D), lambda i: (_lat_tile(i), 0)),
            pl.BlockSpec((TM, D), lambda i: (_ctx_tile(i), 0)),
            pl.BlockSpec((TM, D), lambda i: (i, 0)),
            pl.BlockSpec((1, 6, D), lambda i: (_cond_of_row(i * TM), 0, 0)),
            pl.BlockSpec((D, D), lambda i: (0, 0)),
        ],
        out_specs=pl.BlockSpec((TM, D), lambda i: (i, 0)),
        out_shape=jax.ShapeDtypeStruct((T, D), F32),
        compiler_params=_cparams(("parallel",)),
        name=name,
    )(a_lat, a_ctx, x, mod, w)


def _head_norm(x, g, head0):
    x2 = x * x
    s0 = jnp.sum(jnp.where(head0, x2, 0.0), axis=-1, keepdims=True)
    s1 = jnp.sum(jnp.where(head0, 0.0, x2), axis=-1, keepdims=True)
    ms = jnp.where(head0, s0, s1) * (1.0 / HEAD_DIM)
    return x * lax.rsqrt(ms + EPS) * g


def _nt_dot(a, b):
    return lax.dot_general(a, b, (((1,), (1,)), ((), ())), preferred_element_type=F32)


def _ctx_attn_kernel(q_ref, k_ref, v_ref, qg_ref, kg_ref, o_ref, ko_ref, vo_ref):
    head0 = lax.broadcasted_iota(jnp.int32, (SEQ, LANES), 1) < HEAD_DIM
    qn = _head_norm(q_ref[...], qg_ref[...], head0)
    kn = _head_norm(k_ref[...], kg_ref[...], head0)
    v = v_ref[...]
    ko_ref[0, 0] = kn[:, :HEAD_DIM]
    ko_ref[0, 1] = kn[:, HEAD_DIM:]
    vo_ref[0, 0] = v[:, :HEAD_DIM]
    vo_ref[0, 1] = v[:, HEAD_DIM:]
    knb = kn.astype(BF16)
    vb = v.astype(BF16)
    outs = []
    for hh in range(2):
        qm = jnp.where(head0 if hh == 0 else jnp.logical_not(head0), qn, 0.0).astype(BF16)
        s = _nt_dot(qm, knb) * SCALE
        p = jnp.exp(s - jnp.max(s, axis=-1, keepdims=True))
        l = jnp.sum(p, axis=-1, keepdims=True)
        outs.append(jnp.dot(p.astype(BF16), vb, preferred_element_type=F32) / l)
    o_ref[...] = jnp.where(head0, outs[0], outs[1])


def _ctx_attn_call(qkv, qg2, kg2):
    row0 = T_LAT // SEQ
    blk = lambda c: pl.BlockSpec((SEQ, LANES), lambda b, hp, c=c: (row0 + b, c * HP + hp))
    kv_out = pl.BlockSpec((1, 2, SEQ, HEAD_DIM), lambda b, hp: (b, hp, 0, 0))
    return pl.pallas_call(
        _ctx_attn_kernel,
        grid=(BATCH, HP),
        in_specs=[blk(0), blk(1), blk(2),
                  pl.BlockSpec((1, LANES), lambda b, hp: (0, 0)),
                  pl.BlockSpec((1, LANES), lambda b, hp: (0, 0))],
        out_specs=[pl.BlockSpec((SEQ, LANES), lambda b, hp: (b, hp)), kv_out, kv_out],
        out_shape=[jax.ShapeDtypeStruct((T_CTX, D), F32),
                   jax.ShapeDtypeStruct((BATCH, N_HEADS, SEQ, HEAD_DIM), F32),
                   jax.ShapeDtypeStruct((BATCH, N_HEADS, SEQ, HEAD_DIM), F32)],
        compiler_params=_cparams(("parallel", "parallel")),
        name="ctx_attn",
    )(qkv, qkv, qkv, qg2, kg2)


def _band_of_block(p):
    r_lo, r_hi = QB_ROWS * p, QB_ROWS * p + QB_ROWS - 1
    kr = min(WIN_ROWS, ROWS)
    lo = min(max(r_lo - kr // 2, 0), ROWS - kr)
    hi = min(max(r_hi - kr // 2, 0), ROWS - kr) + kr
    lo -= lo % 2
    hi += hi % 2
    return lo, hi - lo


def _lat_attn_kernel(q_ref, k_ref, v_ref, ck_ref, cv_ref, qg_ref, kg_ref, bp_ref, o_ref,
                     qm0_s, qm1_s, kn_s, v_s):
    head0 = lax.broadcasted_iota(jnp.int32, (DEC_SEQ, LANES), 1) < HEAD_DIM
    qn = _head_norm(q_ref[...], qg_ref[...], head0)
    qm0_s[...] = jnp.where(head0, qn, 0.0).astype(BF16)
    qm1_s[...] = jnp.where(head0, 0.0, qn).astype(BF16)
    kn_s[...] = _head_norm(k_ref[...], kg_ref[...], head0).astype(BF16)
    v_s[...] = v_ref[...].astype(BF16)
    ckb = ck_ref[0].astype(BF16)
    cvb = cv_ref[0].astype(BF16)

    qc = lax.broadcasted_iota(jnp.int32, (GRID_W, LANES), 0)
    kl = lax.broadcasted_iota(jnp.int32, (GRID_W, LANES), 1)
    kc = jnp.where(kl < GRID_W, kl, kl - GRID_W)
    cs = jnp.clip(qc - WIN_COLS // 2, 0, GRID_W - WIN_COLS)
    col_ok = jnp.logical_and(kc >= cs, kc < cs + WIN_COLS)
    left = kl < GRID_W
    mask_of = {
        (True, True): col_ok,
        (True, False): jnp.logical_and(col_ok, left),
        (False, True): jnp.logical_and(col_ok, jnp.logical_not(left)),
    }
    h0q = lax.broadcasted_iota(jnp.int32, (QB_ROWS * GRID_W, LANES), 1) < HEAD_DIM
    kr_win = min(WIN_ROWS, ROWS)

    for p in range(ROWS // QB_ROWS):
        u0, nrows = _band_of_block(p)
        q_lo = p * QB_ROWS * GRID_W
        kb = kn_s[u0 * GRID_W:(u0 + nrows) * GRID_W, :]
        vb = v_s[u0 * GRID_W:(u0 + nrows) * GRID_W, :]
        outs = []
        for hh in range(2):
            qm = (qm0_s if hh == 0 else qm1_s)[q_lo:q_lo + QB_ROWS * GRID_W, :]
            s_loc = _nt_dot(qm, kb) * SCALE
            row_blocks = []
            for i in range(QB_ROWS):
                r = p * QB_ROWS + i
                r0 = min(max(r - kr_win // 2, 0), ROWS - kr_win)
                blocks = []
                for m in range(nrows // 2):
                    kr = u0 + 2 * m
                    ok_l = r0 <= kr < r0 + kr_win
                    ok_r = r0 <= kr + 1 < r0 + kr_win
                    sb = s_loc[i * GRID_W:(i + 1) * GRID_W, m * LANES:(m + 1) * LANES]
                    if not (ok_l or ok_r):
                        blocks.append(jnp.full((GRID_W, LANES), NEG_INF, F32))
                    else:
                        bias = bp_ref[hh, kr - r + WIN_ROWS]
                        blocks.append(jnp.where(mask_of[(ok_l, ok_r)], sb + bias, NEG_INF))
                row_blocks.append(jnp.concatenate(blocks, axis=1))
            s_loc = jnp.concatenate(row_blocks, axis=0)
            s_ctx = _nt_dot(qm, ckb) * SCALE
            mx = jnp.maximum(jnp.max(s_loc, axis=-1, keepdims=True),
                             jnp.max(s_ctx, axis=-1, keepdims=True))
            p_loc = jnp.exp(s_loc - mx)
            p_ctx = jnp.exp(s_ctx - mx)
            l = jnp.sum(p_loc, axis=-1, keepdims=True) + jnp.sum(p_ctx, axis=-1, keepdims=True)
            o = (jnp.dot(p_loc.astype(BF16), vb, preferred_element_type=F32)
                 + jnp.dot(p_ctx.astype(BF16), cvb, preferred_element_type=F32))
            outs.append(o / l)
        o_ref[q_lo:q_lo + QB_ROWS * GRID_W, :] = jnp.where(h0q, outs[0], outs[1])


def _lat_attn_call(qkv, ctx_k, ctx_v, qg2, kg2, bias_pairs):
    blk = lambda c: pl.BlockSpec((DEC_SEQ, LANES), lambda hp, b, c=c: (b, c * HP + hp))
    cblk = pl.BlockSpec((1, ctx_k.shape[1], LANES), lambda hp, b: (b, 0, hp))
    return pl.pallas_call(
        _lat_attn_kernel,
        grid=(HP, DEC_BATCH),
        in_specs=[blk(0), blk(1), blk(2), cblk, cblk,
                  pl.BlockSpec((1, LANES), lambda hp, b: (0, 0)),
                  pl.BlockSpec((1, LANES), lambda hp, b: (0, 0)),
                  pl.BlockSpec((2, 2 * WIN_ROWS, GRID_W, LANES), lambda hp, b: (hp, 0, 0, 0))],
        out_specs=pl.BlockSpec((DEC_SEQ, LANES), lambda hp, b: (b, hp)),
        out_shape=jax.ShapeDtypeStruct((T_LAT, D), F32),
        scratch_shapes=[pltpu.VMEM((DEC_SEQ, LANES), BF16)] * 4,
        compiler_params=_cparams(("parallel", "parallel")),
        name="lat_attn",
    )(qkv, qkv, qkv, ctx_k, ctx_v, qg2, kg2, bias_pairs)


def _bias_pairs(rpb):
    cols = np.arange(GRID_W)
    col_idx = np.clip(cols[None, :] - cols[:, None], -(WIN_COLS - 1), WIN_COLS - 1) + WIN_COLS - 1
    rc = rpb[:, :, col_idx]
    left = jnp.pad(rc, ((0, 0), (1, 0), (0, 0), (0, 0)))
    right = jnp.pad(rc, ((0, 0), (0, 1), (0, 0), (0, 0)))
    return jnp.concatenate([left, right], axis=-1)


def _ffn_kernel(x_ref, g_ref, mod_ref, w1_ref, w3_ref, w2_ref, o_ref, h_s, acc_s):
    f = pl.program_id(1)
    mod = mod_ref[0]

    @pl.when(f == 0)
    def _():
        h_s[...] = _modnorm(x_ref[...], g_ref[...], mod[4:5], mod[3:4]).astype(BF16)
        acc_s[...] = jnp.zeros_like(acc_s)

    h = h_s[...]
    a = _silu(jnp.dot(h, w1_ref[...], preferred_element_type=F32)) \
        * jnp.dot(h, w3_ref[...], preferred_element_type=F32)
    acc_s[...] += jnp.dot(a.astype(BF16), w2_ref[...], preferred_element_type=F32)

    @pl.when(f == pl.num_programs(1) - 1)
    def _():
        o_ref[...] = x_ref[...] + mod[5:6] * acc_s[...]


def _ffn_call(x, ln_g, mod, w13, w2, layer):
    nf = D_FF // TF_FFN
    return pl.pallas_call(
        _ffn_kernel,
        grid=(T // TM, nf),
        in_specs=[
            pl.BlockSpec((TM, D), lambda i, f: (i, 0)),
            pl.BlockSpec((1, D), lambda i, f: (0, 0)),
            pl.BlockSpec((1, 6, D), lambda i, f: (_cond_of_row(i * TM), 0, 0)),
            pl.BlockSpec((None, D, TF_FFN), lambda i, f: (layer, 0, f)),
            pl.BlockSpec((None, D, TF_FFN), lambda i, f: (layer, 0, nf + f)),
            pl.BlockSpec((None, TF_FFN, D), lambda i, f: (layer, f, 0)),
        ],
        out_specs=pl.BlockSpec((TM, D), lambda i, f: (i, 0)),
        out_shape=jax.ShapeDtypeStruct((T, D), F32),
        scratch_shapes=[pltpu.VMEM((TM, D), BF16), pltpu.VMEM((TM, D), F32)],
        compiler_params=_cparams(("parallel", "arbitrary")),
        name="ffn_dense",
    )(x, ln_g.reshape(1, D), mod, w13, w13, w2)


def _s5_kernel(u_ref, bm_ref, cm_ref, ar_ref, ai_ref, hr_ref, hi_ref,
               y_ref, fr_ref, fi_ref, bu_s, st_s, cr_s, ci_s, *, nb, steps):
    d = pl.program_id(0)
    tb = pl.program_id(3)
    rows = nb * steps
    pitch = steps + S5_ROW_PAD
    nch = S5_LANES // LANES
    u = u_ref[...].reshape(rows, S5_SLAB).astype(BF16)
    bu = jnp.dot(u, bm_ref[0, 0], preferred_element_type=F32)
    for c in range(2 * nch):
        for k in range(nb):
            bu_s[c, k * pitch:k * pitch + steps, :] = bu[k * steps:(k + 1) * steps, c * LANES:(c + 1) * LANES]

    @pl.when(tb == 0)
    def _():
        cr_s[...] = hr_ref[0, 0]
        ci_s[...] = hi_ref[0, 0]

    ar = [jnp.broadcast_to(ar_ref[0, 0][:, c * LANES:(c + 1) * LANES], (nb, LANES)) for c in range(nch)]
    ai = [jnp.broadcast_to(ai_ref[0, 0][:, c * LANES:(c + 1) * LANES], (nb, LANES)) for c in range(nch)]

    def step(i, carry):
        l = jnp.where(d == 0, i, steps - 1 - i)
        idx = pl.ds(l, nb, stride=pitch)
        out = []
        for c in range(nch):
            sr, si = carry[2 * c], carry[2 * c + 1]
            nr = ar[c] * sr - ai[c] * si + bu_s[c, idx, :]
            ni = ar[c] * si + ai[c] * sr + bu_s[nch + c, idx, :]
            st_s[c, idx, :] = nr
            st_s[nch + c, idx, :] = ni
            out += [nr, ni]
        return tuple(out)

    init = []
    for c in range(nch):
        init += [cr_s[:, c * LANES:(c + 1) * LANES], ci_s[:, c * LANES:(c + 1) * LANES]]
    fin = lax.fori_loop(0, steps, step, tuple(init), unroll=S5_UNROLL)
    sr = jnp.concatenate([fin[2 * c] for c in range(nch)], axis=1)
    si = jnp.concatenate([fin[2 * c + 1] for c in range(nch)], axis=1)
    cr_s[...] = sr
    ci_s[...] = si
    fr_ref[0, 0] = sr
    fi_ref[0, 0] = si
    states = jnp.concatenate(
        [jnp.concatenate([st_s[c, k * pitch:k * pitch + steps, :] for k in range(nb)], axis=0)
         for c in range(2 * nch)], axis=1).astype(BF16)
    y = jnp.dot(states, cm_ref[0, 0], preferred_element_type=F32)
    y_ref[0] = y.reshape(nb, steps, S5_SLAB)


def _s5_call(u, bmat, cmat, a_re, a_im, h_re, h_im, *, row0, n_seq, seq_len, nb, steps, name):
    n_sg, n_tb = n_seq // nb, seq_len // steps
    sg0 = row0 // (nb * seq_len)

    def tbi(d, tb):
        return jnp.where(d == 0, tb, n_tb - 1 - tb)

    par = lambda last: pl.BlockSpec((1, 1) + last, lambda d, j, sg, tb: (d, j, 0, 0))
    st = pl.BlockSpec((1, 1, nb, S5_LANES), lambda d, j, sg, tb: (d, j, sg, 0))
    y, f_re, f_im = pl.pallas_call(
        functools.partial(_s5_kernel, nb=nb, steps=steps),
        grid=(2, S5_NSLAB, n_sg, n_tb),
        in_specs=[
            pl.BlockSpec((nb, steps, S5_SLAB), lambda d, j, sg, tb: (sg0 + sg, tbi(d, tb), j)),
            par((S5_SLAB, 2 * S5_LANES)),
            par((2 * S5_LANES, S5_SLAB)),
            par((1, S5_LANES)), par((1, S5_LANES)),
            st, st,
        ],
        out_specs=[
            pl.BlockSpec((1, nb, steps, S5_SLAB), lambda d, j, sg, tb: (d, sg, tbi(d, tb), j)),
            st, st,
        ],
        out_shape=[
            jax.ShapeDtypeStruct((2, n_seq, seq_len, D), F32),
            jax.ShapeDtypeStruct((2, S5_NSLAB, n_seq, S5_LANES), F32),
            jax.ShapeDtypeStruct((2, S5_NSLAB, n_seq, S5_LANES), F32),
        ],
        scratch_shapes=[pltpu.VMEM((2 * S5_LANES // LANES, nb * (steps + S5_ROW_PAD), LANES), F32)] * 2
                       + [pltpu.VMEM((nb, S5_LANES), F32)] * 2,
        compiler_params=_cparams(("parallel", "parallel", "parallel", "arbitrary")),
        name=name,
    )(u.reshape(T // seq_len, seq_len, D), bmat, cmat, a_re, a_im, h_re, h_im)
    return y.reshape(2, n_seq * seq_len, D), f_re, f_im


def _s5_params(lam_re, lam_im, log_step, b_re, b_im, c_re, c_im):
    step = jnp.exp(log_step)[..., None]
    zr, zi = lam_re * step, lam_im * step
    mag = jnp.exp(zr)
    a_re, a_im = mag * jnp.cos(zi), mag * jnp.sin(zi)
    nr, ni = a_re - 1.0, a_im
    den = lam_re * lam_re + lam_im * lam_im
    k_re = (nr * lam_re + ni * lam_im) / den
    k_im = (ni * lam_re - nr * lam_im) / den
    bb_re = k_re[..., None] * b_re - k_im[..., None] * b_im
    bb_im = k_re[..., None] * b_im + k_im[..., None] * b_re
    gl = S5_SLAB // SSM_GROUP
    eye = jnp.eye(gl, dtype=F32)

    def bdiag_in(w):
        w = w.reshape(2, S5_NSLAB, gl, SSM_STATE, SSM_GROUP)
        return jnp.einsum('dsgnp,gh->dsgphn', w, eye).reshape(2, S5_NSLAB, S5_SLAB, S5_LANES)

    def bdiag_out(w):
        w = w.reshape(2, S5_NSLAB, gl, SSM_GROUP, SSM_STATE)
        return jnp.einsum('dsgpn,gh->dsgnhp', w, eye).reshape(2, S5_NSLAB, S5_LANES, S5_SLAB)

    bmat = jnp.concatenate([bdiag_in(bb_re), bdiag_in(bb_im)], axis=-1).astype(BF16)
    cmat = jnp.concatenate([bdiag_out(c_re), -bdiag_out(c_im)], axis=-2).astype(BF16)
    slab = lambda a: a.reshape(2, S5_NSLAB, 1, S5_LANES)
    return bmat, cmat, slab(a_re), slab(a_im)


def _state_to_slabs(h):
    b = h.shape[0]
    return jnp.transpose(h.reshape(b, 2, S5_NSLAB, S5_LANES), (1, 2, 0, 3))


def _slabs_to_state(f):
    b = f.shape[2]
    return jnp.transpose(f, (2, 0, 1, 3)).reshape(b, 2, SSM_GROUPS, SSM_STATE)


def _glu_res_kernel(u_ref, yl_ref, yc_ref, x_ref, mod_ref, d_ref, w_ref, b_ref, o_ref):
    is_lat = pl.program_id(0) < LAT_TILES
    y = jnp.where(is_lat, yl_ref[0] + yl_ref[1], yc_ref[0] + yc_ref[1])
    yt = u_ref[...] * d_ref[...] + y
    z = _gelu_tanh(yt)
    gate = _sigmoid(jnp.dot(z.astype(BF16), w_ref[...], preferred_element_type=F32) + b_ref[...])
    o_ref[...] = x_ref[...] + mod_ref[0][2:3] * (z * gate)


def _glu_res_call(u, y_lat, y_ctx, x, mod, d_skip, w_glu, b_glu):
    return pl.pallas_call(
        _glu_res_kernel,
        grid=(T // TM,),
        in_specs=[
            pl.BlockSpec((TM, D), lambda i: (i, 0)),
            pl.BlockSpec((2, TM, D), lambda i: (0, _lat_tile(i), 0)),
            pl.BlockSpec((2, TM, D), lambda i: (0, _ctx_tile(i), 0)),
            pl.BlockSpec((TM, D), lambda i: (i, 0)),
            pl.BlockSpec((1, 6, D), lambda i: (_cond_of_row(i * TM), 0, 0)),
            pl.BlockSpec((1, D), lambda i: (0, 0)),
            pl.BlockSpec((D, D), lambda i: (0, 0)),
            pl.BlockSpec((1, D), lambda i: (0, 0)),
        ],
        out_specs=pl.BlockSpec((TM, D), lambda i: (i, 0)),
        out_shape=jax.ShapeDtypeStruct((T, D), F32),
        compiler_params=_cparams(("parallel",)),
        name="s5_glu_res",
    )(u, y_lat, y_ctx, x, mod, d_skip.reshape(1, D), w_glu, b_glu.reshape(1, D))


def _conv_kernel(bg_ref, cg_ref, xi_ref, cgp_ref, xip_ref, cgn_ref, xin_ref,
                 x_ref, mod_ref, cw_ref, cb_ref, w_ref, o_ref):
    i = pl.program_id(0)
    tiles_per_seq = DEC_SEQ // TM_CONV
    is_lat = i < T_LAT // TM_CONV
    first = jnp.logical_or(jnp.logical_not(is_lat), i % tiles_per_seq == 0)
    last = jnp.logical_or(jnp.logical_not(is_lat), i % tiles_per_seq == tiles_per_seq - 1)
    z = cg_ref[...] * xi_ref[...]
    zp_row = jnp.where(first, 0.0, cgp_ref[7:8, :] * xip_ref[7:8, :])
    zn_row = jnp.where(last, 0.0, cgn_ref[0:1, :] * xin_ref[0:1, :])
    row = lax.broadcasted_iota(jnp.int32, (TM_CONV, D), 0)
    zp = jnp.where(row == 0, zp_row, pltpu.roll(z, 1, axis=0))
    zn = jnp.where(row == TM_CONV - 1, zn_row, pltpu.roll(z, TM_CONV - 1, axis=0))
    cw = cw_ref[...]
    zc = cw[0:1] * zp + cw[1:2] * z + cw[2:3] * zn + cb_ref[...]
    a = (bg_ref[...] * zc).astype(BF16)
    y = jnp.dot(a, w_ref[...], preferred_element_type=F32)
    o_ref[...] = x_ref[...] + mod_ref[0][2:3] * y


def _conv_call(proj, x, mod, conv_w, conv_b, w_out):
    r8 = TM_CONV // 8
    n8 = T // 8
    col = lambda c: pl.BlockSpec((TM_CONV, D), lambda i, c=c: (i, c))
    prev = lambda c: pl.BlockSpec((8, D), lambda i, c=c: (jnp.maximum(i * r8 - 1, 0), c))
    nxt = lambda c: pl.BlockSpec((8, D), lambda i, c=c: (jnp.minimum((i + 1) * r8, n8 - 1), c))
    return pl.pallas_call(
        _conv_kernel,
        grid=(T // TM_CONV,),
        in_specs=[
            col(0), col(1), col(2), prev(1), prev(2), nxt(1), nxt(2),
            pl.BlockSpec((TM_CONV, D), lambda i: (i, 0)),
            pl.BlockSpec((1, 6, D), lambda i: (_cond_of_row(i * TM_CONV), 0, 0)),
            pl.BlockSpec((3, D), lambda i: (0, 0)),
            pl.BlockSpec((1, D), lambda i: (0, 0)),
            pl.BlockSpec((D, D), lambda i: (0, 0)),
        ],
        out_specs=pl.BlockSpec((TM_CONV, D), lambda i: (i, 0)),
        out_shape=jax.ShapeDtypeStruct((T, D), F32),
        compiler_params=_cparams(("parallel",)),
        name="conv_mix",
    )(proj, proj, proj, proj, proj, proj, proj, x, mod, conv_w, conv_b.reshape(1, D), w_out)


def _router_kernel(x_ref, g_ref, mod_ref, r_ref, h_ref, rt_ref):
    mod = mod_ref[0]
    h = _modnorm(x_ref[...], g_ref[...], mod[4:5], mod[3:4])
    h_ref[...] = h
    logits = jnp.dot(h, r_ref[...], preferred_element_type=F32, precision=lax.Precision.HIGHEST)
    lane = lax.broadcasted_iota(jnp.int32, logits.shape, 1)
    logits = jnp.where(lane < N_EXPERTS, logits, -jnp.inf)
    m1 = jnp.max(logits, axis=-1, keepdims=True)
    i1 = jnp.min(jnp.where(logits == m1, lane, LANES), axis=-1, keepdims=True)
    rest = jnp.where(lane == i1, -jnp.inf, logits)
    m2 = jnp.max(rest, axis=-1, keepdims=True)
    i2 = jnp.min(jnp.where(rest == m2, lane, LANES), axis=-1, keepdims=True)
    e2 = jnp.exp(m2 - m1)
    w1 = 1.0 / (1.0 + e2)
    w2 = e2 / (1.0 + e2)
    rt_ref[...] = jnp.where(lane == 0, i1.astype(F32),
                            jnp.where(lane == 1, i2.astype(F32),
                                      jnp.where(lane == 2, w1, jnp.where(lane == 3, w2, 0.0))))


def _router_call(x, ln_g, mod, router_pad):
    return pl.pallas_call(
        _router_kernel,
        grid=(T // TM,),
        in_specs=[
            pl.BlockSpec((TM, D), lambda i: (i, 0)),
            pl.BlockSpec((1, D), lambda i: (0, 0)),
            pl.BlockSpec((1, 6, D), lambda i: (_cond_of_row(i * TM), 0, 0)),
            pl.BlockSpec((D, LANES), lambda i: (0, 0)),
        ],
        out_specs=[pl.BlockSpec((TM, D), lambda i: (i, 0)),
                   pl.BlockSpec((TM, LANES), lambda i: (i, 0))],
        out_shape=[jax.ShapeDtypeStruct((T, D), F32), jax.ShapeDtypeStruct((T, LANES), F32)],
        compiler_params=_cparams(("parallel",)),
        name="moe_router",
    )(x, ln_g.reshape(1, D), mod, router_pad)


R_MAX = 2 * T + N_EXPERTS * TM_MOE
N_TILES = R_MAX // TM_MOE
DMA_UNROLL = 8


def _row_copy(src_hbm, src_row, dst_vmem, dst_row, sem):
    return pltpu.make_async_copy(src_hbm.at[pl.ds(src_row, 1), :], dst_vmem.at[pl.ds(dst_row, 1), :], sem)


def _gather_kernel(nv_ref, src_ref, h_hbm, o_ref, buf, sem):
    i = pl.program_id(0)
    nv = nv_ref[0]

    def issue_tile(t):
        slot = t % 2
        base = t * TM_MOE

        def issue(r, c):
            _row_copy(h_hbm, src_ref[base + r], buf.at[slot], r, sem.at[slot]).start()
            return c

        lax.fori_loop(0, TM_MOE, issue, 0, unroll=DMA_UNROLL)

    @pl.when(jnp.logical_and(i == 0, nv > 0))
    def _():
        issue_tile(0)

    @pl.when(i + 1 < nv)
    def _():
        issue_tile(i + 1)

    @pl.when(i < nv)
    def _():
        slot = i % 2
        pltpu.make_async_copy(h_hbm.at[pl.ds(0, TM_MOE), :], buf.at[slot], sem.at[slot]).wait()
        o_ref[...] = buf[slot].astype(BF16)

    @pl.when(i >= nv)
    def _():
        o_ref[...] = jnp.zeros_like(o_ref)


def _gather_call(n_valid, src_tok, h):
    return pl.pallas_call(
        _gather_kernel,
        grid_spec=pltpu.PrefetchScalarGridSpec(
            num_scalar_prefetch=2,
            grid=(N_TILES,),
            in_specs=[pl.BlockSpec(memory_space=pl.ANY)],
            out_specs=pl.BlockSpec((TM_MOE, D), lambda i, nv, src: (i, 0)),
            scratch_shapes=[pltpu.VMEM((2, TM_MOE, D), F32), pltpu.SemaphoreType.DMA((2,))],
        ),
        out_shape=jax.ShapeDtypeStruct((R_MAX, D), BF16),
        compiler_params=_cparams(("arbitrary",)),
        name="moe_gather",
    )(n_valid, src_tok, h)


def _moe_kernel(nv_ref, te_ref, xs_ref, w1_ref, w3_ref, w2_ref, o_ref, acc_s):
    i = pl.program_id(0)
    f = pl.program_id(1)
    nf = pl.num_programs(1)
    valid = i < nv_ref[0]

    @pl.when(jnp.logical_and(valid, f == 0))
    def _():
        acc_s[...] = jnp.zeros_like(acc_s)

    @pl.when(valid)
    def _():
        xs = xs_ref[...]
        a = _silu(jnp.dot(xs, w1_ref[...], preferred_element_type=F32)) \
            * jnp.dot(xs, w3_ref[...], preferred_element_type=F32)
        acc_s[...] += jnp.dot(a.astype(BF16), w2_ref[...], preferred_element_type=F32)

    @pl.when(f == nf - 1)
    def _():
        o_ref[...] = jnp.where(valid, acc_s[...], 0.0)


def _moe_call(n_valid, tile_expert, xs, w13, w2, layer):
    nf = D_FF_EXPERT // TF_MOE

    def fe(i, f, nv):
        return jnp.where(i < nv[0], f, nf - 1)

    return pl.pallas_call(
        _moe_kernel,
        grid_spec=pltpu.PrefetchScalarGridSpec(
            num_scalar_prefetch=2,
            grid=(N_TILES, nf),
            in_specs=[
                pl.BlockSpec((TM_MOE, D), lambda i, f, nv, te: (i, 0)),
                pl.BlockSpec((None, None, D, TF_MOE),
                             lambda i, f, nv, te: (layer, te[i], 0, fe(i, f, nv))),
                pl.BlockSpec((None, None, D, TF_MOE),
                             lambda i, f, nv, te: (layer, te[i], 0, nf + fe(i, f, nv))),
                pl.BlockSpec((None, None, TF_MOE, D),
                             lambda i, f, nv, te: (layer, te[i], fe(i, f, nv), 0)),
            ],
            out_specs=pl.BlockSpec((TM_MOE, D), lambda i, f, nv, te: (i, 0)),
            scratch_shapes=[pltpu.VMEM((TM_MOE, D), F32)],
        ),
        out_shape=jax.ShapeDtypeStruct((R_MAX, D), F32),
        compiler_params=_cparams(("arbitrary", "arbitrary")),
        name="moe_experts",
    )(n_valid, tile_expert, xs, w13, w13, w2)


TM_COMB = 256


def _combine_kernel(pa_ref, pb_ref, y_hbm, x_ref, rt_ref, mod_ref, o_ref, buf_a, buf_b, sem):
    i = pl.program_id(0)

    def issue_tile(t):
        slot = t % 2
        base = t * TM_COMB

        def issue(r, c):
            _row_copy(y_hbm, pa_ref[base + r], buf_a.at[slot], r, sem.at[slot]).start()
            _row_copy(y_hbm, pb_ref[base + r], buf_b.at[slot], r, sem.at[slot]).start()
            return c

        lax.fori_loop(0, TM_COMB, issue, 0, unroll=DMA_UNROLL)

    @pl.when(i == 0)
    def _():
        issue_tile(0)

    @pl.when(i + 1 < pl.num_programs(0))
    def _():
        issue_tile(i + 1)

    slot = i % 2
    for buf in (buf_a, buf_b):
        pltpu.make_async_copy(y_hbm.at[pl.ds(0, TM_COMB), :], buf.at[slot], sem.at[slot]).wait()
    rt = rt_ref[...]
    ffn = rt[:, 2:3] * buf_a[slot] + rt[:, 3:4] * buf_b[slot]
    o_ref[...] = x_ref[...] + mod_ref[0][5:6] * ffn


def _combine_call(pos_a, pos_b, y, x, rt, mod):
    return pl.pallas_call(
        _combine_kernel,
        grid_spec=pltpu.PrefetchScalarGridSpec(
            num_scalar_prefetch=2,
            grid=(T // TM_COMB,),
            in_specs=[
                pl.BlockSpec(memory_space=pl.ANY),
                pl.BlockSpec((TM_COMB, D), lambda i, pa, pb: (i, 0)),
                pl.BlockSpec((TM_COMB, LANES), lambda i, pa, pb: (i, 0)),
                pl.BlockSpec((1, 6, D), lambda i, pa, pb: (_cond_of_row(i * TM_COMB), 0, 0)),
            ],
            out_specs=pl.BlockSpec((TM_COMB, D), lambda i, pa, pb: (i, 0)),
            scratch_shapes=[pltpu.VMEM((2, TM_COMB, D), F32), pltpu.VMEM((2, TM_COMB, D), F32),
                            pltpu.SemaphoreType.DMA((2,))],
        ),
        out_shape=jax.ShapeDtypeStruct((T, D), F32),
        compiler_params=_cparams(("arbitrary",)),
        name="moe_combine",
    )(pos_a, pos_b, y, x, rt, mod)


def _routing_tables(rt):
    ea = rt[:, 0:2].astype(jnp.int32).reshape(-1)
    onehot = (ea[:, None] == jnp.arange(N_EXPERTS, dtype=jnp.int32)[None, :]).astype(jnp.int32)
    csum = jnp.cumsum(onehot, axis=0)
    rank = jnp.take_along_axis(csum, ea[:, None], axis=1)[:, 0] - 1
    counts = csum[-1]
    padded = ((counts + TM_MOE - 1) // TM_MOE) * TM_MOE
    ends = jnp.cumsum(padded)
    pos = (ends - padded)[ea] + rank
    n_valid = (ends[-1] // TM_MOE).astype(jnp.int32).reshape(1)
    tile_start = jnp.arange(N_TILES, dtype=jnp.int32) * TM_MOE
    expert_of_row = lambda r: jnp.sum((r[:, None] >= ends[None, :]).astype(jnp.int32), axis=1)
    tile_expert = jnp.minimum(expert_of_row(tile_start), expert_of_row(ends[-1:] - 1))
    tok = jnp.arange(2 * T, dtype=jnp.int32) // 2
    src_tok = jnp.zeros((R_MAX,), jnp.int32).at[pos].set(tok)
    pos2 = pos.reshape(T, 2).astype(jnp.int32)
    return n_valid, tile_expert, src_tok, pos2[:, 0], pos2[:, 1]


def _moe_layer(x, ln_g, mod, router, w13, w2, layer):
    router_pad = jnp.pad(router, ((0, 0), (0, LANES - N_EXPERTS)))
    h, rt = _router_call(x, ln_g, mod, router_pad)
    n_valid, tile_expert, src_tok, pos_a, pos_b = _routing_tables(rt)
    xs = _gather_call(n_valid, src_tok, h)
    y = _moe_call(n_valid, tile_expert, xs, w13, w2, layer)
    return _combine_call(pos_a, pos_b, y, x, rt, mod)


def kernel(x_prompt, x_sample, c, cache_na_k, cache_na_v, state_ssm_re, state_ssm_im, c_ctx,
           ln1_g, ln2_g, ada_w, ada_b, na_w_qkv, na_w_o, na_q_g, na_k_g, na_rpb,
           ssm_lam_re, ssm_lam_im, ssm_log_step, ssm_b_re, ssm_b_im, ssm_c_re, ssm_c_im,
           ssm_d, ssm_w_glu, ssm_b_glu, cv_w_in, cv_conv_w, cv_conv_b, cv_w_out,
           ffn_w13, ffn_w2, moe_router, moe_w13, moe_w2):
    depth = ada_w.shape[0]
    x = jnp.concatenate([x_sample.reshape(T_LAT, D), x_prompt.reshape(T_CTX, D)], axis=0)
    conds = jnp.concatenate([c, c_ctx[None, :], jnp.zeros((N_COND - DEC_BATCH - 1, D), F32)], axis=0)
    mods = _ada_call(conds, ada_w, ada_b).reshape(depth, N_COND, 6, D)
    ffn_w13_b, ffn_w2_b = ffn_w13.astype(BF16), ffn_w2.astype(BF16)
    moe_w13_b, moe_w2_b = moe_w13.astype(BF16), moe_w2.astype(BF16)

    ks, vs, s_re, s_im = [], [], [], []
    for l in range(depth):
        kind, j = l % 3, l // 3
        mod = mods[l]
        if kind == 0:
            qkv = _normproj_call(x, ln1_g[l], mod, na_w_qkv[j].astype(BF16), shift=0, scale=1,
                                 name="qkv_proj")
            qg2 = jnp.tile(na_q_g[j], 2).reshape(1, LANES)
            kg2 = jnp.tile(na_k_g[j], 2).reshape(1, LANES)
            past = cache_na_k.shape[3]
            ctx_k = jnp.transpose(cache_na_k[:, j], (0, 2, 1, 3)).reshape(DEC_BATCH, past, D)
            ctx_v = jnp.transpose(cache_na_v[:, j], (0, 2, 1, 3)).reshape(DEC_BATCH, past, D)
            o_lat = _lat_attn_call(qkv, ctx_k, ctx_v, qg2, kg2, _bias_pairs(na_rpb[j]))
            o_ctx, k_new, v_new = _ctx_attn_call(qkv, qg2, kg2)
            ks.append(k_new)
            vs.append(v_new)
            x = _proj_res_call(o_lat, o_ctx, x, mod, na_w_o[j].astype(BF16), gate=2, name="attn_out")
        elif kind == 1:
            u = _norm_call(x, ln1_g[l], mod, shift=0, scale=1, name="s5_norm")
            bmat, cmat, a_re, a_im = _s5_params(ssm_lam_re[j], ssm_lam_im[j], ssm_log_step[j],
                                                ssm_b_re[j], ssm_b_im[j], ssm_c_re[j], ssm_c_im[j])
            y_lat, _, _ = _s5_call(u, bmat, cmat, a_re, a_im,
                                   _state_to_slabs(state_ssm_re[:, j]), _state_to_slabs(state_ssm_im[:, j]),
                                   row0=0, n_seq=DEC_BATCH, seq_len=DEC_SEQ,
                                   nb=DEC_BATCH, steps=S5_T_LAT, name="s5_lat")
            zero = jnp.zeros((2, S5_NSLAB, BATCH, S5_LANES), F32)
            y_ctx, f_re, f_im = _s5_call(u, bmat, cmat, a_re, a_im, zero, zero,
                                         row0=T_LAT, n_seq=BATCH, seq_len=SEQ, nb=8, steps=SEQ,
                                         name="s5_ctx")
            s_re.append(_slabs_to_state(f_re))
            s_im.append(_slabs_to_state(f_im))
            x = _glu_res_call(u, y_lat, y_ctx, x, mod, ssm_d[j], ssm_w_glu[j].astype(BF16), ssm_b_glu[j])
        else:
            proj = _normproj_call(x, ln1_g[l], mod, cv_w_in[j].astype(BF16), shift=0, scale=1,
                                  name="conv_in")
            x = _conv_call(proj, x, mod, cv_conv_w[j], cv_conv_b[j], cv_w_out[j].astype(BF16))
        jj = l // 2
        if l % 2 == 0:
            x = _ffn_call(x, ln2_g[l], mod, ffn_w13_b, ffn_w2_b, jj)
        else:
            x = _moe_layer(x, ln2_g[l], mod, moe_router[jj], moe_w13_b, moe_w2_b, jj)

    y_sample = x[:T_LAT].reshape(DEC_BATCH, DEC_SEQ, D)
    y_prompt = x[T_LAT:].reshape(BATCH, SEQ, D)
    return (y_prompt, y_sample, jnp.stack(ks, axis=1), jnp.stack(vs, axis=1),
            jnp.stack(s_re, axis=1), jnp.stack(s_im, axis=1))
```

```python
import functools
import math

import numpy as np
import jax
import jax.numpy as jnp
from jax import lax
from jax.experimental import pallas as pl
from jax.experimental.pallas import tpu as pltpu

F32 = jnp.float32
BF16 = jnp.bfloat16

D = 1024
BATCH, SEQ = 16, 256
DEC_BATCH, DEC_SEQ = 4, 2048
GRID_W = 64
ROWS = DEC_SEQ // GRID_W
N_HEADS, HEAD_DIM = 16, 64
WIN_ROWS, WIN_COLS = 8, 16
SSM_GROUP, SSM_GROUPS, SSM_STATE = 16, 64, 64
D_FF = 2816
N_EXPERTS = 8
D_FF_EXPERT = 3584
EPS = 1e-6
NEG_INF = -1e30
SCALE = HEAD_DIM ** -0.5

T_LAT = DEC_BATCH * DEC_SEQ
T_CTX = BATCH * SEQ
T = T_LAT + T_CTX
N_COND = 8
CTX_COND = DEC_BATCH

LANES = 128
VMEM_LIMIT = 56 * 1024 * 1024

TM = 512
TM_CONV = 256
TM_MOE = 1024
TF_MOE = 512
TF_FFN = D_FF // 2
HP = N_HEADS // 2
QB_ROWS = 4
S5_SLAB = 128
S5_NSLAB = D // S5_SLAB
S5_LANES = (S5_SLAB // SSM_GROUP) * SSM_STATE
S5_T_LAT = 512
S5_ROW_PAD = 4
S5_UNROLL = 4


def _cparams(sem):
    return pltpu.CompilerParams(dimension_semantics=sem, vmem_limit_bytes=VMEM_LIMIT)


def _cond_of_row(row):
    return jnp.minimum(row // DEC_SEQ, CTX_COND)


def _modnorm(x, g, sc, sh):
    ms = jnp.mean(x * x, axis=-1, keepdims=True)
    y = x * lax.rsqrt(ms + EPS) * g
    return y * (1.0 + sc) + sh


def _sigmoid(x):
    return 1.0 / (1.0 + jnp.exp(-x))


def _silu(x):
    return x * _sigmoid(x)


def _gelu_tanh(x):
    c = math.sqrt(2.0 / math.pi)
    return 0.5 * x * (1.0 + jnp.tanh(c * (x + 0.044715 * (x * x * x))))


def _ada_kernel(c_ref, w_ref, b_ref, o_ref):
    s = _silu(c_ref[...]).astype(BF16)
    o_ref[0] = jnp.dot(s, w_ref[0].astype(BF16), preferred_element_type=F32) + b_ref[0]


def _ada_call(conds, ada_w, ada_b):
    depth = ada_w.shape[0]
    tn = 1536
    return pl.pallas_call(
        _ada_kernel,
        grid=(depth, 6 * D // tn),
        in_specs=[
            pl.BlockSpec((N_COND, D), lambda l, n: (0, 0)),
            pl.BlockSpec((1, D, tn), lambda l, n: (l, 0, n)),
            pl.BlockSpec((1, 1, tn), lambda l, n: (l, 0, n)),
        ],
        out_specs=pl.BlockSpec((1, N_COND, tn), lambda l, n: (l, 0, n)),
        out_shape=jax.ShapeDtypeStruct((depth, N_COND, 6 * D), F32),
        compiler_params=_cparams(("parallel", "parallel")),
        name="ada_mod",
    )(conds, ada_w, ada_b.reshape(depth, 1, 6 * D))


def _normproj_kernel(x_ref, g_ref, mod_ref, w_ref, o_ref, *, shift, scale):
    mod = mod_ref[0]
    h = _modnorm(x_ref[...], g_ref[...], mod[scale:scale + 1], mod[shift:shift + 1])
    o_ref[...] = jnp.dot(h.astype(BF16), w_ref[...], preferred_element_type=F32)


def _normproj_call(x, ln_g, mod, w, *, shift, scale, name):
    n = w.shape[1]
    return pl.pallas_call(
        functools.partial(_normproj_kernel, shift=shift, scale=scale),
        grid=(T // TM,),
        in_specs=[
            pl.BlockSpec((TM, D), lambda i: (i, 0)),
            pl.BlockSpec((1, D), lambda i: (0, 0)),
            pl.BlockSpec((1, 6, D), lambda i: (_cond_of_row(i * TM), 0, 0)),
            pl.BlockSpec((D, n), lambda i: (0, 0)),
        ],
        out_specs=pl.BlockSpec((TM, n), lambda i: (i, 0)),
        out_shape=jax.ShapeDtypeStruct((T, n), F32),
        compiler_params=_cparams(("parallel",)),
        name=name,
    )(x, ln_g.reshape(1, D), mod, w)


def _norm_kernel(x_ref, g_ref, mod_ref, o_ref, *, shift, scale):
    mod = mod_ref[0]
    o_ref[...] = _modnorm(x_ref[...], g_ref[...], mod[scale:scale + 1], mod[shift:shift + 1])


def _norm_call(x, ln_g, mod, *, shift, scale, name):
    return pl.pallas_call(
        functools.partial(_norm_kernel, shift=shift, scale=scale),
        grid=(T // TM,),
        in_specs=[
            pl.BlockSpec((TM, D), lambda i: (i, 0)),
            pl.BlockSpec((1, D), lambda i: (0, 0)),
            pl.BlockSpec((1, 6, D), lambda i: (_cond_of_row(i * TM), 0, 0)),
        ],
        out_specs=pl.BlockSpec((TM, D), lambda i: (i, 0)),
        out_shape=jax.ShapeDtypeStruct((T, D), F32),
        compiler_params=_cparams(("parallel",)),
        name=name,
    )(x, ln_g.reshape(1, D), mod)


LAT_TILES = T_LAT // TM


def _lat_tile(i):
    return jnp.minimum(i, LAT_TILES - 1)


def _ctx_tile(i):
    return jnp.maximum(i - LAT_TILES, 0)


def _proj_res_kernel(al_ref, ac_ref, x_ref, mod_ref, w_ref, o_ref, *, gate):
    a = jnp.where(pl.program_id(0) < LAT_TILES, al_ref[...], ac_ref[...])
    y = jnp.dot(a.astype(BF16), w_ref[...], preferred_element_type=F32)
    o_ref[...] = x_ref[...] + mod_ref[0][gate:gate + 1] * y


def _proj_res_call(a_lat, a_ctx, x, mod, w, *, gate, name):
    return pl.pallas_call(
        functools.partial(_proj_res_kernel, gate=gate),
        grid=(T // TM,),
        in_specs=[
            pl.BlockSpec((TM, D), lambda i: (_lat_tile(i), 0)),
            pl.BlockSpec((TM, D), lambda i: (_ctx_tile(i), 0)),
            pl.BlockSpec((TM, D), lambda i: (i, 0)),
            pl.BlockSpec((1, 6, D), lambda i: (_cond_of_row(i * TM), 0, 0)),
            pl.BlockSpec((D, D), lambda i: (0, 0)),
        ],
        out_specs=pl.BlockSpec((TM, D), lambda i: (i, 0)),
        out_shape=jax.ShapeDtypeStruct((T, D), F32),
        compiler_params=_cparams(("parallel",)),
        name=name,
    )(a_lat, a_ctx, x, mod, w)


def _head_norm(x, g, head0):
    x2 = x * x
    s0 = jnp.sum(jnp.where(head0, x2, 0.0), axis=-1, keepdims=True)
    s1 = jnp.sum(jnp.where(head0, 0.0, x2), axis=-1, keepdims=True)
    ms = jnp.where(head0, s0, s1) * (1.0 / HEAD_DIM)
    return x * lax.rsqrt(ms + EPS) * g


def _nt_dot(a, b):
    return lax.dot_general(a, b, (((1,), (1,)), ((), ())), preferred_element_type=F32)


def _ctx_attn_kernel(q_ref, k_ref, v_ref, qg_ref, kg_ref, o_ref, ko_ref, vo_ref):
    head0 = lax.broadcasted_iota(jnp.int32, (SEQ, LANES), 1) < HEAD_DIM
    for hp in range(HP):
        cols = slice(hp * LANES, (hp + 1) * LANES)
        qn = _head_norm(q_ref[:, cols], qg_ref[...], head0)
        kn = _head_norm(k_ref[:, cols], kg_ref[...], head0)
        v = v_ref[:, cols]
        ko_ref[0, 2 * hp] = kn[:, :HEAD_DIM]
        ko_ref[0, 2 * hp + 1] = kn[:, HEAD_DIM:]
        vo_ref[0, 2 * hp] = v[:, :HEAD_DIM]
        vo_ref[0, 2 * hp + 1] = v[:, HEAD_DIM:]
        knb = kn.astype(BF16)
        vb = v.astype(BF16)
        outs = []
        for hh in range(2):
            qm = jnp.where(head0 if hh == 0 else jnp.logical_not(head0), qn * SCALE, 0.0).astype(BF16)
            s = _nt_dot(qm, knb)
            p = jnp.exp(s - jnp.max(s, axis=-1, keepdims=True))
            l = jnp.sum(p, axis=-1, keepdims=True)
            outs.append(jnp.dot(p.astype(BF16), vb, preferred_element_type=F32) / l)
        o_ref[:, cols] = jnp.where(head0, outs[0], outs[1])


def _ctx_attn_call(qkv, qg2, kg2):
    row0 = T_LAT // SEQ
    blk = lambda c: pl.BlockSpec((SEQ, D), lambda b, c=c: (row0 + b, c))
    kv_out = pl.BlockSpec((1, N_HEADS, SEQ, HEAD_DIM), lambda b: (b, 0, 0, 0))
    return pl.pallas_call(
        _ctx_attn_kernel,
        grid=(BATCH,),
        in_specs=[blk(0), blk(1), blk(2),
                  pl.BlockSpec((1, LANES), lambda b: (0, 0)),
                  pl.BlockSpec((1, LANES), lambda b: (0, 0))],
        out_specs=[pl.BlockSpec((SEQ, D), lambda b: (b, 0)), kv_out, kv_out],
        out_shape=[jax.ShapeDtypeStruct((T_CTX, D), F32),
                   jax.ShapeDtypeStruct((BATCH, N_HEADS, SEQ, HEAD_DIM), F32),
                   jax.ShapeDtypeStruct((BATCH, N_HEADS, SEQ, HEAD_DIM), F32)],
        compiler_params=_cparams(("parallel",)),
        name="ctx_attn",
    )(qkv, qkv, qkv, qg2, kg2)


def _band_of_block(p):
    r_lo, r_hi = QB_ROWS * p, QB_ROWS * p + QB_ROWS - 1
    kr = min(WIN_ROWS, ROWS)
    lo = min(max(r_lo - kr // 2, 0), ROWS - kr)
    hi = min(max(r_hi - kr // 2, 0), ROWS - kr) + kr
    lo -= lo % 2
    hi += hi % 2
    return lo, hi - lo


def _lat_attn_kernel(q_ref, k_ref, v_ref, ck_ref, cv_ref, qg_ref, kg_ref, bp_ref, o_ref,
                     qm0_s, qm1_s, kn_s, v_s):
    head0 = lax.broadcasted_iota(jnp.int32, (DEC_SEQ, LANES), 1) < HEAD_DIM
    qn = _head_norm(q_ref[...], qg_ref[...], head0)
    qm0_s[...] = jnp.where(head0, qn * SCALE, 0.0).astype(BF16)
    qm1_s[...] = jnp.where(head0, 0.0, qn * SCALE).astype(BF16)
    kn_s[...] = _head_norm(k_ref[...], kg_ref[...], head0).astype(BF16)
    v_s[...] = v_ref[...].astype(BF16)
    ckb = ck_ref[0].astype(BF16)
    cvb = cv_ref[0].astype(BF16)

    qc = lax.broadcasted_iota(jnp.int32, (GRID_W, LANES), 0)
    kl = lax.broadcasted_iota(jnp.int32, (GRID_W, LANES), 1)
    kc = jnp.where(kl < GRID_W, kl, kl - GRID_W)
    cs = jnp.clip(qc - WIN_COLS // 2, 0, GRID_W - WIN_COLS)
    col_ok = jnp.logical_and(kc >= cs, kc < cs + WIN_COLS)
    left = kl < GRID_W
    mask_of = {
        (True, True): col_ok,
        (True, False): jnp.logical_and(col_ok, left),
        (False, True): jnp.logical_and(col_ok, jnp.logical_not(left)),
    }
    h0q = lax.broadcasted_iota(jnp.int32, (QB_ROWS * GRID_W, LANES), 1) < HEAD_DIM
    kr_win = min(WIN_ROWS, ROWS)

    for p in range(ROWS // QB_ROWS):
        u0, nrows = _band_of_block(p)
        q_lo = p * QB_ROWS * GRID_W
        kb = kn_s[u0 * GRID_W:(u0 + nrows) * GRID_W, :]
        vb = v_s[u0 * GRID_W:(u0 + nrows) * GRID_W, :]
        outs = []
        for hh in range(2):
            qm = (qm0_s if hh == 0 else qm1_s)[q_lo:q_lo + QB_ROWS * GRID_W, :]
            s_loc = _nt_dot(qm, kb)
            s_ctx = _nt_dot(qm, ckb)
            p_rows, pc_rows, l_rows = [], [], []
            for i in range(QB_ROWS):
                r = p * QB_ROWS + i
                r0 = min(max(r - kr_win // 2, 0), ROWS - kr_win)
                rows = slice(i * GRID_W, (i + 1) * GRID_W)
                sbs = {}
                for m in range(nrows // 2):
                    kr = u0 + 2 * m
                    ok_l = r0 <= kr < r0 + kr_win
                    ok_r = r0 <= kr + 1 < r0 + kr_win
                    if ok_l or ok_r:
                        sb = s_loc[rows, m * LANES:(m + 1) * LANES] + bp_ref[hh, kr - r + WIN_ROWS]
                        sbs[m] = jnp.where(mask_of[(ok_l, ok_r)], sb, NEG_INF)
                sc = s_ctx[rows, :]
                mx = jnp.maximum(
                    jnp.max(functools.reduce(jnp.maximum, sbs.values()), axis=-1, keepdims=True),
                    jnp.max(sc, axis=-1, keepdims=True))
                pbs = {m: jnp.exp(sb - mx) for m, sb in sbs.items()}
                pc = jnp.exp(sc - mx)
                l_rows.append(jnp.sum(functools.reduce(jnp.add, pbs.values()), axis=-1, keepdims=True)
                              + jnp.sum(pc, axis=-1, keepdims=True))
                zero = jnp.zeros((GRID_W, LANES), BF16)
                p_rows.append(jnp.concatenate(
                    [pbs[m].astype(BF16) if m in pbs else zero for m in range(nrows // 2)], axis=1))
                pc_rows.append(pc.astype(BF16))
            o = (jnp.dot(jnp.concatenate(p_rows, axis=0), vb, preferred_element_type=F32)
                 + jnp.dot(jnp.concatenate(pc_rows, axis=0), cvb, preferred_element_type=F32))
            outs.append(o / jnp.concatenate(l_rows, axis=0))
        o_ref[q_lo:q_lo + QB_ROWS * GRID_W, :] = jnp.where(h0q, outs[0], outs[1])


def _lat_attn_call(qkv, ctx_k, ctx_v, qg2, kg2, bias_pairs):
    blk = lambda c: pl.BlockSpec((DEC_SEQ, LANES), lambda hp, b, c=c: (b, c * HP + hp))
    cblk = pl.BlockSpec((1, ctx_k.shape[1], LANES), lambda hp, b: (b, 0, hp))
    return pl.pallas_call(
        _lat_attn_kernel,
        grid=(HP, DEC_BATCH),
        in_specs=[blk(0), blk(1), blk(2), cblk, cblk,
                  pl.BlockSpec((1, LANES), lambda hp, b: (0, 0)),
                  pl.BlockSpec((1, LANES), lambda hp, b: (0, 0)),
                  pl.BlockSpec((2, 2 * WIN_ROWS, GRID_W, LANES), lambda hp, b: (hp, 0, 0, 0))],
        out_specs=pl.BlockSpec((DEC_SEQ, LANES), lambda hp, b: (b, hp)),
        out_shape=jax.ShapeDtypeStruct((T_LAT, D), F32),
        scratch_shapes=[pltpu.VMEM((DEC_SEQ, LANES), BF16)] * 4,
        compiler_params=_cparams(("parallel", "parallel")),
        name="lat_attn",
    )(qkv, qkv, qkv, ctx_k, ctx_v, qg2, kg2, bias_pairs)


def _bias_pairs(rpb):
    cols = np.arange(GRID_W)
    col_idx = np.clip(cols[None, :] - cols[:, None], -(WIN_COLS - 1), WIN_COLS - 1) + WIN_COLS - 1
    rc = rpb[:, :, col_idx]
    left = jnp.pad(rc, ((0, 0), (1, 0), (0, 0), (0, 0)))
    right = jnp.pad(rc, ((0, 0), (0, 1), (0, 0), (0, 0)))
    return jnp.concatenate([left, right], axis=-1)


def _ffn_kernel(x_ref, g_ref, mod_ref, w1_ref, w3_ref, w2_ref, o_ref, h_s, acc_s):
    f = pl.program_id(1)
    mod = mod_ref[0]

    @pl.when(f == 0)
    def _():
        h_s[...] = _modnorm(x_ref[...], g_ref[...], mod[4:5], mod[3:4]).astype(BF16)
        acc_s[...] = jnp.zeros_like(acc_s)

    h = h_s[...]
    a = _silu(jnp.dot(h, w1_ref[...], preferred_element_type=F32)) \
        * jnp.dot(h, w3_ref[...], preferred_element_type=F32)
    acc_s[...] += jnp.dot(a.astype(BF16), w2_ref[...], preferred_element_type=F32)

    @pl.when(f == pl.num_programs(1) - 1)
    def _():
        o_ref[...] = x_ref[...] + mod[5:6] * acc_s[...]


def _ffn_call(x, ln_g, mod, w13, w2, layer):
    nf = D_FF // TF_FFN
    return pl.pallas_call(
        _ffn_kernel,
        grid=(T // TM, nf),
        in_specs=[
            pl.BlockSpec((TM, D), lambda i, f: (i, 0)),
            pl.BlockSpec((1, D), lambda i, f: (0, 0)),
            pl.BlockSpec((1, 6, D), lambda i, f: (_cond_of_row(i * TM), 0, 0)),
            pl.BlockSpec((None, D, TF_FFN), lambda i, f: (layer, 0, f)),
            pl.BlockSpec((None, D, TF_FFN), lambda i, f: (layer, 0, nf + f)),
            pl.BlockSpec((None, TF_FFN, D), lambda i, f: (layer, f, 0)),
        ],
        out_specs=pl.BlockSpec((TM, D), lambda i, f: (i, 0)),
        out_shape=jax.ShapeDtypeStruct((T, D), F32),
        scratch_shapes=[pltpu.VMEM((TM, D), BF16), pltpu.VMEM((TM, D), F32)],
        compiler_params=_cparams(("parallel", "arbitrary")),
        name="ffn_dense",
    )(x, ln_g.reshape(1, D), mod, w13, w13, w2)


def _s5_kernel(u_ref, bm_ref, cm_ref, ar_ref, ai_ref, hr_ref, hi_ref,
               y_ref, fr_ref, fi_ref, bu_s, st_s, cr_s, ci_s, *, nb, steps):
    d = pl.program_id(0)
    tb = pl.program_id(3)
    rows = nb * steps
    pitch = steps + S5_ROW_PAD
    nch = S5_LANES // LANES
    u = u_ref[...].reshape(rows, S5_SLAB).astype(BF16)
    bu = jnp.dot(u, bm_ref[0, 0], preferred_element_type=F32)
    for c in range(2 * nch):
        for k in range(nb):
            bu_s[c, k * pitch:k * pitch + steps, :] = bu[k * steps:(k + 1) * steps, c * LANES:(c + 1) * LANES]

    @pl.when(tb == 0)
    def _():
        cr_s[...] = hr_ref[0, 0]
        ci_s[...] = hi_ref[0, 0]

    ar = [jnp.broadcast_to(ar_ref[0, 0][:, c * LANES:(c + 1) * LANES], (nb, LANES)) for c in range(nch)]
    ai = [jnp.broadcast_to(ai_ref[0, 0][:, c * LANES:(c + 1) * LANES], (nb, LANES)) for c in range(nch)]

    def step(i, carry):
        l = jnp.where(d == 0, i, steps - 1 - i)
        idx = pl.ds(l, nb, stride=pitch)
        out = []
        for c in range(nch):
            sr, si = carry[2 * c], carry[2 * c + 1]
            nr = ar[c] * sr - ai[c] * si + bu_s[c, idx, :]
            ni = ar[c] * si + ai[c] * sr + bu_s[nch + c, idx, :]
            st_s[c, idx, :] = nr
            st_s[nch + c, idx, :] = ni
            out += [nr, ni]
        return tuple(out)

    init = []
    for c in range(nch):
        init += [cr_s[:, c * LANES:(c + 1) * LANES], ci_s[:, c * LANES:(c + 1) * LANES]]
    fin = lax.fori_loop(0, steps, step, tuple(init), unroll=S5_UNROLL)
    sr = jnp.concatenate([fin[2 * c] for c in range(nch)], axis=1)
    si = jnp.concatenate([fin[2 * c + 1] for c in range(nch)], axis=1)
    cr_s[...] = sr
    ci_s[...] = si
    fr_ref[0, 0] = sr
    fi_ref[0, 0] = si
    states = jnp.concatenate(
        [jnp.concatenate([st_s[c, k * pitch:k * pitch + steps, :] for k in range(nb)], axis=0)
         for c in range(2 * nch)], axis=1).astype(BF16)
    y = jnp.dot(states, cm_ref[0, 0], preferred_element_type=F32)
    y_ref[0] = y.reshape(nb, steps, S5_SLAB)


def _s5_call(u, bmat, cmat, a_re, a_im, h_re, h_im, *, row0, n_seq, seq_len, nb, steps, name):
    n_sg, n_tb = n_seq // nb, seq_len // steps
    sg0 = row0 // (nb * seq_len)

    def tbi(d, tb):
        return jnp.where(d == 0, tb, n_tb - 1 - tb)

    par = lambda last: pl.BlockSpec((1, 1) + last, lambda d, j, sg, tb: (d, j, 0, 0))
    st = pl.BlockSpec((1, 1, nb, S5_LANES), lambda d, j, sg, tb: (d, j, sg, 0))
    y, f_re, f_im = pl.pallas_call(
        functools.partial(_s5_kernel, nb=nb, steps=steps),
        grid=(2, S5_NSLAB, n_sg, n_tb),
        in_specs=[
            pl.BlockSpec((nb, steps, S5_SLAB), lambda d, j, sg, tb: (sg0 + sg, tbi(d, tb), j)),
            par((S5_SLAB, 2 * S5_LANES)),
            par((2 * S5_LANES, S5_SLAB)),
            par((1, S5_LANES)), par((1, S5_LANES)),
            st, st,
        ],
        out_specs=[
            pl.BlockSpec((1, nb, steps, S5_SLAB), lambda d, j, sg, tb: (d, sg, tbi(d, tb), j)),
            st, st,
        ],
        out_shape=[
            jax.ShapeDtypeStruct((2, n_seq, seq_len, D), F32),
            jax.ShapeDtypeStruct((2, S5_NSLAB, n_seq, S5_LANES), F32),
            jax.ShapeDtypeStruct((2, S5_NSLAB, n_seq, S5_LANES), F32),
        ],
        scratch_shapes=[pltpu.VMEM((2 * S5_LANES // LANES, nb * (steps + S5_ROW_PAD), LANES), F32)] * 2
                       + [pltpu.VMEM((nb, S5_LANES), F32)] * 2,
        compiler_params=_cparams(("parallel", "parallel", "parallel", "arbitrary")),
        name=name,
    )(u.reshape(T // seq_len, seq_len, D), bmat, cmat, a_re, a_im, h_re, h_im)
    return y.reshape(2, n_seq * seq_len, D), f_re, f_im


def _s5_params(lam_re, lam_im, log_step, b_re, b_im, c_re, c_im):
    step = jnp.exp(log_step)[..., None]
    zr, zi = lam_re * step, lam_im * step
    mag = jnp.exp(zr)
    a_re, a_im = mag * jnp.cos(zi), mag * jnp.sin(zi)
    nr, ni = a_re - 1.0, a_im
    den = lam_re * lam_re + lam_im * lam_im
    k_re = (nr * lam_re + ni * lam_im) / den
    k_im = (ni * lam_re - nr * lam_im) / den
    bb_re = k_re[..., None] * b_re - k_im[..., None] * b_im
    bb_im = k_re[..., None] * b_im + k_im[..., None] * b_re
    gl = S5_SLAB // SSM_GROUP
    eye = jnp.eye(gl, dtype=F32)

    def bdiag_in(w):
        w = w.reshape(2, S5_NSLAB, gl, SSM_STATE, SSM_GROUP)
        return jnp.einsum('dsgnp,gh->dsgphn', w, eye).reshape(2, S5_NSLAB, S5_SLAB, S5_LANES)

    def bdiag_out(w):
        w = w.reshape(2, S5_NSLAB, gl, SSM_GROUP, SSM_STATE)
        return jnp.einsum('dsgpn,gh->dsgnhp', w, eye).reshape(2, S5_NSLAB, S5_LANES, S5_SLAB)

    bmat = jnp.concatenate([bdiag_in(bb_re), bdiag_in(bb_im)], axis=-1).astype(BF16)
    cmat = jnp.concatenate([bdiag_out(c_re), -bdiag_out(c_im)], axis=-2).astype(BF16)
    slab = lambda a: a.reshape(2, S5_NSLAB, 1, S5_LANES)
    return bmat, cmat, slab(a_re), slab(a_im)


def _state_to_slabs(h):
    b = h.shape[0]
    return jnp.transpose(h.reshape(b, 2, S5_NSLAB, S5_LANES), (1, 2, 0, 3))


def _slabs_to_state(f):
    b = f.shape[2]
    return jnp.transpose(f, (2, 0, 1, 3)).reshape(b, 2, SSM_GROUPS, SSM_STATE)


def _glu_res_kernel(u_ref, yl_ref, yc_ref, x_ref, mod_ref, d_ref, w_ref, b_ref, o_ref):
    is_lat = pl.program_id(0) < LAT_TILES
    y = jnp.where(is_lat, yl_ref[0] + yl_ref[1], yc_ref[0] + yc_ref[1])
    yt = u_ref[...] * d_ref[...] + y
    z = _gelu_tanh(yt)
    gate = _sigmoid(jnp.dot(z.astype(BF16), w_ref[...], preferred_element_type=F32) + b_ref[...])
    o_ref[...] = x_ref[...] + mod_ref[0][2:3] * (z * gate)


def _glu_res_call(u, y_lat, y_ctx, x, mod, d_skip, w_glu, b_glu):
    return pl.pallas_call(
        _glu_res_kernel,
        grid=(T // TM,),
        in_specs=[
            pl.BlockSpec((TM, D), lambda i: (i, 0)),
            pl.BlockSpec((2, TM, D), lambda i: (0, _lat_tile(i), 0)),
            pl.BlockSpec((2, TM, D), lambda i: (0, _ctx_tile(i), 0)),
            pl.BlockSpec((TM, D), lambda i: (i, 0)),
            pl.BlockSpec((1, 6, D), lambda i: (_cond_of_row(i * TM), 0, 0)),
            pl.BlockSpec((1, D), lambda i: (0, 0)),
            pl.BlockSpec((D, D), lambda i: (0, 0)),
            pl.BlockSpec((1, D), lambda i: (0, 0)),
        ],
        out_specs=pl.BlockSpec((TM, D), lambda i: (i, 0)),
        out_shape=jax.ShapeDtypeStruct((T, D), F32),
        compiler_params=_cparams(("parallel",)),
        name="s5_glu_res",
    )(u, y_lat, y_ctx, x, mod, d_skip.reshape(1, D), w_glu, b_glu.reshape(1, D))


def _conv_kernel(bg_ref, cg_ref, xi_ref, cgp_ref, xip_ref, cgn_ref, xin_ref,
                 x_ref, mod_ref, cw_ref, cb_ref, w_ref, o_ref):
    i = pl.program_id(0)
    tiles_per_seq = DEC_SEQ // TM_CONV
    is_lat = i < T_LAT // TM_CONV
    first = jnp.logical_or(jnp.logical_not(is_lat), i % tiles_per_seq == 0)
    last = jnp.logical_or(jnp.logical_not(is_lat), i % tiles_per_seq == tiles_per_seq - 1)
    z = cg_ref[...] * xi_ref[...]
    zp_row = jnp.where(first, 0.0, cgp_ref[7:8, :] * xip_ref[7:8, :])
    zn_row = jnp.where(last, 0.0, cgn_ref[0:1, :] * xin_ref[0:1, :])
    row = lax.broadcasted_iota(jnp.int32, (TM_CONV, D), 0)
    zp = jnp.where(row == 0, zp_row, pltpu.roll(z, 1, axis=0))
    zn = jnp.where(row == TM_CONV - 1, zn_row, pltpu.roll(z, TM_CONV - 1, axis=0))
    cw = cw_ref[...]
    zc = cw[0:1] * zp + cw[1:2] * z + cw[2:3] * zn + cb_ref[...]
    a = (bg_ref[...] * zc).astype(BF16)
    y = jnp.dot(a, w_ref[...], preferred_element_type=F32)
    o_ref[...] = x_ref[...] + mod_ref[0][2:3] * y


def _conv_call(proj, x, mod, conv_w, conv_b, w_out):
    r8 = TM_CONV // 8
    n8 = T // 8
    col = lambda c: pl.BlockSpec((TM_CONV, D), lambda i, c=c: (i, c))
    prev = lambda c: pl.BlockSpec((8, D), lambda i, c=c: (jnp.maximum(i * r8 - 1, 0), c))
    nxt = lambda c: pl.BlockSpec((8, D), lambda i, c=c: (jnp.minimum((i + 1) * r8, n8 - 1), c))
    return pl.pallas_call(
        _conv_kernel,
        grid=(T // TM_CONV,),
        in_specs=[
            col(0), col(1), col(2), prev(1), prev(2), nxt(1), nxt(2),
            pl.BlockSpec((TM_CONV, D), lambda i: (i, 0)),
            pl.BlockSpec((1, 6, D), lambda i: (_cond_of_row(i * TM_CONV), 0, 0)),
            pl.BlockSpec((3, D), lambda i: (0, 0)),
            pl.BlockSpec((1, D), lambda i: (0, 0)),
            pl.BlockSpec((D, D), lambda i: (0, 0)),
        ],
        out_specs=pl.BlockSpec((TM_CONV, D), lambda i: (i, 0)),
        out_shape=jax.ShapeDtypeStruct((T, D), F32),
        compiler_params=_cparams(("parallel",)),
        name="conv_mix",
    )(proj, proj, proj, proj, proj, proj, proj, x, mod, conv_w, conv_b.reshape(1, D), w_out)


def _router_kernel(x_ref, g_ref, mod_ref, r_ref, h_ref, rt_ref):
    mod = mod_ref[0]
    h = _modnorm(x_ref[...], g_ref[...], mod[4:5], mod[3:4])
    h_ref[...] = h
    logits = jnp.dot(h, r_ref[...], preferred_element_type=F32, precision=lax.Precision.HIGHEST)
    lane = lax.broadcasted_iota(jnp.int32, logits.shape, 1)
    logits = jnp.where(lane < N_EXPERTS, logits, -jnp.inf)
    m1 = jnp.max(logits, axis=-1, keepdims=True)
    i1 = jnp.min(jnp.where(logits == m1, lane, LANES), axis=-1, keepdims=True)
    rest = jnp.where(lane == i1, -jnp.inf, logits)
    m2 = jnp.max(rest, axis=-1, keepdims=True)
    i2 = jnp.min(jnp.where(rest == m2, lane, LANES), axis=-1, keepdims=True)
    e2 = jnp.exp(m2 - m1)
    w1 = 1.0 / (1.0 + e2)
    w2 = e2 / (1.0 + e2)
    rt_ref[...] = jnp.where(lane == 0, i1.astype(F32),
                            jnp.where(lane == 1, i2.astype(F32),
                                      jnp.where(lane == 2, w1, jnp.where(lane == 3, w2, 0.0))))


def _router_call(x, ln_g, mod, router_pad):
    return pl.pallas_call(
        _router_kernel,
        grid=(T // TM,),
        in_specs=[
            pl.BlockSpec((TM, D), lambda i: (i, 0)),
            pl.BlockSpec((1, D), lambda i: (0, 0)),
            pl.BlockSpec((1, 6, D), lambda i: (_cond_of_row(i * TM), 0, 0)),
            pl.BlockSpec((D, LANES), lambda i: (0, 0)),
        ],
        out_specs=[pl.BlockSpec((TM, D), lambda i: (i, 0)),
                   pl.BlockSpec((TM, LANES), lambda i: (i, 0))],
        out_shape=[jax.ShapeDtypeStruct((T, D), F32), jax.ShapeDtypeStruct((T, LANES), F32)],
        compiler_params=_cparams(("parallel",)),
        name="moe_router",
    )(x, ln_g.reshape(1, D), mod, router_pad)


R_MAX = 2 * T + N_EXPERTS * TM_MOE
N_TILES = R_MAX // TM_MOE
DMA_UNROLL = 8


def _row_copy(src_hbm, src_row, dst_vmem, dst_row, sem):
    return pltpu.make_async_copy(src_hbm.at[pl.ds(src_row, 1), :], dst_vmem.at[pl.ds(dst_row, 1), :], sem)


DISPATCH_CHUNK = 1024


def _dispatch_kernel(pos_ref, h_hbm, xs_init_hbm, xs_hbm, sem):
    del xs_init_hbm
    c = pl.program_id(0)
    n_chunks = pl.num_programs(0)

    def wait_chunk(slot):
        pltpu.make_async_copy(h_hbm.at[pl.ds(0, DISPATCH_CHUNK), :],
                              xs_hbm.at[pl.ds(0, DISPATCH_CHUNK), :], sem.at[slot]).wait()

    base = c * DISPATCH_CHUNK
    slot = c % 2

    def issue(r, carry):
        a = base + r
        pltpu.make_async_copy(h_hbm.at[pl.ds(lax.shift_right_logical(a, 1), 1), :],
                              xs_hbm.at[pl.ds(pos_ref[a], 1), :], sem.at[slot]).start()
        return carry

    lax.fori_loop(0, DISPATCH_CHUNK, issue, 0, unroll=DMA_UNROLL)

    @pl.when(c > 0)
    def _():
        wait_chunk(1 - slot)

    @pl.when(c == n_chunks - 1)
    def _():
        wait_chunk(slot)


def _dispatch_call(pos, h):
    xs_init = jnp.zeros((R_MAX, D), F32)
    return pl.pallas_call(
        _dispatch_kernel,
        grid_spec=pltpu.PrefetchScalarGridSpec(
            num_scalar_prefetch=1,
            grid=(2 * T // DISPATCH_CHUNK,),
            in_specs=[pl.BlockSpec(memory_space=pl.ANY), pl.BlockSpec(memory_space=pl.ANY)],
            out_specs=pl.BlockSpec(memory_space=pl.ANY),
            scratch_shapes=[pltpu.SemaphoreType.DMA((2,))],
        ),
        out_shape=jax.ShapeDtypeStruct((R_MAX, D), F32),
        input_output_aliases={2: 0},
        compiler_params=_cparams(("arbitrary",)),
        name="moe_dispatch",
    )(pos, h, xs_init)


def _moe_kernel(nv_ref, te_ref, xs_ref, w1_ref, w3_ref, w2_ref, o_ref, xs_s, acc_s):
    i = pl.program_id(0)
    f = pl.program_id(1)
    nf = pl.num_programs(1)
    valid = i < nv_ref[0]

    @pl.when(jnp.logical_and(valid, f == 0))
    def _():
        xs_s[...] = xs_ref[...].astype(BF16)
        acc_s[...] = jnp.zeros_like(acc_s)

    @pl.when(valid)
    def _():
        xs = xs_s[...]
        a = _silu(jnp.dot(xs, w1_ref[...].astype(BF16), preferred_element_type=F32)) \
            * jnp.dot(xs, w3_ref[...].astype(BF16), preferred_element_type=F32)
        acc_s[...] += jnp.dot(a.astype(BF16), w2_ref[...].astype(BF16), preferred_element_type=F32)

    @pl.when(f == nf - 1)
    def _():
        o_ref[...] = jnp.where(valid, acc_s[...], 0.0)


def _moe_call(n_valid, tile_expert, xs, w13, w2, layer):
    nf = D_FF_EXPERT // TF_MOE

    def fe(i, f, nv):
        return jnp.where(i < nv[0], f, nf - 1)

    return pl.pallas_call(
        _moe_kernel,
        grid_spec=pltpu.PrefetchScalarGridSpec(
            num_scalar_prefetch=2,
            grid=(N_TILES, nf),
            in_specs=[
                pl.BlockSpec((TM_MOE, D), lambda i, f, nv, te: (i, 0)),
                pl.BlockSpec((None, None, D, TF_MOE),
                             lambda i, f, nv, te: (layer, te[i], 0, fe(i, f, nv))),
                pl.BlockSpec((None, None, D, TF_MOE),
                             lambda i, f, nv, te: (layer, te[i], 0, nf + fe(i, f, nv))),
                pl.BlockSpec((None, None, TF_MOE, D),
                             lambda i, f, nv, te: (layer, te[i], fe(i, f, nv), 0)),
            ],
            out_specs=pl.BlockSpec((TM_MOE, D), lambda i, f, nv, te: (i, 0)),
            scratch_shapes=[pltpu.VMEM((TM_MOE, D), BF16), pltpu.VMEM((TM_MOE, D), F32)],
        ),
        out_shape=jax.ShapeDtypeStruct((R_MAX, D), F32),
        compiler_params=_cparams(("arbitrary", "arbitrary")),
        name="moe_experts",
    )(n_valid, tile_expert, xs, w13, w13, w2)


TM_COMB = 256


def _combine_kernel(pa_ref, pb_ref, y_hbm, x_ref, rt_ref, mod_ref, o_ref, buf_a, buf_b, sem):
    i = pl.program_id(0)

    def issue_tile(t):
        slot = t % 2
        base = t * TM_COMB

        def issue(r, c):
            _row_copy(y_hbm, pa_ref[base + r], buf_a.at[slot], r, sem.at[slot]).start()
            _row_copy(y_hbm, pb_ref[base + r], buf_b.at[slot], r, sem.at[slot]).start()
            return c

        lax.fori_loop(0, TM_COMB, issue, 0, unroll=DMA_UNROLL)

    @pl.when(i == 0)
    def _():
        issue_tile(0)

    @pl.when(i + 1 < pl.num_programs(0))
    def _():
        issue_tile(i + 1)

    slot = i % 2
    for buf in (buf_a, buf_b):
        pltpu.make_async_copy(y_hbm.at[pl.ds(0, TM_COMB), :], buf.at[slot], sem.at[slot]).wait()
    rt = rt_ref[...]
    ffn = rt[:, 2:3] * buf_a[slot] + rt[:, 3:4] * buf_b[slot]
    o_ref[...] = x_ref[...] + mod_ref[0][5:6] * ffn


def _combine_call(pos_a, pos_b, y, x, rt, mod):
    return pl.pallas_call(
        _combine_kernel,
        grid_spec=pltpu.PrefetchScalarGridSpec(
            num_scalar_prefetch=2,
            grid=(T // TM_COMB,),
            in_specs=[
                pl.BlockSpec(memory_space=pl.ANY),
                pl.BlockSpec((TM_COMB, D), lambda i, pa, pb: (i, 0)),
                pl.BlockSpec((TM_COMB, LANES), lambda i, pa, pb: (i, 0)),
                pl.BlockSpec((1, 6, D), lambda i, pa, pb: (_cond_of_row(i * TM_COMB), 0, 0)),
            ],
            out_specs=pl.BlockSpec((TM_COMB, D), lambda i, pa, pb: (i, 0)),
            scratch_shapes=[pltpu.VMEM((2, TM_COMB, D), F32), pltpu.VMEM((2, TM_COMB, D), F32),
                            pltpu.SemaphoreType.DMA((2,))],
        ),
        out_shape=jax.ShapeDtypeStruct((T, D), F32),
        compiler_params=_cparams(("arbitrary",)),
        name="moe_combine",
    )(pos_a, pos_b, y, x, rt, mod)


def _routing_tables(rt):
    ea = rt[:, 0:2].astype(jnp.int32).reshape(-1)
    onehot = (ea[:, None] == jnp.arange(N_EXPERTS, dtype=jnp.int32)[None, :]).astype(jnp.int32)
    csum = jnp.cumsum(onehot, axis=0)
    rank = jnp.take_along_axis(csum, ea[:, None], axis=1)[:, 0] - 1
    counts = csum[-1]
    padded = ((counts + TM_MOE - 1) // TM_MOE) * TM_MOE
    ends = jnp.cumsum(padded)
    pos = (ends - padded)[ea] + rank
    n_valid = (ends[-1] // TM_MOE).astype(jnp.int32).reshape(1)
    tile_start = jnp.arange(N_TILES, dtype=jnp.int32) * TM_MOE
    expert_of_row = lambda r: jnp.sum((r[:, None] >= ends[None, :]).astype(jnp.int32), axis=1)
    tile_expert = jnp.minimum(expert_of_row(tile_start), expert_of_row(ends[-1:] - 1))
    pos = pos.astype(jnp.int32)
    pos2 = pos.reshape(T, 2)
    return n_valid, tile_expert, pos, pos2[:, 0], pos2[:, 1]


def _moe_layer(x, ln_g, mod, router, w13, w2, layer):
    router_pad = jnp.pad(router, ((0, 0), (0, LANES - N_EXPERTS)))
    h, rt = _router_call(x, ln_g, mod, router_pad)
    n_valid, tile_expert, pos, pos_a, pos_b = _routing_tables(rt)
    xs = _dispatch_call(pos, h)
    y = _moe_call(n_valid, tile_expert, xs, w13, w2, layer)
    return _combine_call(pos_a, pos_b, y, x, rt, mod)


def kernel(x_prompt, x_sample, c, cache_na_k, cache_na_v, state_ssm_re, state_ssm_im, c_ctx,
           ln1_g, ln2_g, ada_w, ada_b, na_w_qkv, na_w_o, na_q_g, na_k_g, na_rpb,
           ssm_lam_re, ssm_lam_im, ssm_log_step, ssm_b_re, ssm_b_im, ssm_c_re, ssm_c_im,
           ssm_d, ssm_w_glu, ssm_b_glu, cv_w_in, cv_conv_w, cv_conv_b, cv_w_out,
           ffn_w13, ffn_w2, moe_router, moe_w13, moe_w2):
    depth = ada_w.shape[0]
    x = jnp.concatenate([x_sample.reshape(T_LAT, D), x_prompt.reshape(T_CTX, D)], axis=0)
    conds = jnp.concatenate([c, c_ctx[None, :], jnp.zeros((N_COND - DEC_BATCH - 1, D), F32)], axis=0)
    mods = _ada_call(conds, ada_w, ada_b).reshape(depth, N_COND, 6, D)
    ffn_w13_b, ffn_w2_b = ffn_w13.astype(BF16), ffn_w2.astype(BF16)

    ks, vs, s_re, s_im = [], [], [], []
    for l in range(depth):
        kind, j = l % 3, l // 3
        mod = mods[l]
        if kind == 0:
            qkv = _normproj_call(x, ln1_g[l], mod, na_w_qkv[j].astype(BF16), shift=0, scale=1,
                                 name="qkv_proj")
            qg2 = jnp.tile(na_q_g[j], 2).reshape(1, LANES)
            kg2 = jnp.tile(na_k_g[j], 2).reshape(1, LANES)
            past = cache_na_k.shape[3]
            ctx_k = jnp.transpose(cache_na_k[:, j], (0, 2, 1, 3)).reshape(DEC_BATCH, past, D)
            ctx_v = jnp.transpose(cache_na_v[:, j], (0, 2, 1, 3)).reshape(DEC_BATCH, past, D)
            o_lat = _lat_attn_call(qkv, ctx_k, ctx_v, qg2, kg2, _bias_pairs(na_rpb[j]))
            o_ctx, k_new, v_new = _ctx_attn_call(qkv, qg2, kg2)
            ks.append(k_new)
            vs.append(v_new)
            x = _proj_res_call(o_lat, o_ctx, x, mod, na_w_o[j].astype(BF16), gate=2, name="attn_out")
        elif kind == 1:
            u = _norm_call(x, ln1_g[l], mod, shift=0, scale=1, name="s5_norm")
            bmat, cmat, a_re, a_im = _s5_params(ssm_lam_re[j], ssm_lam_im[j], ssm_log_step[j],
                                                ssm_b_re[j], ssm_b_im[j], ssm_c_re[j], ssm_c_im[j])
            y_lat, _, _ = _s5_call(u, bmat, cmat, a_re, a_im,
                                   _state_to_slabs(state_ssm_re[:, j]), _state_to_slabs(state_ssm_im[:, j]),
                                   row0=0, n_seq=DEC_BATCH, seq_len=DEC_SEQ,
                                   nb=DEC_BATCH, steps=S5_T_LAT, name="s5_lat")
            zero = jnp.zeros((2, S5_NSLAB, BATCH, S5_LANES), F32)
            y_ctx, f_re, f_im = _s5_call(u, bmat, cmat, a_re, a_im, zero, zero,
                                         row0=T_LAT, n_seq=BATCH, seq_len=SEQ, nb=8, steps=SEQ,
                                         name="s5_ctx")
            s_re.append(_slabs_to_state(f_re))
            s_im.append(_slabs_to_state(f_im))
            x = _glu_res_call(u, y_lat, y_ctx, x, mod, ssm_d[j], ssm_w_glu[j].astype(BF16), ssm_b_glu[j])
        else:
            proj = _normproj_call(x, ln1_g[l], mod, cv_w_in[j].astype(BF16), shift=0, scale=1,
                                  name="conv_in")
            x = _conv_call(proj, x, mod, cv_conv_w[j], cv_conv_b[j], cv_w_out[j].astype(BF16))
        jj = l // 2
        if l % 2 == 0:
            x = _ffn_call(x, ln2_g[l], mod, ffn_w13_b, ffn_w2_b, jj)
        else:
            x = _moe_layer(x, ln2_g[l], mod, moe_router[jj], moe_w13, moe_w2, jj)

    y_sample = x[:T_LAT].reshape(DEC_BATCH, DEC_SEQ, D)
    y_prompt = x[T_LAT:].reshape(BATCH, SEQ, D)
    return (y_prompt, y_sample, jnp.stack(ks, axis=1), jnp.stack(vs, axis=1),
            jnp.stack(s_re, axis=1), jnp.stack(s_im, axis=1))
```

```python
import functools
import math

import numpy as np
import jax
import jax.numpy as jnp
from jax import lax
from jax.experimental import pallas as pl
from jax.experimental.pallas import tpu as pltpu

F32 = jnp.float32
BF16 = jnp.bfloat16

D = 1024
BATCH, SEQ = 16, 256
DEC_BATCH, DEC_SEQ = 4, 2048
GRID_W = 64
ROWS = DEC_SEQ // GRID_W
N_HEADS, HEAD_DIM = 16, 64
WIN_ROWS, WIN_COLS = 8, 16
SSM_GROUP, SSM_GROUPS, SSM_STATE = 16, 64, 64
D_FF = 2816
N_EXPERTS = 8
D_FF_EXPERT = 3584
EPS = 1e-6
NEG_INF = -1e30
SCALE = HEAD_DIM ** -0.5

T_LAT = DEC_BATCH * DEC_SEQ
T_CTX = BATCH * SEQ
T = T_LAT + T_CTX
N_COND = 8
CTX_COND = DEC_BATCH

LANES = 128
VMEM_LIMIT = 56 * 1024 * 1024

TM = 512
TM_CONV = 256
TM_MOE = 1024
TF_MOE = 512
HP = N_HEADS // 2
QB_ROWS = 4
S5_SLAB = 128
S5_NSLAB = D // S5_SLAB
S5_LANES = (S5_SLAB // SSM_GROUP) * SSM_STATE
S5_T_LAT = 512
S5_ROW_PAD = 4
S5_UNROLL = 4


def _cparams(sem):
    return pltpu.CompilerParams(dimension_semantics=sem, vmem_limit_bytes=VMEM_LIMIT)


def _cond_of_row(row):
    return jnp.minimum(row // DEC_SEQ, CTX_COND)


def _modnorm(x, g, sc, sh):
    ms = jnp.mean(x * x, axis=-1, keepdims=True)
    y = x * lax.rsqrt(ms + EPS) * g
    return y * (1.0 + sc) + sh


def _sigmoid(x):
    return 1.0 / (1.0 + jnp.exp(-x))


def _silu(x):
    return x * _sigmoid(x)


def _gelu_tanh(x):
    c = math.sqrt(2.0 / math.pi)
    return 0.5 * x * (1.0 + jnp.tanh(c * (x + 0.044715 * (x * x * x))))


def _ada_kernel(c_ref, w_ref, b_ref, o_ref):
    s = _silu(c_ref[...]).astype(BF16)
    o_ref[0] = jnp.dot(s, w_ref[0].astype(BF16), preferred_element_type=F32) + b_ref[0]


def _ada_call(conds, ada_w, ada_b):
    depth = ada_w.shape[0]
    tn = 1536
    return pl.pallas_call(
        _ada_kernel,
        grid=(depth, 6 * D // tn),
        in_specs=[
            pl.BlockSpec((N_COND, D), lambda l, n: (0, 0)),
            pl.BlockSpec((1, D, tn), lambda l, n: (l, 0, n)),
            pl.BlockSpec((1, 1, tn), lambda l, n: (l, 0, n)),
        ],
        out_specs=pl.BlockSpec((1, N_COND, tn), lambda l, n: (l, 0, n)),
        out_shape=jax.ShapeDtypeStruct((depth, N_COND, 6 * D), F32),
        compiler_params=_cparams(("parallel", "parallel")),
        name="ada_mod",
    )(conds, ada_w, ada_b.reshape(depth, 1, 6 * D))


LAT_TILES = T_LAT // TM


def _lat_tile(i):
    return jnp.minimum(i, LAT_TILES - 1)


def _ctx_tile(i):
    return jnp.maximum(i - LAT_TILES, 0)


def _rows_specs(x):
    if isinstance(x, tuple):
        return [pl.BlockSpec((TM, D), lambda i: (_lat_tile(i), 0)),
                pl.BlockSpec((TM, D), lambda i: (_ctx_tile(i), 0))], list(x)
    return [pl.BlockSpec((TM, D), lambda i: (i, 0))], [x]


def _rows_value(refs):
    if len(refs) == 2:
        return jnp.where(pl.program_id(0) < LAT_TILES, refs[0][...], refs[1][...])
    return refs[0][...]


def _normproj_kernel(*refs, shift, scale):
    *x_refs, g_ref, mod_ref, w_ref, o_ref = refs
    mod = mod_ref[0]
    h = _modnorm(_rows_value(x_refs), g_ref[...], mod[scale:scale + 1], mod[shift:shift + 1])
    o_ref[...] = jnp.dot(h.astype(BF16), w_ref[...], preferred_element_type=F32)


def _normproj_call(x, ln_g, mod, w, *, shift, scale, name):
    n = w.shape[1]
    x_specs, x_args = _rows_specs(x)
    return pl.pallas_call(
        functools.partial(_normproj_kernel, shift=shift, scale=scale),
        grid=(T // TM,),
        in_specs=x_specs + [
            pl.BlockSpec((1, D), lambda i: (0, 0)),
            pl.BlockSpec((1, 6, D), lambda i: (_cond_of_row(i * TM), 0, 0)),
            pl.BlockSpec((D, n), lambda i: (0, 0)),
        ],
        out_specs=pl.BlockSpec((TM, n), lambda i: (i, 0)),
        out_shape=jax.ShapeDtypeStruct((T, n), F32),
        compiler_params=_cparams(("parallel",)),
        name=name,
    )(*x_args, ln_g.reshape(1, D), mod, w)


def _norm_kernel(x_ref, g_ref, mod_ref, o_ref, *, shift, scale):
    mod = mod_ref[0]
    o_ref[...] = _modnorm(x_ref[...], g_ref[...], mod[scale:scale + 1], mod[shift:shift + 1])


def _norm_call(x, ln_g, mod, *, shift, scale, name):
    return pl.pallas_call(
        functools.partial(_norm_kernel, shift=shift, scale=scale),
        grid=(T // TM,),
        in_specs=[
            pl.BlockSpec((TM, D), lambda i: (i, 0)),
            pl.BlockSpec((1, D), lambda i: (0, 0)),
            pl.BlockSpec((1, 6, D), lambda i: (_cond_of_row(i * TM), 0, 0)),
        ],
        out_specs=pl.BlockSpec((TM, D), lambda i: (i, 0)),
        out_shape=jax.ShapeDtypeStruct((T, D), F32),
        compiler_params=_cparams(("parallel",)),
        name=name,
    )(x, ln_g.reshape(1, D), mod)


def _proj_res_kernel(*refs, gate, n_x):
    al_ref, ac_ref = refs[:2]
    x_refs = refs[2:2 + n_x]
    mod_ref, w_ref, o_ref = refs[2 + n_x:]
    a = _rows_value((al_ref, ac_ref))
    y = jnp.dot(a.astype(BF16), w_ref[...], preferred_element_type=F32)
    o_ref[...] = _rows_value(x_refs) + mod_ref[0][gate:gate + 1] * y


def _proj_res_call(a_lat, a_ctx, x, mod, w, *, gate, name):
    a_specs, a_args = _rows_specs((a_lat, a_ctx))
    x_specs, x_args = _rows_specs(x)
    return pl.pallas_call(
        functools.partial(_proj_res_kernel, gate=gate, n_x=len(x_args)),
        grid=(T // TM,),
        in_specs=a_specs + x_specs + [
            pl.BlockSpec((1, 6, D), lambda i: (_cond_of_row(i * TM), 0, 0)),
            pl.BlockSpec((D, D), lambda i: (0, 0)),
        ],
        out_specs=pl.BlockSpec((TM, D), lambda i: (i, 0)),
        out_shape=jax.ShapeDtypeStruct((T, D), F32),
        compiler_params=_cparams(("parallel",)),
        name=name,
    )(*a_args, *x_args, mod, w)


def _head_norm(x, g, head0):
    x2 = x * x
    s0 = jnp.sum(jnp.where(head0, x2, 0.0), axis=-1, keepdims=True)
    s1 = jnp.sum(jnp.where(head0, 0.0, x2), axis=-1, keepdims=True)
    ms = jnp.where(head0, s0, s1) * (1.0 / HEAD_DIM)
    return x * lax.rsqrt(ms + EPS) * g


def _nt_dot(a, b):
    return lax.dot_general(a, b, (((1,), (1,)), ((), ())), preferred_element_type=F32)


def _ctx_attn_kernel(q_ref, k_ref, v_ref, qg_ref, kg_ref, o_ref, ko_ref, vo_ref):
    head0 = lax.broadcasted_iota(jnp.int32, (SEQ, LANES), 1) < HEAD_DIM
    for hp in range(HP):
        cols = slice(hp * LANES, (hp + 1) * LANES)
        qn = _head_norm(q_ref[:, cols], qg_ref[...], head0)
        kn = _head_norm(k_ref[:, cols], kg_ref[...], head0)
        v = v_ref[:, cols]
        ko_ref[0, 2 * hp] = kn[:, :HEAD_DIM]
        ko_ref[0, 2 * hp + 1] = kn[:, HEAD_DIM:]
        vo_ref[0, 2 * hp] = v[:, :HEAD_DIM]
        vo_ref[0, 2 * hp + 1] = v[:, HEAD_DIM:]
        knb = kn.astype(BF16)
        vb = v.astype(BF16)
        outs = []
        for hh in range(2):
            qm = jnp.where(head0 if hh == 0 else jnp.logical_not(head0), qn * SCALE, 0.0).astype(BF16)
            s = _nt_dot(qm, knb)
            p = jnp.exp(s - jnp.max(s, axis=-1, keepdims=True))
            l = jnp.sum(p, axis=-1, keepdims=True)
            outs.append(jnp.dot(p.astype(BF16), vb, preferred_element_type=F32) / l)
        o_ref[:, cols] = jnp.where(head0, outs[0], outs[1])


def _ctx_attn_call(qkv, qg2, kg2):
    row0 = T_LAT // SEQ
    blk = lambda c: pl.BlockSpec((SEQ, D), lambda b, c=c: (row0 + b, c))
    kv_out = pl.BlockSpec((1, N_HEADS, SEQ, HEAD_DIM), lambda b: (b, 0, 0, 0))
    return pl.pallas_call(
        _ctx_attn_kernel,
        grid=(BATCH,),
        in_specs=[blk(0), blk(1), blk(2),
                  pl.BlockSpec((1, LANES), lambda b: (0, 0)),
                  pl.BlockSpec((1, LANES), lambda b: (0, 0))],
        out_specs=[pl.BlockSpec((SEQ, D), lambda b: (b, 0)), kv_out, kv_out],
        out_shape=[jax.ShapeDtypeStruct((T_CTX, D), F32),
                   jax.ShapeDtypeStruct((BATCH, N_HEADS, SEQ, HEAD_DIM), F32),
                   jax.ShapeDtypeStruct((BATCH, N_HEADS, SEQ, HEAD_DIM), F32)],
        compiler_params=_cparams(("parallel",)),
        name="ctx_attn",
    )(qkv, qkv, qkv, qg2, kg2)


def _band_of_block(p):
    r_lo, r_hi = QB_ROWS * p, QB_ROWS * p + QB_ROWS - 1
    kr = min(WIN_ROWS, ROWS)
    lo = min(max(r_lo - kr // 2, 0), ROWS - kr)
    hi = min(max(r_hi - kr // 2, 0), ROWS - kr) + kr
    lo -= lo % 2
    hi += hi % 2
    return lo, hi - lo


def _lat_attn_kernel(q_ref, k_ref, v_ref, ck_ref, cv_ref, qg_ref, kg_ref, bp_ref, o_ref,
                     qm0_s, qm1_s, kn_s, v_s):
    head0 = lax.broadcasted_iota(jnp.int32, (DEC_SEQ, LANES), 1) < HEAD_DIM
    qn = _head_norm(q_ref[...], qg_ref[...], head0)
    qm0_s[...] = jnp.where(head0, qn * SCALE, 0.0).astype(BF16)
    qm1_s[...] = jnp.where(head0, 0.0, qn * SCALE).astype(BF16)
    kn_s[...] = _head_norm(k_ref[...], kg_ref[...], head0).astype(BF16)
    v_s[...] = v_ref[...].astype(BF16)
    ckb = ck_ref[0].astype(BF16)
    cvb = cv_ref[0].astype(BF16)

    qc = lax.broadcasted_iota(jnp.int32, (GRID_W, LANES), 0)
    kl = lax.broadcasted_iota(jnp.int32, (GRID_W, LANES), 1)
    kc = jnp.where(kl < GRID_W, kl, kl - GRID_W)
    cs = jnp.clip(qc - WIN_COLS // 2, 0, GRID_W - WIN_COLS)
    col_ok = jnp.logical_and(kc >= cs, kc < cs + WIN_COLS)
    left = kl < GRID_W
    mask_of = {
        (True, True): col_ok,
        (True, False): jnp.logical_and(col_ok, left),
        (False, True): jnp.logical_and(col_ok, jnp.logical_not(left)),
    }
    h0q = lax.broadcasted_iota(jnp.int32, (QB_ROWS * GRID_W, LANES), 1) < HEAD_DIM
    kr_win = min(WIN_ROWS, ROWS)

    for p in range(ROWS // QB_ROWS):
        u0, nrows = _band_of_block(p)
        q_lo = p * QB_ROWS * GRID_W
        kb = kn_s[u0 * GRID_W:(u0 + nrows) * GRID_W, :]
        vb = v_s[u0 * GRID_W:(u0 + nrows) * GRID_W, :]
        outs = []
        for hh in range(2):
            qm = (qm0_s if hh == 0 else qm1_s)[q_lo:q_lo + QB_ROWS * GRID_W, :]
            s_loc = _nt_dot(qm, kb)
            s_ctx = _nt_dot(qm, ckb)
            p_rows, pc_rows, l_rows = [], [], []
            for i in range(QB_ROWS):
                r = p * QB_ROWS + i
                r0 = min(max(r - kr_win // 2, 0), ROWS - kr_win)
                rows = slice(i * GRID_W, (i + 1) * GRID_W)
                sbs = {}
                for m in range(nrows // 2):
                    kr = u0 + 2 * m
                    ok_l = r0 <= kr < r0 + kr_win
                    ok_r = r0 <= kr + 1 < r0 + kr_win
                    if ok_l or ok_r:
                        sb = s_loc[rows, m * LANES:(m + 1) * LANES] + bp_ref[hh, kr - r + WIN_ROWS]
                        sbs[m] = jnp.where(mask_of[(ok_l, ok_r)], sb, NEG_INF)
                sc = s_ctx[rows, :]
                mx = jnp.maximum(
                    jnp.max(functools.reduce(jnp.maximum, sbs.values()), axis=-1, keepdims=True),
                    jnp.max(sc, axis=-1, keepdims=True))
                pbs = {m: jnp.exp(sb - mx) for m, sb in sbs.items()}
                pc = jnp.exp(sc - mx)
                l_rows.append(jnp.sum(functools.reduce(jnp.add, pbs.values()), axis=-1, keepdims=True)
                              + jnp.sum(pc, axis=-1, keepdims=True))
                zero = jnp.zeros((GRID_W, LANES), BF16)
                p_rows.append(jnp.concatenate(
                    [pbs[m].astype(BF16) if m in pbs else zero for m in range(nrows // 2)], axis=1))
                pc_rows.append(pc.astype(BF16))
            o = (jnp.dot(jnp.concatenate(p_rows, axis=0), vb, preferred_element_type=F32)
                 + jnp.dot(jnp.concatenate(pc_rows, axis=0), cvb, preferred_element_type=F32))
            outs.append(o / jnp.concatenate(l_rows, axis=0))
        o_ref[q_lo:q_lo + QB_ROWS * GRID_W, :] = jnp.where(h0q, outs[0], outs[1])


def _lat_attn_call(qkv, ctx_k, ctx_v, qg2, kg2, bias_pairs):
    blk = lambda c: pl.BlockSpec((DEC_SEQ, LANES), lambda hp, b, c=c: (b, c * HP + hp))
    cblk = pl.BlockSpec((1, ctx_k.shape[1], LANES), lambda hp, b: (b, 0, hp))
    return pl.pallas_call(
        _lat_attn_kernel,
        grid=(HP, DEC_BATCH),
        in_specs=[blk(0), blk(1), blk(2), cblk, cblk,
                  pl.BlockSpec((1, LANES), lambda hp, b: (0, 0)),
                  pl.BlockSpec((1, LANES), lambda hp, b: (0, 0)),
                  pl.BlockSpec((2, 2 * WIN_ROWS, GRID_W, LANES), lambda hp, b: (hp, 0, 0, 0))],
        out_specs=pl.BlockSpec((DEC_SEQ, LANES), lambda hp, b: (b, hp)),
        out_shape=jax.ShapeDtypeStruct((T_LAT, D), F32),
        scratch_shapes=[pltpu.VMEM((DEC_SEQ, LANES), BF16)] * 4,
        compiler_params=_cparams(("parallel", "parallel")),
        name="lat_attn",
    )(qkv, qkv, qkv, ctx_k, ctx_v, qg2, kg2, bias_pairs)


def _bias_pairs(rpb):
    cols = np.arange(GRID_W)
    col_idx = np.clip(cols[None, :] - cols[:, None], -(WIN_COLS - 1), WIN_COLS - 1) + WIN_COLS - 1
    rc = rpb[:, :, col_idx]
    left = jnp.pad(rc, ((0, 0), (1, 0), (0, 0), (0, 0)))
    right = jnp.pad(rc, ((0, 0), (0, 1), (0, 0), (0, 0)))
    return jnp.concatenate([left, right], axis=-1)


def _ffn_kernel(x_ref, g_ref, mod_ref, w1_ref, w3_ref, w2_ref, o_ref):
    mod = mod_ref[0]
    x = x_ref[...]
    h = _modnorm(x, g_ref[...], mod[4:5], mod[3:4]).astype(BF16)
    a = _silu(jnp.dot(h, w1_ref[...], preferred_element_type=F32)) \
        * jnp.dot(h, w3_ref[...], preferred_element_type=F32)
    y = jnp.dot(a.astype(BF16), w2_ref[...], preferred_element_type=F32)
    o_ref[...] = x + mod[5:6] * y


def _ffn_call(x, ln_g, mod, w13, w2, layer):
    resident = dict(pipeline_mode=pl.Buffered(1))
    return pl.pallas_call(
        _ffn_kernel,
        grid=(T // TM,),
        in_specs=[
            pl.BlockSpec((TM, D), lambda i: (i, 0)),
            pl.BlockSpec((1, D), lambda i: (0, 0)),
            pl.BlockSpec((1, 6, D), lambda i: (_cond_of_row(i * TM), 0, 0)),
            pl.BlockSpec((None, D, D_FF), lambda i: (layer, 0, 0), **resident),
            pl.BlockSpec((None, D, D_FF), lambda i: (layer, 0, 1), **resident),
            pl.BlockSpec((None, D_FF, D), lambda i: (layer, 0, 0), **resident),
        ],
        out_specs=pl.BlockSpec((TM, D), lambda i: (i, 0)),
        out_shape=jax.ShapeDtypeStruct((T, D), F32),
        compiler_params=_cparams(("parallel",)),
        name="ffn_dense",
    )(x, ln_g.reshape(1, D), mod, w13, w13, w2)


def _s5_kernel(u_ref, bm_ref, cm_ref, ar_ref, ai_ref, hr_ref, hi_ref,
               y_ref, fr_ref, fi_ref, bu_s, st_s, cr_s, ci_s, *, nb, steps):
    d = pl.program_id(0)
    tb = pl.program_id(3)
    rows = nb * steps
    pitch = steps + S5_ROW_PAD
    nch = S5_LANES // LANES
    u = u_ref[...].reshape(rows, S5_SLAB).astype(BF16)
    bu = jnp.dot(u, bm_ref[0, 0], preferred_element_type=F32)
    for c in range(2 * nch):
        for k in range(nb):
            bu_s[c, k * pitch:k * pitch + steps, :] = bu[k * steps:(k + 1) * steps, c * LANES:(c + 1) * LANES]

    @pl.when(tb == 0)
    def _():
        cr_s[...] = hr_ref[0, 0]
        ci_s[...] = hi_ref[0, 0]

    ar = [jnp.broadcast_to(ar_ref[0, 0][:, c * LANES:(c + 1) * LANES], (nb, LANES)) for c in range(nch)]
    ai = [jnp.broadcast_to(ai_ref[0, 0][:, c * LANES:(c + 1) * LANES], (nb, LANES)) for c in range(nch)]

    def step(i, carry):
        l = jnp.where(d == 0, i, steps - 1 - i)
        idx = pl.ds(l, nb, stride=pitch)
        out = []
        for c in range(nch):
            sr, si = carry[2 * c], carry[2 * c + 1]
            nr = ar[c] * sr - ai[c] * si + bu_s[c, idx, :]
            ni = ar[c] * si + ai[c] * sr + bu_s[nch + c, idx, :]
            st_s[c, idx, :] = nr
            st_s[nch + c, idx, :] = ni
            out += [nr, ni]
        return tuple(out)

    init = []
    for c in range(nch):
        init += [cr_s[:, c * LANES:(c + 1) * LANES], ci_s[:, c * LANES:(c + 1) * LANES]]
    fin = lax.fori_loop(0, steps, step, tuple(init), unroll=S5_UNROLL)
    sr = jnp.concatenate([fin[2 * c] for c in range(nch)], axis=1)
    si = jnp.concatenate([fin[2 * c + 1] for c in range(nch)], axis=1)
    cr_s[...] = sr
    ci_s[...] = si
    fr_ref[0, 0] = sr
    fi_ref[0, 0] = si
    states = jnp.concatenate(
        [jnp.concatenate([st_s[c, k * pitch:k * pitch + steps, :] for k in range(nb)], axis=0)
         for c in range(2 * nch)], axis=1).astype(BF16)
    y = jnp.dot(states, cm_ref[0, 0], preferred_element_type=F32)
    y_ref[0] = y.reshape(nb, steps, S5_SLAB)


def _s5_call(u, bmat, cmat, a_re, a_im, h_re, h_im, *, row0, n_seq, seq_len, nb, steps, name):
    n_sg, n_tb = n_seq // nb, seq_len // steps
    sg0 = row0 // (nb * seq_len)

    def tbi(d, tb):
        return jnp.where(d == 0, tb, n_tb - 1 - tb)

    par = lambda last: pl.BlockSpec((1, 1) + last, lambda d, j, sg, tb: (d, j, 0, 0))
    st = pl.BlockSpec((1, 1, nb, S5_LANES), lambda d, j, sg, tb: (d, j, sg, 0))
    y, f_re, f_im = pl.pallas_call(
        functools.partial(_s5_kernel, nb=nb, steps=steps),
        grid=(2, S5_NSLAB, n_sg, n_tb),
        in_specs=[
            pl.BlockSpec((nb, steps, S5_SLAB), lambda d, j, sg, tb: (sg0 + sg, tbi(d, tb), j)),
            par((S5_SLAB, 2 * S5_LANES)),
            par((2 * S5_LANES, S5_SLAB)),
            par((1, S5_LANES)), par((1, S5_LANES)),
            st, st,
        ],
        out_specs=[
            pl.BlockSpec((1, nb, steps, S5_SLAB), lambda d, j, sg, tb: (d, sg, tbi(d, tb), j)),
            st, st,
        ],
        out_shape=[
            jax.ShapeDtypeStruct((2, n_seq, seq_len, D), F32),
            jax.ShapeDtypeStruct((2, S5_NSLAB, n_seq, S5_LANES), F32),
            jax.ShapeDtypeStruct((2, S5_NSLAB, n_seq, S5_LANES), F32),
        ],
        scratch_shapes=[pltpu.VMEM((2 * S5_LANES // LANES, nb * (steps + S5_ROW_PAD), LANES), F32)] * 2
                       + [pltpu.VMEM((nb, S5_LANES), F32)] * 2,
        compiler_params=_cparams(("parallel", "parallel", "parallel", "arbitrary")),
        name=name,
    )(u.reshape(T // seq_len, seq_len, D), bmat, cmat, a_re, a_im, h_re, h_im)
    return y.reshape(2, n_seq * seq_len, D), f_re, f_im


def _s5_params(lam_re, lam_im, log_step, b_re, b_im, c_re, c_im):
    step = jnp.exp(log_step)[..., None]
    zr, zi = lam_re * step, lam_im * step
    mag = jnp.exp(zr)
    a_re, a_im = mag * jnp.cos(zi), mag * jnp.sin(zi)
    nr, ni = a_re - 1.0, a_im
    den = lam_re * lam_re + lam_im * lam_im
    k_re = (nr * lam_re + ni * lam_im) / den
    k_im = (ni * lam_re - nr * lam_im) / den
    bb_re = k_re[..., None] * b_re - k_im[..., None] * b_im
    bb_im = k_re[..., None] * b_im + k_im[..., None] * b_re
    gl = S5_SLAB // SSM_GROUP
    eye = jnp.eye(gl, dtype=F32)

    def bdiag_in(w):
        w = w.reshape(2, S5_NSLAB, gl, SSM_STATE, SSM_GROUP)
        return jnp.einsum('dsgnp,gh->dsgphn', w, eye).reshape(2, S5_NSLAB, S5_SLAB, S5_LANES)

    def bdiag_out(w):
        w = w.reshape(2, S5_NSLAB, gl, SSM_GROUP, SSM_STATE)
        return jnp.einsum('dsgpn,gh->dsgnhp', w, eye).reshape(2, S5_NSLAB, S5_LANES, S5_SLAB)

    bmat = jnp.concatenate([bdiag_in(bb_re), bdiag_in(bb_im)], axis=-1).astype(BF16)
    cmat = jnp.concatenate([bdiag_out(c_re), -bdiag_out(c_im)], axis=-2).astype(BF16)
    slab = lambda a: a.reshape(2, S5_NSLAB, 1, S5_LANES)
    return bmat, cmat, slab(a_re), slab(a_im)


def _state_to_slabs(h):
    b = h.shape[0]
    return jnp.transpose(h.reshape(b, 2, S5_NSLAB, S5_LANES), (1, 2, 0, 3))


def _slabs_to_state(f):
    b = f.shape[2]
    return jnp.transpose(f, (2, 0, 1, 3)).reshape(b, 2, SSM_GROUPS, SSM_STATE)


def _glu_res_kernel(u_ref, yl_ref, yc_ref, x_ref, mod_ref, d_ref, w_ref, b_ref, o_ref):
    is_lat = pl.program_id(0) < LAT_TILES
    y = jnp.where(is_lat, yl_ref[0] + yl_ref[1], yc_ref[0] + yc_ref[1])
    yt = u_ref[...] * d_ref[...] + y
    z = _gelu_tanh(yt)
    gate = _sigmoid(jnp.dot(z.astype(BF16), w_ref[...], preferred_element_type=F32) + b_ref[...])
    o_ref[...] = x_ref[...] + mod_ref[0][2:3] * (z * gate)


def _glu_res_call(u, y_lat, y_ctx, x, mod, d_skip, w_glu, b_glu):
    return pl.pallas_call(
        _glu_res_kernel,
        grid=(T // TM,),
        in_specs=[
            pl.BlockSpec((TM, D), lambda i: (i, 0)),
            pl.BlockSpec((2, TM, D), lambda i: (0, _lat_tile(i), 0)),
            pl.BlockSpec((2, TM, D), lambda i: (0, _ctx_tile(i), 0)),
            pl.BlockSpec((TM, D), lambda i: (i, 0)),
            pl.BlockSpec((1, 6, D), lambda i: (_cond_of_row(i * TM), 0, 0)),
            pl.BlockSpec((1, D), lambda i: (0, 0)),
            pl.BlockSpec((D, D), lambda i: (0, 0)),
            pl.BlockSpec((1, D), lambda i: (0, 0)),
        ],
        out_specs=pl.BlockSpec((TM, D), lambda i: (i, 0)),
        out_shape=jax.ShapeDtypeStruct((T, D), F32),
        compiler_params=_cparams(("parallel",)),
        name="s5_glu_res",
    )(u, y_lat, y_ctx, x, mod, d_skip.reshape(1, D), w_glu, b_glu.reshape(1, D))


def _conv_kernel(bg_ref, cg_ref, xi_ref, cgp_ref, xip_ref, cgn_ref, xin_ref,
                 x_ref, mod_ref, cw_ref, cb_ref, w_ref, o_ref):
    i = pl.program_id(0)
    tiles_per_seq = DEC_SEQ // TM_CONV
    is_lat = i < T_LAT // TM_CONV
    first = jnp.logical_or(jnp.logical_not(is_lat), i % tiles_per_seq == 0)
    last = jnp.logical_or(jnp.logical_not(is_lat), i % tiles_per_seq == tiles_per_seq - 1)
    z = cg_ref[...] * xi_ref[...]
    zp_row = jnp.where(first, 0.0, cgp_ref[7:8, :] * xip_ref[7:8, :])
    zn_row = jnp.where(last, 0.0, cgn_ref[0:1, :] * xin_ref[0:1, :])
    row = lax.broadcasted_iota(jnp.int32, (TM_CONV, D), 0)
    zp = jnp.where(row == 0, zp_row, pltpu.roll(z, 1, axis=0))
    zn = jnp.where(row == TM_CONV - 1, zn_row, pltpu.roll(z, TM_CONV - 1, axis=0))
    cw = cw_ref[...]
    zc = cw[0:1] * zp + cw[1:2] * z + cw[2:3] * zn + cb_ref[...]
    a = (bg_ref[...] * zc).astype(BF16)
    y = jnp.dot(a, w_ref[...], preferred_element_type=F32)
    o_ref[...] = x_ref[...] + mod_ref[0][2:3] * y


def _conv_call(proj, x, mod, conv_w, conv_b, w_out):
    r8 = TM_CONV // 8
    n8 = T // 8
    col = lambda c: pl.BlockSpec((TM_CONV, D), lambda i, c=c: (i, c))
    prev = lambda c: pl.BlockSpec((8, D), lambda i, c=c: (jnp.maximum(i * r8 - 1, 0), c))
    nxt = lambda c: pl.BlockSpec((8, D), lambda i, c=c: (jnp.minimum((i + 1) * r8, n8 - 1), c))
    return pl.pallas_call(
        _conv_kernel,
        grid=(T // TM_CONV,),
        in_specs=[
            col(0), col(1), col(2), prev(1), prev(2), nxt(1), nxt(2),
            pl.BlockSpec((TM_CONV, D), lambda i: (i, 0)),
            pl.BlockSpec((1, 6, D), lambda i: (_cond_of_row(i * TM_CONV), 0, 0)),
            pl.BlockSpec((3, D), lambda i: (0, 0)),
            pl.BlockSpec((1, D), lambda i: (0, 0)),
            pl.BlockSpec((D, D), lambda i: (0, 0)),
        ],
        out_specs=pl.BlockSpec((TM_CONV, D), lambda i: (i, 0)),
        out_shape=jax.ShapeDtypeStruct((T, D), F32),
        compiler_params=_cparams(("parallel",)),
        name="conv_mix",
    )(proj, proj, proj, proj, proj, proj, proj, x, mod, conv_w, conv_b.reshape(1, D), w_out)


def _router_kernel(x_ref, g_ref, mod_ref, r_ref, h_ref, rt_ref):
    mod = mod_ref[0]
    h = _modnorm(x_ref[...], g_ref[...], mod[4:5], mod[3:4])
    h_ref[...] = h
    logits = jnp.dot(h, r_ref[...], preferred_element_type=F32, precision=lax.Precision.HIGHEST)
    lane = lax.broadcasted_iota(jnp.int32, logits.shape, 1)
    logits = jnp.where(lane < N_EXPERTS, logits, -jnp.inf)
    m1 = jnp.max(logits, axis=-1, keepdims=True)
    i1 = jnp.min(jnp.where(logits == m1, lane, LANES), axis=-1, keepdims=True)
    rest = jnp.where(lane == i1, -jnp.inf, logits)
    m2 = jnp.max(rest, axis=-1, keepdims=True)
    i2 = jnp.min(jnp.where(rest == m2, lane, LANES), axis=-1, keepdims=True)
    e2 = jnp.exp(m2 - m1)
    w1 = 1.0 / (1.0 + e2)
    w2 = e2 / (1.0 + e2)
    rt_ref[...] = jnp.where(lane == 0, i1.astype(F32),
                            jnp.where(lane == 1, i2.astype(F32),
                                      jnp.where(lane == 2, w1, jnp.where(lane == 3, w2, 0.0))))


def _router_call(x, ln_g, mod, router_pad):
    return pl.pallas_call(
        _router_kernel,
        grid=(T // TM,),
        in_specs=[
            pl.BlockSpec((TM, D), lambda i: (i, 0)),
            pl.BlockSpec((1, D), lambda i: (0, 0)),
            pl.BlockSpec((1, 6, D), lambda i: (_cond_of_row(i * TM), 0, 0)),
            pl.BlockSpec((D, LANES), lambda i: (0, 0)),
        ],
        out_specs=[pl.BlockSpec((TM, D), lambda i: (i, 0)),
                   pl.BlockSpec((TM, LANES), lambda i: (i, 0))],
        out_shape=[jax.ShapeDtypeStruct((T, D), F32), jax.ShapeDtypeStruct((T, LANES), F32)],
        compiler_params=_cparams(("parallel",)),
        name="moe_router",
    )(x, ln_g.reshape(1, D), mod, router_pad)


R_MAX = 2 * T + N_EXPERTS * TM_MOE
N_TILES = R_MAX // TM_MOE
DMA_UNROLL = 8


def _row_copy(src_hbm, src_row, dst_vmem, dst_row, sem):
    return pltpu.make_async_copy(src_hbm.at[pl.ds(src_row, 1), :], dst_vmem.at[pl.ds(dst_row, 1), :], sem)


def _gather_kernel(nv_ref, src_ref, h_hbm, o_ref, buf, sem):
    i = pl.program_id(0)
    nv = nv_ref[0]

    def issue_tile(t):
        slot = t % 2
        base = t * TM_MOE

        def issue(r, c):
            _row_copy(h_hbm, src_ref[base + r], buf.at[slot], r, sem.at[slot]).start()
            return c

        lax.fori_loop(0, TM_MOE, issue, 0, unroll=DMA_UNROLL)

    @pl.when(jnp.logical_and(i == 0, nv > 0))
    def _():
        issue_tile(0)

    @pl.when(i + 1 < nv)
    def _():
        issue_tile(i + 1)

    @pl.when(i < nv)
    def _():
        slot = i % 2
        pltpu.make_async_copy(h_hbm.at[pl.ds(0, TM_MOE), :], buf.at[slot], sem.at[slot]).wait()
        o_ref[...] = buf[slot].astype(BF16)

    @pl.when(i >= nv)
    def _():
        o_ref[...] = jnp.zeros_like(o_ref)


def _gather_call(n_valid, src_tok, h):
    return pl.pallas_call(
        _gather_kernel,
        grid_spec=pltpu.PrefetchScalarGridSpec(
            num_scalar_prefetch=2,
            grid=(N_TILES,),
            in_specs=[pl.BlockSpec(memory_space=pl.ANY)],
            out_specs=pl.BlockSpec((TM_MOE, D), lambda i, nv, src: (i, 0)),
            scratch_shapes=[pltpu.VMEM((2, TM_MOE, D), F32), pltpu.SemaphoreType.DMA((2,))],
        ),
        out_shape=jax.ShapeDtypeStruct((R_MAX, D), BF16),
        compiler_params=_cparams(("arbitrary",)),
        name="moe_gather",
    )(n_valid, src_tok, h)


def _moe_kernel(nv_ref, te_ref, xs_ref, w1_ref, w3_ref, w2_ref, o_ref, acc_s):
    i = pl.program_id(0)
    f = pl.program_id(1)
    nf = pl.num_programs(1)
    valid = i < nv_ref[0]

    @pl.when(jnp.logical_and(valid, f == 0))
    def _():
        acc_s[...] = jnp.zeros_like(acc_s)

    @pl.when(valid)
    def _():
        xs = xs_ref[...]
        a = _silu(jnp.dot(xs, w1_ref[...].astype(BF16), preferred_element_type=F32)) \
            * jnp.dot(xs, w3_ref[...].astype(BF16), preferred_element_type=F32)
        acc_s[...] += jnp.dot(a.astype(BF16), w2_ref[...].astype(BF16), preferred_element_type=F32)

    @pl.when(f == nf - 1)
    def _():
        o_ref[...] = jnp.where(valid, acc_s[...], 0.0)


def _moe_call(n_valid, tile_expert, xs, w13, w2, layer):
    nf = D_FF_EXPERT // TF_MOE

    def fe(i, f, nv):
        return jnp.where(i < nv[0], f, nf - 1)

    return pl.pallas_call(
        _moe_kernel,
        grid_spec=pltpu.PrefetchScalarGridSpec(
            num_scalar_prefetch=2,
            grid=(N_TILES, nf),
            in_specs=[
                pl.BlockSpec((TM_MOE, D), lambda i, f, nv, te: (i, 0)),
                pl.BlockSpec((None, None, D, TF_MOE),
                             lambda i, f, nv, te: (layer, te[i], 0, fe(i, f, nv))),
                pl.BlockSpec((None, None, D, TF_MOE),
                             lambda i, f, nv, te: (layer, te[i], 0, nf + fe(i, f, nv))),
                pl.BlockSpec((None, None, TF_MOE, D),
                             lambda i, f, nv, te: (layer, te[i], fe(i, f, nv), 0)),
            ],
            out_specs=pl.BlockSpec((TM_MOE, D), lambda i, f, nv, te: (i, 0)),
            scratch_shapes=[pltpu.VMEM((TM_MOE, D), F32)],
        ),
        out_shape=jax.ShapeDtypeStruct((R_MAX, D), F32),
        compiler_params=_cparams(("arbitrary", "arbitrary")),
        name="moe_experts",
    )(n_valid, tile_expert, xs, w13, w13, w2)


TM_COMB = 256


def _combine_kernel(pa_ref, pb_ref, y_hbm, x_ref, rt_ref, mod_ref, o_ref, buf_a, buf_b, sem):
    i = pl.program_id(0)

    def issue_tile(t):
        slot = t % 2
        base = t * TM_COMB

        def issue(r, c):
            _row_copy(y_hbm, pa_ref[base + r], buf_a.at[slot], r, sem.at[slot]).start()
            _row_copy(y_hbm, pb_ref[base + r], buf_b.at[slot], r, sem.at[slot]).start()
            return c

        lax.fori_loop(0, TM_COMB, issue, 0, unroll=DMA_UNROLL)

    @pl.when(i == 0)
    def _():
        issue_tile(0)

    @pl.when(i + 1 < pl.num_programs(0))
    def _():
        issue_tile(i + 1)

    slot = i % 2
    for buf in (buf_a, buf_b):
        pltpu.make_async_copy(y_hbm.at[pl.ds(0, TM_COMB), :], buf.at[slot], sem.at[slot]).wait()
    rt = rt_ref[...]
    ffn = rt[:, 2:3] * buf_a[slot] + rt[:, 3:4] * buf_b[slot]
    o_ref[...] = x_ref[...] + mod_ref[0][5:6] * ffn


def _combine_call(pos_a, pos_b, y, x, rt, mod):
    return pl.pallas_call(
        _combine_kernel,
        grid_spec=pltpu.PrefetchScalarGridSpec(
            num_scalar_prefetch=2,
            grid=(T // TM_COMB,),
            in_specs=[
                pl.BlockSpec(memory_space=pl.ANY),
                pl.BlockSpec((TM_COMB, D), lambda i, pa, pb: (i, 0)),
                pl.BlockSpec((TM_COMB, LANES), lambda i, pa, pb: (i, 0)),
                pl.BlockSpec((1, 6, D), lambda i, pa, pb: (_cond_of_row(i * TM_COMB), 0, 0)),
            ],
            out_specs=pl.BlockSpec((TM_COMB, D), lambda i, pa, pb: (i, 0)),
            scratch_shapes=[pltpu.VMEM((2, TM_COMB, D), F32), pltpu.VMEM((2, TM_COMB, D), F32),
                            pltpu.SemaphoreType.DMA((2,))],
        ),
        out_shape=jax.ShapeDtypeStruct((T, D), F32),
        compiler_params=_cparams(("arbitrary",)),
        name="moe_combine",
    )(pos_a, pos_b, y, x, rt, mod)


def _routing_tables(rt):
    ea = rt[:, 0:2].astype(jnp.int32).reshape(-1)
    onehot = (ea[:, None] == jnp.arange(N_EXPERTS, dtype=jnp.int32)[None, :]).astype(jnp.int32)
    csum = jnp.cumsum(onehot, axis=0)
    rank = jnp.take_along_axis(csum, ea[:, None], axis=1)[:, 0] - 1
    counts = csum[-1]
    padded = ((counts + TM_MOE - 1) // TM_MOE) * TM_MOE
    ends = jnp.cumsum(padded)
    pos = (ends - padded)[ea] + rank
    n_valid = (ends[-1] // TM_MOE).astype(jnp.int32).reshape(1)
    tile_start = jnp.arange(N_TILES, dtype=jnp.int32) * TM_MOE
    expert_of_row = lambda r: jnp.sum((r[:, None] >= ends[None, :]).astype(jnp.int32), axis=1)
    tile_expert = jnp.minimum(expert_of_row(tile_start), expert_of_row(ends[-1:] - 1))
    tok = jnp.arange(2 * T, dtype=jnp.int32) // 2
    src_tok = jnp.zeros((R_MAX,), jnp.int32).at[pos].set(tok)
    pos2 = pos.reshape(T, 2).astype(jnp.int32)
    return n_valid, tile_expert, src_tok, pos2[:, 0], pos2[:, 1]


def _moe_layer(x, ln_g, mod, router, w13, w2, layer):
    router_pad = jnp.pad(router, ((0, 0), (0, LANES - N_EXPERTS)))
    h, rt = _router_call(x, ln_g, mod, router_pad)
    n_valid, tile_expert, src_tok, pos_a, pos_b = _routing_tables(rt)
    xs = _gather_call(n_valid, src_tok, h)
    y = _moe_call(n_valid, tile_expert, xs, w13, w2, layer)
    return _combine_call(pos_a, pos_b, y, x, rt, mod)


def kernel(x_prompt, x_sample, c, cache_na_k, cache_na_v, state_ssm_re, state_ssm_im, c_ctx,
           ln1_g, ln2_g, ada_w, ada_b, na_w_qkv, na_w_o, na_q_g, na_k_g, na_rpb,
           ssm_lam_re, ssm_lam_im, ssm_log_step, ssm_b_re, ssm_b_im, ssm_c_re, ssm_c_im,
           ssm_d, ssm_w_glu, ssm_b_glu, cv_w_in, cv_conv_w, cv_conv_b, cv_w_out,
           ffn_w13, ffn_w2, moe_router, moe_w13, moe_w2):
    depth = ada_w.shape[0]
    x = (x_sample.reshape(T_LAT, D), x_prompt.reshape(T_CTX, D))
    conds = jnp.concatenate([c, c_ctx[None, :], jnp.zeros((N_COND - DEC_BATCH - 1, D), F32)], axis=0)
    mods = _ada_call(conds, ada_w, ada_b).reshape(depth, N_COND, 6, D)
    ffn_w13_b, ffn_w2_b = ffn_w13.astype(BF16), ffn_w2.astype(BF16)

    ks, vs, s_re, s_im = [], [], [], []
    for l in range(depth):
        kind, j = l % 3, l // 3
        mod = mods[l]
        if kind == 0:
            qkv = _normproj_call(x, ln1_g[l], mod, na_w_qkv[j].astype(BF16), shift=0, scale=1,
                                 name="qkv_proj")
            qg2 = jnp.tile(na_q_g[j], 2).reshape(1, LANES)
            kg2 = jnp.tile(na_k_g[j], 2).reshape(1, LANES)
            past = cache_na_k.shape[3]
            ctx_k = jnp.transpose(cache_na_k[:, j], (0, 2, 1, 3)).reshape(DEC_BATCH, past, D)
            ctx_v = jnp.transpose(cache_na_v[:, j], (0, 2, 1, 3)).reshape(DEC_BATCH, past, D)
            o_lat = _lat_attn_call(qkv, ctx_k, ctx_v, qg2, kg2, _bias_pairs(na_rpb[j]))
            o_ctx, k_new, v_new = _ctx_attn_call(qkv, qg2, kg2)
            ks.append(k_new)
            vs.append(v_new)
            x = _proj_res_call(o_lat, o_ctx, x, mod, na_w_o[j].astype(BF16), gate=2, name="attn_out")
        elif kind == 1:
            u = _norm_call(x, ln1_g[l], mod, shift=0, scale=1, name="s5_norm")
            bmat, cmat, a_re, a_im = _s5_params(ssm_lam_re[j], ssm_lam_im[j], ssm_log_step[j],
                                                ssm_b_re[j], ssm_b_im[j], ssm_c_re[j], ssm_c_im[j])
            y_lat, _, _ = _s5_call(u, bmat, cmat, a_re, a_im,
                                   _state_to_slabs(state_ssm_re[:, j]), _state_to_slabs(state_ssm_im[:, j]),
                                   row0=0, n_seq=DEC_BATCH, seq_len=DEC_SEQ,
                                   nb=DEC_BATCH, steps=S5_T_LAT, name="s5_lat")
            zero = jnp.zeros((2, S5_NSLAB, BATCH, S5_LANES), F32)
            y_ctx, f_re, f_im = _s5_call(u, bmat, cmat, a_re, a_im, zero, zero,
                                         row0=T_LAT, n_seq=BATCH, seq_len=SEQ, nb=8, steps=SEQ,
                                         name="s5_ctx")
            s_re.append(_slabs_to_state(f_re))
            s_im.append(_slabs_to_state(f_im))
            x = _glu_res_call(u, y_lat, y_ctx, x, mod, ssm_d[j], ssm_w_glu[j].astype(BF16), ssm_b_glu[j])
        else:
            proj = _normproj_call(x, ln1_g[l], mod, cv_w_in[j].astype(BF16), shift=0, scale=1,
                                  name="conv_in")
            x = _conv_call(proj, x, mod, cv_conv_w[j], cv_conv_b[j], cv_w_out[j].astype(BF16))
        jj = l // 2
        if l % 2 == 0:
            x = _ffn_call(x, ln2_g[l], mod, ffn_w13_b, ffn_w2_b, jj)
        else:
            x = _moe_layer(x, ln2_g[l], mod, moe_router[jj], moe_w13, moe_w2, jj)

    y_sample = x[:T_LAT].reshape(DEC_BATCH, DEC_SEQ, D)
    y_prompt = x[T_LAT:].reshape(BATCH, SEQ, D)
    return (y_prompt, y_sample, jnp.stack(ks, axis=1), jnp.stack(vs, axis=1),
            jnp.stack(s_re, axis=1), jnp.stack(s_im, axis=1))
```

```python
import functools
import math

import numpy as np
import jax
import jax.numpy as jnp
from jax import lax
from jax.experimental import pallas as pl
from jax.experimental.pallas import tpu as pltpu

F32 = jnp.float32
BF16 = jnp.bfloat16

D = 1024
BATCH, SEQ = 16, 256
DEC_BATCH, DEC_SEQ = 4, 2048
GRID_W = 64
ROWS = DEC_SEQ // GRID_W
N_HEADS, HEAD_DIM = 16, 64
WIN_ROWS, WIN_COLS = 8, 16
SSM_GROUP, SSM_GROUPS, SSM_STATE = 16, 64, 64
D_FF = 2816
N_EXPERTS = 8
D_FF_EXPERT = 3584
EPS = 1e-6
NEG_INF = -1e30
SCALE = HEAD_DIM ** -0.5

T_LAT = DEC_BATCH * DEC_SEQ
T_CTX = BATCH * SEQ
T = T_LAT + T_CTX
N_COND = 8
CTX_COND = DEC_BATCH

LANES = 128
VMEM_LIMIT = 56 * 1024 * 1024

TM = 512
TM_CONV = 256
TM_MOE = 1024
TF_MOE = 512
HP = N_HEADS // 2
QB_ROWS = 4
S5_SLAB = 128
S5_NSLAB = D // S5_SLAB
S5_LANES = (S5_SLAB // SSM_GROUP) * SSM_STATE
S5_T_LAT = 512
S5_ROW_PAD = 4
S5_UNROLL = 4


def _cparams(sem):
    return pltpu.CompilerParams(dimension_semantics=sem, vmem_limit_bytes=VMEM_LIMIT)


def _cond_of_row(row):
    return jnp.minimum(row // DEC_SEQ, CTX_COND)


def _modnorm(x, g, sc, sh):
    ms = jnp.mean(x * x, axis=-1, keepdims=True)
    y = x * lax.rsqrt(ms + EPS) * g
    return y * (1.0 + sc) + sh


def _sigmoid(x):
    return 1.0 / (1.0 + jnp.exp(-x))


def _silu(x):
    return x * _sigmoid(x)


def _gelu_tanh(x):
    c = math.sqrt(2.0 / math.pi)
    return 0.5 * x * (1.0 + jnp.tanh(c * (x + 0.044715 * (x * x * x))))


def _ada_kernel(c_ref, w_ref, b_ref, o_ref):
    s = _silu(c_ref[...]).astype(BF16)
    o_ref[0] = jnp.dot(s, w_ref[0].astype(BF16), preferred_element_type=F32) + b_ref[0]


def _ada_call(conds, ada_w, ada_b):
    depth = ada_w.shape[0]
    tn = 1536
    return pl.pallas_call(
        _ada_kernel,
        grid=(depth, 6 * D // tn),
        in_specs=[
            pl.BlockSpec((N_COND, D), lambda l, n: (0, 0)),
            pl.BlockSpec((1, D, tn), lambda l, n: (l, 0, n)),
            pl.BlockSpec((1, 1, tn), lambda l, n: (l, 0, n)),
        ],
        out_specs=pl.BlockSpec((1, N_COND, tn), lambda l, n: (l, 0, n)),
        out_shape=jax.ShapeDtypeStruct((depth, N_COND, 6 * D), F32),
        compiler_params=_cparams(("parallel", "parallel")),
        name="ada_mod",
    )(conds, ada_w, ada_b.reshape(depth, 1, 6 * D))


LAT_TILES = T_LAT // TM


def _lat_tile(i):
    return jnp.minimum(i, LAT_TILES - 1)


def _ctx_tile(i):
    return jnp.maximum(i - LAT_TILES, 0)


def _rows_specs(x):
    if isinstance(x, tuple):
        return [pl.BlockSpec((TM, D), lambda i: (_lat_tile(i), 0)),
                pl.BlockSpec((TM, D), lambda i: (_ctx_tile(i), 0))], list(x)
    return [pl.BlockSpec((TM, D), lambda i: (i, 0))], [x]


def _rows_value(refs):
    if len(refs) == 2:
        return jnp.where(pl.program_id(0) < LAT_TILES, refs[0][...], refs[1][...])
    return refs[0][...]


def _normproj_kernel(*refs, shift, scale):
    *x_refs, g_ref, mod_ref, w_ref, o_ref = refs
    mod = mod_ref[0]
    h = _modnorm(_rows_value(x_refs), g_ref[...], mod[scale:scale + 1], mod[shift:shift + 1])
    o_ref[...] = jnp.dot(h.astype(BF16), w_ref[...], preferred_element_type=F32)


def _normproj_call(x, ln_g, mod, w, *, shift, scale, name):
    n = w.shape[1]
    x_specs, x_args = _rows_specs(x)
    return pl.pallas_call(
        functools.partial(_normproj_kernel, shift=shift, scale=scale),
        grid=(T // TM,),
        in_specs=x_specs + [
            pl.BlockSpec((1, D), lambda i: (0, 0)),
            pl.BlockSpec((1, 6, D), lambda i: (_cond_of_row(i * TM), 0, 0)),
            pl.BlockSpec((D, n), lambda i: (0, 0)),
        ],
        out_specs=pl.BlockSpec((TM, n), lambda i: (i, 0)),
        out_shape=jax.ShapeDtypeStruct((T, n), F32),
        compiler_params=_cparams(("parallel",)),
        name=name,
    )(*x_args, ln_g.reshape(1, D), mod, w)


def _norm_kernel(x_ref, g_ref, mod_ref, o_ref, *, shift, scale):
    mod = mod_ref[0]
    o_ref[...] = _modnorm(x_ref[...], g_ref[...], mod[scale:scale + 1], mod[shift:shift + 1])


def _norm_call(x, ln_g, mod, *, shift, scale, name):
    return pl.pallas_call(
        functools.partial(_norm_kernel, shift=shift, scale=scale),
        grid=(T // TM,),
        in_specs=[
            pl.BlockSpec((TM, D), lambda i: (i, 0)),
            pl.BlockSpec((1, D), lambda i: (0, 0)),
            pl.BlockSpec((1, 6, D), lambda i: (_cond_of_row(i * TM), 0, 0)),
        ],
        out_specs=pl.BlockSpec((TM, D), lambda i: (i, 0)),
        out_shape=jax.ShapeDtypeStruct((T, D), F32),
        compiler_params=_cparams(("parallel",)),
        name=name,
    )(x, ln_g.reshape(1, D), mod)


def _proj_res_kernel(*refs, gate, n_x):
    al_ref, ac_ref = refs[:2]
    x_refs = refs[2:2 + n_x]
    mod_ref, w_ref, o_ref = refs[2 + n_x:]
    a = _rows_value((al_ref, ac_ref))
    y = jnp.dot(a.astype(BF16), w_ref[...], preferred_element_type=F32)
    o_ref[...] = _rows_value(x_refs) + mod_ref[0][gate:gate + 1] * y


def _proj_res_call(a_lat, a_ctx, x, mod, w, *, gate, name):
    a_specs, a_args = _rows_specs((a_lat, a_ctx))
    x_specs, x_args = _rows_specs(x)
    return pl.pallas_call(
        functools.partial(_proj_res_kernel, gate=gate, n_x=len(x_args)),
        grid=(T // TM,),
        in_specs=a_specs + x_specs + [
            pl.BlockSpec((1, 6, D), lambda i: (_cond_of_row(i * TM), 0, 0)),
            pl.BlockSpec((D, D), lambda i: (0, 0)),
        ],
        out_specs=pl.BlockSpec((TM, D), lambda i: (i, 0)),
        out_shape=jax.ShapeDtypeStruct((T, D), F32),
        compiler_params=_cparams(("parallel",)),
        name=name,
    )(*a_args, *x_args, mod, w)


def _head_norm(x, g, head0):
    x2 = x * x
    s0 = jnp.sum(jnp.where(head0, x2, 0.0), axis=-1, keepdims=True)
    s1 = jnp.sum(jnp.where(head0, 0.0, x2), axis=-1, keepdims=True)
    ms = jnp.where(head0, s0, s1) * (1.0 / HEAD_DIM)
    return x * lax.rsqrt(ms + EPS) * g


def _nt_dot(a, b):
    return lax.dot_general(a, b, (((1,), (1,)), ((), ())), preferred_element_type=F32)


def _ctx_attn_kernel(q_ref, k_ref, v_ref, qg_ref, kg_ref, o_ref, ko_ref, vo_ref):
    head0 = lax.broadcasted_iota(jnp.int32, (SEQ, LANES), 1) < HEAD_DIM
    for hp in range(HP):
        cols = slice(hp * LANES, (hp + 1) * LANES)
        qn = _head_norm(q_ref[:, cols], qg_ref[...], head0)
        kn = _head_norm(k_ref[:, cols], kg_ref[...], head0)
        v = v_ref[:, cols]
        ko_ref[0, 2 * hp] = kn[:, :HEAD_DIM]
        ko_ref[0, 2 * hp + 1] = kn[:, HEAD_DIM:]
        vo_ref[0, 2 * hp] = v[:, :HEAD_DIM]
        vo_ref[0, 2 * hp + 1] = v[:, HEAD_DIM:]
        knb = kn.astype(BF16)
        vb = v.astype(BF16)
        outs = []
        for hh in range(2):
            qm = jnp.where(head0 if hh == 0 else jnp.logical_not(head0), qn * SCALE, 0.0).astype(BF16)
            s = _nt_dot(qm, knb)
            p = jnp.exp(s - jnp.max(s, axis=-1, keepdims=True))
            l = jnp.sum(p, axis=-1, keepdims=True)
            outs.append(jnp.dot(p.astype(BF16), vb, preferred_element_type=F32) / l)
        o_ref[:, cols] = jnp.where(head0, outs[0], outs[1])


def _ctx_attn_call(qkv, qg2, kg2):
    row0 = T_LAT // SEQ
    blk = lambda c: pl.BlockSpec((SEQ, D), lambda b, c=c: (row0 + b, c))
    kv_out = pl.BlockSpec((1, N_HEADS, SEQ, HEAD_DIM), lambda b: (b, 0, 0, 0))
    return pl.pallas_call(
        _ctx_attn_kernel,
        grid=(BATCH,),
        in_specs=[blk(0), blk(1), blk(2),
                  pl.BlockSpec((1, LANES), lambda b: (0, 0)),
                  pl.BlockSpec((1, LANES), lambda b: (0, 0))],
        out_specs=[pl.BlockSpec((SEQ, D), lambda b: (b, 0)), kv_out, kv_out],
        out_shape=[jax.ShapeDtypeStruct((T_CTX, D), F32),
                   jax.ShapeDtypeStruct((BATCH, N_HEADS, SEQ, HEAD_DIM), F32),
                   jax.ShapeDtypeStruct((BATCH, N_HEADS, SEQ, HEAD_DIM), F32)],
        compiler_params=_cparams(("parallel",)),
        name="ctx_attn",
    )(qkv, qkv, qkv, qg2, kg2)


def _band_of_block(p):
    r_lo, r_hi = QB_ROWS * p, QB_ROWS * p + QB_ROWS - 1
    kr = min(WIN_ROWS, ROWS)
    lo = min(max(r_lo - kr // 2, 0), ROWS - kr)
    hi = min(max(r_hi - kr // 2, 0), ROWS - kr) + kr
    lo -= lo % 2
    hi += hi % 2
    return lo, hi - lo


def _lat_attn_kernel(q_ref, k_ref, v_ref, ck_ref, cv_ref, qg_ref, kg_ref, bp_ref, o_ref,
                     qm0_s, qm1_s, kn_s, v_s):
    head0 = lax.broadcasted_iota(jnp.int32, (DEC_SEQ, LANES), 1) < HEAD_DIM
    qn = _head_norm(q_ref[...], qg_ref[...], head0)
    qm0_s[...] = jnp.where(head0, qn * SCALE, 0.0).astype(BF16)
    qm1_s[...] = jnp.where(head0, 0.0, qn * SCALE).astype(BF16)
    kn_s[...] = _head_norm(k_ref[...], kg_ref[...], head0).astype(BF16)
    v_s[...] = v_ref[...].astype(BF16)
    ckb = ck_ref[0].astype(BF16)
    cvb = cv_ref[0].astype(BF16)

    qc = lax.broadcasted_iota(jnp.int32, (GRID_W, LANES), 0)
    kl = lax.broadcasted_iota(jnp.int32, (GRID_W, LANES), 1)
    kc = jnp.where(kl < GRID_W, kl, kl - GRID_W)
    cs = jnp.clip(qc - WIN_COLS // 2, 0, GRID_W - WIN_COLS)
    col_ok = jnp.logical_and(kc >= cs, kc < cs + WIN_COLS)
    left = kl < GRID_W
    mask_of = {
        (True, True): col_ok,
        (True, False): jnp.logical_and(col_ok, left),
        (False, True): jnp.logical_and(col_ok, jnp.logical_not(left)),
    }
    h0q = lax.broadcasted_iota(jnp.int32, (QB_ROWS * GRID_W, LANES), 1) < HEAD_DIM
    kr_win = min(WIN_ROWS, ROWS)

    for p in range(ROWS // QB_ROWS):
        u0, nrows = _band_of_block(p)
        q_lo = p * QB_ROWS * GRID_W
        kb = kn_s[u0 * GRID_W:(u0 + nrows) * GRID_W, :]
        vb = v_s[u0 * GRID_W:(u0 + nrows) * GRID_W, :]
        outs = []
        for hh in range(2):
            qm = (qm0_s if hh == 0 else qm1_s)[q_lo:q_lo + QB_ROWS * GRID_W, :]
            s_loc = _nt_dot(qm, kb)
            s_ctx = _nt_dot(qm, ckb)
            p_rows, pc_rows, l_rows = [], [], []
            for i in range(QB_ROWS):
                r = p * QB_ROWS + i
                r0 = min(max(r - kr_win // 2, 0), ROWS - kr_win)
                rows = slice(i * GRID_W, (i + 1) * GRID_W)
                sbs = {}
                for m in range(nrows // 2):
                    kr = u0 + 2 * m
                    ok_l = r0 <= kr < r0 + kr_win
                    ok_r = r0 <= kr + 1 < r0 + kr_win
                    if ok_l or ok_r:
                        sb = s_loc[rows, m * LANES:(m + 1) * LANES] + bp_ref[hh, kr - r + WIN_ROWS]
                        sbs[m] = jnp.where(mask_of[(ok_l, ok_r)], sb, NEG_INF)
                sc = s_ctx[rows, :]
                mx = jnp.maximum(
                    jnp.max(functools.reduce(jnp.maximum, sbs.values()), axis=-1, keepdims=True),
                    jnp.max(sc, axis=-1, keepdims=True))
                pbs = {m: jnp.exp(sb - mx) for m, sb in sbs.items()}
                pc = jnp.exp(sc - mx)
                l_rows.append(jnp.sum(functools.reduce(jnp.add, pbs.values()), axis=-1, keepdims=True)
                              + jnp.sum(pc, axis=-1, keepdims=True))
                zero = jnp.zeros((GRID_W, LANES), BF16)
                p_rows.append(jnp.concatenate(
                    [pbs[m].astype(BF16) if m in pbs else zero for m in range(nrows // 2)], axis=1))
                pc_rows.append(pc.astype(BF16))
            o = (jnp.dot(jnp.concatenate(p_rows, axis=0), vb, preferred_element_type=F32)
                 + jnp.dot(jnp.concatenate(pc_rows, axis=0), cvb, preferred_element_type=F32))
            outs.append(o / jnp.concatenate(l_rows, axis=0))
        o_ref[q_lo:q_lo + QB_ROWS * GRID_W, :] = jnp.where(h0q, outs[0], outs[1])


def _lat_attn_call(qkv, ctx_k, ctx_v, qg2, kg2, bias_pairs):
    blk = lambda c: pl.BlockSpec((DEC_SEQ, LANES), lambda hp, b, c=c: (b, c * HP + hp))
    cblk = pl.BlockSpec((1, ctx_k.shape[1], LANES), lambda hp, b: (b, 0, hp))
    return pl.pallas_call(
        _lat_attn_kernel,
        grid=(HP, DEC_BATCH),
        in_specs=[blk(0), blk(1), blk(2), cblk, cblk,
                  pl.BlockSpec((1, LANES), lambda hp, b: (0, 0)),
                  pl.BlockSpec((1, LANES), lambda hp, b: (0, 0)),
                  pl.BlockSpec((2, 2 * WIN_ROWS, GRID_W, LANES), lambda hp, b: (hp, 0, 0, 0))],
        out_specs=pl.BlockSpec((DEC_SEQ, LANES), lambda hp, b: (b, hp)),
        out_shape=jax.ShapeDtypeStruct((T_LAT, D), F32),
        scratch_shapes=[pltpu.VMEM((DEC_SEQ, LANES), BF16)] * 4,
        compiler_params=_cparams(("parallel", "parallel")),
        name="lat_attn",
    )(qkv, qkv, qkv, ctx_k, ctx_v, qg2, kg2, bias_pairs)


def _bias_pairs(rpb):
    cols = np.arange(GRID_W)
    col_idx = np.clip(cols[None, :] - cols[:, None], -(WIN_COLS - 1), WIN_COLS - 1) + WIN_COLS - 1
    rc = rpb[:, :, col_idx]
    left = jnp.pad(rc, ((0, 0), (1, 0), (0, 0), (0, 0)))
    right = jnp.pad(rc, ((0, 0), (0, 1), (0, 0), (0, 0)))
    return jnp.concatenate([left, right], axis=-1)


def _ffn_kernel(x_ref, g_ref, mod_ref, w1_ref, w3_ref, w2_ref, o_ref):
    mod = mod_ref[0]
    x = x_ref[...]
    h = _modnorm(x, g_ref[...], mod[4:5], mod[3:4]).astype(BF16)
    a = _silu(jnp.dot(h, w1_ref[...], preferred_element_type=F32)) \
        * jnp.dot(h, w3_ref[...], preferred_element_type=F32)
    y = jnp.dot(a.astype(BF16), w2_ref[...], preferred_element_type=F32)
    o_ref[...] = x + mod[5:6] * y


def _ffn_call(x, ln_g, mod, w13, w2, layer):
    resident = dict(pipeline_mode=pl.Buffered(1))
    return pl.pallas_call(
        _ffn_kernel,
        grid=(T // TM,),
        in_specs=[
            pl.BlockSpec((TM, D), lambda i: (i, 0)),
            pl.BlockSpec((1, D), lambda i: (0, 0)),
            pl.BlockSpec((1, 6, D), lambda i: (_cond_of_row(i * TM), 0, 0)),
            pl.BlockSpec((None, D, D_FF), lambda i: (layer, 0, 0), **resident),
            pl.BlockSpec((None, D, D_FF), lambda i: (layer, 0, 1), **resident),
            pl.BlockSpec((None, D_FF, D), lambda i: (layer, 0, 0), **resident),
        ],
        out_specs=pl.BlockSpec((TM, D), lambda i: (i, 0)),
        out_shape=jax.ShapeDtypeStruct((T, D), F32),
        compiler_params=_cparams(("parallel",)),
        name="ffn_dense",
    )(x, ln_g.reshape(1, D), mod, w13, w13, w2)


def _s5_kernel(u_ref, bm_ref, cm_ref, ar_ref, ai_ref, hr_ref, hi_ref,
               y_ref, fr_ref, fi_ref, bu_s, st_s, cr_s, ci_s, *, nb, steps):
    d = pl.program_id(0)
    tb = pl.program_id(3)
    rows = nb * steps
    pitch = steps + S5_ROW_PAD
    nch = S5_LANES // LANES
    u = u_ref[...].reshape(rows, S5_SLAB).astype(BF16)
    bu = jnp.dot(u, bm_ref[0, 0], preferred_element_type=F32)
    for c in range(2 * nch):
        for k in range(nb):
            bu_s[c, k * pitch:k * pitch + steps, :] = bu[k * steps:(k + 1) * steps, c * LANES:(c + 1) * LANES]

    @pl.when(tb == 0)
    def _():
        cr_s[...] = hr_ref[0, 0]
        ci_s[...] = hi_ref[0, 0]

    ar = [jnp.broadcast_to(ar_ref[0, 0][:, c * LANES:(c + 1) * LANES], (nb, LANES)) for c in range(nch)]
    ai = [jnp.broadcast_to(ai_ref[0, 0][:, c * LANES:(c + 1) * LANES], (nb, LANES)) for c in range(nch)]

    def step(i, carry):
        l = jnp.where(d == 0, i, steps - 1 - i)
        idx = pl.ds(l, nb, stride=pitch)
        out = []
        for c in range(nch):
            sr, si = carry[2 * c], carry[2 * c + 1]
            nr = ar[c] * sr - ai[c] * si + bu_s[c, idx, :]
            ni = ar[c] * si + ai[c] * sr + bu_s[nch + c, idx, :]
            st_s[c, idx, :] = nr
            st_s[nch + c, idx, :] = ni
            out += [nr, ni]
        return tuple(out)

    init = []
    for c in range(nch):
        init += [cr_s[:, c * LANES:(c + 1) * LANES], ci_s[:, c * LANES:(c + 1) * LANES]]
    fin = lax.fori_loop(0, steps, step, tuple(init), unroll=S5_UNROLL)
    sr = jnp.concatenate([fin[2 * c] for c in range(nch)], axis=1)
    si = jnp.concatenate([fin[2 * c + 1] for c in range(nch)], axis=1)
    cr_s[...] = sr
    ci_s[...] = si
    fr_ref[0, 0] = sr
    fi_ref[0, 0] = si
    states = jnp.concatenate(
        [jnp.concatenate([st_s[c, k * pitch:k * pitch + steps, :] for k in range(nb)], axis=0)
         for c in range(2 * nch)], axis=1).astype(BF16)
    y = jnp.dot(states, cm_ref[0, 0], preferred_element_type=F32)
    y_ref[0] = y.reshape(nb, steps, S5_SLAB)


def _s5_call(u, bmat, cmat, a_re, a_im, h_re, h_im, *, row0, n_seq, seq_len, nb, steps, name):
    n_sg, n_tb = n_seq // nb, seq_len // steps
    sg0 = row0 // (nb * seq_len)

    def tbi(d, tb):
        return jnp.where(d == 0, tb, n_tb - 1 - tb)

    par = lambda last: pl.BlockSpec((1, 1) + last, lambda d, j, sg, tb: (d, j, 0, 0))
    st = pl.BlockSpec((1, 1, nb, S5_LANES), lambda d, j, sg, tb: (d, j, sg, 0))
    y, f_re, f_im = pl.pallas_call(
        functools.partial(_s5_kernel, nb=nb, steps=steps),
        grid=(2, S5_NSLAB, n_sg, n_tb),
        in_specs=[
            pl.BlockSpec((nb, steps, S5_SLAB), lambda d, j, sg, tb: (sg0 + sg, tbi(d, tb), j)),
            par((S5_SLAB, 2 * S5_LANES)),
            par((2 * S5_LANES, S5_SLAB)),
            par((1, S5_LANES)), par((1, S5_LANES)),
            st, st,
        ],
        out_specs=[
            pl.BlockSpec((1, nb, steps, S5_SLAB), lambda d, j, sg, tb: (d, sg, tbi(d, tb), j)),
            st, st,
        ],
        out_shape=[
            jax.ShapeDtypeStruct((2, n_seq, seq_len, D), F32),
            jax.ShapeDtypeStruct((2, S5_NSLAB, n_seq, S5_LANES), F32),
            jax.ShapeDtypeStruct((2, S5_NSLAB, n_seq, S5_LANES), F32),
        ],
        scratch_shapes=[pltpu.VMEM((2 * S5_LANES // LANES, nb * (steps + S5_ROW_PAD), LANES), F32)] * 2
                       + [pltpu.VMEM((nb, S5_LANES), F32)] * 2,
        compiler_params=_cparams(("parallel", "parallel", "parallel", "arbitrary")),
        name=name,
    )(u.reshape(T // seq_len, seq_len, D), bmat, cmat, a_re, a_im, h_re, h_im)
    return y.reshape(2, n_seq * seq_len, D), f_re, f_im


def _s5_params(lam_re, lam_im, log_step, b_re, b_im, c_re, c_im):
    step = jnp.exp(log_step)[..., None]
    zr, zi = lam_re * step, lam_im * step
    mag = jnp.exp(zr)
    a_re, a_im = mag * jnp.cos(zi), mag * jnp.sin(zi)
    nr, ni = a_re - 1.0, a_im
    den = lam_re * lam_re + lam_im * lam_im
    k_re = (nr * lam_re + ni * lam_im) / den
    k_im = (ni * lam_re - nr * lam_im) / den
    bb_re = k_re[..., None] * b_re - k_im[..., None] * b_im
    bb_im = k_re[..., None] * b_im + k_im[..., None] * b_re
    gl = S5_SLAB // SSM_GROUP
    eye = jnp.eye(gl, dtype=F32)

    def bdiag_in(w):
        w = w.reshape(2, S5_NSLAB, gl, SSM_STATE, SSM_GROUP)
        return jnp.einsum('dsgnp,gh->dsgphn', w, eye).reshape(2, S5_NSLAB, S5_SLAB, S5_LANES)

    def bdiag_out(w):
        w = w.reshape(2, S5_NSLAB, gl, SSM_GROUP, SSM_STATE)
        return jnp.einsum('dsgpn,gh->dsgnhp', w, eye).reshape(2, S5_NSLAB, S5_LANES, S5_SLAB)

    bmat = jnp.concatenate([bdiag_in(bb_re), bdiag_in(bb_im)], axis=-1).astype(BF16)
    cmat = jnp.concatenate([bdiag_out(c_re), -bdiag_out(c_im)], axis=-2).astype(BF16)
    slab = lambda a: a.reshape(2, S5_NSLAB, 1, S5_LANES)
    return bmat, cmat, slab(a_re), slab(a_im)


def _state_to_slabs(h):
    b = h.shape[0]
    return jnp.transpose(h.reshape(b, 2, S5_NSLAB, S5_LANES), (1, 2, 0, 3))


def _slabs_to_state(f):
    b = f.shape[2]
    return jnp.transpose(f, (2, 0, 1, 3)).reshape(b, 2, SSM_GROUPS, SSM_STATE)


def _glu_res_kernel(u_ref, yl_ref, yc_ref, x_ref, mod_ref, d_ref, w_ref, b_ref, o_ref):
    is_lat = pl.program_id(0) < LAT_TILES
    y = jnp.where(is_lat, yl_ref[0] + yl_ref[1], yc_ref[0] + yc_ref[1])
    yt = u_ref[...] * d_ref[...] + y
    z = _gelu_tanh(yt)
    gate = _sigmoid(jnp.dot(z.astype(BF16), w_ref[...], preferred_element_type=F32) + b_ref[...])
    o_ref[...] = x_ref[...] + mod_ref[0][2:3] * (z * gate)


def _glu_res_call(u, y_lat, y_ctx, x, mod, d_skip, w_glu, b_glu):
    return pl.pallas_call(
        _glu_res_kernel,
        grid=(T // TM,),
        in_specs=[
            pl.BlockSpec((TM, D), lambda i: (i, 0)),
            pl.BlockSpec((2, TM, D), lambda i: (0, _lat_tile(i), 0)),
            pl.BlockSpec((2, TM, D), lambda i: (0, _ctx_tile(i), 0)),
            pl.BlockSpec((TM, D), lambda i: (i, 0)),
            pl.BlockSpec((1, 6, D), lambda i: (_cond_of_row(i * TM), 0, 0)),
            pl.BlockSpec((1, D), lambda i: (0, 0)),
            pl.BlockSpec((D, D), lambda i: (0, 0)),
            pl.BlockSpec((1, D), lambda i: (0, 0)),
        ],
        out_specs=pl.BlockSpec((TM, D), lambda i: (i, 0)),
        out_shape=jax.ShapeDtypeStruct((T, D), F32),
        compiler_params=_cparams(("parallel",)),
        name="s5_glu_res",
    )(u, y_lat, y_ctx, x, mod, d_skip.reshape(1, D), w_glu, b_glu.reshape(1, D))


def _conv_kernel(bg_ref, cg_ref, xi_ref, cgp_ref, xip_ref, cgn_ref, xin_ref,
                 x_ref, mod_ref, cw_ref, cb_ref, w_ref, o_ref):
    i = pl.program_id(0)
    tiles_per_seq = DEC_SEQ // TM_CONV
    is_lat = i < T_LAT // TM_CONV
    first = jnp.logical_or(jnp.logical_not(is_lat), i % tiles_per_seq == 0)
    last = jnp.logical_or(jnp.logical_not(is_lat), i % tiles_per_seq == tiles_per_seq - 1)
    z = cg_ref[...] * xi_ref[...]
    zp_row = jnp.where(first, 0.0, cgp_ref[7:8, :] * xip_ref[7:8, :])
    zn_row = jnp.where(last, 0.0, cgn_ref[0:1, :] * xin_ref[0:1, :])
    row = lax.broadcasted_iota(jnp.int32, (TM_CONV, D), 0)
    zp = jnp.where(row == 0, zp_row, pltpu.roll(z, 1, axis=0))
    zn = jnp.where(row == TM_CONV - 1, zn_row, pltpu.roll(z, TM_CONV - 1, axis=0))
    cw = cw_ref[...]
    zc = cw[0:1] * zp + cw[1:2] * z + cw[2:3] * zn + cb_ref[...]
    a = (bg_ref[...] * zc).astype(BF16)
    y = jnp.dot(a, w_ref[...], preferred_element_type=F32)
    o_ref[...] = x_ref[...] + mod_ref[0][2:3] * y


def _conv_call(proj, x, mod, conv_w, conv_b, w_out):
    r8 = TM_CONV // 8
    n8 = T // 8
    col = lambda c: pl.BlockSpec((TM_CONV, D), lambda i, c=c: (i, c))
    prev = lambda c: pl.BlockSpec((8, D), lambda i, c=c: (jnp.maximum(i * r8 - 1, 0), c))
    nxt = lambda c: pl.BlockSpec((8, D), lambda i, c=c: (jnp.minimum((i + 1) * r8, n8 - 1), c))
    return pl.pallas_call(
        _conv_kernel,
        grid=(T // TM_CONV,),
        in_specs=[
            col(0), col(1), col(2), prev(1), prev(2), nxt(1), nxt(2),
            pl.BlockSpec((TM_CONV, D), lambda i: (i, 0)),
            pl.BlockSpec((1, 6, D), lambda i: (_cond_of_row(i * TM_CONV), 0, 0)),
            pl.BlockSpec((3, D), lambda i: (0, 0)),
            pl.BlockSpec((1, D), lambda i: (0, 0)),
            pl.BlockSpec((D, D), lambda i: (0, 0)),
        ],
        out_specs=pl.BlockSpec((TM_CONV, D), lambda i: (i, 0)),
        out_shape=jax.ShapeDtypeStruct((T, D), F32),
        compiler_params=_cparams(("parallel",)),
        name="conv_mix",
    )(proj, proj, proj, proj, proj, proj, proj, x, mod, conv_w, conv_b.reshape(1, D), w_out)


def _router_kernel(x_ref, g_ref, mod_ref, r_ref, h_ref, rt_ref):
    mod = mod_ref[0]
    h = _modnorm(x_ref[...], g_ref[...], mod[4:5], mod[3:4])
    h_ref[...] = h
    logits = jnp.dot(h, r_ref[...], preferred_element_type=F32, precision=lax.Precision.HIGHEST)
    lane = lax.broadcasted_iota(jnp.int32, logits.shape, 1)
    logits = jnp.where(lane < N_EXPERTS, logits, -jnp.inf)
    m1 = jnp.max(logits, axis=-1, keepdims=True)
    i1 = jnp.min(jnp.where(logits == m1, lane, LANES), axis=-1, keepdims=True)
    rest = jnp.where(lane == i1, -jnp.inf, logits)
    m2 = jnp.max(rest, axis=-1, keepdims=True)
    i2 = jnp.min(jnp.where(rest == m2, lane, LANES), axis=-1, keepdims=True)
    e2 = jnp.exp(m2 - m1)
    w1 = 1.0 / (1.0 + e2)
    w2 = e2 / (1.0 + e2)
    rt_ref[...] = jnp.where(lane == 0, i1.astype(F32),
                            jnp.where(lane == 1, i2.astype(F32),
                                      jnp.where(lane == 2, w1, jnp.where(lane == 3, w2, 0.0))))


def _router_call(x, ln_g, mod, router_pad):
    return pl.pallas_call(
        _router_kernel,
        grid=(T // TM,),
        in_specs=[
            pl.BlockSpec((TM, D), lambda i: (i, 0)),
            pl.BlockSpec((1, D), lambda i: (0, 0)),
            pl.BlockSpec((1, 6, D), lambda i: (_cond_of_row(i * TM), 0, 0)),
            pl.BlockSpec((D, LANES), lambda i: (0, 0)),
        ],
        out_specs=[pl.BlockSpec((TM, D), lambda i: (i, 0)),
                   pl.BlockSpec((TM, LANES), lambda i: (i, 0))],
        out_shape=[jax.ShapeDtypeStruct((T, D), F32), jax.ShapeDtypeStruct((T, LANES), F32)],
        compiler_params=_cparams(("parallel",)),
        name="moe_router",
    )(x, ln_g.reshape(1, D), mod, router_pad)


R_MAX = 2 * T + N_EXPERTS * TM_MOE
N_TILES = R_MAX // TM_MOE
DMA_UNROLL = 8
GATHER_SHIFT = 3
GATHER_SPREAD = 1 << GATHER_SHIFT


def _row_copy(src_hbm, src_row, dst_vmem, dst_row, sem):
    return pltpu.make_async_copy(src_hbm.at[pl.ds(src_row, 1), :], dst_vmem.at[pl.ds(dst_row, 1), :], sem)


def _gather_kernel(nv_ref, src_ref, h_hbm, o_ref, buf, sem):
    i = pl.program_id(0)
    nv = nv_ref[0]

    def issue_tile(t):
        slot = t % 2
        base = t * TM_MOE

        def issue(j, c):
            r = (j & (GATHER_SPREAD - 1)) * (TM_MOE // GATHER_SPREAD) + lax.shift_right_logical(j, GATHER_SHIFT)
            _row_copy(h_hbm, src_ref[base + r], buf.at[slot], r, sem.at[slot]).start()
            return c

        lax.fori_loop(0, TM_MOE, issue, 0, unroll=DMA_UNROLL)

    @pl.when(jnp.logical_and(i == 0, nv > 0))
    def _():
        issue_tile(0)

    @pl.when(i + 1 < nv)
    def _():
        issue_tile(i + 1)

    @pl.when(i < nv)
    def _():
        slot = i % 2
        pltpu.make_async_copy(h_hbm.at[pl.ds(0, TM_MOE), :], buf.at[slot], sem.at[slot]).wait()
        o_ref[...] = buf[slot].astype(BF16)

    @pl.when(i >= nv)
    def _():
        o_ref[...] = jnp.zeros_like(o_ref)


def _gather_call(n_valid, src_tok, h):
    return pl.pallas_call(
        _gather_kernel,
        grid_spec=pltpu.PrefetchScalarGridSpec(
            num_scalar_prefetch=2,
            grid=(N_TILES,),
            in_specs=[pl.BlockSpec(memory_space=pl.ANY)],
            out_specs=pl.BlockSpec((TM_MOE, D), lambda i, nv, src: (i, 0)),
            scratch_shapes=[pltpu.VMEM((2, TM_MOE, D), F32), pltpu.SemaphoreType.DMA((2,))],
        ),
        out_shape=jax.ShapeDtypeStruct((R_MAX, D), BF16),
        compiler_params=_cparams(("arbitrary",)),
        name="moe_gather",
    )(n_valid, src_tok, h)


def _moe_kernel(nv_ref, te_ref, xs_ref, w1_ref, w3_ref, w2_ref, o_ref, acc_s):
    i = pl.program_id(0)
    f = pl.program_id(1)
    nf = pl.num_programs(1)
    valid = i < nv_ref[0]

    @pl.when(jnp.logical_and(valid, f == 0))
    def _():
        acc_s[...] = jnp.zeros_like(acc_s)

    @pl.when(valid)
    def _():
        xs = xs_ref[...]
        w13 = jnp.concatenate([w1_ref[...].astype(BF16), w3_ref[...].astype(BF16)], axis=1)
        gu = jnp.dot(xs, w13, preferred_element_type=F32)
        a = _silu(gu[:, :TF_MOE]) * gu[:, TF_MOE:]
        acc_s[...] += jnp.dot(a.astype(BF16), w2_ref[...].astype(BF16), preferred_element_type=F32)

    @pl.when(f == nf - 1)
    def _():
        o_ref[...] = jnp.where(valid, acc_s[...], 0.0)


def _moe_call(n_valid, tile_expert, xs, w13, w2, layer):
    nf = D_FF_EXPERT // TF_MOE

    def fe(i, f, nv):
        return jnp.where(i < nv[0], f, nf - 1)

    return pl.pallas_call(
        _moe_kernel,
        grid_spec=pltpu.PrefetchScalarGridSpec(
            num_scalar_prefetch=2,
            grid=(N_TILES, nf),
            in_specs=[
                pl.BlockSpec((TM_MOE, D), lambda i, f, nv, te: (i, 0)),
                pl.BlockSpec((None, None, D, TF_MOE),
                             lambda i, f, nv, te: (layer, te[i], 0, fe(i, f, nv))),
                pl.BlockSpec((None, None, D, TF_MOE),
                             lambda i, f, nv, te: (layer, te[i], 0, nf + fe(i, f, nv))),
                pl.BlockSpec((None, None, TF_MOE, D),
                             lambda i, f, nv, te: (layer, te[i], fe(i, f, nv), 0)),
            ],
            out_specs=pl.BlockSpec((TM_MOE, D), lambda i, f, nv, te: (i, 0)),
            scratch_shapes=[pltpu.VMEM((TM_MOE, D), F32)],
        ),
        out_shape=jax.ShapeDtypeStruct((R_MAX, D), F32),
        compiler_params=_cparams(("arbitrary", "arbitrary")),
        name="moe_experts",
    )(n_valid, tile_expert, xs, w13, w13, w2)


TM_COMB = 512


def _combine_kernel(pa_ref, pb_ref, y_hbm, x_ref, rt_ref, mod_ref, o_ref, buf_a, buf_b, sem):
    i = pl.program_id(0)

    def issue_tile(t):
        slot = t % 2
        base = t * TM_COMB

        def issue(r, c):
            _row_copy(y_hbm, pa_ref[base + r], buf_a.at[slot], r, sem.at[slot]).start()
            _row_copy(y_hbm, pb_ref[base + r], buf_b.at[slot], r, sem.at[slot]).start()
            return c

        lax.fori_loop(0, TM_COMB, issue, 0, unroll=DMA_UNROLL)

    @pl.when(i == 0)
    def _():
        issue_tile(0)

    @pl.when(i + 1 < pl.num_programs(0))
    def _():
        issue_tile(i + 1)

    slot = i % 2
    for buf in (buf_a, buf_b):
        pltpu.make_async_copy(y_hbm.at[pl.ds(0, TM_COMB), :], buf.at[slot], sem.at[slot]).wait()
    rt = rt_ref[...]
    ffn = rt[:, 2:3] * buf_a[slot] + rt[:, 3:4] * buf_b[slot]
    o_ref[...] = x_ref[...] + mod_ref[0][5:6] * ffn


def _combine_call(pos_a, pos_b, y, x, rt, mod):
    return pl.pallas_call(
        _combine_kernel,
        grid_spec=pltpu.PrefetchScalarGridSpec(
            num_scalar_prefetch=2,
            grid=(T // TM_COMB,),
            in_specs=[
                pl.BlockSpec(memory_space=pl.ANY),
                pl.BlockSpec((TM_COMB, D), lambda i, pa, pb: (i, 0)),
                pl.BlockSpec((TM_COMB, LANES), lambda i, pa, pb: (i, 0)),
                pl.BlockSpec((1, 6, D), lambda i, pa, pb: (_cond_of_row(i * TM_COMB), 0, 0)),
            ],
            out_specs=pl.BlockSpec((TM_COMB, D), lambda i, pa, pb: (i, 0)),
            scratch_shapes=[pltpu.VMEM((2, TM_COMB, D), F32), pltpu.VMEM((2, TM_COMB, D), F32),
                            pltpu.SemaphoreType.DMA((2,))],
        ),
        out_shape=jax.ShapeDtypeStruct((T, D), F32),
        compiler_params=_cparams(("arbitrary",)),
        name="moe_combine",
    )(pos_a, pos_b, y, x, rt, mod)


def _routing_tables(rt):
    ea = rt[:, 0:2].astype(jnp.int32).reshape(-1)
    onehot = (ea[:, None] == jnp.arange(N_EXPERTS, dtype=jnp.int32)[None, :]).astype(jnp.int32)
    csum = jnp.cumsum(onehot, axis=0)
    rank = jnp.take_along_axis(csum, ea[:, None], axis=1)[:, 0] - 1
    counts = csum[-1]
    padded = ((counts + TM_MOE - 1) // TM_MOE) * TM_MOE
    ends = jnp.cumsum(padded)
    pos = (ends - padded)[ea] + rank
    n_valid = (ends[-1] // TM_MOE).astype(jnp.int32).reshape(1)
    tile_start = jnp.arange(N_TILES, dtype=jnp.int32) * TM_MOE
    expert_of_row = lambda r: jnp.sum((r[:, None] >= ends[None, :]).astype(jnp.int32), axis=1)
    tile_expert = jnp.minimum(expert_of_row(tile_start), expert_of_row(ends[-1:] - 1))
    tok = jnp.arange(2 * T, dtype=jnp.int32) // 2
    src_tok = (jnp.arange(R_MAX, dtype=jnp.int32) % T).at[pos].set(tok)
    pos2 = pos.reshape(T, 2).astype(jnp.int32)
    return n_valid, tile_expert, src_tok, pos2[:, 0], pos2[:, 1]


def _moe_layer(x, ln_g, mod, router, w13, w2, layer):
    router_pad = jnp.pad(router, ((0, 0), (0, LANES - N_EXPERTS)))
    h, rt = _router_call(x, ln_g, mod, router_pad)
    n_valid, tile_expert, src_tok, pos_a, pos_b = _routing_tables(rt)
    xs = _gather_call(n_valid, src_tok, h)
    y = _moe_call(n_valid, tile_expert, xs, w13, w2, layer)
    return _combine_call(pos_a, pos_b, y, x, rt, mod)


def kernel(x_prompt, x_sample, c, cache_na_k, cache_na_v, state_ssm_re, state_ssm_im, c_ctx,
           ln1_g, ln2_g, ada_w, ada_b, na_w_qkv, na_w_o, na_q_g, na_k_g, na_rpb,
           ssm_lam_re, ssm_lam_im, ssm_log_step, ssm_b_re, ssm_b_im, ssm_c_re, ssm_c_im,
           ssm_d, ssm_w_glu, ssm_b_glu, cv_w_in, cv_conv_w, cv_conv_b, cv_w_out,
           ffn_w13, ffn_w2, moe_router, moe_w13, moe_w2):
    depth = ada_w.shape[0]
    x = (x_sample.reshape(T_LAT, D), x_prompt.reshape(T_CTX, D))
    conds = jnp.concatenate([c, c_ctx[None, :], jnp.zeros((N_COND - DEC_BATCH - 1, D), F32)], axis=0)
    mods = _ada_call(conds, ada_w, ada_b).reshape(depth, N_COND, 6, D)
    ffn_w13_b, ffn_w2_b = ffn_w13.astype(BF16), ffn_w2.astype(BF16)

    ks, vs, s_re, s_im = [], [], [], []
    for l in range(depth):
        kind, j = l % 3, l // 3
        mod = mods[l]
        if kind == 0:
            qkv = _normproj_call(x, ln1_g[l], mod, na_w_qkv[j].astype(BF16), shift=0, scale=1,
                                 name="qkv_proj")
            qg2 = jnp.tile(na_q_g[j], 2).reshape(1, LANES)
            kg2 = jnp.tile(na_k_g[j], 2).reshape(1, LANES)
            past = cache_na_k.shape[3]
            ctx_k = jnp.transpose(cache_na_k[:, j], (0, 2, 1, 3)).reshape(DEC_BATCH, past, D)
            ctx_v = jnp.transpose(cache_na_v[:, j], (0, 2, 1, 3)).reshape(DEC_BATCH, past, D)
            o_lat = _lat_attn_call(qkv, ctx_k, ctx_v, qg2, kg2, _bias_pairs(na_rpb[j]))
            o_ctx, k_new, v_new = _ctx_attn_call(qkv, qg2, kg2)
            ks.append(k_new)
            vs.append(v_new)
            x = _proj_res_call(o_lat, o_ctx, x, mod, na_w_o[j].astype(BF16), gate=2, name="attn_out")
        elif kind == 1:
            u = _norm_call(x, ln1_g[l], mod, shift=0, scale=1, name="s5_norm")
            bmat, cmat, a_re, a_im = _s5_params(ssm_lam_re[j], ssm_lam_im[j], ssm_log_step[j],
                                                ssm_b_re[j], ssm_b_im[j], ssm_c_re[j], ssm_c_im[j])
            y_lat, _, _ = _s5_call(u, bmat, cmat, a_re, a_im,
                                   _state_to_slabs(state_ssm_re[:, j]), _state_to_slabs(state_ssm_im[:, j]),
                                   row0=0, n_seq=DEC_BATCH, seq_len=DEC_SEQ,
                                   nb=DEC_BATCH, steps=S5_T_LAT, name="s5_lat")
            zero = jnp.zeros((2, S5_NSLAB, BATCH, S5_LANES), F32)
            y_ctx, f_re, f_im = _s5_call(u, bmat, cmat, a_re, a_im, zero, zero,
                                         row0=T_LAT, n_seq=BATCH, seq_len=SEQ, nb=8, steps=SEQ,
                                         name="s5_ctx")
            s_re.append(_slabs_to_state(f_re))
            s_im.append(_slabs_to_state(f_im))
            x = _glu_res_call(u, y_lat, y_ctx, x, mod, ssm_d[j], ssm_w_glu[j].astype(BF16), ssm_b_glu[j])
        else:
            proj = _normproj_call(x, ln1_g[l], mod, cv_w_in[j].astype(BF16), shift=0, scale=1,
                                  name="conv_in")
            x = _conv_call(proj, x, mod, cv_conv_w[j], cv_conv_b[j], cv_w_out[j].astype(BF16))
        jj = l // 2
        if l % 2 == 0:
            x = _ffn_call(x, ln2_g[l], mod, ffn_w13_b, ffn_w2_b, jj)
        else:
            x = _moe_layer(x, ln2_g[l], mod, moe_router[jj], moe_w13, moe_w2, jj)

    y_sample = x[:T_LAT].reshape(DEC_BATCH, DEC_SEQ, D)
    y_prompt = x[T_LAT:].reshape(BATCH, SEQ, D)
    return (y_prompt, y_sample, jnp.stack(ks, axis=1), jnp.stack(vs, axis=1),
            jnp.stack(s_re, axis=1), jnp.stack(s_im, axis=1))
```

```python
import functools
import math

import numpy as np
import jax
import jax.numpy as jnp
from jax import lax
from jax.experimental import pallas as pl
from jax.experimental.pallas import tpu as pltpu

F32 = jnp.float32
BF16 = jnp.bfloat16

D = 1024
BATCH, SEQ = 16, 256
DEC_BATCH, DEC_SEQ = 4, 2048
GRID_W = 64
ROWS = DEC_SEQ // GRID_W
N_HEADS, HEAD_DIM = 16, 64
WIN_ROWS, WIN_COLS = 8, 16
SSM_GROUP, SSM_GROUPS, SSM_STATE = 16, 64, 64
D_FF = 2816
N_EXPERTS = 8
D_FF_EXPERT = 3584
EPS = 1e-6
NEG_INF = -1e30
SCALE = HEAD_DIM ** -0.5

T_LAT = DEC_BATCH * DEC_SEQ
T_CTX = BATCH * SEQ
T = T_LAT + T_CTX
N_COND = 8
CTX_COND = DEC_BATCH

LANES = 128
VMEM_LIMIT = 56 * 1024 * 1024

TM = 512
TM_CONV = 256
TM_MOE = 1024
TF_MOE = 512
HP = N_HEADS // 2
QB_ROWS = 4
S5_SLAB = 128
S5_NSLAB = D // S5_SLAB
S5_LANES = (S5_SLAB // SSM_GROUP) * SSM_STATE
S5_T_LAT = 512
S5_ROW_PAD = 4
S5_UNROLL = 4


def _cparams(sem):
    return pltpu.CompilerParams(dimension_semantics=sem, vmem_limit_bytes=VMEM_LIMIT)


def _cond_of_row(row):
    return jnp.minimum(row // DEC_SEQ, CTX_COND)


def _modnorm(x, g, sc, sh):
    ms = jnp.mean(x * x, axis=-1, keepdims=True)
    y = x * lax.rsqrt(ms + EPS) * g
    return y * (1.0 + sc) + sh


def _sigmoid(x):
    return 1.0 / (1.0 + jnp.exp(-x))


def _silu(x):
    return x * _sigmoid(x)


def _gelu_tanh(x):
    c = math.sqrt(2.0 / math.pi)
    return 0.5 * x * (1.0 + jnp.tanh(c * (x + 0.044715 * (x * x * x))))


def _ada_kernel(c_ref, w_ref, b_ref, o_ref):
    s = _silu(c_ref[...]).astype(BF16)
    o_ref[0] = jnp.dot(s, w_ref[0].astype(BF16), preferred_element_type=F32) + b_ref[0]


def _ada_call(conds, ada_w, ada_b):
    depth = ada_w.shape[0]
    tn = 1536
    return pl.pallas_call(
        _ada_kernel,
        grid=(depth, 6 * D // tn),
        in_specs=[
            pl.BlockSpec((N_COND, D), lambda l, n: (0, 0)),
            pl.BlockSpec((1, D, tn), lambda l, n: (l, 0, n)),
            pl.BlockSpec((1, 1, tn), lambda l, n: (l, 0, n)),
        ],
        out_specs=pl.BlockSpec((1, N_COND, tn), lambda l, n: (l, 0, n)),
        out_shape=jax.ShapeDtypeStruct((depth, N_COND, 6 * D), F32),
        compiler_params=_cparams(("parallel", "parallel")),
        name="ada_mod",
    )(conds, ada_w, ada_b.reshape(depth, 1, 6 * D))


LAT_TILES = T_LAT // TM


def _lat_tile(i):
    return jnp.minimum(i, LAT_TILES - 1)


def _ctx_tile(i):
    return jnp.maximum(i - LAT_TILES, 0)


def _rows_specs(x):
    if isinstance(x, tuple):
        return [pl.BlockSpec((TM, D), lambda i: (_lat_tile(i), 0)),
                pl.BlockSpec((TM, D), lambda i: (_ctx_tile(i), 0))], list(x)
    return [pl.BlockSpec((TM, D), lambda i: (i, 0))], [x]


def _rows_value(refs):
    if len(refs) == 2:
        return jnp.where(pl.program_id(0) < LAT_TILES, refs[0][...], refs[1][...])
    return refs[0][...]


def _store_slabs(o_ref, val):
    for c in range(val.shape[1] // LANES):
        o_ref[c] = val[:, c * LANES:(c + 1) * LANES]


def _load_slabs(ref):
    return jnp.concatenate([ref[c] for c in range(ref.shape[0])], axis=1)


def _normproj_kernel(*refs, shift, scale, slab_out):
    *x_refs, g_ref, mod_ref, w_ref, o_ref = refs
    mod = mod_ref[0]
    h = _modnorm(_rows_value(x_refs), g_ref[...], mod[scale:scale + 1], mod[shift:shift + 1])
    res = jnp.dot(h.astype(BF16), w_ref[...], preferred_element_type=F32)
    if slab_out:
        _store_slabs(o_ref, res)
    else:
        o_ref[...] = res


def _normproj_call(x, ln_g, mod, w, *, shift, scale, name, slab_out=False):
    n = w.shape[1]
    x_specs, x_args = _rows_specs(x)
    if slab_out:
        out_spec = pl.BlockSpec((n // LANES, TM, LANES), lambda i: (0, i, 0))
        out_shape = jax.ShapeDtypeStruct((n // LANES, T, LANES), F32)
    else:
        out_spec = pl.BlockSpec((TM, n), lambda i: (i, 0))
        out_shape = jax.ShapeDtypeStruct((T, n), F32)
    return pl.pallas_call(
        functools.partial(_normproj_kernel, shift=shift, scale=scale, slab_out=slab_out),
        grid=(T // TM,),
        in_specs=x_specs + [
            pl.BlockSpec((1, D), lambda i: (0, 0)),
            pl.BlockSpec((1, 6, D), lambda i: (_cond_of_row(i * TM), 0, 0)),
            pl.BlockSpec((D, n), lambda i: (0, 0)),
        ],
        out_specs=out_spec,
        out_shape=out_shape,
        compiler_params=_cparams(("parallel",)),
        name=name,
    )(*x_args, ln_g.reshape(1, D), mod, w)


def _norm_kernel(x_ref, g_ref, mod_ref, o_ref, *, shift, scale):
    mod = mod_ref[0]
    _store_slabs(o_ref, _modnorm(x_ref[...], g_ref[...], mod[scale:scale + 1], mod[shift:shift + 1]))


def _norm_call(x, ln_g, mod, *, shift, scale, name):
    return pl.pallas_call(
        functools.partial(_norm_kernel, shift=shift, scale=scale),
        grid=(T // TM,),
        in_specs=[
            pl.BlockSpec((TM, D), lambda i: (i, 0)),
            pl.BlockSpec((1, D), lambda i: (0, 0)),
            pl.BlockSpec((1, 6, D), lambda i: (_cond_of_row(i * TM), 0, 0)),
        ],
        out_specs=pl.BlockSpec((D // LANES, TM, LANES), lambda i: (0, i, 0)),
        out_shape=jax.ShapeDtypeStruct((D // LANES, T, LANES), F32),
        compiler_params=_cparams(("parallel",)),
        name=name,
    )(x, ln_g.reshape(1, D), mod)


def _proj_res_kernel(*refs, gate, n_x):
    al_ref, ac_ref = refs[:2]
    x_refs = refs[2:2 + n_x]
    mod_ref, w_ref, o_ref = refs[2 + n_x:]
    a = jnp.where(pl.program_id(0) < LAT_TILES, _load_slabs(al_ref), _load_slabs(ac_ref))
    y = jnp.dot(a.astype(BF16), w_ref[...], preferred_element_type=F32)
    o_ref[...] = _rows_value(x_refs) + mod_ref[0][gate:gate + 1] * y


def _proj_res_call(a_lat, a_ctx, x, mod, w, *, gate, name):
    ns = D // LANES
    a_specs = [pl.BlockSpec((ns, TM, LANES), lambda i: (0, _lat_tile(i), 0)),
               pl.BlockSpec((ns, TM, LANES), lambda i: (0, _ctx_tile(i), 0))]
    a_args = [a_lat, a_ctx]
    x_specs, x_args = _rows_specs(x)
    return pl.pallas_call(
        functools.partial(_proj_res_kernel, gate=gate, n_x=len(x_args)),
        grid=(T // TM,),
        in_specs=a_specs + x_specs + [
            pl.BlockSpec((1, 6, D), lambda i: (_cond_of_row(i * TM), 0, 0)),
            pl.BlockSpec((D, D), lambda i: (0, 0)),
        ],
        out_specs=pl.BlockSpec((TM, D), lambda i: (i, 0)),
        out_shape=jax.ShapeDtypeStruct((T, D), F32),
        compiler_params=_cparams(("parallel",)),
        name=name,
    )(*a_args, *x_args, mod, w)


def _head_norm(x, g, head0):
    x2 = x * x
    s0 = jnp.sum(jnp.where(head0, x2, 0.0), axis=-1, keepdims=True)
    s1 = jnp.sum(jnp.where(head0, 0.0, x2), axis=-1, keepdims=True)
    ms = jnp.where(head0, s0, s1) * (1.0 / HEAD_DIM)
    return x * lax.rsqrt(ms + EPS) * g


def _nt_dot(a, b):
    return lax.dot_general(a, b, (((1,), (1,)), ((), ())), preferred_element_type=F32)


def _ctx_attn_kernel(q_ref, k_ref, v_ref, qg_ref, kg_ref, o_ref, ko_ref, vo_ref):
    head0 = lax.broadcasted_iota(jnp.int32, (SEQ, LANES), 1) < HEAD_DIM
    for hp in range(HP):
        qn = _head_norm(q_ref[hp], qg_ref[...], head0)
        kn = _head_norm(k_ref[hp], kg_ref[...], head0)
        v = v_ref[hp]
        ko_ref[0, 2 * hp] = kn[:, :HEAD_DIM]
        ko_ref[0, 2 * hp + 1] = kn[:, HEAD_DIM:]
        vo_ref[0, 2 * hp] = v[:, :HEAD_DIM]
        vo_ref[0, 2 * hp + 1] = v[:, HEAD_DIM:]
        knb = kn.astype(BF16)
        vb = v.astype(BF16)
        outs = []
        for hh in range(2):
            qm = jnp.where(head0 if hh == 0 else jnp.logical_not(head0), qn * SCALE, 0.0).astype(BF16)
            s = _nt_dot(qm, knb)
            p = jnp.exp(s - jnp.max(s, axis=-1, keepdims=True))
            l = jnp.sum(p, axis=-1, keepdims=True)
            outs.append(jnp.dot(p.astype(BF16), vb, preferred_element_type=F32) / l)
        o_ref[hp] = jnp.where(head0, outs[0], outs[1])


def _ctx_attn_call(qkv, qg2, kg2):
    row0 = T_LAT // SEQ
    blk = lambda c: pl.BlockSpec((HP, SEQ, LANES), lambda b, c=c: (c, row0 + b, 0))
    kv_out = pl.BlockSpec((1, N_HEADS, SEQ, HEAD_DIM), lambda b: (b, 0, 0, 0))
    return pl.pallas_call(
        _ctx_attn_kernel,
        grid=(BATCH,),
        in_specs=[blk(0), blk(1), blk(2),
                  pl.BlockSpec((1, LANES), lambda b: (0, 0)),
                  pl.BlockSpec((1, LANES), lambda b: (0, 0))],
        out_specs=[pl.BlockSpec((HP, SEQ, LANES), lambda b: (0, b, 0)), kv_out, kv_out],
        out_shape=[jax.ShapeDtypeStruct((HP, T_CTX, LANES), F32),
                   jax.ShapeDtypeStruct((BATCH, N_HEADS, SEQ, HEAD_DIM), F32),
                   jax.ShapeDtypeStruct((BATCH, N_HEADS, SEQ, HEAD_DIM), F32)],
        compiler_params=_cparams(("parallel",)),
        name="ctx_attn",
    )(qkv, qkv, qkv, qg2, kg2)


def _band_of_block(p):
    r_lo, r_hi = QB_ROWS * p, QB_ROWS * p + QB_ROWS - 1
    kr = min(WIN_ROWS, ROWS)
    lo = min(max(r_lo - kr // 2, 0), ROWS - kr)
    hi = min(max(r_hi - kr // 2, 0), ROWS - kr) + kr
    lo -= lo % 2
    hi += hi % 2
    return lo, hi - lo


def _lat_attn_kernel(q_ref, k_ref, v_ref, ck_ref, cv_ref, qg_ref, kg_ref, bp_ref, o_ref,
                     qm0_s, qm1_s, kn_s, v_s):
    head0 = lax.broadcasted_iota(jnp.int32, (DEC_SEQ, LANES), 1) < HEAD_DIM
    qn = _head_norm(q_ref[...], qg_ref[...], head0)
    qm0_s[...] = jnp.where(head0, qn * SCALE, 0.0).astype(BF16)
    qm1_s[...] = jnp.where(head0, 0.0, qn * SCALE).astype(BF16)
    kn_s[...] = _head_norm(k_ref[...], kg_ref[...], head0).astype(BF16)
    v_s[...] = v_ref[...].astype(BF16)
    ckb = ck_ref[...].astype(BF16)
    cvb = cv_ref[...].astype(BF16)

    qc = lax.broadcasted_iota(jnp.int32, (GRID_W, LANES), 0)
    kl = lax.broadcasted_iota(jnp.int32, (GRID_W, LANES), 1)
    kc = jnp.where(kl < GRID_W, kl, kl - GRID_W)
    cs = jnp.clip(qc - WIN_COLS // 2, 0, GRID_W - WIN_COLS)
    col_ok = jnp.logical_and(kc >= cs, kc < cs + WIN_COLS)
    left = kl < GRID_W
    mask_of = {
        (True, True): col_ok,
        (True, False): jnp.logical_and(col_ok, left),
        (False, True): jnp.logical_and(col_ok, jnp.logical_not(left)),
    }
    h0q = lax.broadcasted_iota(jnp.int32, (QB_ROWS * GRID_W, LANES), 1) < HEAD_DIM
    kr_win = min(WIN_ROWS, ROWS)

    for p in range(ROWS // QB_ROWS):
        u0, nrows = _band_of_block(p)
        q_lo = p * QB_ROWS * GRID_W
        kb = kn_s[u0 * GRID_W:(u0 + nrows) * GRID_W, :]
        vb = v_s[u0 * GRID_W:(u0 + nrows) * GRID_W, :]
        outs = []
        for hh in range(2):
            qm = (qm0_s if hh == 0 else qm1_s)[q_lo:q_lo + QB_ROWS * GRID_W, :]
            s_loc = _nt_dot(qm, kb)
            s_ctx = _nt_dot(qm, ckb)
            p_rows, pc_rows, l_rows = [], [], []
            for i in range(QB_ROWS):
                r = p * QB_ROWS + i
                r0 = min(max(r - kr_win // 2, 0), ROWS - kr_win)
                rows = slice(i * GRID_W, (i + 1) * GRID_W)
                sbs = {}
                for m in range(nrows // 2):
                    kr = u0 + 2 * m
                    ok_l = r0 <= kr < r0 + kr_win
                    ok_r = r0 <= kr + 1 < r0 + kr_win
                    if ok_l or ok_r:
                        sb = s_loc[rows, m * LANES:(m + 1) * LANES] + bp_ref[hh, kr - r + WIN_ROWS]
                        sbs[m] = jnp.where(mask_of[(ok_l, ok_r)], sb, NEG_INF)
                sc = s_ctx[rows, :]
                mx = jnp.maximum(
                    jnp.max(functools.reduce(jnp.maximum, sbs.values()), axis=-1, keepdims=True),
                    jnp.max(sc, axis=-1, keepdims=True))
                pbs = {m: jnp.exp(sb - mx) for m, sb in sbs.items()}
                pc = jnp.exp(sc - mx)
                l_rows.append(jnp.sum(functools.reduce(jnp.add, pbs.values()), axis=-1, keepdims=True)
                              + jnp.sum(pc, axis=-1, keepdims=True))
                zero = jnp.zeros((GRID_W, LANES), BF16)
                p_rows.append(jnp.concatenate(
                    [pbs[m].astype(BF16) if m in pbs else zero for m in range(nrows // 2)], axis=1))
                pc_rows.append(pc.astype(BF16))
            o = (jnp.dot(jnp.concatenate(p_rows, axis=0), vb, preferred_element_type=F32)
                 + jnp.dot(jnp.concatenate(pc_rows, axis=0), cvb, preferred_element_type=F32))
            outs.append(o / jnp.concatenate(l_rows, axis=0))
        o_ref[q_lo:q_lo + QB_ROWS * GRID_W, :] = jnp.where(h0q, outs[0], outs[1])


def _lat_attn_call(qkv, ctx_k, ctx_v, qg2, kg2, bias_pairs):
    blk = lambda c: pl.BlockSpec((None, DEC_SEQ, LANES), lambda hp, b, c=c: (c * HP + hp, b, 0))
    cblk = pl.BlockSpec((None, None, ctx_k.shape[2], LANES), lambda hp, b: (b, hp, 0, 0))
    return pl.pallas_call(
        _lat_attn_kernel,
        grid=(HP, DEC_BATCH),
        in_specs=[blk(0), blk(1), blk(2), cblk, cblk,
                  pl.BlockSpec((1, LANES), lambda hp, b: (0, 0)),
                  pl.BlockSpec((1, LANES), lambda hp, b: (0, 0)),
                  pl.BlockSpec((2, 2 * WIN_ROWS, GRID_W, LANES), lambda hp, b: (hp, 0, 0, 0))],
        out_specs=pl.BlockSpec((None, DEC_SEQ, LANES), lambda hp, b: (hp, b, 0)),
        out_shape=jax.ShapeDtypeStruct((HP, T_LAT, LANES), F32),
        scratch_shapes=[pltpu.VMEM((DEC_SEQ, LANES), BF16)] * 4,
        compiler_params=_cparams(("parallel", "parallel")),
        name="lat_attn",
    )(qkv, qkv, qkv, ctx_k, ctx_v, qg2, kg2, bias_pairs)


def _bias_pairs(rpb):
    cols = np.arange(GRID_W)
    col_idx = np.clip(cols[None, :] - cols[:, None], -(WIN_COLS - 1), WIN_COLS - 1) + WIN_COLS - 1
    rc = rpb[:, :, col_idx]
    left = jnp.pad(rc, ((0, 0), (1, 0), (0, 0), (0, 0)))
    right = jnp.pad(rc, ((0, 0), (0, 1), (0, 0), (0, 0)))
    return jnp.concatenate([left, right], axis=-1)


def _ffn_kernel(x_ref, g_ref, mod_ref, w1_ref, w3_ref, w2_ref, o_ref):
    mod = mod_ref[0]
    x = x_ref[...]
    h = _modnorm(x, g_ref[...], mod[4:5], mod[3:4]).astype(BF16)
    a = _silu(jnp.dot(h, w1_ref[...], preferred_element_type=F32)) \
        * jnp.dot(h, w3_ref[...], preferred_element_type=F32)
    y = jnp.dot(a.astype(BF16), w2_ref[...], preferred_element_type=F32)
    o_ref[...] = x + mod[5:6] * y


def _ffn_call(x, ln_g, mod, w13, w2, layer):
    resident = dict(pipeline_mode=pl.Buffered(1))
    return pl.pallas_call(
        _ffn_kernel,
        grid=(T // TM,),
        in_specs=[
            pl.BlockSpec((TM, D), lambda i: (i, 0)),
            pl.BlockSpec((1, D), lambda i: (0, 0)),
            pl.BlockSpec((1, 6, D), lambda i: (_cond_of_row(i * TM), 0, 0)),
            pl.BlockSpec((None, D, D_FF), lambda i: (layer, 0, 0), **resident),
            pl.BlockSpec((None, D, D_FF), lambda i: (layer, 0, 1), **resident),
            pl.BlockSpec((None, D_FF, D), lambda i: (layer, 0, 0), **resident),
        ],
        out_specs=pl.BlockSpec((TM, D), lambda i: (i, 0)),
        out_shape=jax.ShapeDtypeStruct((T, D), F32),
        compiler_params=_cparams(("parallel",)),
        name="ffn_dense",
    )(x, ln_g.reshape(1, D), mod, w13, w13, w2)


def _s5_kernel(u_ref, bm_ref, cm_ref, ar_ref, ai_ref, hr_ref, hi_ref,
               y_ref, fr_ref, fi_ref, bu_s, st_s, cr_s, ci_s, *, nb, steps):
    d = pl.program_id(0)
    tb = pl.program_id(3)
    rows = nb * steps
    pitch = steps + S5_ROW_PAD
    nch = S5_LANES // LANES
    u = u_ref[...].reshape(rows, S5_SLAB).astype(BF16)
    bu = jnp.dot(u, bm_ref[0, 0], preferred_element_type=F32)
    for c in range(2 * nch):
        for k in range(nb):
            bu_s[c, k * pitch:k * pitch + steps, :] = bu[k * steps:(k + 1) * steps, c * LANES:(c + 1) * LANES]

    @pl.when(tb == 0)
    def _():
        cr_s[...] = hr_ref[0, 0]
        ci_s[...] = hi_ref[0, 0]

    ar = [jnp.broadcast_to(ar_ref[0, 0][:, c * LANES:(c + 1) * LANES], (nb, LANES)) for c in range(nch)]
    ai = [jnp.broadcast_to(ai_ref[0, 0][:, c * LANES:(c + 1) * LANES], (nb, LANES)) for c in range(nch)]

    def step(i, carry):
        l = jnp.where(d == 0, i, steps - 1 - i)
        idx = pl.ds(l, nb, stride=pitch)
        out = []
        for c in range(nch):
            sr, si = carry[2 * c], carry[2 * c + 1]
            nr = ar[c] * sr - ai[c] * si + bu_s[c, idx, :]
            ni = ar[c] * si + ai[c] * sr + bu_s[nch + c, idx, :]
            st_s[c, idx, :] = nr
            st_s[nch + c, idx, :] = ni
            out += [nr, ni]
        return tuple(out)

    init = []
    for c in range(nch):
        init += [cr_s[:, c * LANES:(c + 1) * LANES], ci_s[:, c * LANES:(c + 1) * LANES]]
    fin = lax.fori_loop(0, steps, step, tuple(init), unroll=S5_UNROLL)
    sr = jnp.concatenate([fin[2 * c] for c in range(nch)], axis=1)
    si = jnp.concatenate([fin[2 * c + 1] for c in range(nch)], axis=1)
    cr_s[...] = sr
    ci_s[...] = si
    fr_ref[0, 0] = sr
    fi_ref[0, 0] = si
    states = jnp.concatenate(
        [jnp.concatenate([st_s[c, k * pitch:k * pitch + steps, :] for k in range(nb)], axis=0)
         for c in range(2 * nch)], axis=1).astype(BF16)
    y = jnp.dot(states, cm_ref[0, 0], preferred_element_type=F32)
    y_ref[...] = y.reshape(nb, steps, S5_SLAB)


def _s5_call(u, bmat, cmat, a_re, a_im, h_re, h_im, *, row0, n_seq, seq_len, nb, steps, name):
    n_sg, n_tb = n_seq // nb, seq_len // steps
    sg0 = row0 // (nb * seq_len)

    def tbi(d, tb):
        return jnp.where(d == 0, tb, n_tb - 1 - tb)

    par = lambda last: pl.BlockSpec((1, 1) + last, lambda d, j, sg, tb: (d, j, 0, 0))
    st = pl.BlockSpec((1, 1, nb, S5_LANES), lambda d, j, sg, tb: (d, j, sg, 0))
    y, f_re, f_im = pl.pallas_call(
        functools.partial(_s5_kernel, nb=nb, steps=steps),
        grid=(2, S5_NSLAB, n_sg, n_tb),
        in_specs=[
            pl.BlockSpec((None, nb, steps, S5_SLAB), lambda d, j, sg, tb: (j, sg0 + sg, tbi(d, tb), 0)),
            par((S5_SLAB, 2 * S5_LANES)),
            par((2 * S5_LANES, S5_SLAB)),
            par((1, S5_LANES)), par((1, S5_LANES)),
            st, st,
        ],
        out_specs=[
            pl.BlockSpec((None, None, nb, steps, S5_SLAB), lambda d, j, sg, tb: (d, j, sg, tbi(d, tb), 0)),
            st, st,
        ],
        out_shape=[
            jax.ShapeDtypeStruct((2, S5_NSLAB, n_seq, seq_len, S5_SLAB), F32),
            jax.ShapeDtypeStruct((2, S5_NSLAB, n_seq, S5_LANES), F32),
            jax.ShapeDtypeStruct((2, S5_NSLAB, n_seq, S5_LANES), F32),
        ],
        scratch_shapes=[pltpu.VMEM((2 * S5_LANES // LANES, nb * (steps + S5_ROW_PAD), LANES), F32)] * 2
                       + [pltpu.VMEM((nb, S5_LANES), F32)] * 2,
        compiler_params=_cparams(("parallel", "parallel", "parallel", "arbitrary")),
        name=name,
    )(u.reshape(S5_NSLAB, T // seq_len, seq_len, S5_SLAB), bmat, cmat, a_re, a_im, h_re, h_im)
    return y.reshape(2, S5_NSLAB, n_seq * seq_len, S5_SLAB), f_re, f_im


def _s5_params(lam_re, lam_im, log_step, b_re, b_im, c_re, c_im):
    step = jnp.exp(log_step)[..., None]
    zr, zi = lam_re * step, lam_im * step
    mag = jnp.exp(zr)
    a_re, a_im = mag * jnp.cos(zi), mag * jnp.sin(zi)
    nr, ni = a_re - 1.0, a_im
    den = lam_re * lam_re + lam_im * lam_im
    k_re = (nr * lam_re + ni * lam_im) / den
    k_im = (ni * lam_re - nr * lam_im) / den
    bb_re = k_re[..., None] * b_re - k_im[..., None] * b_im
    bb_im = k_re[..., None] * b_im + k_im[..., None] * b_re
    gl = S5_SLAB // SSM_GROUP
    eye = jnp.eye(gl, dtype=F32)

    def bdiag_in(w):
        w = w.reshape(2, S5_NSLAB, gl, SSM_STATE, SSM_GROUP)
        return jnp.einsum('dsgnp,gh->dsgphn', w, eye).reshape(2, S5_NSLAB, S5_SLAB, S5_LANES)

    def bdiag_out(w):
        w = w.reshape(2, S5_NSLAB, gl, SSM_GROUP, SSM_STATE)
        return jnp.einsum('dsgpn,gh->dsgnhp', w, eye).reshape(2, S5_NSLAB, S5_LANES, S5_SLAB)

    bmat = jnp.concatenate([bdiag_in(bb_re), bdiag_in(bb_im)], axis=-1).astype(BF16)
    cmat = jnp.concatenate([bdiag_out(c_re), -bdiag_out(c_im)], axis=-2).astype(BF16)
    slab = lambda a: a.reshape(2, S5_NSLAB, 1, S5_LANES)
    return bmat, cmat, slab(a_re), slab(a_im)


def _state_to_slabs(h):
    b = h.shape[0]
    return jnp.transpose(h.reshape(b, 2, S5_NSLAB, S5_LANES), (1, 2, 0, 3))


def _slabs_to_state(f):
    b = f.shape[2]
    return jnp.transpose(f, (2, 0, 1, 3)).reshape(b, 2, SSM_GROUPS, SSM_STATE)


def _glu_res_kernel(u_ref, yl_ref, yc_ref, x_ref, mod_ref, d_ref, w_ref, b_ref, o_ref):
    is_lat = pl.program_id(0) < LAT_TILES
    y = jnp.where(is_lat, _load_slabs(yl_ref.at[0]) + _load_slabs(yl_ref.at[1]),
                  _load_slabs(yc_ref.at[0]) + _load_slabs(yc_ref.at[1]))
    yt = _load_slabs(u_ref) * d_ref[...] + y
    z = _gelu_tanh(yt)
    gate = _sigmoid(jnp.dot(z.astype(BF16), w_ref[...], preferred_element_type=F32) + b_ref[...])
    o_ref[...] = x_ref[...] + mod_ref[0][2:3] * (z * gate)


def _glu_res_call(u, y_lat, y_ctx, x, mod, d_skip, w_glu, b_glu):
    return pl.pallas_call(
        _glu_res_kernel,
        grid=(T // TM,),
        in_specs=[
            pl.BlockSpec((S5_NSLAB, TM, LANES), lambda i: (0, i, 0)),
            pl.BlockSpec((2, S5_NSLAB, TM, LANES), lambda i: (0, 0, _lat_tile(i), 0)),
            pl.BlockSpec((2, S5_NSLAB, TM, LANES), lambda i: (0, 0, _ctx_tile(i), 0)),
            pl.BlockSpec((TM, D), lambda i: (i, 0)),
            pl.BlockSpec((1, 6, D), lambda i: (_cond_of_row(i * TM), 0, 0)),
            pl.BlockSpec((1, D), lambda i: (0, 0)),
            pl.BlockSpec((D, D), lambda i: (0, 0)),
            pl.BlockSpec((1, D), lambda i: (0, 0)),
        ],
        out_specs=pl.BlockSpec((TM, D), lambda i: (i, 0)),
        out_shape=jax.ShapeDtypeStruct((T, D), F32),
        compiler_params=_cparams(("parallel",)),
        name="s5_glu_res",
    )(u, y_lat, y_ctx, x, mod, d_skip.reshape(1, D), w_glu, b_glu.reshape(1, D))


def _conv_kernel(bg_ref, cg_ref, xi_ref, cgp_ref, xip_ref, cgn_ref, xin_ref,
                 x_ref, mod_ref, cw_ref, cb_ref, w_ref, o_ref):
    i = pl.program_id(0)
    tiles_per_seq = DEC_SEQ // TM_CONV
    is_lat = i < T_LAT // TM_CONV
    first = jnp.logical_or(jnp.logical_not(is_lat), i % tiles_per_seq == 0)
    last = jnp.logical_or(jnp.logical_not(is_lat), i % tiles_per_seq == tiles_per_seq - 1)
    z = cg_ref[...] * xi_ref[...]
    zp_row = jnp.where(first, 0.0, cgp_ref[7:8, :] * xip_ref[7:8, :])
    zn_row = jnp.where(last, 0.0, cgn_ref[0:1, :] * xin_ref[0:1, :])
    row = lax.broadcasted_iota(jnp.int32, (TM_CONV, D), 0)
    zp = jnp.where(row == 0, zp_row, pltpu.roll(z, 1, axis=0))
    zn = jnp.where(row == TM_CONV - 1, zn_row, pltpu.roll(z, TM_CONV - 1, axis=0))
    cw = cw_ref[...]
    zc = cw[0:1] * zp + cw[1:2] * z + cw[2:3] * zn + cb_ref[...]
    a = (bg_ref[...] * zc).astype(BF16)
    y = jnp.dot(a, w_ref[...], preferred_element_type=F32)
    o_ref[...] = x_ref[...] + mod_ref[0][2:3] * y


def _conv_call(proj, x, mod, conv_w, conv_b, w_out):
    r8 = TM_CONV // 8
    n8 = T // 8
    col = lambda c: pl.BlockSpec((TM_CONV, D), lambda i, c=c: (i, c))
    prev = lambda c: pl.BlockSpec((8, D), lambda i, c=c: (jnp.maximum(i * r8 - 1, 0), c))
    nxt = lambda c: pl.BlockSpec((8, D), lambda i, c=c: (jnp.minimum((i + 1) * r8, n8 - 1), c))
    return pl.pallas_call(
        _conv_kernel,
        grid=(T // TM_CONV,),
        in_specs=[
            col(0), col(1), col(2), prev(1), prev(2), nxt(1), nxt(2),
            pl.BlockSpec((TM_CONV, D), lambda i: (i, 0)),
            pl.BlockSpec((1, 6, D), lambda i: (_cond_of_row(i * TM_CONV), 0, 0)),
            pl.BlockSpec((3, D), lambda i: (0, 0)),
            pl.BlockSpec((1, D), lambda i: (0, 0)),
            pl.BlockSpec((D, D), lambda i: (0, 0)),
        ],
        out_specs=pl.BlockSpec((TM_CONV, D), lambda i: (i, 0)),
        out_shape=jax.ShapeDtypeStruct((T, D), F32),
        compiler_params=_cparams(("parallel",)),
        name="conv_mix",
    )(proj, proj, proj, proj, proj, proj, proj, x, mod, conv_w, conv_b.reshape(1, D), w_out)


def _router_kernel(x_ref, g_ref, mod_ref, r_ref, h_ref, rt_ref):
    mod = mod_ref[0]
    h = _modnorm(x_ref[...], g_ref[...], mod[4:5], mod[3:4])
    h_ref[...] = h
    logits = jnp.dot(h, r_ref[...], preferred_element_type=F32, precision=lax.Precision.HIGHEST)
    lane = lax.broadcasted_iota(jnp.int32, logits.shape, 1)
    logits = jnp.where(lane < N_EXPERTS, logits, -jnp.inf)
    m1 = jnp.max(logits, axis=-1, keepdims=True)
    i1 = jnp.min(jnp.where(logits == m1, lane, LANES), axis=-1, keepdims=True)
    rest = jnp.where(lane == i1, -jnp.inf, logits)
    m2 = jnp.max(rest, axis=-1, keepdims=True)
    i2 = jnp.min(jnp.where(rest == m2, lane, LANES), axis=-1, keepdims=True)
    e2 = jnp.exp(m2 - m1)
    w1 = 1.0 / (1.0 + e2)
    w2 = e2 / (1.0 + e2)
    rt_ref[...] = jnp.where(lane == 0, i1.astype(F32),
                            jnp.where(lane == 1, i2.astype(F32),
                                      jnp.where(lane == 2, w1, jnp.where(lane == 3, w2, 0.0))))


def _router_call(x, ln_g, mod, router_pad):
    return pl.pallas_call(
        _router_kernel,
        grid=(T // TM,),
        in_specs=[
            pl.BlockSpec((TM, D), lambda i: (i, 0)),
            pl.BlockSpec((1, D), lambda i: (0, 0)),
            pl.BlockSpec((1, 6, D), lambda i: (_cond_of_row(i * TM), 0, 0)),
            pl.BlockSpec((D, LANES), lambda i: (0, 0)),
        ],
        out_specs=[pl.BlockSpec((TM, D), lambda i: (i, 0)),
                   pl.BlockSpec((TM, LANES), lambda i: (i, 0))],
        out_shape=[jax.ShapeDtypeStruct((T, D), F32), jax.ShapeDtypeStruct((T, LANES), F32)],
        compiler_params=_cparams(("parallel",)),
        name="moe_router",
    )(x, ln_g.reshape(1, D), mod, router_pad)


R_MAX = 2 * T + N_EXPERTS * TM_MOE
N_TILES = R_MAX // TM_MOE
DMA_UNROLL = 8
GATHER_SHIFT = 3
GATHER_SPREAD = 1 << GATHER_SHIFT


def _row_copy(src_hbm, src_row, dst_vmem, dst_row, sem):
    return pltpu.make_async_copy(src_hbm.at[pl.ds(src_row, 1), :], dst_vmem.at[pl.ds(dst_row, 1), :], sem)


def _gather_kernel(nv_ref, src_ref, h_hbm, o_ref, buf, sem):
    i = pl.program_id(0)
    nv = nv_ref[0]

    def issue_tile(t):
        slot = t % 2
        base = t * TM_MOE

        def issue(j, c):
            r = (j & (GATHER_SPREAD - 1)) * (TM_MOE // GATHER_SPREAD) + lax.shift_right_logical(j, GATHER_SHIFT)
            _row_copy(h_hbm, src_ref[base + r], buf.at[slot], r, sem.at[slot]).start()
            return c

        lax.fori_loop(0, TM_MOE, issue, 0, unroll=DMA_UNROLL)

    @pl.when(jnp.logical_and(i == 0, nv > 0))
    def _():
        issue_tile(0)

    @pl.when(i + 1 < nv)
    def _():
        issue_tile(i + 1)

    @pl.when(i < nv)
    def _():
        slot = i % 2
        pltpu.make_async_copy(h_hbm.at[pl.ds(0, TM_MOE), :], buf.at[slot], sem.at[slot]).wait()
        o_ref[...] = buf[slot].astype(BF16)

    @pl.when(i >= nv)
    def _():
        o_ref[...] = jnp.zeros_like(o_ref)


def _gather_call(n_valid, src_tok, h):
    return pl.pallas_call(
        _gather_kernel,
        grid_spec=pltpu.PrefetchScalarGridSpec(
            num_scalar_prefetch=2,
            grid=(N_TILES,),
            in_specs=[pl.BlockSpec(memory_space=pl.ANY)],
            out_specs=pl.BlockSpec((TM_MOE, D), lambda i, nv, src: (i, 0)),
            scratch_shapes=[pltpu.VMEM((2, TM_MOE, D), F32), pltpu.SemaphoreType.DMA((2,))],
        ),
        out_shape=jax.ShapeDtypeStruct((R_MAX, D), BF16),
        compiler_params=_cparams(("arbitrary",)),
        name="moe_gather",
    )(n_valid, src_tok, h)


def _moe_kernel(nv_ref, te_ref, xs_ref, w1_ref, w3_ref, w2_ref, o_ref, acc_s):
    i = pl.program_id(0)
    f = pl.program_id(1)
    nf = pl.num_programs(1)
    valid = i < nv_ref[0]

    @pl.when(jnp.logical_and(valid, f == 0))
    def _():
        acc_s[...] = jnp.zeros_like(acc_s)

    @pl.when(valid)
    def _():
        xs = xs_ref[...]
        w13 = jnp.concatenate([w1_ref[...].astype(BF16), w3_ref[...].astype(BF16)], axis=1)
        gu = jnp.dot(xs, w13, preferred_element_type=F32)
        a = _silu(gu[:, :TF_MOE]) * gu[:, TF_MOE:]
        acc_s[...] += jnp.dot(a.astype(BF16), w2_ref[...].astype(BF16), preferred_element_type=F32)

    @pl.when(f == nf - 1)
    def _():
        o_ref[...] = jnp.where(valid, acc_s[...], 0.0)


def _moe_call(n_valid, tile_expert, xs, w13, w2, layer):
    nf = D_FF_EXPERT // TF_MOE

    def fe(i, f, nv):
        return jnp.where(i < nv[0], f, nf - 1)

    return pl.pallas_call(
        _moe_kernel,
        grid_spec=pltpu.PrefetchScalarGridSpec(
            num_scalar_prefetch=2,
            grid=(N_TILES, nf),
            in_specs=[
                pl.BlockSpec((TM_MOE, D), lambda i, f, nv, te: (i, 0)),
                pl.BlockSpec((None, None, D, TF_MOE),
                             lambda i, f, nv, te: (layer, te[i], 0, fe(i, f, nv))),
                pl.BlockSpec((None, None, D, TF_MOE),
                             lambda i, f, nv, te: (layer, te[i], 0, nf + fe(i, f, nv))),
                pl.BlockSpec((None, None, TF_MOE, D),
                             lambda i, f, nv, te: (layer, te[i], fe(i, f, nv), 0)),
            ],
            out_specs=pl.BlockSpec((TM_MOE, D), lambda i, f, nv, te: (i, 0)),
            scratch_shapes=[pltpu.VMEM((TM_MOE, D), F32)],
        ),
        out_shape=jax.ShapeDtypeStruct((R_MAX, D), F32),
        compiler_params=_cparams(("arbitrary", "arbitrary")),
        name="moe_experts",
    )(n_valid, tile_expert, xs, w13, w13, w2)


TM_COMB = 512


def _combine_kernel(pa_ref, pb_ref, y_hbm, x_ref, rt_ref, mod_ref, o_ref, buf_a, buf_b, sem):
    i = pl.program_id(0)

    def issue_tile(t):
        slot = t % 2
        base = t * TM_COMB

        def issue(r, c):
            _row_copy(y_hbm, pa_ref[base + r], buf_a.at[slot], r, sem.at[slot]).start()
            _row_copy(y_hbm, pb_ref[base + r], buf_b.at[slot], r, sem.at[slot]).start()
            return c

        lax.fori_loop(0, TM_COMB, issue, 0, unroll=DMA_UNROLL)

    @pl.when(i == 0)
    def _():
        issue_tile(0)

    @pl.when(i + 1 < pl.num_programs(0))
    def _():
        issue_tile(i + 1)

    slot = i % 2
    for buf in (buf_a, buf_b):
        pltpu.make_async_copy(y_hbm.at[pl.ds(0, TM_COMB), :], buf.at[slot], sem.at[slot]).wait()
    rt = rt_ref[...]
    ffn = rt[:, 2:3] * buf_a[slot] + rt[:, 3:4] * buf_b[slot]
    o_ref[...] = x_ref[...] + mod_ref[0][5:6] * ffn


def _combine_call(pos_a, pos_b, y, x, rt, mod):
    return pl.pallas_call(
        _combine_kernel,
        grid_spec=pltpu.PrefetchScalarGridSpec(
            num_scalar_prefetch=2,
            grid=(T // TM_COMB,),
            in_specs=[
                pl.BlockSpec(memory_space=pl.ANY),
                pl.BlockSpec((TM_COMB, D), lambda i, pa, pb: (i, 0)),
                pl.BlockSpec((TM_COMB, LANES), lambda i, pa, pb: (i, 0)),
                pl.BlockSpec((1, 6, D), lambda i, pa, pb: (_cond_of_row(i * TM_COMB), 0, 0)),
            ],
            out_specs=pl.BlockSpec((TM_COMB, D), lambda i, pa, pb: (i, 0)),
            scratch_shapes=[pltpu.VMEM((2, TM_COMB, D), F32), pltpu.VMEM((2, TM_COMB, D), F32),
                            pltpu.SemaphoreType.DMA((2,))],
        ),
        out_shape=jax.ShapeDtypeStruct((T, D), F32),
        compiler_params=_cparams(("arbitrary",)),
        name="moe_combine",
    )(pos_a, pos_b, y, x, rt, mod)


def _routing_tables(rt):
    ea = rt[:, 0:2].astype(jnp.int32).reshape(-1)
    onehot = (ea[:, None] == jnp.arange(N_EXPERTS, dtype=jnp.int32)[None, :]).astype(jnp.int32)
    csum = jnp.cumsum(onehot, axis=0)
    rank = jnp.take_along_axis(csum, ea[:, None], axis=1)[:, 0] - 1
    counts = csum[-1]
    padded = ((counts + TM_MOE - 1) // TM_MOE) * TM_MOE
    ends = jnp.cumsum(padded)
    pos = (ends - padded)[ea] + rank
    n_valid = (ends[-1] // TM_MOE).astype(jnp.int32).reshape(1)
    tile_start = jnp.arange(N_TILES, dtype=jnp.int32) * TM_MOE
    expert_of_row = lambda r: jnp.sum((r[:, None] >= ends[None, :]).astype(jnp.int32), axis=1)
    tile_expert = jnp.minimum(expert_of_row(tile_start), expert_of_row(ends[-1:] - 1))
    tok = jnp.arange(2 * T, dtype=jnp.int32) // 2
    src_tok = (jnp.arange(R_MAX, dtype=jnp.int32) % T).at[pos].set(tok)
    pos2 = pos.reshape(T, 2).astype(jnp.int32)
    return n_valid, tile_expert, src_tok, pos2[:, 0], pos2[:, 1]


def _moe_layer(x, ln_g, mod, router, w13, w2, layer):
    router_pad = jnp.pad(router, ((0, 0), (0, LANES - N_EXPERTS)))
    h, rt = _router_call(x, ln_g, mod, router_pad)
    n_valid, tile_expert, src_tok, pos_a, pos_b = _routing_tables(rt)
    xs = _gather_call(n_valid, src_tok, h)
    y = _moe_call(n_valid, tile_expert, xs, w13, w2, layer)
    return _combine_call(pos_a, pos_b, y, x, rt, mod)


def kernel(x_prompt, x_sample, c, cache_na_k, cache_na_v, state_ssm_re, state_ssm_im, c_ctx,
           ln1_g, ln2_g, ada_w, ada_b, na_w_qkv, na_w_o, na_q_g, na_k_g, na_rpb,
           ssm_lam_re, ssm_lam_im, ssm_log_step, ssm_b_re, ssm_b_im, ssm_c_re, ssm_c_im,
           ssm_d, ssm_w_glu, ssm_b_glu, cv_w_in, cv_conv_w, cv_conv_b, cv_w_out,
           ffn_w13, ffn_w2, moe_router, moe_w13, moe_w2):
    depth = ada_w.shape[0]
    x = (x_sample.reshape(T_LAT, D), x_prompt.reshape(T_CTX, D))
    conds = jnp.concatenate([c, c_ctx[None, :], jnp.zeros((N_COND - DEC_BATCH - 1, D), F32)], axis=0)
    mods = _ada_call(conds, ada_w, ada_b).reshape(depth, N_COND, 6, D)
    ffn_w13_b, ffn_w2_b = ffn_w13.astype(BF16), ffn_w2.astype(BF16)

    ks, vs, s_re, s_im = [], [], [], []
    for l in range(depth):
        kind, j = l % 3, l // 3
        mod = mods[l]
        if kind == 0:
            qkv = _normproj_call(x, ln1_g[l], mod, na_w_qkv[j].astype(BF16), shift=0, scale=1,
                                 name="qkv_proj", slab_out=True)
            qg2 = jnp.tile(na_q_g[j], 2).reshape(1, LANES)
            kg2 = jnp.tile(na_k_g[j], 2).reshape(1, LANES)
            past = cache_na_k.shape[3]
            pair = lambda a: jnp.transpose(a[:, j].reshape(DEC_BATCH, HP, 2, past, HEAD_DIM),
                                           (0, 1, 3, 2, 4)).reshape(DEC_BATCH, HP, past, LANES)
            ctx_k, ctx_v = pair(cache_na_k), pair(cache_na_v)
            o_lat = _lat_attn_call(qkv, ctx_k, ctx_v, qg2, kg2, _bias_pairs(na_rpb[j]))
            o_ctx, k_new, v_new = _ctx_attn_call(qkv, qg2, kg2)
            ks.append(k_new)
            vs.append(v_new)
            x = _proj_res_call(o_lat, o_ctx, x, mod, na_w_o[j].astype(BF16), gate=2, name="attn_out")
        elif kind == 1:
            u = _norm_call(x, ln1_g[l], mod, shift=0, scale=1, name="s5_norm")
            bmat, cmat, a_re, a_im = _s5_params(ssm_lam_re[j], ssm_lam_im[j], ssm_log_step[j],
                                                ssm_b_re[j], ssm_b_im[j], ssm_c_re[j], ssm_c_im[j])
            y_lat, _, _ = _s5_call(u, bmat, cmat, a_re, a_im,
                                   _state_to_slabs(state_ssm_re[:, j]), _state_to_slabs(state_ssm_im[:, j]),
                                   row0=0, n_seq=DEC_BATCH, seq_len=DEC_SEQ,
                                   nb=DEC_BATCH, steps=S5_T_LAT, name="s5_lat")
            zero = jnp.zeros((2, S5_NSLAB, BATCH, S5_LANES), F32)
            y_ctx, f_re, f_im = _s5_call(u, bmat, cmat, a_re, a_im, zero, zero,
                                         row0=T_LAT, n_seq=BATCH, seq_len=SEQ, nb=8, steps=SEQ,
                                         name="s5_ctx")
            s_re.append(_slabs_to_state(f_re))
            s_im.append(_slabs_to_state(f_im))
            x = _glu_res_call(u, y_lat, y_ctx, x, mod, ssm_d[j], ssm_w_glu[j].astype(BF16), ssm_b_glu[j])
        else:
            proj = _normproj_call(x, ln1_g[l], mod, cv_w_in[j].astype(BF16), shift=0, scale=1,
                                  name="conv_in")
            x = _conv_call(proj, x, mod, cv_conv_w[j], cv_conv_b[j], cv_w_out[j].astype(BF16))
        jj = l // 2
        if l % 2 == 0:
            x = _ffn_call(x, ln2_g[l], mod, ffn_w13_b, ffn_w2_b, jj)
        else:
            x = _moe_layer(x, ln2_g[l], mod, moe_router[jj], moe_w13, moe_w2, jj)

    y_sample = x[:T_LAT].reshape(DEC_BATCH, DEC_SEQ, D)
    y_prompt = x[T_LAT:].reshape(BATCH, SEQ, D)
    return (y_prompt, y_sample, jnp.stack(ks, axis=1), jnp.stack(vs, axis=1),
            jnp.stack(s_re, axis=1), jnp.stack(s_im, axis=1))
```

```python
import functools
import math

import numpy as np
import jax
import jax.numpy as jnp
from jax import lax
from jax.experimental import pallas as pl
from jax.experimental.pallas import tpu as pltpu

F32 = jnp.float32
BF16 = jnp.bfloat16

D = 1024
BATCH, SEQ = 16, 256
DEC_BATCH, DEC_SEQ = 4, 2048
GRID_W = 64
ROWS = DEC_SEQ // GRID_W
N_HEADS, HEAD_DIM = 16, 64
WIN_ROWS, WIN_COLS = 8, 16
SSM_GROUP, SSM_GROUPS, SSM_STATE = 16, 64, 64
D_FF = 2816
N_EXPERTS = 8
D_FF_EXPERT = 3584
EPS = 1e-6
NEG_INF = -1e30
SCALE = HEAD_DIM ** -0.5

T_LAT = DEC_BATCH * DEC_SEQ
T_CTX = BATCH * SEQ
T = T_LAT + T_CTX
N_COND = 8
CTX_COND = DEC_BATCH

LANES = 128
VMEM_LIMIT = 56 * 1024 * 1024

TM = 512
TM_CONV = 256
TM_MOE = 1024
TF_MOE = 512
HP = N_HEADS // 2
QB_ROWS = 4
S5_SLAB = 128
S5_NSLAB = D // S5_SLAB
S5_LANES = (S5_SLAB // SSM_GROUP) * SSM_STATE
S5_T_LAT = 512
S5_ROW_PAD = 4
S5_UNROLL = 4


def _cparams(sem):
    return pltpu.CompilerParams(dimension_semantics=sem, vmem_limit_bytes=VMEM_LIMIT)


def _cond_of_row(row):
    return jnp.minimum(row // DEC_SEQ, CTX_COND)


def _modnorm(x, g, sc, sh):
    ms = jnp.mean(x * x, axis=-1, keepdims=True)
    y = x * lax.rsqrt(ms + EPS) * g
    return y * (1.0 + sc) + sh


def _sigmoid(x):
    return 1.0 / (1.0 + jnp.exp(-x))


def _silu(x):
    return x * _sigmoid(x)


def _gelu_tanh(x):
    c = math.sqrt(2.0 / math.pi)
    return 0.5 * x * (1.0 + jnp.tanh(c * (x + 0.044715 * (x * x * x))))


def _ada_kernel(c_ref, w_ref, b_ref, o_ref):
    s = _silu(c_ref[...]).astype(BF16)
    o_ref[0] = jnp.dot(s, w_ref[0].astype(BF16), preferred_element_type=F32) + b_ref[0]


def _ada_call(conds, ada_w, ada_b):
    depth = ada_w.shape[0]
    tn = 1536
    return pl.pallas_call(
        _ada_kernel,
        grid=(depth, 6 * D // tn),
        in_specs=[
            pl.BlockSpec((N_COND, D), lambda l, n: (0, 0)),
            pl.BlockSpec((1, D, tn), lambda l, n: (l, 0, n)),
            pl.BlockSpec((1, 1, tn), lambda l, n: (l, 0, n)),
        ],
        out_specs=pl.BlockSpec((1, N_COND, tn), lambda l, n: (l, 0, n)),
        out_shape=jax.ShapeDtypeStruct((depth, N_COND, 6 * D), F32),
        compiler_params=_cparams(("parallel", "parallel")),
        name="ada_mod",
    )(conds, ada_w, ada_b.reshape(depth, 1, 6 * D))


LAT_TILES = T_LAT // TM


def _lat_tile(i):
    return jnp.minimum(i, LAT_TILES - 1)


def _ctx_tile(i):
    return jnp.maximum(i - LAT_TILES, 0)


def _rows_specs(x):
    if isinstance(x, tuple):
        return [pl.BlockSpec((TM, D), lambda i: (_lat_tile(i), 0)),
                pl.BlockSpec((TM, D), lambda i: (_ctx_tile(i), 0))], list(x)
    return [pl.BlockSpec((TM, D), lambda i: (i, 0))], [x]


def _rows_value(refs):
    if len(refs) == 2:
        return jnp.where(pl.program_id(0) < LAT_TILES, refs[0][...], refs[1][...])
    return refs[0][...]


def _store_slabs(o_ref, val):
    for c in range(val.shape[1] // LANES):
        o_ref[c] = val[:, c * LANES:(c + 1) * LANES]


def _load_slabs(ref):
    return jnp.concatenate([ref[c] for c in range(ref.shape[0])], axis=1)


def _normproj_kernel(*refs, shift, scale, slab_out):
    *x_refs, g_ref, mod_ref, w_ref, o_ref = refs
    mod = mod_ref[0]
    h = _modnorm(_rows_value(x_refs), g_ref[...], mod[scale:scale + 1], mod[shift:shift + 1])
    res = jnp.dot(h.astype(BF16), w_ref[...], preferred_element_type=F32)
    if slab_out:
        _store_slabs(o_ref, res)
    else:
        o_ref[...] = res


def _normproj_call(x, ln_g, mod, w, *, shift, scale, name, slab_out=False):
    n = w.shape[1]
    x_specs, x_args = _rows_specs(x)
    if slab_out:
        out_spec = pl.BlockSpec((n // LANES, TM, LANES), lambda i: (0, i, 0))
        out_shape = jax.ShapeDtypeStruct((n // LANES, T, LANES), F32)
    else:
        out_spec = pl.BlockSpec((TM, n), lambda i: (i, 0))
        out_shape = jax.ShapeDtypeStruct((T, n), F32)
    return pl.pallas_call(
        functools.partial(_normproj_kernel, shift=shift, scale=scale, slab_out=slab_out),
        grid=(T // TM,),
        in_specs=x_specs + [
            pl.BlockSpec((1, D), lambda i: (0, 0)),
            pl.BlockSpec((1, 6, D), lambda i: (_cond_of_row(i * TM), 0, 0)),
            pl.BlockSpec((D, n), lambda i: (0, 0)),
        ],
        out_specs=out_spec,
        out_shape=out_shape,
        compiler_params=_cparams(("parallel",)),
        name=name,
    )(*x_args, ln_g.reshape(1, D), mod, w)


def _norm_kernel(x_ref, g_ref, mod_ref, o_ref, *, shift, scale):
    mod = mod_ref[0]
    _store_slabs(o_ref, _modnorm(x_ref[...], g_ref[...], mod[scale:scale + 1], mod[shift:shift + 1]))


def _norm_call(x, ln_g, mod, *, shift, scale, name):
    return pl.pallas_call(
        functools.partial(_norm_kernel, shift=shift, scale=scale),
        grid=(T // TM,),
        in_specs=[
            pl.BlockSpec((TM, D), lambda i: (i, 0)),
            pl.BlockSpec((1, D), lambda i: (0, 0)),
            pl.BlockSpec((1, 6, D), lambda i: (_cond_of_row(i * TM), 0, 0)),
        ],
        out_specs=pl.BlockSpec((D // LANES, TM, LANES), lambda i: (0, i, 0)),
        out_shape=jax.ShapeDtypeStruct((D // LANES, T, LANES), F32),
        compiler_params=_cparams(("parallel",)),
        name=name,
    )(x, ln_g.reshape(1, D), mod)


def _proj_res_kernel(*refs, gate, n_x):
    al_ref, ac_ref = refs[:2]
    x_refs = refs[2:2 + n_x]
    mod_ref, w_ref, o_ref = refs[2 + n_x:]
    a = jnp.where(pl.program_id(0) < LAT_TILES, _load_slabs(al_ref), _load_slabs(ac_ref))
    y = jnp.dot(a.astype(BF16), w_ref[...], preferred_element_type=F32)
    o_ref[...] = _rows_value(x_refs) + mod_ref[0][gate:gate + 1] * y


def _proj_res_call(a_lat, a_ctx, x, mod, w, *, gate, name):
    ns = D // LANES
    a_specs = [pl.BlockSpec((ns, TM, LANES), lambda i: (0, _lat_tile(i), 0)),
               pl.BlockSpec((ns, TM, LANES), lambda i: (0, _ctx_tile(i), 0))]
    a_args = [a_lat, a_ctx]
    x_specs, x_args = _rows_specs(x)
    return pl.pallas_call(
        functools.partial(_proj_res_kernel, gate=gate, n_x=len(x_args)),
        grid=(T // TM,),
        in_specs=a_specs + x_specs + [
            pl.BlockSpec((1, 6, D), lambda i: (_cond_of_row(i * TM), 0, 0)),
            pl.BlockSpec((D, D), lambda i: (0, 0)),
        ],
        out_specs=pl.BlockSpec((TM, D), lambda i: (i, 0)),
        out_shape=jax.ShapeDtypeStruct((T, D), F32),
        compiler_params=_cparams(("parallel",)),
        name=name,
    )(*a_args, *x_args, mod, w)


def _head_norm(x, g, head0):
    x2 = x * x
    s0 = jnp.sum(jnp.where(head0, x2, 0.0), axis=-1, keepdims=True)
    s1 = jnp.sum(jnp.where(head0, 0.0, x2), axis=-1, keepdims=True)
    ms = jnp.where(head0, s0, s1) * (1.0 / HEAD_DIM)
    return x * lax.rsqrt(ms + EPS) * g


def _nt_dot(a, b):
    return lax.dot_general(a, b, (((1,), (1,)), ((), ())), preferred_element_type=F32)


def _ctx_attn_kernel(q_ref, k_ref, v_ref, qg_ref, kg_ref, o_ref, ko_ref, vo_ref):
    head0 = lax.broadcasted_iota(jnp.int32, (SEQ, LANES), 1) < HEAD_DIM
    for hp in range(HP):
        qn = _head_norm(q_ref[hp], qg_ref[...], head0)
        kn = _head_norm(k_ref[hp], kg_ref[...], head0)
        v = v_ref[hp]
        ko_ref[0, 2 * hp] = kn[:, :HEAD_DIM]
        ko_ref[0, 2 * hp + 1] = kn[:, HEAD_DIM:]
        vo_ref[0, 2 * hp] = v[:, :HEAD_DIM]
        vo_ref[0, 2 * hp + 1] = v[:, HEAD_DIM:]
        knb = kn.astype(BF16)
        vb = v.astype(BF16)
        outs = []
        for hh in range(2):
            qm = jnp.where(head0 if hh == 0 else jnp.logical_not(head0), qn * SCALE, 0.0).astype(BF16)
            s = _nt_dot(qm, knb)
            p = jnp.exp(s - jnp.max(s, axis=-1, keepdims=True))
            l = jnp.sum(p, axis=-1, keepdims=True)
            outs.append(jnp.dot(p.astype(BF16), vb, preferred_element_type=F32) / l)
        o_ref[hp] = jnp.where(head0, outs[0], outs[1])


def _ctx_attn_call(qkv, qg2, kg2):
    row0 = T_LAT // SEQ
    blk = lambda c: pl.BlockSpec((HP, SEQ, LANES), lambda b, c=c: (c, row0 + b, 0))
    kv_out = pl.BlockSpec((1, N_HEADS, SEQ, HEAD_DIM), lambda b: (b, 0, 0, 0))
    return pl.pallas_call(
        _ctx_attn_kernel,
        grid=(BATCH,),
        in_specs=[blk(0), blk(1), blk(2),
                  pl.BlockSpec((1, LANES), lambda b: (0, 0)),
                  pl.BlockSpec((1, LANES), lambda b: (0, 0))],
        out_specs=[pl.BlockSpec((HP, SEQ, LANES), lambda b: (0, b, 0)), kv_out, kv_out],
        out_shape=[jax.ShapeDtypeStruct((HP, T_CTX, LANES), F32),
                   jax.ShapeDtypeStruct((BATCH, N_HEADS, SEQ, HEAD_DIM), F32),
                   jax.ShapeDtypeStruct((BATCH, N_HEADS, SEQ, HEAD_DIM), F32)],
        compiler_params=_cparams(("parallel",)),
        name="ctx_attn",
    )(qkv, qkv, qkv, qg2, kg2)


def _band_of_block(p):
    r_lo, r_hi = QB_ROWS * p, QB_ROWS * p + QB_ROWS - 1
    kr = min(WIN_ROWS, ROWS)
    lo = min(max(r_lo - kr // 2, 0), ROWS - kr)
    hi = min(max(r_hi - kr // 2, 0), ROWS - kr) + kr
    lo -= lo % 2
    hi += hi % 2
    return lo, hi - lo


def _lat_attn_kernel(q_ref, k_ref, v_ref, ck_ref, cv_ref, qg_ref, kg_ref, bp_ref, o_ref,
                     qm0_s, qm1_s, kt_s, v_s):
    head0 = lax.broadcasted_iota(jnp.int32, (DEC_SEQ, LANES), 1) < HEAD_DIM
    qn = _head_norm(q_ref[...], qg_ref[...], head0)
    qm0_s[...] = jnp.where(head0, qn * SCALE, 0.0).astype(BF16)
    qm1_s[...] = jnp.where(head0, 0.0, qn * SCALE).astype(BF16)
    kt_s[...] = _head_norm(k_ref[...], kg_ref[...], head0).T.astype(BF16)
    v_s[...] = v_ref[...].astype(BF16)
    ckt = ck_ref[...].T.astype(BF16)
    cvb = cv_ref[...].astype(BF16)

    qc = lax.broadcasted_iota(jnp.int32, (GRID_W, LANES), 0)
    kl = lax.broadcasted_iota(jnp.int32, (GRID_W, LANES), 1)
    kc = jnp.where(kl < GRID_W, kl, kl - GRID_W)
    cs = jnp.clip(qc - WIN_COLS // 2, 0, GRID_W - WIN_COLS)
    col_ok = jnp.logical_and(kc >= cs, kc < cs + WIN_COLS)
    left = kl < GRID_W
    mask_of = {
        (True, True): col_ok,
        (True, False): jnp.logical_and(col_ok, left),
        (False, True): jnp.logical_and(col_ok, jnp.logical_not(left)),
    }
    h0q = lax.broadcasted_iota(jnp.int32, (QB_ROWS * GRID_W, LANES), 1) < HEAD_DIM
    kr_win = min(WIN_ROWS, ROWS)

    for p in range(ROWS // QB_ROWS):
        u0, nrows = _band_of_block(p)
        q_lo = p * QB_ROWS * GRID_W
        ktb = kt_s[:, u0 * GRID_W:(u0 + nrows) * GRID_W]
        vb = v_s[u0 * GRID_W:(u0 + nrows) * GRID_W, :]
        nq = QB_ROWS * GRID_W
        qm = jnp.concatenate([qm0_s[q_lo:q_lo + nq, :], qm1_s[q_lo:q_lo + nq, :]], axis=0)
        n_loc = nrows * GRID_W
        s_all = jnp.dot(qm, jnp.concatenate([ktb, ckt], axis=1), preferred_element_type=F32)
        s_loc, s_ctx = s_all[:, :n_loc], s_all[:, n_loc:]
        p_rows, l_rows = [], []
        for hh in range(2):
            for i in range(QB_ROWS):
                r = p * QB_ROWS + i
                r0 = min(max(r - kr_win // 2, 0), ROWS - kr_win)
                rows = slice(hh * nq + i * GRID_W, hh * nq + (i + 1) * GRID_W)
                sbs = {}
                for m in range(nrows // 2):
                    kr = u0 + 2 * m
                    ok_l = r0 <= kr < r0 + kr_win
                    ok_r = r0 <= kr + 1 < r0 + kr_win
                    if ok_l or ok_r:
                        sb = s_loc[rows, m * LANES:(m + 1) * LANES] + bp_ref[hh, kr - r + WIN_ROWS]
                        sbs[m] = jnp.where(mask_of[(ok_l, ok_r)], sb, NEG_INF)
                sc = s_ctx[rows, :]
                mx = jnp.maximum(
                    jnp.max(functools.reduce(jnp.maximum, sbs.values()), axis=-1, keepdims=True),
                    jnp.max(sc, axis=-1, keepdims=True))
                pbs = {m: jnp.exp(sb - mx) for m, sb in sbs.items()}
                pc = jnp.exp(sc - mx)
                l_rows.append(jnp.sum(functools.reduce(jnp.add, pbs.values()), axis=-1, keepdims=True)
                              + jnp.sum(pc, axis=-1, keepdims=True))
                zero = jnp.zeros((GRID_W, LANES), BF16)
                p_rows.append(jnp.concatenate(
                    [pbs[m].astype(BF16) if m in pbs else zero for m in range(nrows // 2)]
                    + [pc.astype(BF16)], axis=1))
        o = jnp.dot(jnp.concatenate(p_rows, axis=0), jnp.concatenate([vb, cvb], axis=0),
                    preferred_element_type=F32)
        o = o / jnp.concatenate(l_rows, axis=0)
        o_ref[q_lo:q_lo + nq, :] = jnp.where(h0q, o[:nq], o[nq:])


def _lat_attn_call(qkv, ctx_k, ctx_v, qg2, kg2, bias_pairs):
    blk = lambda c: pl.BlockSpec((None, DEC_SEQ, LANES), lambda hp, b, c=c: (c * HP + hp, b, 0))
    cblk = pl.BlockSpec((None, None, ctx_k.shape[2], LANES), lambda hp, b: (b, hp, 0, 0))
    return pl.pallas_call(
        _lat_attn_kernel,
        grid=(HP, DEC_BATCH),
        in_specs=[blk(0), blk(1), blk(2), cblk, cblk,
                  pl.BlockSpec((1, LANES), lambda hp, b: (0, 0)),
                  pl.BlockSpec((1, LANES), lambda hp, b: (0, 0)),
                  pl.BlockSpec((2, 2 * WIN_ROWS, GRID_W, LANES), lambda hp, b: (hp, 0, 0, 0))],
        out_specs=pl.BlockSpec((None, DEC_SEQ, LANES), lambda hp, b: (hp, b, 0)),
        out_shape=jax.ShapeDtypeStruct((HP, T_LAT, LANES), F32),
        scratch_shapes=[pltpu.VMEM((DEC_SEQ, LANES), BF16), pltpu.VMEM((DEC_SEQ, LANES), BF16),
                        pltpu.VMEM((LANES, DEC_SEQ), BF16), pltpu.VMEM((DEC_SEQ, LANES), BF16)],
        compiler_params=_cparams(("parallel", "parallel")),
        name="lat_attn",
    )(qkv, qkv, qkv, ctx_k, ctx_v, qg2, kg2, bias_pairs)


def _bias_pairs(rpb):
    cols = np.arange(GRID_W)
    col_idx = np.clip(cols[None, :] - cols[:, None], -(WIN_COLS - 1), WIN_COLS - 1) + WIN_COLS - 1
    rc = rpb[:, :, col_idx]
    left = jnp.pad(rc, ((0, 0), (1, 0), (0, 0), (0, 0)))
    right = jnp.pad(rc, ((0, 0), (0, 1), (0, 0), (0, 0)))
    return jnp.concatenate([left, right], axis=-1)


def _ffn_kernel(x_ref, g_ref, mod_ref, w1_ref, w3_ref, w2_ref, o_ref):
    mod = mod_ref[0]
    x = x_ref[...]
    h = _modnorm(x, g_ref[...], mod[4:5], mod[3:4]).astype(BF16)
    a = _silu(jnp.dot(h, w1_ref[...], preferred_element_type=F32)) \
        * jnp.dot(h, w3_ref[...], preferred_element_type=F32)
    y = jnp.dot(a.astype(BF16), w2_ref[...], preferred_element_type=F32)
    o_ref[...] = x + mod[5:6] * y


def _ffn_call(x, ln_g, mod, w13, w2, layer):
    resident = dict(pipeline_mode=pl.Buffered(1))
    return pl.pallas_call(
        _ffn_kernel,
        grid=(T // TM,),
        in_specs=[
            pl.BlockSpec((TM, D), lambda i: (i, 0)),
            pl.BlockSpec((1, D), lambda i: (0, 0)),
            pl.BlockSpec((1, 6, D), lambda i: (_cond_of_row(i * TM), 0, 0)),
            pl.BlockSpec((None, D, D_FF), lambda i: (layer, 0, 0), **resident),
            pl.BlockSpec((None, D, D_FF), lambda i: (layer, 0, 1), **resident),
            pl.BlockSpec((None, D_FF, D), lambda i: (layer, 0, 0), **resident),
        ],
        out_specs=pl.BlockSpec((TM, D), lambda i: (i, 0)),
        out_shape=jax.ShapeDtypeStruct((T, D), F32),
        compiler_params=_cparams(("parallel",)),
        name="ffn_dense",
    )(x, ln_g.reshape(1, D), mod, w13, w13, w2)


def _s5_kernel(u_ref, bm_ref, cm_ref, ar_ref, ai_ref, hr_ref, hi_ref,
               y_ref, fr_ref, fi_ref, bu_s, st_s, cr_s, ci_s, *, nb, steps):
    d = pl.program_id(0)
    tb = pl.program_id(3)
    rows = nb * steps
    pitch = steps + S5_ROW_PAD
    nch = S5_LANES // LANES
    u = u_ref[...].reshape(rows, S5_SLAB).astype(BF16)
    bu = jnp.dot(u, bm_ref[0, 0], preferred_element_type=F32)
    for c in range(2 * nch):
        for k in range(nb):
            bu_s[c, k * pitch:k * pitch + steps, :] = bu[k * steps:(k + 1) * steps, c * LANES:(c + 1) * LANES]

    @pl.when(tb == 0)
    def _():
        cr_s[...] = hr_ref[0, 0]
        ci_s[...] = hi_ref[0, 0]

    ar = [jnp.broadcast_to(ar_ref[0, 0][:, c * LANES:(c + 1) * LANES], (nb, LANES)) for c in range(nch)]
    ai = [jnp.broadcast_to(ai_ref[0, 0][:, c * LANES:(c + 1) * LANES], (nb, LANES)) for c in range(nch)]

    def step(i, carry):
        l = jnp.where(d == 0, i, steps - 1 - i)
        idx = pl.ds(l, nb, stride=pitch)
        out = []
        for c in range(nch):
            sr, si = carry[2 * c], carry[2 * c + 1]
            nr = ar[c] * sr - ai[c] * si + bu_s[c, idx, :]
            ni = ar[c] * si + ai[c] * sr + bu_s[nch + c, idx, :]
            st_s[c, idx, :] = nr
            st_s[nch + c, idx, :] = ni
            out += [nr, ni]
        return tuple(out)

    init = []
    for c in range(nch):
        init += [cr_s[:, c * LANES:(c + 1) * LANES], ci_s[:, c * LANES:(c + 1) * LANES]]
    fin = lax.fori_loop(0, steps, step, tuple(init), unroll=S5_UNROLL)
    sr = jnp.concatenate([fin[2 * c] for c in range(nch)], axis=1)
    si = jnp.concatenate([fin[2 * c + 1] for c in range(nch)], axis=1)
    cr_s[...] = sr
    ci_s[...] = si
    fr_ref[0, 0] = sr
    fi_ref[0, 0] = si
    states = jnp.concatenate(
        [jnp.concatenate([st_s[c, k * pitch:k * pitch + steps, :] for k in range(nb)], axis=0)
         for c in range(2 * nch)], axis=1).astype(BF16)
    y = jnp.dot(states, cm_ref[0, 0], preferred_element_type=F32)
    y_ref[...] = y.reshape(nb, steps, S5_SLAB)


def _s5_call(u, bmat, cmat, a_re, a_im, h_re, h_im, *, row0, n_seq, seq_len, nb, steps, name):
    n_sg, n_tb = n_seq // nb, seq_len // steps
    sg0 = row0 // (nb * seq_len)

    def tbi(d, tb):
        return jnp.where(d == 0, tb, n_tb - 1 - tb)

    par = lambda last: pl.BlockSpec((1, 1) + last, lambda d, j, sg, tb: (d, j, 0, 0))
    st = pl.BlockSpec((1, 1, nb, S5_LANES), lambda d, j, sg, tb: (d, j, sg, 0))
    y, f_re, f_im = pl.pallas_call(
        functools.partial(_s5_kernel, nb=nb, steps=steps),
        grid=(2, S5_NSLAB, n_sg, n_tb),
        in_specs=[
            pl.BlockSpec((None, nb, steps, S5_SLAB), lambda d, j, sg, tb: (j, sg0 + sg, tbi(d, tb), 0)),
            par((S5_SLAB, 2 * S5_LANES)),
            par((2 * S5_LANES, S5_SLAB)),
            par((1, S5_LANES)), par((1, S5_LANES)),
            st, st,
        ],
        out_specs=[
            pl.BlockSpec((None, None, nb, steps, S5_SLAB), lambda d, j, sg, tb: (d, j, sg, tbi(d, tb), 0)),
            st, st,
        ],
        out_shape=[
            jax.ShapeDtypeStruct((2, S5_NSLAB, n_seq, seq_len, S5_SLAB), F32),
            jax.ShapeDtypeStruct((2, S5_NSLAB, n_seq, S5_LANES), F32),
            jax.ShapeDtypeStruct((2, S5_NSLAB, n_seq, S5_LANES), F32),
        ],
        scratch_shapes=[pltpu.VMEM((2 * S5_LANES // LANES, nb * (steps + S5_ROW_PAD), LANES), F32)] * 2
                       + [pltpu.VMEM((nb, S5_LANES), F32)] * 2,
        compiler_params=_cparams(("parallel", "parallel", "parallel", "arbitrary")),
        name=name,
    )(u.reshape(S5_NSLAB, T // seq_len, seq_len, S5_SLAB), bmat, cmat, a_re, a_im, h_re, h_im)
    return y.reshape(2, S5_NSLAB, n_seq * seq_len, S5_SLAB), f_re, f_im


def _s5_params(lam_re, lam_im, log_step, b_re, b_im, c_re, c_im):
    step = jnp.exp(log_step)[..., None]
    zr, zi = lam_re * step, lam_im * step
    mag = jnp.exp(zr)
    a_re, a_im = mag * jnp.cos(zi), mag * jnp.sin(zi)
    nr, ni = a_re - 1.0, a_im
    den = lam_re * lam_re + lam_im * lam_im
    k_re = (nr * lam_re + ni * lam_im) / den
    k_im = (ni * lam_re - nr * lam_im) / den
    bb_re = k_re[..., None] * b_re - k_im[..., None] * b_im
    bb_im = k_re[..., None] * b_im + k_im[..., None] * b_re
    gl = S5_SLAB // SSM_GROUP
    eye = jnp.eye(gl, dtype=F32)

    def bdiag_in(w):
        w = w.reshape(2, S5_NSLAB, gl, SSM_STATE, SSM_GROUP)
        return jnp.einsum('dsgnp,gh->dsgphn', w, eye).reshape(2, S5_NSLAB, S5_SLAB, S5_LANES)

    def bdiag_out(w):
        w = w.reshape(2, S5_NSLAB, gl, SSM_GROUP, SSM_STATE)
        return jnp.einsum('dsgpn,gh->dsgnhp', w, eye).reshape(2, S5_NSLAB, S5_LANES, S5_SLAB)

    bmat = jnp.concatenate([bdiag_in(bb_re), bdiag_in(bb_im)], axis=-1).astype(BF16)
    cmat = jnp.concatenate([bdiag_out(c_re), -bdiag_out(c_im)], axis=-2).astype(BF16)
    slab = lambda a: a.reshape(2, S5_NSLAB, 1, S5_LANES)
    return bmat, cmat, slab(a_re), slab(a_im)


def _state_to_slabs(h):
    b = h.shape[0]
    return jnp.transpose(h.reshape(b, 2, S5_NSLAB, S5_LANES), (1, 2, 0, 3))


def _slabs_to_state(f):
    b = f.shape[2]
    return jnp.transpose(f, (2, 0, 1, 3)).reshape(b, 2, SSM_GROUPS, SSM_STATE)


def _glu_res_kernel(u_ref, yl_ref, yc_ref, x_ref, mod_ref, d_ref, w_ref, b_ref, o_ref):
    is_lat = pl.program_id(0) < LAT_TILES
    y = jnp.where(is_lat, _load_slabs(yl_ref.at[0]) + _load_slabs(yl_ref.at[1]),
                  _load_slabs(yc_ref.at[0]) + _load_slabs(yc_ref.at[1]))
    yt = _load_slabs(u_ref) * d_ref[...] + y
    z = _gelu_tanh(yt)
    gate = _sigmoid(jnp.dot(z.astype(BF16), w_ref[...], preferred_element_type=F32) + b_ref[...])
    o_ref[...] = x_ref[...] + mod_ref[0][2:3] * (z * gate)


def _glu_res_call(u, y_lat, y_ctx, x, mod, d_skip, w_glu, b_glu):
    return pl.pallas_call(
        _glu_res_kernel,
        grid=(T // TM,),
        in_specs=[
            pl.BlockSpec((S5_NSLAB, TM, LANES), lambda i: (0, i, 0)),
            pl.BlockSpec((2, S5_NSLAB, TM, LANES), lambda i: (0, 0, _lat_tile(i), 0)),
            pl.BlockSpec((2, S5_NSLAB, TM, LANES), lambda i: (0, 0, _ctx_tile(i), 0)),
            pl.BlockSpec((TM, D), lambda i: (i, 0)),
            pl.BlockSpec((1, 6, D), lambda i: (_cond_of_row(i * TM), 0, 0)),
            pl.BlockSpec((1, D), lambda i: (0, 0)),
            pl.BlockSpec((D, D), lambda i: (0, 0)),
            pl.BlockSpec((1, D), lambda i: (0, 0)),
        ],
        out_specs=pl.BlockSpec((TM, D), lambda i: (i, 0)),
        out_shape=jax.ShapeDtypeStruct((T, D), F32),
        compiler_params=_cparams(("parallel",)),
        name="s5_glu_res",
    )(u, y_lat, y_ctx, x, mod, d_skip.reshape(1, D), w_glu, b_glu.reshape(1, D))


def _conv_kernel(bg_ref, cg_ref, xi_ref, cgp_ref, xip_ref, cgn_ref, xin_ref,
                 x_ref, mod_ref, cw_ref, cb_ref, w_ref, o_ref):
    i = pl.program_id(0)
    tiles_per_seq = DEC_SEQ // TM_CONV
    is_lat = i < T_LAT // TM_CONV
    first = jnp.logical_or(jnp.logical_not(is_lat), i % tiles_per_seq == 0)
    last = jnp.logical_or(jnp.logical_not(is_lat), i % tiles_per_seq == tiles_per_seq - 1)
    z = cg_ref[...] * xi_ref[...]
    zp_row = jnp.where(first, 0.0, cgp_ref[7:8, :] * xip_ref[7:8, :])
    zn_row = jnp.where(last, 0.0, cgn_ref[0:1, :] * xin_ref[0:1, :])
    row = lax.broadcasted_iota(jnp.int32, (TM_CONV, D), 0)
    zp = jnp.where(row == 0, zp_row, pltpu.roll(z, 1, axis=0))
    zn = jnp.where(row == TM_CONV - 1, zn_row, pltpu.roll(z, TM_CONV - 1, axis=0))
    cw = cw_ref[...]
    zc = cw[0:1] * zp + cw[1:2] * z + cw[2:3] * zn + cb_ref[...]
    a = (bg_ref[...] * zc).astype(BF16)
    y = jnp.dot(a, w_ref[...], preferred_element_type=F32)
    o_ref[...] = x_ref[...] + mod_ref[0][2:3] * y


def _conv_call(proj, x, mod, conv_w, conv_b, w_out):
    r8 = TM_CONV // 8
    n8 = T // 8
    col = lambda c: pl.BlockSpec((TM_CONV, D), lambda i, c=c: (i, c))
    prev = lambda c: pl.BlockSpec((8, D), lambda i, c=c: (jnp.maximum(i * r8 - 1, 0), c))
    nxt = lambda c: pl.BlockSpec((8, D), lambda i, c=c: (jnp.minimum((i + 1) * r8, n8 - 1), c))
    return pl.pallas_call(
        _conv_kernel,
        grid=(T // TM_CONV,),
        in_specs=[
            col(0), col(1), col(2), prev(1), prev(2), nxt(1), nxt(2),
            pl.BlockSpec((TM_CONV, D), lambda i: (i, 0)),
            pl.BlockSpec((1, 6, D), lambda i: (_cond_of_row(i * TM_CONV), 0, 0)),
            pl.BlockSpec((3, D), lambda i: (0, 0)),
            pl.BlockSpec((1, D), lambda i: (0, 0)),
            pl.BlockSpec((D, D), lambda i: (0, 0)),
        ],
        out_specs=pl.BlockSpec((TM_CONV, D), lambda i: (i, 0)),
        out_shape=jax.ShapeDtypeStruct((T, D), F32),
        compiler_params=_cparams(("parallel",)),
        name="conv_mix",
    )(proj, proj, proj, proj, proj, proj, proj, x, mod, conv_w, conv_b.reshape(1, D), w_out)


def _router_kernel(x_ref, g_ref, mod_ref, r_ref, h_ref, rt_ref):
    mod = mod_ref[0]
    h = _modnorm(x_ref[...], g_ref[...], mod[4:5], mod[3:4])
    h_ref[...] = h
    logits = jnp.dot(h, r_ref[...], preferred_element_type=F32, precision=lax.Precision.HIGHEST)
    lane = lax.broadcasted_iota(jnp.int32, logits.shape, 1)
    logits = jnp.where(lane < N_EXPERTS, logits, -jnp.inf)
    m1 = jnp.max(logits, axis=-1, keepdims=True)
    i1 = jnp.min(jnp.where(logits == m1, lane, LANES), axis=-1, keepdims=True)
    rest = jnp.where(lane == i1, -jnp.inf, logits)
    m2 = jnp.max(rest, axis=-1, keepdims=True)
    i2 = jnp.min(jnp.where(rest == m2, lane, LANES), axis=-1, keepdims=True)
    e2 = jnp.exp(m2 - m1)
    w1 = 1.0 / (1.0 + e2)
    w2 = e2 / (1.0 + e2)
    rt_ref[...] = jnp.where(lane == 0, i1.astype(F32),
                            jnp.where(lane == 1, i2.astype(F32),
                                      jnp.where(lane == 2, w1, jnp.where(lane == 3, w2, 0.0))))


def _router_call(x, ln_g, mod, router_pad):
    return pl.pallas_call(
        _router_kernel,
        grid=(T // TM,),
        in_specs=[
            pl.BlockSpec((TM, D), lambda i: (i, 0)),
            pl.BlockSpec((1, D), lambda i: (0, 0)),
            pl.BlockSpec((1, 6, D), lambda i: (_cond_of_row(i * TM), 0, 0)),
            pl.BlockSpec((D, LANES), lambda i: (0, 0)),
        ],
        out_specs=[pl.BlockSpec((TM, D), lambda i: (i, 0)),
                   pl.BlockSpec((TM, LANES), lambda i: (i, 0))],
        out_shape=[jax.ShapeDtypeStruct((T, D), F32), jax.ShapeDtypeStruct((T, LANES), F32)],
        compiler_params=_cparams(("parallel",)),
        name="moe_router",
    )(x, ln_g.reshape(1, D), mod, router_pad)


R_MAX = 2 * T + N_EXPERTS * TM_MOE
N_TILES = R_MAX // TM_MOE
DMA_UNROLL = 8
GATHER_SHIFT = 3
GATHER_SPREAD = 1 << GATHER_SHIFT


def _row_copy(src_hbm, src_row, dst_vmem, dst_row, sem):
    return pltpu.make_async_copy(src_hbm.at[pl.ds(src_row, 1), :], dst_vmem.at[pl.ds(dst_row, 1), :], sem)


def _gather_kernel(nv_ref, src_ref, h_hbm, o_ref, buf, sem):
    i = pl.program_id(0)
    nv = nv_ref[0]

    def issue_tile(t):
        slot = t % 2
        base = t * TM_MOE

        def issue(j, c):
            r = (j & (GATHER_SPREAD - 1)) * (TM_MOE // GATHER_SPREAD) + lax.shift_right_logical(j, GATHER_SHIFT)
            _row_copy(h_hbm, src_ref[base + r], buf.at[slot], r, sem.at[slot]).start()
            return c

        lax.fori_loop(0, TM_MOE, issue, 0, unroll=DMA_UNROLL)

    @pl.when(jnp.logical_and(i == 0, nv > 0))
    def _():
        issue_tile(0)

    @pl.when(i + 1 < nv)
    def _():
        issue_tile(i + 1)

    @pl.when(i < nv)
    def _():
        slot = i % 2
        pltpu.make_async_copy(h_hbm.at[pl.ds(0, TM_MOE), :], buf.at[slot], sem.at[slot]).wait()
        o_ref[...] = buf[slot].astype(BF16)

    @pl.when(i >= nv)
    def _():
        o_ref[...] = jnp.zeros_like(o_ref)


def _gather_call(n_valid, src_tok, h):
    return pl.pallas_call(
        _gather_kernel,
        grid_spec=pltpu.PrefetchScalarGridSpec(
            num_scalar_prefetch=2,
            grid=(N_TILES,),
            in_specs=[pl.BlockSpec(memory_space=pl.ANY)],
            out_specs=pl.BlockSpec((TM_MOE, D), lambda i, nv, src: (i, 0)),
            scratch_shapes=[pltpu.VMEM((2, TM_MOE, D), F32), pltpu.SemaphoreType.DMA((2,))],
        ),
        out_shape=jax.ShapeDtypeStruct((R_MAX, D), BF16),
        compiler_params=_cparams(("arbitrary",)),
        name="moe_gather",
    )(n_valid, src_tok, h)


def _moe_kernel(nv_ref, te_ref, xs_ref, w1_ref, w3_ref, w2_ref, o_ref, acc_s):
    i = pl.program_id(0)
    f = pl.program_id(1)
    nf = pl.num_programs(1)
    valid = i < nv_ref[0]

    @pl.when(jnp.logical_and(valid, f == 0))
    def _():
        acc_s[...] = jnp.zeros_like(acc_s)

    @pl.when(valid)
    def _():
        xs = xs_ref[...]
        w13 = jnp.concatenate([w1_ref[...].astype(BF16), w3_ref[...].astype(BF16)], axis=1)
        gu = jnp.dot(xs, w13, preferred_element_type=F32)
        a = _silu(gu[:, :TF_MOE]) * gu[:, TF_MOE:]
        acc_s[...] += jnp.dot(a.astype(BF16), w2_ref[...].astype(BF16), preferred_element_type=F32)

    @pl.when(f == nf - 1)
    def _():
        o_ref[...] = jnp.where(valid, acc_s[...], 0.0)


def _moe_call(n_valid, tile_expert, xs, w13, w2, layer):
    nf = D_FF_EXPERT // TF_MOE

    def fe(i, f, nv):
        return jnp.where(i < nv[0], f, nf - 1)

    return pl.pallas_call(
        _moe_kernel,
        grid_spec=pltpu.PrefetchScalarGridSpec(
            num_scalar_prefetch=2,
            grid=(N_TILES, nf),
            in_specs=[
                pl.BlockSpec((TM_MOE, D), lambda i, f, nv, te: (i, 0)),
                pl.BlockSpec((None, None, D, TF_MOE),
                             lambda i, f, nv, te: (layer, te[i], 0, fe(i, f, nv))),
                pl.BlockSpec((None, None, D, TF_MOE),
                             lambda i, f, nv, te: (layer, te[i], 0, nf + fe(i, f, nv))),
                pl.BlockSpec((None, None, TF_MOE, D),
                             lambda i, f, nv, te: (layer, te[i], fe(i, f, nv), 0)),
            ],
            out_specs=pl.BlockSpec((TM_MOE, D), lambda i, f, nv, te: (i, 0)),
            scratch_shapes=[pltpu.VMEM((TM_MOE, D), F32)],
        ),
        out_shape=jax.ShapeDtypeStruct((R_MAX, D), F32),
        compiler_params=_cparams(("arbitrary", "arbitrary")),
        name="moe_experts",
    )(n_valid, tile_expert, xs, w13, w13, w2)


TM_COMB = 512


def _combine_kernel(pa_ref, pb_ref, y_hbm, x_ref, rt_ref, mod_ref, o_ref, buf_a, buf_b, sem):
    i = pl.program_id(0)

    def issue_tile(t):
        slot = t % 2
        base = t * TM_COMB

        def issue(r, c):
            _row_copy(y_hbm, pa_ref[base + r], buf_a.at[slot], r, sem.at[slot]).start()
            _row_copy(y_hbm, pb_ref[base + r], buf_b.at[slot], r, sem.at[slot]).start()
            return c

        lax.fori_loop(0, TM_COMB, issue, 0, unroll=DMA_UNROLL)

    @pl.when(i == 0)
    def _():
        issue_tile(0)

    @pl.when(i + 1 < pl.num_programs(0))
    def _():
        issue_tile(i + 1)

    slot = i % 2
    for buf in (buf_a, buf_b):
        pltpu.make_async_copy(y_hbm.at[pl.ds(0, TM_COMB), :], buf.at[slot], sem.at[slot]).wait()
    rt = rt_ref[...]
    ffn = rt[:, 2:3] * buf_a[slot] + rt[:, 3:4] * buf_b[slot]
    o_ref[...] = x_ref[...] + mod_ref[0][5:6] * ffn


def _combine_call(pos_a, pos_b, y, x, rt, mod):
    return pl.pallas_call(
        _combine_kernel,
        grid_spec=pltpu.PrefetchScalarGridSpec(
            num_scalar_prefetch=2,
            grid=(T // TM_COMB,),
            in_specs=[
                pl.BlockSpec(memory_space=pl.ANY),
                pl.BlockSpec((TM_COMB, D), lambda i, pa, pb: (i, 0)),
                pl.BlockSpec((TM_COMB, LANES), lambda i, pa, pb: (i, 0)),
                pl.BlockSpec((1, 6, D), lambda i, pa, pb: (_cond_of_row(i * TM_COMB), 0, 0)),
            ],
            out_specs=pl.BlockSpec((TM_COMB, D), lambda i, pa, pb: (i, 0)),
            scratch_shapes=[pltpu.VMEM((2, TM_COMB, D), F32), pltpu.VMEM((2, TM_COMB, D), F32),
                            pltpu.SemaphoreType.DMA((2,))],
        ),
        out_shape=jax.ShapeDtypeStruct((T, D), F32),
        compiler_params=_cparams(("arbitrary",)),
        name="moe_combine",
    )(pos_a, pos_b, y, x, rt, mod)


def _routing_tables(rt):
    ea = rt[:, 0:2].astype(jnp.int32).reshape(-1)
    onehot = (ea[:, None] == jnp.arange(N_EXPERTS, dtype=jnp.int32)[None, :]).astype(jnp.int32)
    csum = jnp.cumsum(onehot, axis=0)
    rank = jnp.take_along_axis(csum, ea[:, None], axis=1)[:, 0] - 1
    counts = csum[-1]
    padded = ((counts + TM_MOE - 1) // TM_MOE) * TM_MOE
    ends = jnp.cumsum(padded)
    pos = (ends - padded)[ea] + rank
    n_valid = (ends[-1] // TM_MOE).astype(jnp.int32).reshape(1)
    tile_start = jnp.arange(N_TILES, dtype=jnp.int32) * TM_MOE
    expert_of_row = lambda r: jnp.sum((r[:, None] >= ends[None, :]).astype(jnp.int32), axis=1)
    tile_expert = jnp.minimum(expert_of_row(tile_start), expert_of_row(ends[-1:] - 1))
    tok = jnp.arange(2 * T, dtype=jnp.int32) // 2
    src_tok = (jnp.arange(R_MAX, dtype=jnp.int32) % T).at[pos].set(tok)
    pos2 = pos.reshape(T, 2).astype(jnp.int32)
    return n_valid, tile_expert, src_tok, pos2[:, 0], pos2[:, 1]


def _moe_layer(x, ln_g, mod, router, w13, w2, layer):
    router_pad = jnp.pad(router, ((0, 0), (0, LANES - N_EXPERTS)))
    h, rt = _router_call(x, ln_g, mod, router_pad)
    n_valid, tile_expert, src_tok, pos_a, pos_b = _routing_tables(rt)
    xs = _gather_call(n_valid, src_tok, h)
    y = _moe_call(n_valid, tile_expert, xs, w13, w2, layer)
    return _combine_call(pos_a, pos_b, y, x, rt, mod)


def kernel(x_prompt, x_sample, c, cache_na_k, cache_na_v, state_ssm_re, state_ssm_im, c_ctx,
           ln1_g, ln2_g, ada_w, ada_b, na_w_qkv, na_w_o, na_q_g, na_k_g, na_rpb,
           ssm_lam_re, ssm_lam_im, ssm_log_step, ssm_b_re, ssm_b_im, ssm_c_re, ssm_c_im,
           ssm_d, ssm_w_glu, ssm_b_glu, cv_w_in, cv_conv_w, cv_conv_b, cv_w_out,
           ffn_w13, ffn_w2, moe_router, moe_w13, moe_w2):
    depth = ada_w.shape[0]
    x = (x_sample.reshape(T_LAT, D), x_prompt.reshape(T_CTX, D))
    conds = jnp.concatenate([c, c_ctx[None, :], jnp.zeros((N_COND - DEC_BATCH - 1, D), F32)], axis=0)
    mods = _ada_call(conds, ada_w, ada_b).reshape(depth, N_COND, 6, D)
    ffn_w13_b, ffn_w2_b = ffn_w13.astype(BF16), ffn_w2.astype(BF16)

    ks, vs, s_re, s_im = [], [], [], []
    for l in range(depth):
        kind, j = l % 3, l // 3
        mod = mods[l]
        if kind == 0:
            qkv = _normproj_call(x, ln1_g[l], mod, na_w_qkv[j].astype(BF16), shift=0, scale=1,
                                 name="qkv_proj", slab_out=True)
            qg2 = jnp.tile(na_q_g[j], 2).reshape(1, LANES)
            kg2 = jnp.tile(na_k_g[j], 2).reshape(1, LANES)
            past = cache_na_k.shape[3]
            pair = lambda a: jnp.transpose(a[:, j].reshape(DEC_BATCH, HP, 2, past, HEAD_DIM),
                                           (0, 1, 3, 2, 4)).reshape(DEC_BATCH, HP, past, LANES)
            ctx_k, ctx_v = pair(cache_na_k), pair(cache_na_v)
            o_lat = _lat_attn_call(qkv, ctx_k, ctx_v, qg2, kg2, _bias_pairs(na_rpb[j]))
            o_ctx, k_new, v_new = _ctx_attn_call(qkv, qg2, kg2)
            ks.append(k_new)
            vs.append(v_new)
            x = _proj_res_call(o_lat, o_ctx, x, mod, na_w_o[j].astype(BF16), gate=2, name="attn_out")
        elif kind == 1:
            u = _norm_call(x, ln1_g[l], mod, shift=0, scale=1, name="s5_norm")
            bmat, cmat, a_re, a_im = _s5_params(ssm_lam_re[j], ssm_lam_im[j], ssm_log_step[j],
                                                ssm_b_re[j], ssm_b_im[j], ssm_c_re[j], ssm_c_im[j])
            y_lat, _, _ = _s5_call(u, bmat, cmat, a_re, a_im,
                                   _state_to_slabs(state_ssm_re[:, j]), _state_to_slabs(state_ssm_im[:, j]),
                                   row0=0, n_seq=DEC_BATCH, seq_len=DEC_SEQ,
                                   nb=DEC_BATCH, steps=S5_T_LAT, name="s5_lat")
            zero = jnp.zeros((2, S5_NSLAB, BATCH, S5_LANES), F32)
            y_ctx, f_re, f_im = _s5_call(u, bmat, cmat, a_re, a_im, zero, zero,
                                         row0=T_LAT, n_seq=BATCH, seq_len=SEQ, nb=8, steps=SEQ,
                                         name="s5_ctx")
            s_re.append(_slabs_to_state(f_re))
            s_im.append(_slabs_to_state(f_im))
            x = _glu_res_call(u, y_lat, y_ctx, x, mod, ssm_d[j], ssm_w_glu[j].astype(BF16), ssm_b_glu[j])
        else:
            proj = _normproj_call(x, ln1_g[l], mod, cv_w_in[j].astype(BF16), shift=0, scale=1,
                                  name="conv_in")
            x = _conv_call(proj, x, mod, cv_conv_w[j], cv_conv_b[j], cv_w_out[j].astype(BF16))
        jj = l // 2
        if l % 2 == 0:
            x = _ffn_call(x, ln2_g[l], mod, ffn_w13_b, ffn_w2_b, jj)
        else:
            x = _moe_layer(x, ln2_g[l], mod, moe_router[jj], moe_w13, moe_w2, jj)

    y_sample = x[:T_LAT].reshape(DEC_BATCH, DEC_SEQ, D)
    y_prompt = x[T_LAT:].reshape(BATCH, SEQ, D)
    return (y_prompt, y_sample, jnp.stack(ks, axis=1), jnp.stack(vs, axis=1),
            jnp.stack(s_re, axis=1), jnp.stack(s_im, axis=1))
```

```python
import functools
import math

import numpy as np
import jax
import jax.numpy as jnp
from jax import lax
from jax.experimental import pallas as pl
from jax.experimental.pallas import tpu as pltpu

F32 = jnp.float32
BF16 = jnp.bfloat16

D = 1024
BATCH, SEQ = 16, 256
DEC_BATCH, DEC_SEQ = 4, 2048
GRID_W = 64
ROWS = DEC_SEQ // GRID_W
N_HEADS, HEAD_DIM = 16, 64
WIN_ROWS, WIN_COLS = 8, 16
SSM_GROUP, SSM_GROUPS, SSM_STATE = 16, 64, 64
D_FF = 2816
N_EXPERTS = 8
D_FF_EXPERT = 3584
EPS = 1e-6
NEG_INF = -1e30
SCALE = HEAD_DIM ** -0.5

T_LAT = DEC_BATCH * DEC_SEQ
T_CTX = BATCH * SEQ
T = T_LAT + T_CTX
N_COND = 8
CTX_COND = DEC_BATCH

LANES = 128
VMEM_LIMIT = 56 * 1024 * 1024

TM = 512
TM_CONV = 256
TM_MOE = 1024
TF_MOE = 512
HP = N_HEADS // 2
QB_ROWS = 4
S5_SLAB = 128
S5_NSLAB = D // S5_SLAB
S5_LANES = (S5_SLAB // SSM_GROUP) * SSM_STATE
S5_T_LAT = 512
S5_ROW_PAD = 4
S5_UNROLL = 4


def _cparams(sem):
    return pltpu.CompilerParams(dimension_semantics=sem, vmem_limit_bytes=VMEM_LIMIT)


def _cond_of_row(row):
    return jnp.minimum(row // DEC_SEQ, CTX_COND)


def _modnorm(x, g, sc, sh):
    ms = jnp.mean(x * x, axis=-1, keepdims=True)
    y = x * lax.rsqrt(ms + EPS) * g
    return y * (1.0 + sc) + sh


def _sigmoid(x):
    return 1.0 / (1.0 + jnp.exp(-x))


def _silu(x):
    return x * _sigmoid(x)


def _gelu_tanh(x):
    c = math.sqrt(2.0 / math.pi)
    return 0.5 * x * (1.0 + jnp.tanh(c * (x + 0.044715 * (x * x * x))))


def _ada_kernel(c_ref, w_ref, b_ref, o_ref):
    s = _silu(c_ref[...]).astype(BF16)
    o_ref[0] = jnp.dot(s, w_ref[0].astype(BF16), preferred_element_type=F32) + b_ref[0]


def _ada_call(conds, ada_w, ada_b):
    depth = ada_w.shape[0]
    tn = 1536
    return pl.pallas_call(
        _ada_kernel,
        grid=(depth, 6 * D // tn),
        in_specs=[
            pl.BlockSpec((N_COND, D), lambda l, n: (0, 0)),
            pl.BlockSpec((1, D, tn), lambda l, n: (l, 0, n)),
            pl.BlockSpec((1, 1, tn), lambda l, n: (l, 0, n)),
        ],
        out_specs=pl.BlockSpec((1, N_COND, tn), lambda l, n: (l, 0, n)),
        out_shape=jax.ShapeDtypeStruct((depth, N_COND, 6 * D), F32),
        compiler_params=_cparams(("parallel", "parallel")),
        name="ada_mod",
    )(conds, ada_w, ada_b.reshape(depth, 1, 6 * D))


LAT_TILES = T_LAT // TM


def _lat_tile(i):
    return jnp.minimum(i, LAT_TILES - 1)


def _ctx_tile(i):
    return jnp.maximum(i - LAT_TILES, 0)


def _rows_specs(x):
    if isinstance(x, tuple):
        return [pl.BlockSpec((TM, D), lambda i: (_lat_tile(i), 0)),
                pl.BlockSpec((TM, D), lambda i: (_ctx_tile(i), 0))], list(x)
    return [pl.BlockSpec((TM, D), lambda i: (i, 0))], [x]


def _rows_value(refs):
    if len(refs) == 2:
        return jnp.where(pl.program_id(0) < LAT_TILES, refs[0][...], refs[1][...])
    return refs[0][...]


def _store_slabs(o_ref, val):
    for c in range(val.shape[1] // LANES):
        o_ref[c] = val[:, c * LANES:(c + 1) * LANES]


def _load_slabs(ref):
    return jnp.concatenate([ref[c] for c in range(ref.shape[0])], axis=1)


def _normproj_kernel(*refs, shift, scale, slab_out):
    *x_refs, g_ref, mod_ref, w_ref, o_ref = refs
    mod = mod_ref[0]
    h = _modnorm(_rows_value(x_refs), g_ref[...], mod[scale:scale + 1], mod[shift:shift + 1])
    res = jnp.dot(h.astype(BF16), w_ref[...], preferred_element_type=F32)
    if slab_out:
        _store_slabs(o_ref, res)
    else:
        o_ref[...] = res


def _normproj_call(x, ln_g, mod, w, *, shift, scale, name, slab_out=False):
    n = w.shape[1]
    x_specs, x_args = _rows_specs(x)
    if slab_out:
        out_spec = pl.BlockSpec((n // LANES, TM, LANES), lambda i: (0, i, 0))
        out_shape = jax.ShapeDtypeStruct((n // LANES, T, LANES), F32)
    else:
        out_spec = pl.BlockSpec((TM, n), lambda i: (i, 0))
        out_shape = jax.ShapeDtypeStruct((T, n), F32)
    return pl.pallas_call(
        functools.partial(_normproj_kernel, shift=shift, scale=scale, slab_out=slab_out),
        grid=(T // TM,),
        in_specs=x_specs + [
            pl.BlockSpec((1, D), lambda i: (0, 0)),
            pl.BlockSpec((1, 6, D), lambda i: (_cond_of_row(i * TM), 0, 0)),
            pl.BlockSpec((D, n), lambda i: (0, 0)),
        ],
        out_specs=out_spec,
        out_shape=out_shape,
        compiler_params=_cparams(("parallel",)),
        name=name,
    )(*x_args, ln_g.reshape(1, D), mod, w)


def _norm_kernel(x_ref, g_ref, mod_ref, o_ref, *, shift, scale):
    mod = mod_ref[0]
    _store_slabs(o_ref, _modnorm(x_ref[...], g_ref[...], mod[scale:scale + 1], mod[shift:shift + 1]))


def _norm_call(x, ln_g, mod, *, shift, scale, name):
    return pl.pallas_call(
        functools.partial(_norm_kernel, shift=shift, scale=scale),
        grid=(T // TM,),
        in_specs=[
            pl.BlockSpec((TM, D), lambda i: (i, 0)),
            pl.BlockSpec((1, D), lambda i: (0, 0)),
            pl.BlockSpec((1, 6, D), lambda i: (_cond_of_row(i * TM), 0, 0)),
        ],
        out_specs=pl.BlockSpec((D // LANES, TM, LANES), lambda i: (0, i, 0)),
        out_shape=jax.ShapeDtypeStruct((D // LANES, T, LANES), F32),
        compiler_params=_cparams(("parallel",)),
        name=name,
    )(x, ln_g.reshape(1, D), mod)


def _proj_res_kernel(*refs, gate, n_x):
    al_ref, ac_ref = refs[:2]
    x_refs = refs[2:2 + n_x]
    mod_ref, w_ref, o_ref = refs[2 + n_x:]
    a = jnp.where(pl.program_id(0) < LAT_TILES, _load_slabs(al_ref), _load_slabs(ac_ref))
    y = jnp.dot(a.astype(BF16), w_ref[...], preferred_element_type=F32)
    o_ref[...] = _rows_value(x_refs) + mod_ref[0][gate:gate + 1] * y


def _proj_res_call(a_lat, a_ctx, x, mod, w, *, gate, name):
    ns = D // LANES
    a_specs = [pl.BlockSpec((ns, TM, LANES), lambda i: (0, _lat_tile(i), 0)),
               pl.BlockSpec((ns, TM, LANES), lambda i: (0, _ctx_tile(i), 0))]
    a_args = [a_lat, a_ctx]
    x_specs, x_args = _rows_specs(x)
    return pl.pallas_call(
        functools.partial(_proj_res_kernel, gate=gate, n_x=len(x_args)),
        grid=(T // TM,),
        in_specs=a_specs + x_specs + [
            pl.BlockSpec((1, 6, D), lambda i: (_cond_of_row(i * TM), 0, 0)),
            pl.BlockSpec((D, D), lambda i: (0, 0)),
        ],
        out_specs=pl.BlockSpec((TM, D), lambda i: (i, 0)),
        out_shape=jax.ShapeDtypeStruct((T, D), F32),
        compiler_params=_cparams(("parallel",)),
        name=name,
    )(*a_args, *x_args, mod, w)


def _head_norm(x, g, head0):
    x2 = x * x
    s0 = jnp.sum(jnp.where(head0, x2, 0.0), axis=-1, keepdims=True)
    s1 = jnp.sum(jnp.where(head0, 0.0, x2), axis=-1, keepdims=True)
    ms = jnp.where(head0, s0, s1) * (1.0 / HEAD_DIM)
    return x * lax.rsqrt(ms + EPS) * g


def _nt_dot(a, b):
    return lax.dot_general(a, b, (((1,), (1,)), ((), ())), preferred_element_type=F32)


def _ctx_attn_kernel(*refs, n_prev):
    if n_prev:
        q_ref, k_ref, v_ref, qg_ref, kg_ref, pk_ref, pv_ref, o_ref, ko_ref, vo_ref = refs
        ko_ref[0, :n_prev] = pk_ref[0]
        vo_ref[0, :n_prev] = pv_ref[0]
    else:
        q_ref, k_ref, v_ref, qg_ref, kg_ref, o_ref, ko_ref, vo_ref = refs
    head0 = lax.broadcasted_iota(jnp.int32, (SEQ, LANES), 1) < HEAD_DIM
    for hp in range(HP):
        qn = _head_norm(q_ref[hp], qg_ref[...], head0) * SCALE
        kn = _head_norm(k_ref[hp], kg_ref[...], head0)
        v = v_ref[hp]
        ko_ref[0, n_prev, 2 * hp] = kn[:, :HEAD_DIM]
        ko_ref[0, n_prev, 2 * hp + 1] = kn[:, HEAD_DIM:]
        vo_ref[0, n_prev, 2 * hp] = v[:, :HEAD_DIM]
        vo_ref[0, n_prev, 2 * hp + 1] = v[:, HEAD_DIM:]
        qm = jnp.concatenate([jnp.where(head0, qn, 0.0), jnp.where(head0, 0.0, qn)], axis=0).astype(BF16)
        s = jnp.dot(qm, kn.T.astype(BF16), preferred_element_type=F32)
        p = jnp.exp(s - jnp.max(s, axis=-1, keepdims=True))
        l = jnp.sum(p, axis=-1, keepdims=True)
        o = jnp.dot(p.astype(BF16), v.astype(BF16), preferred_element_type=F32) / l
        o_ref[hp] = jnp.where(head0, o[:SEQ], o[SEQ:])


def _ctx_attn_call(qkv, qg2, kg2, prev_k, prev_v):
    row0 = T_LAT // SEQ
    n_prev = 0 if prev_k is None else prev_k.shape[1]
    blk = lambda c: pl.BlockSpec((HP, SEQ, LANES), lambda b, c=c: (c, row0 + b, 0))
    kv_blk = lambda n: pl.BlockSpec((1, n, N_HEADS, SEQ, HEAD_DIM), lambda b: (b, 0, 0, 0, 0))
    kv_shape = jax.ShapeDtypeStruct((BATCH, n_prev + 1, N_HEADS, SEQ, HEAD_DIM), F32)
    prev_specs = [kv_blk(n_prev)] * 2 if n_prev else []
    prev_args = [prev_k, prev_v] if n_prev else []
    return pl.pallas_call(
        functools.partial(_ctx_attn_kernel, n_prev=n_prev),
        grid=(BATCH,),
        in_specs=[blk(0), blk(1), blk(2),
                  pl.BlockSpec((1, LANES), lambda b: (0, 0)),
                  pl.BlockSpec((1, LANES), lambda b: (0, 0))] + prev_specs,
        out_specs=[pl.BlockSpec((HP, SEQ, LANES), lambda b: (0, b, 0)), kv_blk(n_prev + 1), kv_blk(n_prev + 1)],
        out_shape=[jax.ShapeDtypeStruct((HP, T_CTX, LANES), F32), kv_shape, kv_shape],
        compiler_params=_cparams(("parallel",)),
        name="ctx_attn",
    )(qkv, qkv, qkv, qg2, kg2, *prev_args)


def _band_of_block(p):
    r_lo, r_hi = QB_ROWS * p, QB_ROWS * p + QB_ROWS - 1
    kr = min(WIN_ROWS, ROWS)
    lo = min(max(r_lo - kr // 2, 0), ROWS - kr)
    hi = min(max(r_hi - kr // 2, 0), ROWS - kr) + kr
    lo -= lo % 2
    hi += hi % 2
    return lo, hi - lo


def _lat_attn_kernel(q_ref, k_ref, v_ref, ck_ref, cv_ref, qg_ref, kg_ref, bp_ref, o_ref,
                     qm0_s, qm1_s, kt_s, v_s):
    head0 = lax.broadcasted_iota(jnp.int32, (DEC_SEQ, LANES), 1) < HEAD_DIM
    qn = _head_norm(q_ref[...], qg_ref[...], head0)
    qm0_s[...] = jnp.where(head0, qn * SCALE, 0.0).astype(BF16)
    qm1_s[...] = jnp.where(head0, 0.0, qn * SCALE).astype(BF16)
    kt_s[...] = _head_norm(k_ref[...], kg_ref[...], head0).T.astype(BF16)
    v_s[...] = v_ref[...].astype(BF16)
    ckt = ck_ref[...].T.astype(BF16)
    cvb = cv_ref[...].astype(BF16)

    qc = lax.broadcasted_iota(jnp.int32, (GRID_W, LANES), 0)
    kl = lax.broadcasted_iota(jnp.int32, (GRID_W, LANES), 1)
    kc = jnp.where(kl < GRID_W, kl, kl - GRID_W)
    cs = jnp.clip(qc - WIN_COLS // 2, 0, GRID_W - WIN_COLS)
    col_ok = jnp.logical_and(kc >= cs, kc < cs + WIN_COLS)
    left = kl < GRID_W
    mask_of = {
        (True, True): col_ok,
        (True, False): jnp.logical_and(col_ok, left),
        (False, True): jnp.logical_and(col_ok, jnp.logical_not(left)),
    }
    h0q = lax.broadcasted_iota(jnp.int32, (QB_ROWS * GRID_W, LANES), 1) < HEAD_DIM
    kr_win = min(WIN_ROWS, ROWS)

    for p in range(ROWS // QB_ROWS):
        u0, nrows = _band_of_block(p)
        q_lo = p * QB_ROWS * GRID_W
        ktb = kt_s[:, u0 * GRID_W:(u0 + nrows) * GRID_W]
        vb = v_s[u0 * GRID_W:(u0 + nrows) * GRID_W, :]
        nq = QB_ROWS * GRID_W
        qm = jnp.concatenate([qm0_s[q_lo:q_lo + nq, :], qm1_s[q_lo:q_lo + nq, :]], axis=0)
        n_loc = nrows * GRID_W
        s_all = jnp.dot(qm, jnp.concatenate([ktb, ckt], axis=1), preferred_element_type=F32)
        s_loc, s_ctx = s_all[:, :n_loc], s_all[:, n_loc:]
        p_rows, l_rows = [], []
        for hh in range(2):
            for i in range(QB_ROWS):
                r = p * QB_ROWS + i
                r0 = min(max(r - kr_win // 2, 0), ROWS - kr_win)
                rows = slice(hh * nq + i * GRID_W, hh * nq + (i + 1) * GRID_W)
                sbs = {}
                for m in range(nrows // 2):
                    kr = u0 + 2 * m
                    ok_l = r0 <= kr < r0 + kr_win
                    ok_r = r0 <= kr + 1 < r0 + kr_win
                    if ok_l or ok_r:
                        sb = s_loc[rows, m * LANES:(m + 1) * LANES] + bp_ref[hh, kr - r + WIN_ROWS]
                        sbs[m] = jnp.where(mask_of[(ok_l, ok_r)], sb, NEG_INF)
                sc = s_ctx[rows, :]
                mx = jnp.maximum(
                    jnp.max(functools.reduce(jnp.maximum, sbs.values()), axis=-1, keepdims=True),
                    jnp.max(sc, axis=-1, keepdims=True))
                pbs = {m: jnp.exp(sb - mx) for m, sb in sbs.items()}
                pc = jnp.exp(sc - mx)
                l_rows.append(jnp.sum(functools.reduce(jnp.add, pbs.values()), axis=-1, keepdims=True)
                              + jnp.sum(pc, axis=-1, keepdims=True))
                zero = jnp.zeros((GRID_W, LANES), BF16)
                p_rows.append(jnp.concatenate(
                    [pbs[m].astype(BF16) if m in pbs else zero for m in range(nrows // 2)]
                    + [pc.astype(BF16)], axis=1))
        o = jnp.dot(jnp.concatenate(p_rows, axis=0), jnp.concatenate([vb, cvb], axis=0),
                    preferred_element_type=F32)
        o = o / jnp.concatenate(l_rows, axis=0)
        o_ref[q_lo:q_lo + nq, :] = jnp.where(h0q, o[:nq], o[nq:])


def _lat_attn_call(qkv, ctx_k, ctx_v, qg2, kg2, bias_pairs):
    blk = lambda c: pl.BlockSpec((None, DEC_SEQ, LANES), lambda hp, b, c=c: (c * HP + hp, b, 0))
    cblk = pl.BlockSpec((None, None, ctx_k.shape[2], LANES), lambda hp, b: (b, hp, 0, 0))
    return pl.pallas_call(
        _lat_attn_kernel,
        grid=(HP, DEC_BATCH),
        in_specs=[blk(0), blk(1), blk(2), cblk, cblk,
                  pl.BlockSpec((1, LANES), lambda hp, b: (0, 0)),
                  pl.BlockSpec((1, LANES), lambda hp, b: (0, 0)),
                  pl.BlockSpec((2, 2 * WIN_ROWS, GRID_W, LANES), lambda hp, b: (hp, 0, 0, 0))],
        out_specs=pl.BlockSpec((None, DEC_SEQ, LANES), lambda hp, b: (hp, b, 0)),
        out_shape=jax.ShapeDtypeStruct((HP, T_LAT, LANES), F32),
        scratch_shapes=[pltpu.VMEM((DEC_SEQ, LANES), BF16), pltpu.VMEM((DEC_SEQ, LANES), BF16),
                        pltpu.VMEM((LANES, DEC_SEQ), BF16), pltpu.VMEM((DEC_SEQ, LANES), BF16)],
        compiler_params=_cparams(("parallel", "parallel")),
        name="lat_attn",
    )(qkv, qkv, qkv, ctx_k, ctx_v, qg2, kg2, bias_pairs)


def _bias_pairs(rpb):
    cols = np.arange(GRID_W)
    col_idx = np.clip(cols[None, :] - cols[:, None], -(WIN_COLS - 1), WIN_COLS - 1) + WIN_COLS - 1
    rc = rpb[:, :, col_idx]
    left = jnp.pad(rc, ((0, 0), (1, 0), (0, 0), (0, 0)))
    right = jnp.pad(rc, ((0, 0), (0, 1), (0, 0), (0, 0)))
    return jnp.concatenate([left, right], axis=-1)


def _ffn_kernel(x_ref, g_ref, mod_ref, w1_ref, w3_ref, w2_ref, o_ref):
    mod = mod_ref[0]
    x = x_ref[...]
    h = _modnorm(x, g_ref[...], mod[4:5], mod[3:4]).astype(BF16)
    a = _silu(jnp.dot(h, w1_ref[...], preferred_element_type=F32)) \
        * jnp.dot(h, w3_ref[...], preferred_element_type=F32)
    y = jnp.dot(a.astype(BF16), w2_ref[...], preferred_element_type=F32)
    o_ref[...] = x + mod[5:6] * y


def _ffn_call(x, ln_g, mod, w13, w2, layer):
    resident = dict(pipeline_mode=pl.Buffered(1))
    return pl.pallas_call(
        _ffn_kernel,
        grid=(T // TM,),
        in_specs=[
            pl.BlockSpec((TM, D), lambda i: (i, 0)),
            pl.BlockSpec((1, D), lambda i: (0, 0)),
            pl.BlockSpec((1, 6, D), lambda i: (_cond_of_row(i * TM), 0, 0)),
            pl.BlockSpec((None, D, D_FF), lambda i: (layer, 0, 0), **resident),
            pl.BlockSpec((None, D, D_FF), lambda i: (layer, 0, 1), **resident),
            pl.BlockSpec((None, D_FF, D), lambda i: (layer, 0, 0), **resident),
        ],
        out_specs=pl.BlockSpec((TM, D), lambda i: (i, 0)),
        out_shape=jax.ShapeDtypeStruct((T, D), F32),
        compiler_params=_cparams(("parallel",)),
        name="ffn_dense",
    )(x, ln_g.reshape(1, D), mod, w13, w13, w2)


def _s5_kernel(u_ref, bm_ref, cm_ref, ar_ref, ai_ref, hr_ref, hi_ref,
               y_ref, fr_ref, fi_ref, bu_s, st_s, cr_s, ci_s, *, nb, steps):
    d = pl.program_id(0)
    tb = pl.program_id(3)
    rows = nb * steps
    pitch = steps + S5_ROW_PAD
    nch = S5_LANES // LANES
    u = u_ref[...].reshape(rows, S5_SLAB).astype(BF16)
    bu = jnp.dot(u, bm_ref[0, 0], preferred_element_type=F32)
    for c in range(2 * nch):
        for k in range(nb):
            bu_s[c, k * pitch:k * pitch + steps, :] = bu[k * steps:(k + 1) * steps, c * LANES:(c + 1) * LANES]

    @pl.when(tb == 0)
    def _():
        cr_s[...] = hr_ref[0, 0]
        ci_s[...] = hi_ref[0, 0]

    ar = [jnp.broadcast_to(ar_ref[0, 0][:, c * LANES:(c + 1) * LANES], (nb, LANES)) for c in range(nch)]
    ai = [jnp.broadcast_to(ai_ref[0, 0][:, c * LANES:(c + 1) * LANES], (nb, LANES)) for c in range(nch)]

    def step(i, carry):
        l = jnp.where(d == 0, i, steps - 1 - i)
        idx = pl.ds(l, nb, stride=pitch)
        out = []
        for c in range(nch):
            sr, si = carry[2 * c], carry[2 * c + 1]
            nr = ar[c] * sr - ai[c] * si + bu_s[c, idx, :]
            ni = ar[c] * si + ai[c] * sr + bu_s[nch + c, idx, :]
            st_s[c, idx, :] = nr
            st_s[nch + c, idx, :] = ni
            out += [nr, ni]
        return tuple(out)

    init = []
    for c in range(nch):
        init += [cr_s[:, c * LANES:(c + 1) * LANES], ci_s[:, c * LANES:(c + 1) * LANES]]
    fin = lax.fori_loop(0, steps, step, tuple(init), unroll=S5_UNROLL)
    sr = jnp.concatenate([fin[2 * c] for c in range(nch)], axis=1)
    si = jnp.concatenate([fin[2 * c + 1] for c in range(nch)], axis=1)
    cr_s[...] = sr
    ci_s[...] = si
    fr_ref[0, 0] = sr
    fi_ref[0, 0] = si
    states = jnp.concatenate(
        [jnp.concatenate([st_s[c, k * pitch:k * pitch + steps, :] for k in range(nb)], axis=0)
         for c in range(2 * nch)], axis=1).astype(BF16)
    y = jnp.dot(states, cm_ref[0, 0], preferred_element_type=F32)
    y_ref[...] = y.reshape(nb, steps, S5_SLAB)


def _s5_call(u, bmat, cmat, a_re, a_im, h_re, h_im, *, row0, n_seq, seq_len, nb, steps, name):
    n_sg, n_tb = n_seq // nb, seq_len // steps
    sg0 = row0 // (nb * seq_len)

    def tbi(d, tb):
        return jnp.where(d == 0, tb, n_tb - 1 - tb)

    par = lambda last: pl.BlockSpec((1, 1) + last, lambda d, j, sg, tb: (d, j, 0, 0))
    st = pl.BlockSpec((1, 1, nb, S5_LANES), lambda d, j, sg, tb: (d, j, sg, 0))
    y, f_re, f_im = pl.pallas_call(
        functools.partial(_s5_kernel, nb=nb, steps=steps),
        grid=(2, S5_NSLAB, n_sg, n_tb),
        in_specs=[
            pl.BlockSpec((None, nb, steps, S5_SLAB), lambda d, j, sg, tb: (j, sg0 + sg, tbi(d, tb), 0)),
            par((S5_SLAB, 2 * S5_LANES)),
            par((2 * S5_LANES, S5_SLAB)),
            par((1, S5_LANES)), par((1, S5_LANES)),
            st, st,
        ],
        out_specs=[
            pl.BlockSpec((None, None, nb, steps, S5_SLAB), lambda d, j, sg, tb: (d, j, sg, tbi(d, tb), 0)),
            st, st,
        ],
        out_shape=[
            jax.ShapeDtypeStruct((2, S5_NSLAB, n_seq, seq_len, S5_SLAB), F32),
            jax.ShapeDtypeStruct((2, S5_NSLAB, n_seq, S5_LANES), F32),
            jax.ShapeDtypeStruct((2, S5_NSLAB, n_seq, S5_LANES), F32),
        ],
        scratch_shapes=[pltpu.VMEM((2 * S5_LANES // LANES, nb * (steps + S5_ROW_PAD), LANES), F32)] * 2
                       + [pltpu.VMEM((nb, S5_LANES), F32)] * 2,
        compiler_params=_cparams(("parallel", "parallel", "parallel", "arbitrary")),
        name=name,
    )(u.reshape(S5_NSLAB, T // seq_len, seq_len, S5_SLAB), bmat, cmat, a_re, a_im, h_re, h_im)
    return y.reshape(2, S5_NSLAB, n_seq * seq_len, S5_SLAB), f_re, f_im


def _s5_params(lam_re, lam_im, log_step, b_re, b_im, c_re, c_im):
    step = jnp.exp(log_step)[..., None]
    zr, zi = lam_re * step, lam_im * step
    mag = jnp.exp(zr)
    a_re, a_im = mag * jnp.cos(zi), mag * jnp.sin(zi)
    nr, ni = a_re - 1.0, a_im
    den = lam_re * lam_re + lam_im * lam_im
    k_re = (nr * lam_re + ni * lam_im) / den
    k_im = (ni * lam_re - nr * lam_im) / den
    bb_re = k_re[..., None] * b_re - k_im[..., None] * b_im
    bb_im = k_re[..., None] * b_im + k_im[..., None] * b_re
    gl = S5_SLAB // SSM_GROUP
    eye = jnp.eye(gl, dtype=F32)

    def bdiag_in(w):
        w = w.reshape(2, S5_NSLAB, gl, SSM_STATE, SSM_GROUP)
        return jnp.einsum('dsgnp,gh->dsgphn', w, eye).reshape(2, S5_NSLAB, S5_SLAB, S5_LANES)

    def bdiag_out(w):
        w = w.reshape(2, S5_NSLAB, gl, SSM_GROUP, SSM_STATE)
        return jnp.einsum('dsgpn,gh->dsgnhp', w, eye).reshape(2, S5_NSLAB, S5_LANES, S5_SLAB)

    bmat = jnp.concatenate([bdiag_in(bb_re), bdiag_in(bb_im)], axis=-1).astype(BF16)
    cmat = jnp.concatenate([bdiag_out(c_re), -bdiag_out(c_im)], axis=-2).astype(BF16)
    slab = lambda a: a.reshape(2, S5_NSLAB, 1, S5_LANES)
    return bmat, cmat, slab(a_re), slab(a_im)


def _state_to_slabs(h):
    b = h.shape[0]
    return jnp.transpose(h.reshape(b, 2, S5_NSLAB, S5_LANES), (1, 2, 0, 3))


def _slabs_to_state(f):
    b = f.shape[2]
    return jnp.transpose(f, (2, 0, 1, 3)).reshape(b, 2, SSM_GROUPS, SSM_STATE)


def _glu_res_kernel(u_ref, yl_ref, yc_ref, x_ref, mod_ref, d_ref, w_ref, b_ref, o_ref):
    is_lat = pl.program_id(0) < LAT_TILES
    y = jnp.where(is_lat, _load_slabs(yl_ref.at[0]) + _load_slabs(yl_ref.at[1]),
                  _load_slabs(yc_ref.at[0]) + _load_slabs(yc_ref.at[1]))
    yt = _load_slabs(u_ref) * d_ref[...] + y
    z = _gelu_tanh(yt)
    gate = _sigmoid(jnp.dot(z.astype(BF16), w_ref[...], preferred_element_type=F32) + b_ref[...])
    o_ref[...] = x_ref[...] + mod_ref[0][2:3] * (z * gate)


def _glu_res_call(u, y_lat, y_ctx, x, mod, d_skip, w_glu, b_glu):
    return pl.pallas_call(
        _glu_res_kernel,
        grid=(T // TM,),
        in_specs=[
            pl.BlockSpec((S5_NSLAB, TM, LANES), lambda i: (0, i, 0)),
            pl.BlockSpec((2, S5_NSLAB, TM, LANES), lambda i: (0, 0, _lat_tile(i), 0)),
            pl.BlockSpec((2, S5_NSLAB, TM, LANES), lambda i: (0, 0, _ctx_tile(i), 0)),
            pl.BlockSpec((TM, D), lambda i: (i, 0)),
            pl.BlockSpec((1, 6, D), lambda i: (_cond_of_row(i * TM), 0, 0)),
            pl.BlockSpec((1, D), lambda i: (0, 0)),
            pl.BlockSpec((D, D), lambda i: (0, 0)),
            pl.BlockSpec((1, D), lambda i: (0, 0)),
        ],
        out_specs=pl.BlockSpec((TM, D), lambda i: (i, 0)),
        out_shape=jax.ShapeDtypeStruct((T, D), F32),
        compiler_params=_cparams(("parallel",)),
        name="s5_glu_res",
    )(u, y_lat, y_ctx, x, mod, d_skip.reshape(1, D), w_glu, b_glu.reshape(1, D))


def _conv_kernel(bg_ref, cg_ref, xi_ref, cgp_ref, xip_ref, cgn_ref, xin_ref,
                 x_ref, mod_ref, cw_ref, cb_ref, w_ref, o_ref):
    i = pl.program_id(0)
    tiles_per_seq = DEC_SEQ // TM_CONV
    is_lat = i < T_LAT // TM_CONV
    first = jnp.logical_or(jnp.logical_not(is_lat), i % tiles_per_seq == 0)
    last = jnp.logical_or(jnp.logical_not(is_lat), i % tiles_per_seq == tiles_per_seq - 1)
    z = cg_ref[...] * xi_ref[...]
    zp_row = jnp.where(first, 0.0, cgp_ref[7:8, :] * xip_ref[7:8, :])
    zn_row = jnp.where(last, 0.0, cgn_ref[0:1, :] * xin_ref[0:1, :])
    row = lax.broadcasted_iota(jnp.int32, (TM_CONV, D), 0)
    zp = jnp.where(row == 0, zp_row, pltpu.roll(z, 1, axis=0))
    zn = jnp.where(row == TM_CONV - 1, zn_row, pltpu.roll(z, TM_CONV - 1, axis=0))
    cw = cw_ref[...]
    zc = cw[0:1] * zp + cw[1:2] * z + cw[2:3] * zn + cb_ref[...]
    a = (bg_ref[...] * zc).astype(BF16)
    y = jnp.dot(a, w_ref[...], preferred_element_type=F32)
    o_ref[...] = x_ref[...] + mod_ref[0][2:3] * y


def _conv_call(proj, x, mod, conv_w, conv_b, w_out):
    r8 = TM_CONV // 8
    n8 = T // 8
    col = lambda c: pl.BlockSpec((TM_CONV, D), lambda i, c=c: (i, c))
    prev = lambda c: pl.BlockSpec((8, D), lambda i, c=c: (jnp.maximum(i * r8 - 1, 0), c))
    nxt = lambda c: pl.BlockSpec((8, D), lambda i, c=c: (jnp.minimum((i + 1) * r8, n8 - 1), c))
    return pl.pallas_call(
        _conv_kernel,
        grid=(T // TM_CONV,),
        in_specs=[
            col(0), col(1), col(2), prev(1), prev(2), nxt(1), nxt(2),
            pl.BlockSpec((TM_CONV, D), lambda i: (i, 0)),
            pl.BlockSpec((1, 6, D), lambda i: (_cond_of_row(i * TM_CONV), 0, 0)),
            pl.BlockSpec((3, D), lambda i: (0, 0)),
            pl.BlockSpec((1, D), lambda i: (0, 0)),
            pl.BlockSpec((D, D), lambda i: (0, 0)),
        ],
        out_specs=pl.BlockSpec((TM_CONV, D), lambda i: (i, 0)),
        out_shape=jax.ShapeDtypeStruct((T, D), F32),
        compiler_params=_cparams(("parallel",)),
        name="conv_mix",
    )(proj, proj, proj, proj, proj, proj, proj, x, mod, conv_w, conv_b.reshape(1, D), w_out)


def _router_kernel(x_ref, g_ref, mod_ref, r_ref, h_ref, rt_ref):
    mod = mod_ref[0]
    h = _modnorm(x_ref[...], g_ref[...], mod[4:5], mod[3:4])
    h_ref[...] = h
    logits = jnp.dot(h, r_ref[...], preferred_element_type=F32, precision=lax.Precision.HIGHEST)
    lane = lax.broadcasted_iota(jnp.int32, logits.shape, 1)
    logits = jnp.where(lane < N_EXPERTS, logits, -jnp.inf)
    m1 = jnp.max(logits, axis=-1, keepdims=True)
    i1 = jnp.min(jnp.where(logits == m1, lane, LANES), axis=-1, keepdims=True)
    rest = jnp.where(lane == i1, -jnp.inf, logits)
    m2 = jnp.max(rest, axis=-1, keepdims=True)
    i2 = jnp.min(jnp.where(rest == m2, lane, LANES), axis=-1, keepdims=True)
    e2 = jnp.exp(m2 - m1)
    w1 = 1.0 / (1.0 + e2)
    w2 = e2 / (1.0 + e2)
    rt_ref[...] = jnp.where(lane == 0, i1.astype(F32),
                            jnp.where(lane == 1, i2.astype(F32),
                                      jnp.where(lane == 2, w1, jnp.where(lane == 3, w2, 0.0))))


def _router_call(x, ln_g, mod, router_pad):
    return pl.pallas_call(
        _router_kernel,
        grid=(T // TM,),
        in_specs=[
            pl.BlockSpec((TM, D), lambda i: (i, 0)),
            pl.BlockSpec((1, D), lambda i: (0, 0)),
            pl.BlockSpec((1, 6, D), lambda i: (_cond_of_row(i * TM), 0, 0)),
            pl.BlockSpec((D, LANES), lambda i: (0, 0)),
        ],
        out_specs=[pl.BlockSpec((TM, D), lambda i: (i, 0)),
                   pl.BlockSpec((TM, LANES), lambda i: (i, 0))],
        out_shape=[jax.ShapeDtypeStruct((T, D), F32), jax.ShapeDtypeStruct((T, LANES), F32)],
        compiler_params=_cparams(("parallel",)),
        name="moe_router",
    )(x, ln_g.reshape(1, D), mod, router_pad)


R_MAX = 2 * T + N_EXPERTS * TM_MOE
N_TILES = R_MAX // TM_MOE
DMA_UNROLL = 8
GATHER_SHIFT = 3
GATHER_SPREAD = 1 << GATHER_SHIFT


def _row_copy(src_hbm, src_row, dst_vmem, dst_row, sem):
    return pltpu.make_async_copy(src_hbm.at[pl.ds(src_row, 1), :], dst_vmem.at[pl.ds(dst_row, 1), :], sem)


def _gather_kernel(nv_ref, src_ref, h_hbm, o_ref, buf, sem):
    i = pl.program_id(0)
    nv = nv_ref[0]

    def issue_tile(t):
        slot = t % 2
        base = t * TM_MOE

        def issue(j, c):
            r = (j & (GATHER_SPREAD - 1)) * (TM_MOE // GATHER_SPREAD) + lax.shift_right_logical(j, GATHER_SHIFT)
            _row_copy(h_hbm, src_ref[base + r], buf.at[slot], r, sem.at[slot]).start()
            return c

        lax.fori_loop(0, TM_MOE, issue, 0, unroll=DMA_UNROLL)

    @pl.when(jnp.logical_and(i == 0, nv > 0))
    def _():
        issue_tile(0)

    @pl.when(i + 1 < nv)
    def _():
        issue_tile(i + 1)

    @pl.when(i < nv)
    def _():
        slot = i % 2
        pltpu.make_async_copy(h_hbm.at[pl.ds(0, TM_MOE), :], buf.at[slot], sem.at[slot]).wait()
        o_ref[...] = buf[slot].astype(BF16)

    @pl.when(i >= nv)
    def _():
        o_ref[...] = jnp.zeros_like(o_ref)


def _gather_call(n_valid, src_tok, h):
    return pl.pallas_call(
        _gather_kernel,
        grid_spec=pltpu.PrefetchScalarGridSpec(
            num_scalar_prefetch=2,
            grid=(N_TILES,),
            in_specs=[pl.BlockSpec(memory_space=pl.ANY)],
            out_specs=pl.BlockSpec((TM_MOE, D), lambda i, nv, src: (i, 0)),
            scratch_shapes=[pltpu.VMEM((2, TM_MOE, D), F32), pltpu.SemaphoreType.DMA((2,))],
        ),
        out_shape=jax.ShapeDtypeStruct((R_MAX, D), BF16),
        compiler_params=_cparams(("arbitrary",)),
        name="moe_gather",
    )(n_valid, src_tok, h)


def _moe_kernel(nv_ref, te_ref, tr_ref, xs_ref, w1_ref, w3_ref, w2_ref, o_ref, acc_s):
    i = pl.program_id(0)
    f = pl.program_id(1)
    nf = pl.num_programs(1)
    valid = i < nv_ref[0]
    half = TM_MOE // 2
    short = tr_ref[i] <= half

    @pl.when(jnp.logical_and(valid, f == 0))
    def _():
        acc_s[...] = jnp.zeros_like(acc_s)

    def swiglu_rows(rows):
        xs = xs_ref[:rows, :]
        w13 = jnp.concatenate([w1_ref[...].astype(BF16), w3_ref[...].astype(BF16)], axis=1)
        gu = jnp.dot(xs, w13, preferred_element_type=F32)
        a = _silu(gu[:, :TF_MOE]) * gu[:, TF_MOE:]
        acc_s[:rows, :] += jnp.dot(a.astype(BF16), w2_ref[...].astype(BF16), preferred_element_type=F32)

    @pl.when(jnp.logical_and(valid, jnp.logical_not(short)))
    def _():
        swiglu_rows(TM_MOE)

    @pl.when(jnp.logical_and(valid, short))
    def _():
        swiglu_rows(half)

    @pl.when(f == nf - 1)
    def _():
        o_ref[...] = jnp.where(valid, acc_s[...], 0.0)


def _moe_call(n_valid, tile_expert, tile_rows, xs, w13, w2, layer):
    nf = D_FF_EXPERT // TF_MOE

    def fe(i, f, nv):
        return jnp.where(i < nv[0], f, nf - 1)

    return pl.pallas_call(
        _moe_kernel,
        grid_spec=pltpu.PrefetchScalarGridSpec(
            num_scalar_prefetch=3,
            grid=(N_TILES, nf),
            in_specs=[
                pl.BlockSpec((TM_MOE, D), lambda i, f, nv, te, tr: (i, 0)),
                pl.BlockSpec((None, None, D, TF_MOE),
                             lambda i, f, nv, te, tr: (layer, te[i], 0, fe(i, f, nv))),
                pl.BlockSpec((None, None, D, TF_MOE),
                             lambda i, f, nv, te, tr: (layer, te[i], 0, nf + fe(i, f, nv))),
                pl.BlockSpec((None, None, TF_MOE, D),
                             lambda i, f, nv, te, tr: (layer, te[i], fe(i, f, nv), 0)),
            ],
            out_specs=pl.BlockSpec((TM_MOE, D), lambda i, f, nv, te, tr: (i, 0)),
            scratch_shapes=[pltpu.VMEM((TM_MOE, D), F32)],
        ),
        out_shape=jax.ShapeDtypeStruct((R_MAX, D), F32),
        compiler_params=_cparams(("arbitrary", "arbitrary")),
        name="moe_experts",
    )(n_valid, tile_expert, tile_rows, xs, w13, w13, w2)


TM_COMB = 512


def _combine_kernel(pa_ref, pb_ref, y_hbm, x_ref, rt_ref, mod_ref, o_ref, buf_a, buf_b, sem):
    i = pl.program_id(0)

    def issue_tile(t):
        slot = t % 2
        base = t * TM_COMB

        def issue(r, c):
            _row_copy(y_hbm, pa_ref[base + r], buf_a.at[slot], r, sem.at[slot]).start()
            _row_copy(y_hbm, pb_ref[base + r], buf_b.at[slot], r, sem.at[slot]).start()
            return c

        lax.fori_loop(0, TM_COMB, issue, 0, unroll=DMA_UNROLL)

    @pl.when(i == 0)
    def _():
        issue_tile(0)

    @pl.when(i + 1 < pl.num_programs(0))
    def _():
        issue_tile(i + 1)

    slot = i % 2
    for buf in (buf_a, buf_b):
        pltpu.make_async_copy(y_hbm.at[pl.ds(0, TM_COMB), :], buf.at[slot], sem.at[slot]).wait()
    rt = rt_ref[...]
    ffn = rt[:, 2:3] * buf_a[slot] + rt[:, 3:4] * buf_b[slot]
    o_ref[...] = x_ref[...] + mod_ref[0][5:6] * ffn


def _combine_call(pos_a, pos_b, y, x, rt, mod):
    return pl.pallas_call(
        _combine_kernel,
        grid_spec=pltpu.PrefetchScalarGridSpec(
            num_scalar_prefetch=2,
            grid=(T // TM_COMB,),
            in_specs=[
                pl.BlockSpec(memory_space=pl.ANY),
                pl.BlockSpec((TM_COMB, D), lambda i, pa, pb: (i, 0)),
                pl.BlockSpec((TM_COMB, LANES), lambda i, pa, pb: (i, 0)),
                pl.BlockSpec((1, 6, D), lambda i, pa, pb: (_cond_of_row(i * TM_COMB), 0, 0)),
            ],
            out_specs=pl.BlockSpec((TM_COMB, D), lambda i, pa, pb: (i, 0)),
            scratch_shapes=[pltpu.VMEM((2, TM_COMB, D), F32), pltpu.VMEM((2, TM_COMB, D), F32),
                            pltpu.SemaphoreType.DMA((2,))],
        ),
        out_shape=jax.ShapeDtypeStruct((T, D), F32),
        compiler_params=_cparams(("arbitrary",)),
        name="moe_combine",
    )(pos_a, pos_b, y, x, rt, mod)


def _routing_tables(rt):
    ea = rt[:, 0:2].astype(jnp.int32).reshape(-1)
    onehot = (ea[:, None] == jnp.arange(N_EXPERTS, dtype=jnp.int32)[None, :]).astype(jnp.int32)
    csum = jnp.cumsum(onehot, axis=0)
    rank = jnp.take_along_axis(csum, ea[:, None], axis=1)[:, 0] - 1
    counts = csum[-1]
    padded = ((counts + TM_MOE - 1) // TM_MOE) * TM_MOE
    ends = jnp.cumsum(padded)
    pos = (ends - padded)[ea] + rank
    n_valid = (ends[-1] // TM_MOE).astype(jnp.int32).reshape(1)
    tile_start = jnp.arange(N_TILES, dtype=jnp.int32) * TM_MOE
    expert_of_row = lambda r: jnp.sum((r[:, None] >= ends[None, :]).astype(jnp.int32), axis=1)
    tile_expert = jnp.minimum(expert_of_row(tile_start), expert_of_row(ends[-1:] - 1))
    real_end = (ends - padded + counts)[tile_expert]
    tile_rows = jnp.clip(real_end - tile_start, 0, TM_MOE).astype(jnp.int32)
    tok = jnp.arange(2 * T, dtype=jnp.int32) // 2
    src_tok = (jnp.arange(R_MAX, dtype=jnp.int32) % T).at[pos].set(tok)
    pos2 = pos.reshape(T, 2).astype(jnp.int32)
    return n_valid, tile_expert, tile_rows, src_tok, pos2[:, 0], pos2[:, 1]


def _moe_layer(x, ln_g, mod, router, w13, w2, layer):
    router_pad = jnp.pad(router, ((0, 0), (0, LANES - N_EXPERTS)))
    h, rt = _router_call(x, ln_g, mod, router_pad)
    n_valid, tile_expert, tile_rows, src_tok, pos_a, pos_b = _routing_tables(rt)
    xs = _gather_call(n_valid, src_tok, h)
    y = _moe_call(n_valid, tile_expert, tile_rows, xs, w13, w2, layer)
    return _combine_call(pos_a, pos_b, y, x, rt, mod)


def kernel(x_prompt, x_sample, c, cache_na_k, cache_na_v, state_ssm_re, state_ssm_im, c_ctx,
           ln1_g, ln2_g, ada_w, ada_b, na_w_qkv, na_w_o, na_q_g, na_k_g, na_rpb,
           ssm_lam_re, ssm_lam_im, ssm_log_step, ssm_b_re, ssm_b_im, ssm_c_re, ssm_c_im,
           ssm_d, ssm_w_glu, ssm_b_glu, cv_w_in, cv_conv_w, cv_conv_b, cv_w_out,
           ffn_w13, ffn_w2, moe_router, moe_w13, moe_w2):
    depth = ada_w.shape[0]
    x = (x_sample.reshape(T_LAT, D), x_prompt.reshape(T_CTX, D))
    conds = jnp.concatenate([c, c_ctx[None, :], jnp.zeros((N_COND - DEC_BATCH - 1, D), F32)], axis=0)
    mods = _ada_call(conds, ada_w, ada_b).reshape(depth, N_COND, 6, D)
    ffn_w13_b, ffn_w2_b = ffn_w13.astype(BF16), ffn_w2.astype(BF16)

    new_k = new_v = None
    s_re, s_im = [], []
    for l in range(depth):
        kind, j = l % 3, l // 3
        mod = mods[l]
        if kind == 0:
            qkv = _normproj_call(x, ln1_g[l], mod, na_w_qkv[j].astype(BF16), shift=0, scale=1,
                                 name="qkv_proj", slab_out=True)
            qg2 = jnp.tile(na_q_g[j], 2).reshape(1, LANES)
            kg2 = jnp.tile(na_k_g[j], 2).reshape(1, LANES)
            past = cache_na_k.shape[3]
            pair = lambda a: jnp.transpose(a[:, j].reshape(DEC_BATCH, HP, 2, past, HEAD_DIM),
                                           (0, 1, 3, 2, 4)).reshape(DEC_BATCH, HP, past, LANES)
            ctx_k, ctx_v = pair(cache_na_k), pair(cache_na_v)
            o_lat = _lat_attn_call(qkv, ctx_k, ctx_v, qg2, kg2, _bias_pairs(na_rpb[j]))
            o_ctx, new_k, new_v = _ctx_attn_call(qkv, qg2, kg2, new_k, new_v)
            x = _proj_res_call(o_lat, o_ctx, x, mod, na_w_o[j].astype(BF16), gate=2, name="attn_out")
        elif kind == 1:
            u = _norm_call(x, ln1_g[l], mod, shift=0, scale=1, name="s5_norm")
            bmat, cmat, a_re, a_im = _s5_params(ssm_lam_re[j], ssm_lam_im[j], ssm_log_step[j],
                                                ssm_b_re[j], ssm_b_im[j], ssm_c_re[j], ssm_c_im[j])
            y_lat, _, _ = _s5_call(u, bmat, cmat, a_re, a_im,
                                   _state_to_slabs(state_ssm_re[:, j]), _state_to_slabs(state_ssm_im[:, j]),
                                   row0=0, n_seq=DEC_BATCH, seq_len=DEC_SEQ,
                                   nb=DEC_BATCH, steps=S5_T_LAT, name="s5_lat")
            zero = jnp.zeros((2, S5_NSLAB, BATCH, S5_LANES), F32)
            y_ctx, f_re, f_im = _s5_call(u, bmat, cmat, a_re, a_im, zero, zero,
                                         row0=T_LAT, n_seq=BATCH, seq_len=SEQ, nb=8, steps=SEQ,
                                         name="s5_ctx")
            s_re.append(_slabs_to_state(f_re))
            s_im.append(_slabs_to_state(f_im))
            x = _glu_res_call(u, y_lat, y_ctx, x, mod, ssm_d[j], ssm_w_glu[j].astype(BF16), ssm_b_glu[j])
        else:
            proj = _normproj_call(x, ln1_g[l], mod, cv_w_in[j].astype(BF16), shift=0, scale=1,
                                  name="conv_in")
            x = _conv_call(proj, x, mod, cv_conv_w[j], cv_conv_b[j], cv_w_out[j].astype(BF16))
        jj = l // 2
        if l % 2 == 0:
            x = _ffn_call(x, ln2_g[l], mod, ffn_w13_b, ffn_w2_b, jj)
        else:
            x = _moe_layer(x, ln2_g[l], mod, moe_router[jj], moe_w13, moe_w2, jj)

    y_sample = x[:T_LAT].reshape(DEC_BATCH, DEC_SEQ, D)
    y_prompt = x[T_LAT:].reshape(BATCH, SEQ, D)
    return (y_prompt, y_sample, new_k, new_v, jnp.stack(s_re, axis=1), jnp.stack(s_im, axis=1))
```

```python
import functools
import math

import numpy as np
import jax
import jax.numpy as jnp
from jax import lax
from jax.experimental import pallas as pl
from jax.experimental.pallas import tpu as pltpu

F32 = jnp.float32
BF16 = jnp.bfloat16

D = 1024
BATCH, SEQ = 16, 256
DEC_BATCH, DEC_SEQ = 4, 2048
GRID_W = 64
ROWS = DEC_SEQ // GRID_W
N_HEADS, HEAD_DIM = 16, 64
WIN_ROWS, WIN_COLS = 8, 16
SSM_GROUP, SSM_GROUPS, SSM_STATE = 16, 64, 64
D_FF = 2816
N_EXPERTS = 8
D_FF_EXPERT = 3584
EPS = 1e-6
NEG_INF = -1e30
SCALE = HEAD_DIM ** -0.5

T_LAT = DEC_BATCH * DEC_SEQ
T_CTX = BATCH * SEQ
T = T_LAT + T_CTX
N_COND = 8
CTX_COND = DEC_BATCH

LANES = 128
VMEM_LIMIT = 56 * 1024 * 1024

TM = 512
TM_CONV = 256
TM_MOE = 1024
TF_MOE = 512
HP = N_HEADS // 2
QB_ROWS = 4
S5_SLAB = 128
S5_NSLAB = D // S5_SLAB
S5_LANES = (S5_SLAB // SSM_GROUP) * SSM_STATE
S5_T_LAT = 512
S5_ROW_PAD = 4
S5_UNROLL = 4


def _cparams(sem):
    return pltpu.CompilerParams(dimension_semantics=sem, vmem_limit_bytes=VMEM_LIMIT)


def _cond_of_row(row):
    return jnp.minimum(row // DEC_SEQ, CTX_COND)


def _modnorm(x, g, sc, sh):
    ms = jnp.mean(x * x, axis=-1, keepdims=True)
    y = x * lax.rsqrt(ms + EPS) * g
    return y * (1.0 + sc) + sh


def _sigmoid(x):
    return 1.0 / (1.0 + jnp.exp(-x))


def _silu(x):
    return x * _sigmoid(x)


def _gelu_tanh(x):
    c = math.sqrt(2.0 / math.pi)
    return 0.5 * x * (1.0 + jnp.tanh(c * (x + 0.044715 * (x * x * x))))


def _ada_kernel(c_ref, w_ref, b_ref, o_ref):
    s = _silu(c_ref[...]).astype(BF16)
    o_ref[0] = jnp.dot(s, w_ref[0].astype(BF16), preferred_element_type=F32) + b_ref[0]


def _ada_call(conds, ada_w, ada_b):
    depth = ada_w.shape[0]
    tn = 1536
    return pl.pallas_call(
        _ada_kernel,
        grid=(depth, 6 * D // tn),
        in_specs=[
            pl.BlockSpec((N_COND, D), lambda l, n: (0, 0)),
            pl.BlockSpec((1, D, tn), lambda l, n: (l, 0, n)),
            pl.BlockSpec((1, 1, tn), lambda l, n: (l, 0, n)),
        ],
        out_specs=pl.BlockSpec((1, N_COND, tn), lambda l, n: (l, 0, n)),
        out_shape=jax.ShapeDtypeStruct((depth, N_COND, 6 * D), F32),
        compiler_params=_cparams(("parallel", "parallel")),
        name="ada_mod",
    )(conds, ada_w, ada_b.reshape(depth, 1, 6 * D))


LAT_TILES = T_LAT // TM


def _lat_tile(i):
    return jnp.minimum(i, LAT_TILES - 1)


def _ctx_tile(i):
    return jnp.maximum(i - LAT_TILES, 0)


def _rows_specs(x):
    if isinstance(x, tuple):
        return [pl.BlockSpec((TM, D), lambda i: (_lat_tile(i), 0)),
                pl.BlockSpec((TM, D), lambda i: (_ctx_tile(i), 0))], list(x)
    return [pl.BlockSpec((TM, D), lambda i: (i, 0))], [x]


def _rows_value(refs):
    if len(refs) == 2:
        return jnp.where(pl.program_id(0) < LAT_TILES, refs[0][...], refs[1][...])
    return refs[0][...]


def _store_slabs(o_ref, val):
    for c in range(val.shape[1] // LANES):
        o_ref[c] = val[:, c * LANES:(c + 1) * LANES]


def _load_slabs(ref):
    return jnp.concatenate([ref[c] for c in range(ref.shape[0])], axis=1)


def _normproj_kernel(*refs, shift, scale, slab_out):
    *x_refs, g_ref, mod_ref, w_ref, o_ref = refs
    mod = mod_ref[0]
    h = _modnorm(_rows_value(x_refs), g_ref[...], mod[scale:scale + 1], mod[shift:shift + 1])
    res = jnp.dot(h.astype(BF16), w_ref[...], preferred_element_type=F32)
    if slab_out:
        _store_slabs(o_ref, res)
    else:
        o_ref[...] = res


def _normproj_call(x, ln_g, mod, w, *, shift, scale, name, slab_out=False):
    n = w.shape[1]
    x_specs, x_args = _rows_specs(x)
    if slab_out:
        out_spec = pl.BlockSpec((n // LANES, TM, LANES), lambda i: (0, i, 0))
        out_shape = jax.ShapeDtypeStruct((n // LANES, T, LANES), F32)
    else:
        out_spec = pl.BlockSpec((TM, n), lambda i: (i, 0))
        out_shape = jax.ShapeDtypeStruct((T, n), F32)
    return pl.pallas_call(
        functools.partial(_normproj_kernel, shift=shift, scale=scale, slab_out=slab_out),
        grid=(T // TM,),
        in_specs=x_specs + [
            pl.BlockSpec((1, D), lambda i: (0, 0)),
            pl.BlockSpec((1, 6, D), lambda i: (_cond_of_row(i * TM), 0, 0)),
            pl.BlockSpec((D, n), lambda i: (0, 0)),
        ],
        out_specs=out_spec,
        out_shape=out_shape,
        compiler_params=_cparams(("parallel",)),
        name=name,
    )(*x_args, ln_g.reshape(1, D), mod, w)


def _norm_kernel(x_ref, g_ref, mod_ref, o_ref, *, shift, scale):
    mod = mod_ref[0]
    _store_slabs(o_ref, _modnorm(x_ref[...], g_ref[...], mod[scale:scale + 1], mod[shift:shift + 1]))


def _norm_call(x, ln_g, mod, *, shift, scale, name):
    return pl.pallas_call(
        functools.partial(_norm_kernel, shift=shift, scale=scale),
        grid=(T // TM,),
        in_specs=[
            pl.BlockSpec((TM, D), lambda i: (i, 0)),
            pl.BlockSpec((1, D), lambda i: (0, 0)),
            pl.BlockSpec((1, 6, D), lambda i: (_cond_of_row(i * TM), 0, 0)),
        ],
        out_specs=pl.BlockSpec((D // LANES, TM, LANES), lambda i: (0, i, 0)),
        out_shape=jax.ShapeDtypeStruct((D // LANES, T, LANES), F32),
        compiler_params=_cparams(("parallel",)),
        name=name,
    )(x, ln_g.reshape(1, D), mod)


def _proj_res_kernel(*refs, gate, n_x):
    al_ref, ac_ref = refs[:2]
    x_refs = refs[2:2 + n_x]
    mod_ref, w_ref, o_ref = refs[2 + n_x:]
    a = jnp.where(pl.program_id(0) < LAT_TILES, _load_slabs(al_ref), _load_slabs(ac_ref))
    y = jnp.dot(a.astype(BF16), w_ref[...], preferred_element_type=F32)
    o_ref[...] = _rows_value(x_refs) + mod_ref[0][gate:gate + 1] * y


def _proj_res_call(a_lat, a_ctx, x, mod, w, *, gate, name):
    ns = D // LANES
    a_specs = [pl.BlockSpec((ns, TM, LANES), lambda i: (0, _lat_tile(i), 0)),
               pl.BlockSpec((ns, TM, LANES), lambda i: (0, _ctx_tile(i), 0))]
    a_args = [a_lat, a_ctx]
    x_specs, x_args = _rows_specs(x)
    return pl.pallas_call(
        functools.partial(_proj_res_kernel, gate=gate, n_x=len(x_args)),
        grid=(T // TM,),
        in_specs=a_specs + x_specs + [
            pl.BlockSpec((1, 6, D), lambda i: (_cond_of_row(i * TM), 0, 0)),
            pl.BlockSpec((D, D), lambda i: (0, 0)),
        ],
        out_specs=pl.BlockSpec((TM, D), lambda i: (i, 0)),
        out_shape=jax.ShapeDtypeStruct((T, D), F32),
        compiler_params=_cparams(("parallel",)),
        name=name,
    )(*a_args, *x_args, mod, w)


def _head_norm(x, g, head0):
    x2 = x * x
    s0 = jnp.sum(jnp.where(head0, x2, 0.0), axis=-1, keepdims=True)
    s1 = jnp.sum(jnp.where(head0, 0.0, x2), axis=-1, keepdims=True)
    ms = jnp.where(head0, s0, s1) * (1.0 / HEAD_DIM)
    return x * lax.rsqrt(ms + EPS) * g


def _nt_dot(a, b):
    return lax.dot_general(a, b, (((1,), (1,)), ((), ())), preferred_element_type=F32)


def _ctx_attn_kernel(*refs, n_prev):
    if n_prev:
        q_ref, k_ref, v_ref, qg_ref, kg_ref, pk_ref, pv_ref, o_ref, ko_ref, vo_ref = refs
        ko_ref[0, :n_prev] = pk_ref[0]
        vo_ref[0, :n_prev] = pv_ref[0]
    else:
        q_ref, k_ref, v_ref, qg_ref, kg_ref, o_ref, ko_ref, vo_ref = refs
    head0 = lax.broadcasted_iota(jnp.int32, (SEQ, LANES), 1) < HEAD_DIM
    for hp in range(HP):
        qn = _head_norm(q_ref[hp], qg_ref[...], head0) * SCALE
        kn = _head_norm(k_ref[hp], kg_ref[...], head0)
        v = v_ref[hp]
        ko_ref[0, n_prev, 2 * hp] = kn[:, :HEAD_DIM]
        ko_ref[0, n_prev, 2 * hp + 1] = kn[:, HEAD_DIM:]
        vo_ref[0, n_prev, 2 * hp] = v[:, :HEAD_DIM]
        vo_ref[0, n_prev, 2 * hp + 1] = v[:, HEAD_DIM:]
        qm = jnp.concatenate([jnp.where(head0, qn, 0.0), jnp.where(head0, 0.0, qn)], axis=0).astype(BF16)
        s = jnp.dot(qm, kn.T.astype(BF16), preferred_element_type=F32)
        p = jnp.exp(s - jnp.max(s, axis=-1, keepdims=True))
        l = jnp.sum(p, axis=-1, keepdims=True)
        o = jnp.dot(p.astype(BF16), v.astype(BF16), preferred_element_type=F32) / l
        o_ref[hp] = jnp.where(head0, o[:SEQ], o[SEQ:])


def _ctx_attn_call(qkv, qg2, kg2, prev_k, prev_v):
    row0 = T_LAT // SEQ
    n_prev = 0 if prev_k is None else prev_k.shape[1]
    blk = lambda c: pl.BlockSpec((HP, SEQ, LANES), lambda b, c=c: (c, row0 + b, 0))
    kv_blk = lambda n: pl.BlockSpec((1, n, N_HEADS, SEQ, HEAD_DIM), lambda b: (b, 0, 0, 0, 0))
    kv_shape = jax.ShapeDtypeStruct((BATCH, n_prev + 1, N_HEADS, SEQ, HEAD_DIM), F32)
    prev_specs = [kv_blk(n_prev)] * 2 if n_prev else []
    prev_args = [prev_k, prev_v] if n_prev else []
    return pl.pallas_call(
        functools.partial(_ctx_attn_kernel, n_prev=n_prev),
        grid=(BATCH,),
        in_specs=[blk(0), blk(1), blk(2),
                  pl.BlockSpec((1, LANES), lambda b: (0, 0)),
                  pl.BlockSpec((1, LANES), lambda b: (0, 0))] + prev_specs,
        out_specs=[pl.BlockSpec((HP, SEQ, LANES), lambda b: (0, b, 0)), kv_blk(n_prev + 1), kv_blk(n_prev + 1)],
        out_shape=[jax.ShapeDtypeStruct((HP, T_CTX, LANES), F32), kv_shape, kv_shape],
        compiler_params=_cparams(("parallel",)),
        name="ctx_attn",
    )(qkv, qkv, qkv, qg2, kg2, *prev_args)


def _band_of_block(p):
    r_lo, r_hi = QB_ROWS * p, QB_ROWS * p + QB_ROWS - 1
    kr = min(WIN_ROWS, ROWS)
    lo = min(max(r_lo - kr // 2, 0), ROWS - kr)
    hi = min(max(r_hi - kr // 2, 0), ROWS - kr) + kr
    lo -= lo % 2
    hi += hi % 2
    return lo, hi - lo


def _lat_attn_kernel(q_ref, k_ref, v_ref, ck_ref, cv_ref, qg_ref, kg_ref, bp_ref, o_ref,
                     qm0_s, qm1_s, kt_s, v_s):
    head0 = lax.broadcasted_iota(jnp.int32, (DEC_SEQ, LANES), 1) < HEAD_DIM
    qn = _head_norm(q_ref[...], qg_ref[...], head0)
    qm0_s[...] = jnp.where(head0, qn * SCALE, 0.0).astype(BF16)
    qm1_s[...] = jnp.where(head0, 0.0, qn * SCALE).astype(BF16)
    kt_s[...] = _head_norm(k_ref[...], kg_ref[...], head0).T.astype(BF16)
    v_s[...] = v_ref[...].astype(BF16)
    ckt = ck_ref[...].T.astype(BF16)
    cvb = cv_ref[...].astype(BF16)

    qc = lax.broadcasted_iota(jnp.int32, (GRID_W, LANES), 0)
    kl = lax.broadcasted_iota(jnp.int32, (GRID_W, LANES), 1)
    kc = jnp.where(kl < GRID_W, kl, kl - GRID_W)
    cs = jnp.clip(qc - WIN_COLS // 2, 0, GRID_W - WIN_COLS)
    col_ok = jnp.logical_and(kc >= cs, kc < cs + WIN_COLS)
    left = kl < GRID_W
    mask_of = {
        (True, True): col_ok,
        (True, False): jnp.logical_and(col_ok, left),
        (False, True): jnp.logical_and(col_ok, jnp.logical_not(left)),
    }
    h0q = lax.broadcasted_iota(jnp.int32, (QB_ROWS * GRID_W, LANES), 1) < HEAD_DIM
    kr_win = min(WIN_ROWS, ROWS)

    for p in range(ROWS // QB_ROWS):
        u0, nrows = _band_of_block(p)
        q_lo = p * QB_ROWS * GRID_W
        ktb = kt_s[:, u0 * GRID_W:(u0 + nrows) * GRID_W]
        vb = v_s[u0 * GRID_W:(u0 + nrows) * GRID_W, :]
        nq = QB_ROWS * GRID_W
        qm = jnp.concatenate([qm0_s[q_lo:q_lo + nq, :], qm1_s[q_lo:q_lo + nq, :]], axis=0)
        n_loc = nrows * GRID_W
        s_all = jnp.dot(qm, jnp.concatenate([ktb, ckt], axis=1), preferred_element_type=F32)
        s_loc, s_ctx = s_all[:, :n_loc], s_all[:, n_loc:]
        p_rows, l_rows = [], []
        for hh in range(2):
            for i in range(QB_ROWS):
                r = p * QB_ROWS + i
                r0 = min(max(r - kr_win // 2, 0), ROWS - kr_win)
                rows = slice(hh * nq + i * GRID_W, hh * nq + (i + 1) * GRID_W)
                sbs = {}
                for m in range(nrows // 2):
                    kr = u0 + 2 * m
                    ok_l = r0 <= kr < r0 + kr_win
                    ok_r = r0 <= kr + 1 < r0 + kr_win
                    if ok_l or ok_r:
                        sb = s_loc[rows, m * LANES:(m + 1) * LANES] + bp_ref[hh, kr - r + WIN_ROWS]
                        sbs[m] = jnp.where(mask_of[(ok_l, ok_r)], sb, NEG_INF)
                sc = s_ctx[rows, :]
                mx = jnp.maximum(
                    jnp.max(functools.reduce(jnp.maximum, sbs.values()), axis=-1, keepdims=True),
                    jnp.max(sc, axis=-1, keepdims=True))
                pbs = {m: jnp.exp(sb - mx) for m, sb in sbs.items()}
                pc = jnp.exp(sc - mx)
                l_rows.append(jnp.sum(functools.reduce(jnp.add, pbs.values()), axis=-1, keepdims=True)
                              + jnp.sum(pc, axis=-1, keepdims=True))
                zero = jnp.zeros((GRID_W, LANES), BF16)
                p_rows.append(jnp.concatenate(
                    [pbs[m].astype(BF16) if m in pbs else zero for m in range(nrows // 2)]
                    + [pc.astype(BF16)], axis=1))
        o = jnp.dot(jnp.concatenate(p_rows, axis=0), jnp.concatenate([vb, cvb], axis=0),
                    preferred_element_type=F32)
        o = o / jnp.concatenate(l_rows, axis=0)
        o_ref[q_lo:q_lo + nq, :] = jnp.where(h0q, o[:nq], o[nq:])


def _lat_attn_call(qkv, ctx_k, ctx_v, qg2, kg2, bias_pairs):
    blk = lambda c: pl.BlockSpec((None, DEC_SEQ, LANES), lambda hp, b, c=c: (c * HP + hp, b, 0))
    cblk = pl.BlockSpec((None, None, ctx_k.shape[2], LANES), lambda hp, b: (b, hp, 0, 0))
    return pl.pallas_call(
        _lat_attn_kernel,
        grid=(HP, DEC_BATCH),
        in_specs=[blk(0), blk(1), blk(2), cblk, cblk,
                  pl.BlockSpec((1, LANES), lambda hp, b: (0, 0)),
                  pl.BlockSpec((1, LANES), lambda hp, b: (0, 0)),
                  pl.BlockSpec((2, 2 * WIN_ROWS, GRID_W, LANES), lambda hp, b: (hp, 0, 0, 0))],
        out_specs=pl.BlockSpec((None, DEC_SEQ, LANES), lambda hp, b: (hp, b, 0)),
        out_shape=jax.ShapeDtypeStruct((HP, T_LAT, LANES), F32),
        scratch_shapes=[pltpu.VMEM((DEC_SEQ, LANES), BF16), pltpu.VMEM((DEC_SEQ, LANES), BF16),
                        pltpu.VMEM((LANES, DEC_SEQ), BF16), pltpu.VMEM((DEC_SEQ, LANES), BF16)],
        compiler_params=_cparams(("parallel", "parallel")),
        name="lat_attn",
    )(qkv, qkv, qkv, ctx_k, ctx_v, qg2, kg2, bias_pairs)


def _bias_pairs(rpb):
    cols = np.arange(GRID_W)
    col_idx = np.clip(cols[None, :] - cols[:, None], -(WIN_COLS - 1), WIN_COLS - 1) + WIN_COLS - 1
    rc = rpb[:, :, col_idx]
    left = jnp.pad(rc, ((0, 0), (1, 0), (0, 0), (0, 0)))
    right = jnp.pad(rc, ((0, 0), (0, 1), (0, 0), (0, 0)))
    return jnp.concatenate([left, right], axis=-1)


def _ffn_kernel(x_ref, g_ref, mod_ref, w1_ref, w3_ref, w2_ref, o_ref):
    mod = mod_ref[0]
    x = x_ref[...]
    h = _modnorm(x, g_ref[...], mod[4:5], mod[3:4]).astype(BF16)
    a = _silu(jnp.dot(h, w1_ref[...], preferred_element_type=F32)) \
        * jnp.dot(h, w3_ref[...], preferred_element_type=F32)
    y = jnp.dot(a.astype(BF16), w2_ref[...], preferred_element_type=F32)
    o_ref[...] = x + mod[5:6] * y


def _ffn_call(x, ln_g, mod, w13, w2, layer):
    resident = dict(pipeline_mode=pl.Buffered(1))
    return pl.pallas_call(
        _ffn_kernel,
        grid=(T // TM,),
        in_specs=[
            pl.BlockSpec((TM, D), lambda i: (i, 0)),
            pl.BlockSpec((1, D), lambda i: (0, 0)),
            pl.BlockSpec((1, 6, D), lambda i: (_cond_of_row(i * TM), 0, 0)),
            pl.BlockSpec((None, D, D_FF), lambda i: (layer, 0, 0), **resident),
            pl.BlockSpec((None, D, D_FF), lambda i: (layer, 0, 1), **resident),
            pl.BlockSpec((None, D_FF, D), lambda i: (layer, 0, 0), **resident),
        ],
        out_specs=pl.BlockSpec((TM, D), lambda i: (i, 0)),
        out_shape=jax.ShapeDtypeStruct((T, D), F32),
        compiler_params=_cparams(("parallel",)),
        name="ffn_dense",
    )(x, ln_g.reshape(1, D), mod, w13, w13, w2)


def _s5_kernel(u_ref, bm_ref, cm_ref, ar_ref, ai_ref, hr_ref, hi_ref,
               y_ref, fr_ref, fi_ref, bu_s, st_s, cr_s, ci_s, *, nb, steps):
    d = pl.program_id(0)
    tb = pl.program_id(3)
    rows = nb * steps
    pitch = steps + S5_ROW_PAD
    nch = S5_LANES // LANES
    u = u_ref[...].reshape(rows, S5_SLAB).astype(BF16)
    bu = jnp.dot(u, bm_ref[0, 0], preferred_element_type=F32)
    for c in range(2 * nch):
        for k in range(nb):
            bu_s[c, k * pitch:k * pitch + steps, :] = bu[k * steps:(k + 1) * steps, c * LANES:(c + 1) * LANES]

    @pl.when(tb == 0)
    def _():
        cr_s[...] = hr_ref[0, 0]
        ci_s[...] = hi_ref[0, 0]

    ar = [jnp.broadcast_to(ar_ref[0, 0][:, c * LANES:(c + 1) * LANES], (nb, LANES)) for c in range(nch)]
    ai = [jnp.broadcast_to(ai_ref[0, 0][:, c * LANES:(c + 1) * LANES], (nb, LANES)) for c in range(nch)]

    def step(i, carry):
        l = jnp.where(d == 0, i, steps - 1 - i)
        idx = pl.ds(l, nb, stride=pitch)
        out = []
        for c in range(nch):
            sr, si = carry[2 * c], carry[2 * c + 1]
            nr = ar[c] * sr - ai[c] * si + bu_s[c, idx, :]
            ni = ar[c] * si + ai[c] * sr + bu_s[nch + c, idx, :]
            st_s[c, idx, :] = nr
            st_s[nch + c, idx, :] = ni
            out += [nr, ni]
        return tuple(out)

    init = []
    for c in range(nch):
        init += [cr_s[:, c * LANES:(c + 1) * LANES], ci_s[:, c * LANES:(c + 1) * LANES]]
    fin = lax.fori_loop(0, steps, step, tuple(init), unroll=S5_UNROLL)
    sr = jnp.concatenate([fin[2 * c] for c in range(nch)], axis=1)
    si = jnp.concatenate([fin[2 * c + 1] for c in range(nch)], axis=1)
    cr_s[...] = sr
    ci_s[...] = si
    fr_ref[0, 0] = sr
    fi_ref[0, 0] = si
    states = jnp.concatenate(
        [jnp.concatenate([st_s[c, k * pitch:k * pitch + steps, :] for k in range(nb)], axis=0)
         for c in range(2 * nch)], axis=1).astype(BF16)
    y = jnp.dot(states, cm_ref[0, 0], preferred_element_type=F32)
    y_ref[...] = y.reshape(nb, steps, S5_SLAB)


def _s5_call(u, bmat, cmat, a_re, a_im, h_re, h_im, *, row0, n_seq, seq_len, nb, steps, name):
    n_sg, n_tb = n_seq // nb, seq_len // steps
    sg0 = row0 // (nb * seq_len)

    def tbi(d, tb):
        return jnp.where(d == 0, tb, n_tb - 1 - tb)

    par = lambda last: pl.BlockSpec((1, 1) + last, lambda d, j, sg, tb: (d, j, 0, 0))
    st = pl.BlockSpec((1, 1, nb, S5_LANES), lambda d, j, sg, tb: (d, j, sg, 0))
    y, f_re, f_im = pl.pallas_call(
        functools.partial(_s5_kernel, nb=nb, steps=steps),
        grid=(2, S5_NSLAB, n_sg, n_tb),
        in_specs=[
            pl.BlockSpec((None, nb, steps, S5_SLAB), lambda d, j, sg, tb: (j, sg0 + sg, tbi(d, tb), 0)),
            par((S5_SLAB, 2 * S5_LANES)),
            par((2 * S5_LANES, S5_SLAB)),
            par((1, S5_LANES)), par((1, S5_LANES)),
            st, st,
        ],
        out_specs=[
            pl.BlockSpec((None, None, nb, steps, S5_SLAB), lambda d, j, sg, tb: (d, j, sg, tbi(d, tb), 0)),
            st, st,
        ],
        out_shape=[
            jax.ShapeDtypeStruct((2, S5_NSLAB, n_seq, seq_len, S5_SLAB), F32),
            jax.ShapeDtypeStruct((2, S5_NSLAB, n_seq, S5_LANES), F32),
            jax.ShapeDtypeStruct((2, S5_NSLAB, n_seq, S5_LANES), F32),
        ],
        scratch_shapes=[pltpu.VMEM((2 * S5_LANES // LANES, nb * (steps + S5_ROW_PAD), LANES), F32)] * 2
                       + [pltpu.VMEM((nb, S5_LANES), F32)] * 2,
        compiler_params=_cparams(("parallel", "parallel", "parallel", "arbitrary")),
        name=name,
    )(u.reshape(S5_NSLAB, T // seq_len, seq_len, S5_SLAB), bmat, cmat, a_re, a_im, h_re, h_im)
    return y.reshape(2, S5_NSLAB, n_seq * seq_len, S5_SLAB), f_re, f_im


def _s5_params(lam_re, lam_im, log_step, b_re, b_im, c_re, c_im):
    step = jnp.exp(log_step)[..., None]
    zr, zi = lam_re * step, lam_im * step
    mag = jnp.exp(zr)
    a_re, a_im = mag * jnp.cos(zi), mag * jnp.sin(zi)
    nr, ni = a_re - 1.0, a_im
    den = lam_re * lam_re + lam_im * lam_im
    k_re = (nr * lam_re + ni * lam_im) / den
    k_im = (ni * lam_re - nr * lam_im) / den
    bb_re = k_re[..., None] * b_re - k_im[..., None] * b_im
    bb_im = k_re[..., None] * b_im + k_im[..., None] * b_re
    gl = S5_SLAB // SSM_GROUP
    eye = jnp.eye(gl, dtype=F32)

    def bdiag_in(w):
        w = w.reshape(2, S5_NSLAB, gl, SSM_STATE, SSM_GROUP)
        return jnp.einsum('dsgnp,gh->dsgphn', w, eye).reshape(2, S5_NSLAB, S5_SLAB, S5_LANES)

    def bdiag_out(w):
        w = w.reshape(2, S5_NSLAB, gl, SSM_GROUP, SSM_STATE)
        return jnp.einsum('dsgpn,gh->dsgnhp', w, eye).reshape(2, S5_NSLAB, S5_LANES, S5_SLAB)

    bmat = jnp.concatenate([bdiag_in(bb_re), bdiag_in(bb_im)], axis=-1).astype(BF16)
    cmat = jnp.concatenate([bdiag_out(c_re), -bdiag_out(c_im)], axis=-2).astype(BF16)
    slab = lambda a: a.reshape(2, S5_NSLAB, 1, S5_LANES)
    return bmat, cmat, slab(a_re), slab(a_im)


def _state_to_slabs(h):
    b = h.shape[0]
    return jnp.transpose(h.reshape(b, 2, S5_NSLAB, S5_LANES), (1, 2, 0, 3))


def _slabs_to_state(f):
    b = f.shape[2]
    return jnp.transpose(f, (2, 0, 1, 3)).reshape(b, 2, SSM_GROUPS, SSM_STATE)


def _glu_res_kernel(u_ref, yl_ref, yc_ref, x_ref, mod_ref, d_ref, w_ref, b_ref, o_ref):
    is_lat = pl.program_id(0) < LAT_TILES
    y = jnp.where(is_lat, _load_slabs(yl_ref.at[0]) + _load_slabs(yl_ref.at[1]),
                  _load_slabs(yc_ref.at[0]) + _load_slabs(yc_ref.at[1]))
    yt = _load_slabs(u_ref) * d_ref[...] + y
    z = _gelu_tanh(yt)
    gate = _sigmoid(jnp.dot(z.astype(BF16), w_ref[...], preferred_element_type=F32) + b_ref[...])
    o_ref[...] = x_ref[...] + mod_ref[0][2:3] * (z * gate)


def _glu_res_call(u, y_lat, y_ctx, x, mod, d_skip, w_glu, b_glu):
    return pl.pallas_call(
        _glu_res_kernel,
        grid=(T // TM,),
        in_specs=[
            pl.BlockSpec((S5_NSLAB, TM, LANES), lambda i: (0, i, 0)),
            pl.BlockSpec((2, S5_NSLAB, TM, LANES), lambda i: (0, 0, _lat_tile(i), 0)),
            pl.BlockSpec((2, S5_NSLAB, TM, LANES), lambda i: (0, 0, _ctx_tile(i), 0)),
            pl.BlockSpec((TM, D), lambda i: (i, 0)),
            pl.BlockSpec((1, 6, D), lambda i: (_cond_of_row(i * TM), 0, 0)),
            pl.BlockSpec((1, D), lambda i: (0, 0)),
            pl.BlockSpec((D, D), lambda i: (0, 0)),
            pl.BlockSpec((1, D), lambda i: (0, 0)),
        ],
        out_specs=pl.BlockSpec((TM, D), lambda i: (i, 0)),
        out_shape=jax.ShapeDtypeStruct((T, D), F32),
        compiler_params=_cparams(("parallel",)),
        name="s5_glu_res",
    )(u, y_lat, y_ctx, x, mod, d_skip.reshape(1, D), w_glu, b_glu.reshape(1, D))


def _conv_kernel(bg_ref, cg_ref, xi_ref, cgp_ref, xip_ref, cgn_ref, xin_ref,
                 x_ref, mod_ref, cw_ref, cb_ref, w_ref, o_ref):
    i = pl.program_id(0)
    tiles_per_seq = DEC_SEQ // TM_CONV
    is_lat = i < T_LAT // TM_CONV
    first = jnp.logical_or(jnp.logical_not(is_lat), i % tiles_per_seq == 0)
    last = jnp.logical_or(jnp.logical_not(is_lat), i % tiles_per_seq == tiles_per_seq - 1)
    z = cg_ref[...] * xi_ref[...]
    zp_row = jnp.where(first, 0.0, cgp_ref[7:8, :] * xip_ref[7:8, :])
    zn_row = jnp.where(last, 0.0, cgn_ref[0:1, :] * xin_ref[0:1, :])
    row = lax.broadcasted_iota(jnp.int32, (TM_CONV, D), 0)
    zp = jnp.where(row == 0, zp_row, pltpu.roll(z, 1, axis=0))
    zn = jnp.where(row == TM_CONV - 1, zn_row, pltpu.roll(z, TM_CONV - 1, axis=0))
    cw = cw_ref[...]
    zc = cw[0:1] * zp + cw[1:2] * z + cw[2:3] * zn + cb_ref[...]
    a = (bg_ref[...] * zc).astype(BF16)
    y = jnp.dot(a, w_ref[...], preferred_element_type=F32)
    o_ref[...] = x_ref[...] + mod_ref[0][2:3] * y


def _conv_call(proj, x, mod, conv_w, conv_b, w_out):
    r8 = TM_CONV // 8
    n8 = T // 8
    col = lambda c: pl.BlockSpec((TM_CONV, D), lambda i, c=c: (i, c))
    prev = lambda c: pl.BlockSpec((8, D), lambda i, c=c: (jnp.maximum(i * r8 - 1, 0), c))
    nxt = lambda c: pl.BlockSpec((8, D), lambda i, c=c: (jnp.minimum((i + 1) * r8, n8 - 1), c))
    return pl.pallas_call(
        _conv_kernel,
        grid=(T // TM_CONV,),
        in_specs=[
            col(0), col(1), col(2), prev(1), prev(2), nxt(1), nxt(2),
            pl.BlockSpec((TM_CONV, D), lambda i: (i, 0)),
            pl.BlockSpec((1, 6, D), lambda i: (_cond_of_row(i * TM_CONV), 0, 0)),
            pl.BlockSpec((3, D), lambda i: (0, 0)),
            pl.BlockSpec((1, D), lambda i: (0, 0)),
            pl.BlockSpec((D, D), lambda i: (0, 0)),
        ],
        out_specs=pl.BlockSpec((TM_CONV, D), lambda i: (i, 0)),
        out_shape=jax.ShapeDtypeStruct((T, D), F32),
        compiler_params=_cparams(("parallel",)),
        name="conv_mix",
    )(proj, proj, proj, proj, proj, proj, proj, x, mod, conv_w, conv_b.reshape(1, D), w_out)


def _router_kernel(x_ref, g_ref, mod_ref, r_ref, h_ref, rt_ref):
    mod = mod_ref[0]
    h = _modnorm(x_ref[...], g_ref[...], mod[4:5], mod[3:4])
    h_ref[...] = h
    logits = jnp.dot(h, r_ref[...], preferred_element_type=F32, precision=lax.Precision.HIGHEST)
    lane = lax.broadcasted_iota(jnp.int32, logits.shape, 1)
    logits = jnp.where(lane < N_EXPERTS, logits, -jnp.inf)
    m1 = jnp.max(logits, axis=-1, keepdims=True)
    i1 = jnp.min(jnp.where(logits == m1, lane, LANES), axis=-1, keepdims=True)
    rest = jnp.where(lane == i1, -jnp.inf, logits)
    m2 = jnp.max(rest, axis=-1, keepdims=True)
    i2 = jnp.min(jnp.where(rest == m2, lane, LANES), axis=-1, keepdims=True)
    e2 = jnp.exp(m2 - m1)
    w1 = 1.0 / (1.0 + e2)
    w2 = e2 / (1.0 + e2)
    rt_ref[...] = jnp.where(lane == 0, i1.astype(F32),
                            jnp.where(lane == 1, i2.astype(F32),
                                      jnp.where(lane == 2, w1, jnp.where(lane == 3, w2, 0.0))))


def _router_call(x, ln_g, mod, router_pad):
    return pl.pallas_call(
        _router_kernel,
        grid=(T // TM,),
        in_specs=[
            pl.BlockSpec((TM, D), lambda i: (i, 0)),
            pl.BlockSpec((1, D), lambda i: (0, 0)),
            pl.BlockSpec((1, 6, D), lambda i: (_cond_of_row(i * TM), 0, 0)),
            pl.BlockSpec((D, LANES), lambda i: (0, 0)),
        ],
        out_specs=[pl.BlockSpec((TM, D), lambda i: (i, 0)),
                   pl.BlockSpec((TM, LANES), lambda i: (i, 0))],
        out_shape=[jax.ShapeDtypeStruct((T, D), F32), jax.ShapeDtypeStruct((T, LANES), F32)],
        compiler_params=_cparams(("parallel",)),
        name="moe_router",
    )(x, ln_g.reshape(1, D), mod, router_pad)


R_MAX = 2 * T + N_EXPERTS * TM_MOE
N_TILES = R_MAX // TM_MOE
DMA_UNROLL = 8
GATHER_SHIFT = 3
GATHER_SPREAD = 1 << GATHER_SHIFT


def _row_copy(src_hbm, src_row, dst_vmem, dst_row, sem):
    return pltpu.make_async_copy(src_hbm.at[pl.ds(src_row, 1), :], dst_vmem.at[pl.ds(dst_row, 1), :], sem)


def _gather_kernel(nv_ref, src_ref, h_hbm, o_ref, buf, sem):
    i = pl.program_id(0)
    nv = nv_ref[0]

    def issue_tile(t):
        slot = t % 2
        base = t * TM_MOE

        def issue(g, c):
            for k in range(GATHER_SPREAD):
                r = k * (TM_MOE // GATHER_SPREAD) + g
                _row_copy(h_hbm, src_ref[base + r], buf.at[slot], r, sem.at[slot]).start(priority=k % 2)
            return c

        lax.fori_loop(0, TM_MOE // GATHER_SPREAD, issue, 0)

    @pl.when(jnp.logical_and(i == 0, nv > 0))
    def _():
        issue_tile(0)

    @pl.when(i + 1 < nv)
    def _():
        issue_tile(i + 1)

    @pl.when(i < nv)
    def _():
        slot = i % 2
        pltpu.make_async_copy(h_hbm.at[pl.ds(0, TM_MOE), :], buf.at[slot], sem.at[slot]).wait()
        o_ref[...] = buf[slot].astype(BF16)

    @pl.when(i >= nv)
    def _():
        o_ref[...] = jnp.zeros_like(o_ref)


def _gather_call(n_valid, src_tok, h):
    return pl.pallas_call(
        _gather_kernel,
        grid_spec=pltpu.PrefetchScalarGridSpec(
            num_scalar_prefetch=2,
            grid=(N_TILES,),
            in_specs=[pl.BlockSpec(memory_space=pl.ANY)],
            out_specs=pl.BlockSpec((TM_MOE, D), lambda i, nv, src: (i, 0)),
            scratch_shapes=[pltpu.VMEM((2, TM_MOE, D), F32), pltpu.SemaphoreType.DMA((2,))],
        ),
        out_shape=jax.ShapeDtypeStruct((R_MAX, D), BF16),
        compiler_params=_cparams(("arbitrary",)),
        name="moe_gather",
    )(n_valid, src_tok, h)


def _moe_kernel(nv_ref, te_ref, tr_ref, xs_ref, w1_ref, w3_ref, w2_ref, o_ref, acc_s):
    i = pl.program_id(0)
    f = pl.program_id(1)
    nf = pl.num_programs(1)
    valid = i < nv_ref[0]
    half = TM_MOE // 2
    short = tr_ref[i] <= half

    @pl.when(jnp.logical_and(valid, f == 0))
    def _():
        acc_s[...] = jnp.zeros_like(acc_s)

    def swiglu_rows(rows):
        xs = xs_ref[:rows, :]
        w13 = jnp.concatenate([w1_ref[...].astype(BF16), w3_ref[...].astype(BF16)], axis=1)
        gu = jnp.dot(xs, w13, preferred_element_type=F32)
        a = _silu(gu[:, :TF_MOE]) * gu[:, TF_MOE:]
        acc_s[:rows, :] += jnp.dot(a.astype(BF16), w2_ref[...].astype(BF16), preferred_element_type=F32)

    @pl.when(jnp.logical_and(valid, jnp.logical_not(short)))
    def _():
        swiglu_rows(TM_MOE)

    @pl.when(jnp.logical_and(valid, short))
    def _():
        swiglu_rows(half)

    @pl.when(f == nf - 1)
    def _():
        o_ref[...] = jnp.where(valid, acc_s[...], 0.0)


def _moe_call(n_valid, tile_expert, tile_rows, xs, w13, w2, layer):
    nf = D_FF_EXPERT // TF_MOE

    def fe(i, f, nv):
        return jnp.where(i < nv[0], f, nf - 1)

    return pl.pallas_call(
        _moe_kernel,
        grid_spec=pltpu.PrefetchScalarGridSpec(
            num_scalar_prefetch=3,
            grid=(N_TILES, nf),
            in_specs=[
                pl.BlockSpec((TM_MOE, D), lambda i, f, nv, te, tr: (i, 0)),
                pl.BlockSpec((None, None, D, TF_MOE),
                             lambda i, f, nv, te, tr: (layer, te[i], 0, fe(i, f, nv))),
                pl.BlockSpec((None, None, D, TF_MOE),
                             lambda i, f, nv, te, tr: (layer, te[i], 0, nf + fe(i, f, nv))),
                pl.BlockSpec((None, None, TF_MOE, D),
                             lambda i, f, nv, te, tr: (layer, te[i], fe(i, f, nv), 0)),
            ],
            out_specs=pl.BlockSpec((TM_MOE, D), lambda i, f, nv, te, tr: (i, 0)),
            scratch_shapes=[pltpu.VMEM((TM_MOE, D), F32)],
        ),
        out_shape=jax.ShapeDtypeStruct((R_MAX, D), F32),
        compiler_params=_cparams(("arbitrary", "arbitrary")),
        name="moe_experts",
    )(n_valid, tile_expert, tile_rows, xs, w13, w13, w2)


TM_COMB = 512


def _combine_kernel(pa_ref, pb_ref, y_hbm, x_ref, rt_ref, mod_ref, o_ref, buf_a, buf_b, sem):
    i = pl.program_id(0)

    def issue_tile(t):
        slot = t % 2
        base = t * TM_COMB

        def issue(r, c):
            _row_copy(y_hbm, pa_ref[base + r], buf_a.at[slot], r, sem.at[slot]).start(priority=0)
            _row_copy(y_hbm, pb_ref[base + r], buf_b.at[slot], r, sem.at[slot]).start(priority=1)
            return c

        lax.fori_loop(0, TM_COMB, issue, 0, unroll=DMA_UNROLL)

    @pl.when(i == 0)
    def _():
        issue_tile(0)

    @pl.when(i + 1 < pl.num_programs(0))
    def _():
        issue_tile(i + 1)

    slot = i % 2
    for buf in (buf_a, buf_b):
        pltpu.make_async_copy(y_hbm.at[pl.ds(0, TM_COMB), :], buf.at[slot], sem.at[slot]).wait()
    rt = rt_ref[...]
    ffn = rt[:, 2:3] * buf_a[slot] + rt[:, 3:4] * buf_b[slot]
    o_ref[...] = x_ref[...] + mod_ref[0][5:6] * ffn


def _combine_call(pos_a, pos_b, y, x, rt, mod):
    return pl.pallas_call(
        _combine_kernel,
        grid_spec=pltpu.PrefetchScalarGridSpec(
            num_scalar_prefetch=2,
            grid=(T // TM_COMB,),
            in_specs=[
                pl.BlockSpec(memory_space=pl.ANY),
                pl.BlockSpec((TM_COMB, D), lambda i, pa, pb: (i, 0)),
                pl.BlockSpec((TM_COMB, LANES), lambda i, pa, pb: (i, 0)),
                pl.BlockSpec((1, 6, D), lambda i, pa, pb: (_cond_of_row(i * TM_COMB), 0, 0)),
            ],
            out_specs=pl.BlockSpec((TM_COMB, D), lambda i, pa, pb: (i, 0)),
            scratch_shapes=[pltpu.VMEM((2, TM_COMB, D), F32), pltpu.VMEM((2, TM_COMB, D), F32),
                            pltpu.SemaphoreType.DMA((2,))],
        ),
        out_shape=jax.ShapeDtypeStruct((T, D), F32),
        compiler_params=_cparams(("arbitrary",)),
        name="moe_combine",
    )(pos_a, pos_b, y, x, rt, mod)


def _routing_tables(rt):
    ea = rt[:, 0:2].astype(jnp.int32).reshape(-1)
    onehot = (ea[:, None] == jnp.arange(N_EXPERTS, dtype=jnp.int32)[None, :]).astype(jnp.int32)
    csum = jnp.cumsum(onehot, axis=0)
    rank = jnp.take_along_axis(csum, ea[:, None], axis=1)[:, 0] - 1
    counts = csum[-1]
    padded = ((counts + TM_MOE - 1) // TM_MOE) * TM_MOE
    ends = jnp.cumsum(padded)
    pos = (ends - padded)[ea] + rank
    n_valid = (ends[-1] // TM_MOE).astype(jnp.int32).reshape(1)
    tile_start = jnp.arange(N_TILES, dtype=jnp.int32) * TM_MOE
    expert_of_row = lambda r: jnp.sum((r[:, None] >= ends[None, :]).astype(jnp.int32), axis=1)
    tile_expert = jnp.minimum(expert_of_row(tile_start), expert_of_row(ends[-1:] - 1))
    real_end = (ends - padded + counts)[tile_expert]
    tile_rows = jnp.clip(real_end - tile_start, 0, TM_MOE).astype(jnp.int32)
    tok = jnp.arange(2 * T, dtype=jnp.int32) // 2
    src_tok = (jnp.arange(R_MAX, dtype=jnp.int32) % T).at[pos].set(tok)
    pos2 = pos.reshape(T, 2).astype(jnp.int32)
    return n_valid, tile_expert, tile_rows, src_tok, pos2[:, 0], pos2[:, 1]


def _moe_layer(x, ln_g, mod, router, w13, w2, layer):
    router_pad = jnp.pad(router, ((0, 0), (0, LANES - N_EXPERTS)))
    h, rt = _router_call(x, ln_g, mod, router_pad)
    n_valid, tile_expert, tile_rows, src_tok, pos_a, pos_b = _routing_tables(rt)
    xs = _gather_call(n_valid, src_tok, h)
    y = _moe_call(n_valid, tile_expert, tile_rows, xs, w13, w2, layer)
    return _combine_call(pos_a, pos_b, y, x, rt, mod)


def kernel(x_prompt, x_sample, c, cache_na_k, cache_na_v, state_ssm_re, state_ssm_im, c_ctx,
           ln1_g, ln2_g, ada_w, ada_b, na_w_qkv, na_w_o, na_q_g, na_k_g, na_rpb,
           ssm_lam_re, ssm_lam_im, ssm_log_step, ssm_b_re, ssm_b_im, ssm_c_re, ssm_c_im,
           ssm_d, ssm_w_glu, ssm_b_glu, cv_w_in, cv_conv_w, cv_conv_b, cv_w_out,
           ffn_w13, ffn_w2, moe_router, moe_w13, moe_w2):
    depth = ada_w.shape[0]
    x = (x_sample.reshape(T_LAT, D), x_prompt.reshape(T_CTX, D))
    conds = jnp.concatenate([c, c_ctx[None, :], jnp.zeros((N_COND - DEC_BATCH - 1, D), F32)], axis=0)
    mods = _ada_call(conds, ada_w, ada_b).reshape(depth, N_COND, 6, D)
    ffn_w13_b, ffn_w2_b = ffn_w13.astype(BF16), ffn_w2.astype(BF16)

    new_k = new_v = None
    s_re, s_im = [], []
    for l in range(depth):
        kind, j = l % 3, l // 3
        mod = mods[l]
        if kind == 0:
            qkv = _normproj_call(x, ln1_g[l], mod, na_w_qkv[j].astype(BF16), shift=0, scale=1,
                                 name="qkv_proj", slab_out=True)
            qg2 = jnp.tile(na_q_g[j], 2).reshape(1, LANES)
            kg2 = jnp.tile(na_k_g[j], 2).reshape(1, LANES)
            past = cache_na_k.shape[3]
            pair = lambda a: jnp.transpose(a[:, j].reshape(DEC_BATCH, HP, 2, past, HEAD_DIM),
                                           (0, 1, 3, 2, 4)).reshape(DEC_BATCH, HP, past, LANES)
            ctx_k, ctx_v = pair(cache_na_k), pair(cache_na_v)
            o_lat = _lat_attn_call(qkv, ctx_k, ctx_v, qg2, kg2, _bias_pairs(na_rpb[j]))
            o_ctx, new_k, new_v = _ctx_attn_call(qkv, qg2, kg2, new_k, new_v)
            x = _proj_res_call(o_lat, o_ctx, x, mod, na_w_o[j].astype(BF16), gate=2, name="attn_out")
        elif kind == 1:
            u = _norm_call(x, ln1_g[l], mod, shift=0, scale=1, name="s5_norm")
            bmat, cmat, a_re, a_im = _s5_params(ssm_lam_re[j], ssm_lam_im[j], ssm_log_step[j],
                                                ssm_b_re[j], ssm_b_im[j], ssm_c_re[j], ssm_c_im[j])
            y_lat, _, _ = _s5_call(u, bmat, cmat, a_re, a_im,
                                   _state_to_slabs(state_ssm_re[:, j]), _state_to_slabs(state_ssm_im[:, j]),
                                   row0=0, n_seq=DEC_BATCH, seq_len=DEC_SEQ,
                                   nb=DEC_BATCH, steps=S5_T_LAT, name="s5_lat")
            zero = jnp.zeros((2, S5_NSLAB, BATCH, S5_LANES), F32)
            y_ctx, f_re, f_im = _s5_call(u, bmat, cmat, a_re, a_im, zero, zero,
                                         row0=T_LAT, n_seq=BATCH, seq_len=SEQ, nb=8, steps=SEQ,
                                         name="s5_ctx")
            s_re.append(_slabs_to_state(f_re))
            s_im.append(_slabs_to_state(f_im))
            x = _glu_res_call(u, y_lat, y_ctx, x, mod, ssm_d[j], ssm_w_glu[j].astype(BF16), ssm_b_glu[j])
        else:
            proj = _normproj_call(x, ln1_g[l], mod, cv_w_in[j].astype(BF16), shift=0, scale=1,
                                  name="conv_in")
            x = _conv_call(proj, x, mod, cv_conv_w[j], cv_conv_b[j], cv_w_out[j].astype(BF16))
        jj = l // 2
        if l % 2 == 0:
            x = _ffn_call(x, ln2_g[l], mod, ffn_w13_b, ffn_w2_b, jj)
        else:
            x = _moe_layer(x, ln2_g[l], mod, moe_router[jj], moe_w13, moe_w2, jj)

    y_sample = x[:T_LAT].reshape(DEC_BATCH, DEC_SEQ, D)
    y_prompt = x[T_LAT:].reshape(BATCH, SEQ, D)
    return (y_prompt, y_sample, new_k, new_v, jnp.stack(s_re, axis=1), jnp.stack(s_im, axis=1))
```

```python
import functools
import math

import numpy as np
import jax
import jax.numpy as jnp
from jax import lax
from jax.experimental import pallas as pl
from jax.experimental.pallas import tpu as pltpu

F32 = jnp.float32
BF16 = jnp.bfloat16

D = 1024
BATCH, SEQ = 16, 256
DEC_BATCH, DEC_SEQ = 4, 2048
GRID_W = 64
ROWS = DEC_SEQ // GRID_W
N_HEADS, HEAD_DIM = 16, 64
WIN_ROWS, WIN_COLS = 8, 16
SSM_GROUP, SSM_GROUPS, SSM_STATE = 16, 64, 64
D_FF = 2816
N_EXPERTS = 8
D_FF_EXPERT = 3584
EPS = 1e-6
NEG_INF = -1e30
SCALE = HEAD_DIM ** -0.5

T_LAT = DEC_BATCH * DEC_SEQ
T_CTX = BATCH * SEQ
T = T_LAT + T_CTX
N_COND = 8
CTX_COND = DEC_BATCH

LANES = 128
VMEM_LIMIT = 56 * 1024 * 1024

TM = 512
TM_MOE = 1024
TF_MOE = 512
HP = N_HEADS // 2
QB_ROWS = 4
S5_SLAB = 128
S5_NSLAB = D // S5_SLAB
S5_LANES = (S5_SLAB // SSM_GROUP) * SSM_STATE
S5_T_LAT = 512
S5_ROW_PAD = 4
S5_UNROLL = 8


def _cparams(sem):
    return pltpu.CompilerParams(dimension_semantics=sem, vmem_limit_bytes=VMEM_LIMIT)


def _cond_of_row(row):
    return jnp.minimum(row // DEC_SEQ, CTX_COND)


def _modnorm(x, g, sc, sh):
    ms = jnp.mean(x * x, axis=-1, keepdims=True)
    y = x * lax.rsqrt(ms + EPS) * g
    return y * (1.0 + sc) + sh


def _sigmoid(x):
    return 1.0 / (1.0 + jnp.exp(-x))


def _silu(x):
    return x * _sigmoid(x)


def _gelu_tanh(x):
    c = math.sqrt(2.0 / math.pi)
    return 0.5 * x * (1.0 + jnp.tanh(c * (x + 0.044715 * (x * x * x))))


def _ada_kernel(c_ref, w_ref, b_ref, o_ref):
    s = _silu(c_ref[...]).astype(BF16)
    o_ref[0] = jnp.dot(s, w_ref[0].astype(BF16), preferred_element_type=F32) + b_ref[0]


def _ada_call(conds, ada_w, ada_b):
    depth = ada_w.shape[0]
    tn = 1536
    return pl.pallas_call(
        _ada_kernel,
        grid=(depth, 6 * D // tn),
        in_specs=[
            pl.BlockSpec((N_COND, D), lambda l, n: (0, 0)),
            pl.BlockSpec((1, D, tn), lambda l, n: (l, 0, n)),
            pl.BlockSpec((1, 1, tn), lambda l, n: (l, 0, n)),
        ],
        out_specs=pl.BlockSpec((1, N_COND, tn), lambda l, n: (l, 0, n)),
        out_shape=jax.ShapeDtypeStruct((depth, N_COND, 6 * D), F32),
        compiler_params=_cparams(("parallel", "parallel")),
        name="ada_mod",
    )(conds, ada_w, ada_b.reshape(depth, 1, 6 * D))


LAT_TILES = T_LAT // TM


def _lat_tile(i):
    return jnp.minimum(i, LAT_TILES - 1)


def _ctx_tile(i):
    return jnp.maximum(i - LAT_TILES, 0)


def _rows_specs(x):
    if isinstance(x, tuple):
        return [pl.BlockSpec((TM, D), lambda i: (_lat_tile(i), 0)),
                pl.BlockSpec((TM, D), lambda i: (_ctx_tile(i), 0))], list(x)
    return [pl.BlockSpec((TM, D), lambda i: (i, 0))], [x]


def _rows_value(refs):
    if len(refs) == 2:
        return jnp.where(pl.program_id(0) < LAT_TILES, refs[0][...], refs[1][...])
    return refs[0][...]


def _store_slabs(o_ref, val):
    for c in range(val.shape[1] // LANES):
        o_ref[c] = val[:, c * LANES:(c + 1) * LANES]


def _load_slabs(ref):
    return jnp.concatenate([ref[c] for c in range(ref.shape[0])], axis=1)


def _normproj_kernel(*refs, shift, scale, slab_out):
    *x_refs, g_ref, mod_ref, w_ref, o_ref = refs
    mod = mod_ref[0]
    h = _modnorm(_rows_value(x_refs), g_ref[...], mod[scale:scale + 1], mod[shift:shift + 1])
    res = jnp.dot(h.astype(BF16), w_ref[...], preferred_element_type=F32)
    if slab_out:
        _store_slabs(o_ref, res)
    else:
        o_ref[...] = res


def _normproj_call(x, ln_g, mod, w, *, shift, scale, name, slab_out=False):
    n = w.shape[1]
    x_specs, x_args = _rows_specs(x)
    if slab_out:
        out_spec = pl.BlockSpec((n // LANES, TM, LANES), lambda i: (0, i, 0))
        out_shape = jax.ShapeDtypeStruct((n // LANES, T, LANES), F32)
    else:
        out_spec = pl.BlockSpec((TM, n), lambda i: (i, 0))
        out_shape = jax.ShapeDtypeStruct((T, n), F32)
    return pl.pallas_call(
        functools.partial(_normproj_kernel, shift=shift, scale=scale, slab_out=slab_out),
        grid=(T // TM,),
        in_specs=x_specs + [
            pl.BlockSpec((1, D), lambda i: (0, 0)),
            pl.BlockSpec((1, 6, D), lambda i: (_cond_of_row(i * TM), 0, 0)),
            pl.BlockSpec((D, n), lambda i: (0, 0)),
        ],
        out_specs=out_spec,
        out_shape=out_shape,
        compiler_params=_cparams(("parallel",)),
        name=name,
    )(*x_args, ln_g.reshape(1, D), mod, w)


def _norm_kernel(x_ref, g_ref, mod_ref, o_ref, *, shift, scale):
    mod = mod_ref[0]
    _store_slabs(o_ref, _modnorm(x_ref[...], g_ref[...], mod[scale:scale + 1], mod[shift:shift + 1]))


def _norm_call(x, ln_g, mod, *, shift, scale, name):
    return pl.pallas_call(
        functools.partial(_norm_kernel, shift=shift, scale=scale),
        grid=(T // TM,),
        in_specs=[
            pl.BlockSpec((TM, D), lambda i: (i, 0)),
            pl.BlockSpec((1, D), lambda i: (0, 0)),
            pl.BlockSpec((1, 6, D), lambda i: (_cond_of_row(i * TM), 0, 0)),
        ],
        out_specs=pl.BlockSpec((D // LANES, TM, LANES), lambda i: (0, i, 0)),
        out_shape=jax.ShapeDtypeStruct((D // LANES, T, LANES), F32),
        compiler_params=_cparams(("parallel",)),
        name=name,
    )(x, ln_g.reshape(1, D), mod)


def _proj_res_kernel(*refs, gate, n_x):
    al_ref, ac_ref = refs[:2]
    x_refs = refs[2:2 + n_x]
    mod_ref, w_ref, o_ref = refs[2 + n_x:]
    a = jnp.where(pl.program_id(0) < LAT_TILES, _load_slabs(al_ref), _load_slabs(ac_ref))
    y = jnp.dot(a.astype(BF16), w_ref[...], preferred_element_type=F32)
    o_ref[...] = _rows_value(x_refs) + mod_ref[0][gate:gate + 1] * y


def _proj_res_call(a_lat, a_ctx, x, mod, w, *, gate, name):
    ns = D // LANES
    a_specs = [pl.BlockSpec((ns, TM, LANES), lambda i: (0, _lat_tile(i), 0)),
               pl.BlockSpec((ns, TM, LANES), lambda i: (0, _ctx_tile(i), 0))]
    a_args = [a_lat, a_ctx]
    x_specs, x_args = _rows_specs(x)
    return pl.pallas_call(
        functools.partial(_proj_res_kernel, gate=gate, n_x=len(x_args)),
        grid=(T // TM,),
        in_specs=a_specs + x_specs + [
            pl.BlockSpec((1, 6, D), lambda i: (_cond_of_row(i * TM), 0, 0)),
            pl.BlockSpec((D, D), lambda i: (0, 0)),
        ],
        out_specs=pl.BlockSpec((TM, D), lambda i: (i, 0)),
        out_shape=jax.ShapeDtypeStruct((T, D), F32),
        compiler_params=_cparams(("parallel",)),
        name=name,
    )(*a_args, *x_args, mod, w)


def _head_norm(x, g, head0):
    x2 = x * x
    s0 = jnp.sum(jnp.where(head0, x2, 0.0), axis=-1, keepdims=True)
    s1 = jnp.sum(jnp.where(head0, 0.0, x2), axis=-1, keepdims=True)
    ms = jnp.where(head0, s0, s1) * (1.0 / HEAD_DIM)
    return x * lax.rsqrt(ms + EPS) * g


def _nt_dot(a, b):
    return lax.dot_general(a, b, (((1,), (1,)), ((), ())), preferred_element_type=F32)


def _ctx_attn_kernel(*refs, n_prev):
    if n_prev:
        q_ref, k_ref, v_ref, qg_ref, kg_ref, pk_ref, pv_ref, o_ref, ko_ref, vo_ref = refs
        ko_ref[0, :n_prev] = pk_ref[0]
        vo_ref[0, :n_prev] = pv_ref[0]
    else:
        q_ref, k_ref, v_ref, qg_ref, kg_ref, o_ref, ko_ref, vo_ref = refs
    head0 = lax.broadcasted_iota(jnp.int32, (SEQ, LANES), 1) < HEAD_DIM
    for hp in range(HP):
        qn = _head_norm(q_ref[hp], qg_ref[...], head0) * SCALE
        kn = _head_norm(k_ref[hp], kg_ref[...], head0)
        v = v_ref[hp]
        ko_ref[0, n_prev, 2 * hp] = kn[:, :HEAD_DIM]
        ko_ref[0, n_prev, 2 * hp + 1] = kn[:, HEAD_DIM:]
        vo_ref[0, n_prev, 2 * hp] = v[:, :HEAD_DIM]
        vo_ref[0, n_prev, 2 * hp + 1] = v[:, HEAD_DIM:]
        qm = jnp.concatenate([jnp.where(head0, qn, 0.0), jnp.where(head0, 0.0, qn)], axis=0).astype(BF16)
        s = jnp.dot(qm, kn.T.astype(BF16), preferred_element_type=F32)
        p = jnp.exp(s - jnp.max(s, axis=-1, keepdims=True))
        l = jnp.sum(p, axis=-1, keepdims=True)
        o = jnp.dot(p.astype(BF16), v.astype(BF16), preferred_element_type=F32) / l
        o_ref[hp] = jnp.where(head0, o[:SEQ], o[SEQ:])


def _ctx_attn_call(qkv, qg2, kg2, prev_k, prev_v):
    row0 = T_LAT // SEQ
    n_prev = 0 if prev_k is None else prev_k.shape[1]
    blk = lambda c: pl.BlockSpec((HP, SEQ, LANES), lambda b, c=c: (c, row0 + b, 0))
    kv_blk = lambda n: pl.BlockSpec((1, n, N_HEADS, SEQ, HEAD_DIM), lambda b: (b, 0, 0, 0, 0))
    kv_shape = jax.ShapeDtypeStruct((BATCH, n_prev + 1, N_HEADS, SEQ, HEAD_DIM), F32)
    prev_specs = [kv_blk(n_prev)] * 2 if n_prev else []
    prev_args = [prev_k, prev_v] if n_prev else []
    return pl.pallas_call(
        functools.partial(_ctx_attn_kernel, n_prev=n_prev),
        grid=(BATCH,),
        in_specs=[blk(0), blk(1), blk(2),
                  pl.BlockSpec((1, LANES), lambda b: (0, 0)),
                  pl.BlockSpec((1, LANES), lambda b: (0, 0))] + prev_specs,
        out_specs=[pl.BlockSpec((HP, SEQ, LANES), lambda b: (0, b, 0)), kv_blk(n_prev + 1), kv_blk(n_prev + 1)],
        out_shape=[jax.ShapeDtypeStruct((HP, T_CTX, LANES), F32), kv_shape, kv_shape],
        compiler_params=_cparams(("parallel",)),
        name="ctx_attn",
    )(qkv, qkv, qkv, qg2, kg2, *prev_args)


def _band_of_block(p):
    r_lo, r_hi = QB_ROWS * p, QB_ROWS * p + QB_ROWS - 1
    kr = min(WIN_ROWS, ROWS)
    lo = min(max(r_lo - kr // 2, 0), ROWS - kr)
    hi = min(max(r_hi - kr // 2, 0), ROWS - kr) + kr
    lo -= lo % 2
    hi += hi % 2
    return lo, hi - lo


def _lat_attn_kernel(q_ref, k_ref, v_ref, ck_ref, cv_ref, qg_ref, kg_ref, bp_ref, o_ref,
                     qm0_s, qm1_s, kt_s, v_s):
    head0 = lax.broadcasted_iota(jnp.int32, (DEC_SEQ, LANES), 1) < HEAD_DIM
    qn = _head_norm(q_ref[...], qg_ref[...], head0)
    qm0_s[...] = jnp.where(head0, qn * SCALE, 0.0).astype(BF16)
    qm1_s[...] = jnp.where(head0, 0.0, qn * SCALE).astype(BF16)
    kt_s[...] = _head_norm(k_ref[...], kg_ref[...], head0).T.astype(BF16)
    v_s[...] = v_ref[...].astype(BF16)
    ckt = ck_ref[...].T.astype(BF16)
    cvb = cv_ref[...].astype(BF16)

    qc = lax.broadcasted_iota(jnp.int32, (GRID_W, LANES), 0)
    kl = lax.broadcasted_iota(jnp.int32, (GRID_W, LANES), 1)
    kc = jnp.where(kl < GRID_W, kl, kl - GRID_W)
    cs = jnp.clip(qc - WIN_COLS // 2, 0, GRID_W - WIN_COLS)
    col_ok = jnp.logical_and(kc >= cs, kc < cs + WIN_COLS)
    left = kl < GRID_W
    mask_of = {
        (True, True): col_ok,
        (True, False): jnp.logical_and(col_ok, left),
        (False, True): jnp.logical_and(col_ok, jnp.logical_not(left)),
    }
    h0q = lax.broadcasted_iota(jnp.int32, (QB_ROWS * GRID_W, LANES), 1) < HEAD_DIM
    kr_win = min(WIN_ROWS, ROWS)

    for p in range(ROWS // QB_ROWS):
        u0, nrows = _band_of_block(p)
        q_lo = p * QB_ROWS * GRID_W
        ktb = kt_s[:, u0 * GRID_W:(u0 + nrows) * GRID_W]
        vb = v_s[u0 * GRID_W:(u0 + nrows) * GRID_W, :]
        nq = QB_ROWS * GRID_W
        qm = jnp.concatenate([qm0_s[q_lo:q_lo + nq, :], qm1_s[q_lo:q_lo + nq, :]], axis=0)
        n_loc = nrows * GRID_W
        s_all = jnp.dot(qm, jnp.concatenate([ktb, ckt], axis=1), preferred_element_type=F32)
        s_loc, s_ctx = s_all[:, :n_loc], s_all[:, n_loc:]
        p_rows, l_rows = [], []
        for hh in range(2):
            for i in range(QB_ROWS):
                r = p * QB_ROWS + i
                r0 = min(max(r - kr_win // 2, 0), ROWS - kr_win)
                rows = slice(hh * nq + i * GRID_W, hh * nq + (i + 1) * GRID_W)
                sbs = {}
                for m in range(nrows // 2):
                    kr = u0 + 2 * m
                    ok_l = r0 <= kr < r0 + kr_win
                    ok_r = r0 <= kr + 1 < r0 + kr_win
                    if ok_l or ok_r:
                        sb = s_loc[rows, m * LANES:(m + 1) * LANES] + bp_ref[hh, kr - r + WIN_ROWS]
                        sbs[m] = jnp.where(mask_of[(ok_l, ok_r)], sb, NEG_INF)
                sc = s_ctx[rows, :]
                mx = jnp.maximum(
                    jnp.max(functools.reduce(jnp.maximum, sbs.values()), axis=-1, keepdims=True),
                    jnp.max(sc, axis=-1, keepdims=True))
                pbs = {m: jnp.exp(sb - mx) for m, sb in sbs.items()}
                pc = jnp.exp(sc - mx)
                l_rows.append(jnp.sum(functools.reduce(jnp.add, pbs.values()), axis=-1, keepdims=True)
                              + jnp.sum(pc, axis=-1, keepdims=True))
                zero = jnp.zeros((GRID_W, LANES), BF16)
                p_rows.append(jnp.concatenate(
                    [pbs[m].astype(BF16) if m in pbs else zero for m in range(nrows // 2)]
                    + [pc.astype(BF16)], axis=1))
        o = jnp.dot(jnp.concatenate(p_rows, axis=0), jnp.concatenate([vb, cvb], axis=0),
                    preferred_element_type=F32)
        o = o / jnp.concatenate(l_rows, axis=0)
        o_ref[q_lo:q_lo + nq, :] = jnp.where(h0q, o[:nq], o[nq:])


def _lat_attn_call(qkv, ctx_k, ctx_v, qg2, kg2, bias_pairs):
    blk = lambda c: pl.BlockSpec((None, DEC_SEQ, LANES), lambda hp, b, c=c: (c * HP + hp, b, 0))
    cblk = pl.BlockSpec((None, None, ctx_k.shape[2], LANES), lambda hp, b: (b, hp, 0, 0))
    return pl.pallas_call(
        _lat_attn_kernel,
        grid=(HP, DEC_BATCH),
        in_specs=[blk(0), blk(1), blk(2), cblk, cblk,
                  pl.BlockSpec((1, LANES), lambda hp, b: (0, 0)),
                  pl.BlockSpec((1, LANES), lambda hp, b: (0, 0)),
                  pl.BlockSpec((2, 2 * WIN_ROWS, GRID_W, LANES), lambda hp, b: (hp, 0, 0, 0))],
        out_specs=pl.BlockSpec((None, DEC_SEQ, LANES), lambda hp, b: (hp, b, 0)),
        out_shape=jax.ShapeDtypeStruct((HP, T_LAT, LANES), F32),
        scratch_shapes=[pltpu.VMEM((DEC_SEQ, LANES), BF16), pltpu.VMEM((DEC_SEQ, LANES), BF16),
                        pltpu.VMEM((LANES, DEC_SEQ), BF16), pltpu.VMEM((DEC_SEQ, LANES), BF16)],
        compiler_params=_cparams(("parallel", "parallel")),
        name="lat_attn",
    )(qkv, qkv, qkv, ctx_k, ctx_v, qg2, kg2, bias_pairs)


def _bias_pairs(rpb):
    cols = np.arange(GRID_W)
    col_idx = np.clip(cols[None, :] - cols[:, None], -(WIN_COLS - 1), WIN_COLS - 1) + WIN_COLS - 1
    rc = rpb[:, :, col_idx]
    left = jnp.pad(rc, ((0, 0), (1, 0), (0, 0), (0, 0)))
    right = jnp.pad(rc, ((0, 0), (0, 1), (0, 0), (0, 0)))
    return jnp.concatenate([left, right], axis=-1)


def _ffn_kernel(x_ref, g_ref, mod_ref, w1_ref, w3_ref, w2_ref, o_ref):
    mod = mod_ref[0]
    x = x_ref[...]
    h = _modnorm(x, g_ref[...], mod[4:5], mod[3:4]).astype(BF16)
    a = _silu(jnp.dot(h, w1_ref[...], preferred_element_type=F32)) \
        * jnp.dot(h, w3_ref[...], preferred_element_type=F32)
    y = jnp.dot(a.astype(BF16), w2_ref[...], preferred_element_type=F32)
    o_ref[...] = x + mod[5:6] * y


def _ffn_call(x, ln_g, mod, w13, w2, layer):
    resident = dict(pipeline_mode=pl.Buffered(1))
    return pl.pallas_call(
        _ffn_kernel,
        grid=(T // TM,),
        in_specs=[
            pl.BlockSpec((TM, D), lambda i: (i, 0)),
            pl.BlockSpec((1, D), lambda i: (0, 0)),
            pl.BlockSpec((1, 6, D), lambda i: (_cond_of_row(i * TM), 0, 0)),
            pl.BlockSpec((None, D, D_FF), lambda i: (layer, 0, 0), **resident),
            pl.BlockSpec((None, D, D_FF), lambda i: (layer, 0, 1), **resident),
            pl.BlockSpec((None, D_FF, D), lambda i: (layer, 0, 0), **resident),
        ],
        out_specs=pl.BlockSpec((TM, D), lambda i: (i, 0)),
        out_shape=jax.ShapeDtypeStruct((T, D), F32),
        compiler_params=_cparams(("parallel",)),
        name="ffn_dense",
    )(x, ln_g.reshape(1, D), mod, w13, w13, w2)


def _s5_kernel(u_ref, bm_ref, cm_ref, ar_ref, ai_ref, hr_ref, hi_ref,
               y_ref, fr_ref, fi_ref, bu_s, st_s, cr_s, ci_s, *, nb, steps):
    d = pl.program_id(0)
    tb = pl.program_id(3)
    rows = nb * steps
    pitch = steps + S5_ROW_PAD
    nch = S5_LANES // LANES
    u = u_ref[...].reshape(rows, S5_SLAB).astype(BF16)
    bu = jnp.dot(u, bm_ref[0, 0], preferred_element_type=F32)
    for c in range(2 * nch):
        for k in range(nb):
            bu_s[c, k * pitch:k * pitch + steps, :] = bu[k * steps:(k + 1) * steps, c * LANES:(c + 1) * LANES]

    @pl.when(tb == 0)
    def _():
        cr_s[...] = hr_ref[0, 0]
        ci_s[...] = hi_ref[0, 0]

    ar = [jnp.broadcast_to(ar_ref[0, 0][:, c * LANES:(c + 1) * LANES], (nb, LANES)) for c in range(nch)]
    ai = [jnp.broadcast_to(ai_ref[0, 0][:, c * LANES:(c + 1) * LANES], (nb, LANES)) for c in range(nch)]

    def step(i, carry):
        l = jnp.where(d == 0, i, steps - 1 - i)
        idx = pl.ds(l, nb, stride=pitch)
        out = []
        for c in range(nch):
            sr, si = carry[2 * c], carry[2 * c + 1]
            nr = ar[c] * sr - ai[c] * si + bu_s[c, idx, :]
            ni = ar[c] * si + ai[c] * sr + bu_s[nch + c, idx, :]
            st_s[c, idx, :] = nr
            st_s[nch + c, idx, :] = ni
            out += [nr, ni]
        return tuple(out)

    init = []
    for c in range(nch):
        init += [cr_s[:, c * LANES:(c + 1) * LANES], ci_s[:, c * LANES:(c + 1) * LANES]]
    fin = lax.fori_loop(0, steps, step, tuple(init), unroll=S5_UNROLL)
    sr = jnp.concatenate([fin[2 * c] for c in range(nch)], axis=1)
    si = jnp.concatenate([fin[2 * c + 1] for c in range(nch)], axis=1)
    cr_s[...] = sr
    ci_s[...] = si
    fr_ref[0, 0] = sr
    fi_ref[0, 0] = si
    states = jnp.concatenate(
        [jnp.concatenate([st_s[c, k * pitch:k * pitch + steps, :] for k in range(nb)], axis=0)
         for c in range(2 * nch)], axis=1).astype(BF16)
    y = jnp.dot(states, cm_ref[0, 0], preferred_element_type=F32)
    y_ref[...] = y.reshape(nb, steps, S5_SLAB)


def _s5_call(u, bmat, cmat, a_re, a_im, h_re, h_im, *, row0, n_seq, seq_len, nb, steps, name):
    n_sg, n_tb = n_seq // nb, seq_len // steps
    sg0 = row0 // (nb * seq_len)

    def tbi(d, tb):
        return jnp.where(d == 0, tb, n_tb - 1 - tb)

    par = lambda last: pl.BlockSpec((1, 1) + last, lambda d, j, sg, tb: (d, j, 0, 0))
    st = pl.BlockSpec((1, 1, nb, S5_LANES), lambda d, j, sg, tb: (d, j, sg, 0))
    y, f_re, f_im = pl.pallas_call(
        functools.partial(_s5_kernel, nb=nb, steps=steps),
        grid=(2, S5_NSLAB, n_sg, n_tb),
        in_specs=[
            pl.BlockSpec((None, nb, steps, S5_SLAB), lambda d, j, sg, tb: (j, sg0 + sg, tbi(d, tb), 0)),
            par((S5_SLAB, 2 * S5_LANES)),
            par((2 * S5_LANES, S5_SLAB)),
            par((1, S5_LANES)), par((1, S5_LANES)),
            st, st,
        ],
        out_specs=[
            pl.BlockSpec((None, None, nb, steps, S5_SLAB), lambda d, j, sg, tb: (d, j, sg, tbi(d, tb), 0)),
            st, st,
        ],
        out_shape=[
            jax.ShapeDtypeStruct((2, S5_NSLAB, n_seq, seq_len, S5_SLAB), F32),
            jax.ShapeDtypeStruct((2, S5_NSLAB, n_seq, S5_LANES), F32),
            jax.ShapeDtypeStruct((2, S5_NSLAB, n_seq, S5_LANES), F32),
        ],
        scratch_shapes=[pltpu.VMEM((2 * S5_LANES // LANES, nb * (steps + S5_ROW_PAD), LANES), F32)] * 2
                       + [pltpu.VMEM((nb, S5_LANES), F32)] * 2,
        compiler_params=_cparams(("parallel", "parallel", "parallel", "arbitrary")),
        name=name,
    )(u.reshape(S5_NSLAB, T // seq_len, seq_len, S5_SLAB), bmat, cmat, a_re, a_im, h_re, h_im)
    return y.reshape(2, S5_NSLAB, n_seq * seq_len, S5_SLAB), f_re, f_im


def _s5_params(lam_re, lam_im, log_step, b_re, b_im, c_re, c_im):
    step = jnp.exp(log_step)[..., None]
    zr, zi = lam_re * step, lam_im * step
    mag = jnp.exp(zr)
    a_re, a_im = mag * jnp.cos(zi), mag * jnp.sin(zi)
    nr, ni = a_re - 1.0, a_im
    den = lam_re * lam_re + lam_im * lam_im
    k_re = (nr * lam_re + ni * lam_im) / den
    k_im = (ni * lam_re - nr * lam_im) / den
    bb_re = k_re[..., None] * b_re - k_im[..., None] * b_im
    bb_im = k_re[..., None] * b_im + k_im[..., None] * b_re
    gl = S5_SLAB // SSM_GROUP
    eye = jnp.eye(gl, dtype=F32)

    def bdiag_in(w):
        w = w.reshape(2, S5_NSLAB, gl, SSM_STATE, SSM_GROUP)
        return jnp.einsum('dsgnp,gh->dsgphn', w, eye).reshape(2, S5_NSLAB, S5_SLAB, S5_LANES)

    def bdiag_out(w):
        w = w.reshape(2, S5_NSLAB, gl, SSM_GROUP, SSM_STATE)
        return jnp.einsum('dsgpn,gh->dsgnhp', w, eye).reshape(2, S5_NSLAB, S5_LANES, S5_SLAB)

    bmat = jnp.concatenate([bdiag_in(bb_re), bdiag_in(bb_im)], axis=-1).astype(BF16)
    cmat = jnp.concatenate([bdiag_out(c_re), -bdiag_out(c_im)], axis=-2).astype(BF16)
    slab = lambda a: a.reshape(2, S5_NSLAB, 1, S5_LANES)
    return bmat, cmat, slab(a_re), slab(a_im)


def _state_to_slabs(h):
    b = h.shape[0]
    return jnp.transpose(h.reshape(b, 2, S5_NSLAB, S5_LANES), (1, 2, 0, 3))


def _slabs_to_state(f):
    b = f.shape[2]
    return jnp.transpose(f, (2, 0, 1, 3)).reshape(b, 2, SSM_GROUPS, SSM_STATE)


def _glu_res_kernel(u_ref, yl_ref, yc_ref, x_ref, mod_ref, d_ref, w_ref, b_ref, o_ref):
    is_lat = pl.program_id(0) < LAT_TILES
    y = jnp.where(is_lat, _load_slabs(yl_ref.at[0]) + _load_slabs(yl_ref.at[1]),
                  _load_slabs(yc_ref.at[0]) + _load_slabs(yc_ref.at[1]))
    yt = _load_slabs(u_ref) * d_ref[...] + y
    z = _gelu_tanh(yt)
    gate = _sigmoid(jnp.dot(z.astype(BF16), w_ref[...], preferred_element_type=F32) + b_ref[...])
    o_ref[...] = x_ref[...] + mod_ref[0][2:3] * (z * gate)


def _glu_res_call(u, y_lat, y_ctx, x, mod, d_skip, w_glu, b_glu):
    return pl.pallas_call(
        _glu_res_kernel,
        grid=(T // TM,),
        in_specs=[
            pl.BlockSpec((S5_NSLAB, TM, LANES), lambda i: (0, i, 0)),
            pl.BlockSpec((2, S5_NSLAB, TM, LANES), lambda i: (0, 0, _lat_tile(i), 0)),
            pl.BlockSpec((2, S5_NSLAB, TM, LANES), lambda i: (0, 0, _ctx_tile(i), 0)),
            pl.BlockSpec((TM, D), lambda i: (i, 0)),
            pl.BlockSpec((1, 6, D), lambda i: (_cond_of_row(i * TM), 0, 0)),
            pl.BlockSpec((1, D), lambda i: (0, 0)),
            pl.BlockSpec((D, D), lambda i: (0, 0)),
            pl.BlockSpec((1, D), lambda i: (0, 0)),
        ],
        out_specs=pl.BlockSpec((TM, D), lambda i: (i, 0)),
        out_shape=jax.ShapeDtypeStruct((T, D), F32),
        compiler_params=_cparams(("parallel",)),
        name="s5_glu_res",
    )(u, y_lat, y_ctx, x, mod, d_skip.reshape(1, D), w_glu, b_glu.reshape(1, D))


HALO = 8


def _conv_kernel(x_ref, xp_ref, xn_ref, g_ref, mod_ref, win_ref, cw_ref, cb_ref, wout_ref, o_ref):
    i = pl.program_id(0)
    mod = mod_ref[0]
    x = x_ref[...]
    x_ext = jnp.concatenate([xp_ref[...], x, xn_ref[...]], axis=0)
    h = _modnorm(x_ext, g_ref[...], mod[1:2], mod[0:1]).astype(BF16)
    proj = jnp.dot(h, win_ref[...], preferred_element_type=F32)
    z = proj[:, D:2 * D] * proj[:, 2 * D:]
    seq_mask = jnp.where(i < LAT_TILES, DEC_SEQ - 1, SEQ - 1)
    pos = (lax.broadcasted_iota(jnp.int32, (TM, 1), 0) + i * TM) & seq_mask
    zp = jnp.where(pos == 0, 0.0, z[HALO - 1:HALO - 1 + TM])
    zn = jnp.where(pos == seq_mask, 0.0, z[HALO + 1:HALO + 1 + TM])
    cw = cw_ref[...]
    zc = cw[0:1] * zp + cw[1:2] * z[HALO:HALO + TM] + cw[2:3] * zn + cb_ref[...]
    a = (proj[HALO:HALO + TM, :D] * zc).astype(BF16)
    y = jnp.dot(a, wout_ref[...], preferred_element_type=F32)
    o_ref[...] = x + mod[2:3] * y


def _conv_call(x, ln_g, mod, w_in, conv_w, conv_b, w_out):
    rh = TM // HALO
    nh = T // HALO
    resident = dict(pipeline_mode=pl.Buffered(1))
    return pl.pallas_call(
        _conv_kernel,
        grid=(T // TM,),
        in_specs=[
            pl.BlockSpec((TM, D), lambda i: (i, 0)),
            pl.BlockSpec((HALO, D), lambda i: (jnp.maximum(i * rh - 1, 0), 0)),
            pl.BlockSpec((HALO, D), lambda i: (jnp.minimum((i + 1) * rh, nh - 1), 0)),
            pl.BlockSpec((1, D), lambda i: (0, 0)),
            pl.BlockSpec((1, 6, D), lambda i: (_cond_of_row(i * TM), 0, 0)),
            pl.BlockSpec((D, 3 * D), lambda i: (0, 0), **resident),
            pl.BlockSpec((3, D), lambda i: (0, 0)),
            pl.BlockSpec((1, D), lambda i: (0, 0)),
            pl.BlockSpec((D, D), lambda i: (0, 0), **resident),
        ],
        out_specs=pl.BlockSpec((TM, D), lambda i: (i, 0)),
        out_shape=jax.ShapeDtypeStruct((T, D), F32),
        compiler_params=_cparams(("parallel",)),
        name="conv_mix",
    )(x, x, x, ln_g.reshape(1, D), mod, w_in, conv_w, conv_b.reshape(1, D), w_out)


def _router_kernel(x_ref, g_ref, mod_ref, r_ref, h_ref, rt_ref):
    mod = mod_ref[0]
    h = _modnorm(x_ref[...], g_ref[...], mod[4:5], mod[3:4])
    h_ref[...] = h
    logits = jnp.dot(h, r_ref[...], preferred_element_type=F32, precision=lax.Precision.HIGHEST)
    lane = lax.broadcasted_iota(jnp.int32, logits.shape, 1)
    logits = jnp.where(lane < N_EXPERTS, logits, -jnp.inf)
    m1 = jnp.max(logits, axis=-1, keepdims=True)
    i1 = jnp.min(jnp.where(logits == m1, lane, LANES), axis=-1, keepdims=True)
    rest = jnp.where(lane == i1, -jnp.inf, logits)
    m2 = jnp.max(rest, axis=-1, keepdims=True)
    i2 = jnp.min(jnp.where(rest == m2, lane, LANES), axis=-1, keepdims=True)
    e2 = jnp.exp(m2 - m1)
    w1 = 1.0 / (1.0 + e2)
    w2 = e2 / (1.0 + e2)
    rt_ref[...] = jnp.where(lane == 0, i1.astype(F32),
                            jnp.where(lane == 1, i2.astype(F32),
                                      jnp.where(lane == 2, w1, jnp.where(lane == 3, w2, 0.0))))


def _router_call(x, ln_g, mod, router_pad):
    return pl.pallas_call(
        _router_kernel,
        grid=(T // TM,),
        in_specs=[
            pl.BlockSpec((TM, D), lambda i: (i, 0)),
            pl.BlockSpec((1, D), lambda i: (0, 0)),
            pl.BlockSpec((1, 6, D), lambda i: (_cond_of_row(i * TM), 0, 0)),
            pl.BlockSpec((D, LANES), lambda i: (0, 0)),
        ],
        out_specs=[pl.BlockSpec((TM, D), lambda i: (i, 0)),
                   pl.BlockSpec((TM, LANES), lambda i: (i, 0))],
        out_shape=[jax.ShapeDtypeStruct((T, D), F32), jax.ShapeDtypeStruct((T, LANES), F32)],
        compiler_params=_cparams(("parallel",)),
        name="moe_router",
    )(x, ln_g.reshape(1, D), mod, router_pad)


R_MAX = 2 * T + N_EXPERTS * TM_MOE
N_TILES = R_MAX // TM_MOE
DMA_UNROLL = 8
GATHER_SHIFT = 3
GATHER_SPREAD = 1 << GATHER_SHIFT


def _row_copy(src_hbm, src_row, dst_vmem, dst_row, sem):
    return pltpu.make_async_copy(src_hbm.at[pl.ds(src_row, 1), :], dst_vmem.at[pl.ds(dst_row, 1), :], sem)


def _gather_kernel(nv_ref, src_ref, h_hbm, o_ref, buf, sem):
    i = pl.program_id(0)
    nv = nv_ref[0]

    def issue_tile(t):
        slot = t % 2
        base = t * TM_MOE

        def issue(g, c):
            for k in range(GATHER_SPREAD):
                r = k * (TM_MOE // GATHER_SPREAD) + g
                _row_copy(h_hbm, src_ref[base + r], buf.at[slot], r, sem.at[slot]).start(priority=k % 2)
            return c

        lax.fori_loop(0, TM_MOE // GATHER_SPREAD, issue, 0)

    @pl.when(jnp.logical_and(i == 0, nv > 0))
    def _():
        issue_tile(0)

    @pl.when(i + 1 < nv)
    def _():
        issue_tile(i + 1)

    @pl.when(i < nv)
    def _():
        slot = i % 2
        pltpu.make_async_copy(h_hbm.at[pl.ds(0, TM_MOE), :], buf.at[slot], sem.at[slot]).wait()
        o_ref[...] = buf[slot].astype(BF16)

    @pl.when(i >= nv)
    def _():
        o_ref[...] = jnp.zeros_like(o_ref)


def _gather_call(n_valid, src_tok, h):
    return pl.pallas_call(
        _gather_kernel,
        grid_spec=pltpu.PrefetchScalarGridSpec(
            num_scalar_prefetch=2,
            grid=(N_TILES,),
            in_specs=[pl.BlockSpec(memory_space=pl.ANY)],
            out_specs=pl.BlockSpec((TM_MOE, D), lambda i, nv, src: (i, 0)),
            scratch_shapes=[pltpu.VMEM((2, TM_MOE, D), F32), pltpu.SemaphoreType.DMA((2,))],
        ),
        out_shape=jax.ShapeDtypeStruct((R_MAX, D), BF16),
        compiler_params=_cparams(("arbitrary",)),
        name="moe_gather",
    )(n_valid, src_tok, h)


def _moe_kernel(nv_ref, te_ref, tr_ref, xs_ref, w1_ref, w3_ref, w2_ref, o_ref, acc_s):
    i = pl.program_id(0)
    f = pl.program_id(1)
    nf = pl.num_programs(1)
    valid = i < nv_ref[0]
    half = TM_MOE // 2
    short = tr_ref[i] <= half

    @pl.when(jnp.logical_and(valid, f == 0))
    def _():
        acc_s[...] = jnp.zeros_like(acc_s)

    def swiglu_rows(rows):
        xs = xs_ref[:rows, :]
        w13 = jnp.concatenate([w1_ref[...].astype(BF16), w3_ref[...].astype(BF16)], axis=1)
        gu = jnp.dot(xs, w13, preferred_element_type=F32)
        a = _silu(gu[:, :TF_MOE]) * gu[:, TF_MOE:]
        acc_s[:rows, :] += jnp.dot(a.astype(BF16), w2_ref[...].astype(BF16), preferred_element_type=F32)

    @pl.when(jnp.logical_and(valid, jnp.logical_not(short)))
    def _():
        swiglu_rows(TM_MOE)

    @pl.when(jnp.logical_and(valid, short))
    def _():
        swiglu_rows(half)

    @pl.when(f == nf - 1)
    def _():
        o_ref[...] = jnp.where(valid, acc_s[...], 0.0)


def _moe_call(n_valid, tile_expert, tile_rows, xs, w13, w2, layer):
    nf = D_FF_EXPERT // TF_MOE

    def fe(i, f, nv):
        return jnp.where(i < nv[0], f, nf - 1)

    return pl.pallas_call(
        _moe_kernel,
        grid_spec=pltpu.PrefetchScalarGridSpec(
            num_scalar_prefetch=3,
            grid=(N_TILES, nf),
            in_specs=[
                pl.BlockSpec((TM_MOE, D), lambda i, f, nv, te, tr: (i, 0)),
                pl.BlockSpec((None, None, D, TF_MOE),
                             lambda i, f, nv, te, tr: (layer, te[i], 0, fe(i, f, nv))),
                pl.BlockSpec((None, None, D, TF_MOE),
                             lambda i, f, nv, te, tr: (layer, te[i], 0, nf + fe(i, f, nv))),
                pl.BlockSpec((None, None, TF_MOE, D),
                             lambda i, f, nv, te, tr: (layer, te[i], fe(i, f, nv), 0)),
            ],
            out_specs=pl.BlockSpec((TM_MOE, D), lambda i, f, nv, te, tr: (i, 0)),
            scratch_shapes=[pltpu.VMEM((TM_MOE, D), F32)],
        ),
        out_shape=jax.ShapeDtypeStruct((R_MAX, D), F32),
        compiler_params=_cparams(("arbitrary", "arbitrary")),
        name="moe_experts",
    )(n_valid, tile_expert, tile_rows, xs, w13, w13, w2)


TM_COMB = 512


def _combine_kernel(pa_ref, pb_ref, y_hbm, x_ref, rt_ref, mod_ref, o_ref, buf_a, buf_b, sem):
    i = pl.program_id(0)

    def issue_tile(t):
        slot = t % 2
        base = t * TM_COMB

        def issue(r, c):
            _row_copy(y_hbm, pa_ref[base + r], buf_a.at[slot], r, sem.at[slot]).start(priority=0)
            _row_copy(y_hbm, pb_ref[base + r], buf_b.at[slot], r, sem.at[slot]).start(priority=1)
            return c

        lax.fori_loop(0, TM_COMB, issue, 0, unroll=DMA_UNROLL)

    @pl.when(i == 0)
    def _():
        issue_tile(0)

    @pl.when(i + 1 < pl.num_programs(0))
    def _():
        issue_tile(i + 1)

    slot = i % 2
    for buf in (buf_a, buf_b):
        pltpu.make_async_copy(y_hbm.at[pl.ds(0, TM_COMB), :], buf.at[slot], sem.at[slot]).wait()
    rt = rt_ref[...]
    ffn = rt[:, 2:3] * buf_a[slot] + rt[:, 3:4] * buf_b[slot]
    o_ref[...] = x_ref[...] + mod_ref[0][5:6] * ffn


def _combine_call(pos_a, pos_b, y, x, rt, mod):
    return pl.pallas_call(
        _combine_kernel,
        grid_spec=pltpu.PrefetchScalarGridSpec(
            num_scalar_prefetch=2,
            grid=(T // TM_COMB,),
            in_specs=[
                pl.BlockSpec(memory_space=pl.ANY),
                pl.BlockSpec((TM_COMB, D), lambda i, pa, pb: (i, 0)),
                pl.BlockSpec((TM_COMB, LANES), lambda i, pa, pb: (i, 0)),
                pl.BlockSpec((1, 6, D), lambda i, pa, pb: (_cond_of_row(i * TM_COMB), 0, 0)),
            ],
            out_specs=pl.BlockSpec((TM_COMB, D), lambda i, pa, pb: (i, 0)),
            scratch_shapes=[pltpu.VMEM((2, TM_COMB, D), F32), pltpu.VMEM((2, TM_COMB, D), F32),
                            pltpu.SemaphoreType.DMA((2,))],
        ),
        out_shape=jax.ShapeDtypeStruct((T, D), F32),
        compiler_params=_cparams(("arbitrary",)),
        name="moe_combine",
    )(pos_a, pos_b, y, x, rt, mod)


def _routing_tables(rt):
    ea = rt[:, 0:2].astype(jnp.int32).reshape(-1)
    onehot = (ea[:, None] == jnp.arange(N_EXPERTS, dtype=jnp.int32)[None, :]).astype(jnp.int32)
    csum = jnp.cumsum(onehot, axis=0)
    rank = jnp.take_along_axis(csum, ea[:, None], axis=1)[:, 0] - 1
    counts = csum[-1]
    padded = ((counts + TM_MOE - 1) // TM_MOE) * TM_MOE
    ends = jnp.cumsum(padded)
    pos = (ends - padded)[ea] + rank
    n_valid = (ends[-1] // TM_MOE).astype(jnp.int32).reshape(1)
    tile_start = jnp.arange(N_TILES, dtype=jnp.int32) * TM_MOE
    expert_of_row = lambda r: jnp.sum((r[:, None] >= ends[None, :]).astype(jnp.int32), axis=1)
    tile_expert = jnp.minimum(expert_of_row(tile_start), expert_of_row(ends[-1:] - 1))
    real_end = (ends - padded + counts)[tile_expert]
    tile_rows = jnp.clip(real_end - tile_start, 0, TM_MOE).astype(jnp.int32)
    tok = jnp.arange(2 * T, dtype=jnp.int32) // 2
    src_tok = (jnp.arange(R_MAX, dtype=jnp.int32) % T).at[pos].set(tok)
    pos2 = pos.reshape(T, 2).astype(jnp.int32)
    return n_valid, tile_expert, tile_rows, src_tok, pos2[:, 0], pos2[:, 1]


def _moe_layer(x, ln_g, mod, router, w13, w2, layer):
    router_pad = jnp.pad(router, ((0, 0), (0, LANES - N_EXPERTS)))
    h, rt = _router_call(x, ln_g, mod, router_pad)
    n_valid, tile_expert, tile_rows, src_tok, pos_a, pos_b = _routing_tables(rt)
    xs = _gather_call(n_valid, src_tok, h)
    y = _moe_call(n_valid, tile_expert, tile_rows, xs, w13, w2, layer)
    return _combine_call(pos_a, pos_b, y, x, rt, mod)


def kernel(x_prompt, x_sample, c, cache_na_k, cache_na_v, state_ssm_re, state_ssm_im, c_ctx,
           ln1_g, ln2_g, ada_w, ada_b, na_w_qkv, na_w_o, na_q_g, na_k_g, na_rpb,
           ssm_lam_re, ssm_lam_im, ssm_log_step, ssm_b_re, ssm_b_im, ssm_c_re, ssm_c_im,
           ssm_d, ssm_w_glu, ssm_b_glu, cv_w_in, cv_conv_w, cv_conv_b, cv_w_out,
           ffn_w13, ffn_w2, moe_router, moe_w13, moe_w2):
    depth = ada_w.shape[0]
    x = (x_sample.reshape(T_LAT, D), x_prompt.reshape(T_CTX, D))
    conds = jnp.concatenate([c, c_ctx[None, :], jnp.zeros((N_COND - DEC_BATCH - 1, D), F32)], axis=0)
    mods = _ada_call(conds, ada_w, ada_b).reshape(depth, N_COND, 6, D)
    ffn_w13_b, ffn_w2_b = ffn_w13.astype(BF16), ffn_w2.astype(BF16)

    new_k = new_v = None
    s_re, s_im = [], []
    for l in range(depth):
        kind, j = l % 3, l // 3
        mod = mods[l]
        if kind == 0:
            qkv = _normproj_call(x, ln1_g[l], mod, na_w_qkv[j].astype(BF16), shift=0, scale=1,
                                 name="qkv_proj", slab_out=True)
            qg2 = jnp.tile(na_q_g[j], 2).reshape(1, LANES)
            kg2 = jnp.tile(na_k_g[j], 2).reshape(1, LANES)
            past = cache_na_k.shape[3]
            pair = lambda a: jnp.transpose(a[:, j].reshape(DEC_BATCH, HP, 2, past, HEAD_DIM),
                                           (0, 1, 3, 2, 4)).reshape(DEC_BATCH, HP, past, LANES)
            ctx_k, ctx_v = pair(cache_na_k), pair(cache_na_v)
            o_lat = _lat_attn_call(qkv, ctx_k, ctx_v, qg2, kg2, _bias_pairs(na_rpb[j]))
            o_ctx, new_k, new_v = _ctx_attn_call(qkv, qg2, kg2, new_k, new_v)
            x = _proj_res_call(o_lat, o_ctx, x, mod, na_w_o[j].astype(BF16), gate=2, name="attn_out")
        elif kind == 1:
            u = _norm_call(x, ln1_g[l], mod, shift=0, scale=1, name="s5_norm")
            bmat, cmat, a_re, a_im = _s5_params(ssm_lam_re[j], ssm_lam_im[j], ssm_log_step[j],
                                                ssm_b_re[j], ssm_b_im[j], ssm_c_re[j], ssm_c_im[j])
            y_lat, _, _ = _s5_call(u, bmat, cmat, a_re, a_im,
                                   _state_to_slabs(state_ssm_re[:, j]), _state_to_slabs(state_ssm_im[:, j]),
                                   row0=0, n_seq=DEC_BATCH, seq_len=DEC_SEQ,
                                   nb=DEC_BATCH, steps=S5_T_LAT, name="s5_lat")
            zero = jnp.zeros((2, S5_NSLAB, BATCH, S5_LANES), F32)
            y_ctx, f_re, f_im = _s5_call(u, bmat, cmat, a_re, a_im, zero, zero,
                                         row0=T_LAT, n_seq=BATCH, seq_len=SEQ, nb=8, steps=SEQ,
                                         name="s5_ctx")
            s_re.append(_slabs_to_state(f_re))
            s_im.append(_slabs_to_state(f_im))
            x = _glu_res_call(u, y_lat, y_ctx, x, mod, ssm_d[j], ssm_w_glu[j].astype(BF16), ssm_b_glu[j])
        else:
            x = _conv_call(x, ln1_g[l], mod, cv_w_in[j].astype(BF16), cv_conv_w[j], cv_conv_b[j],
                           cv_w_out[j].astype(BF16))
        jj = l // 2
        if l % 2 == 0:
            x = _ffn_call(x, ln2_g[l], mod, ffn_w13_b, ffn_w2_b, jj)
        else:
            x = _moe_layer(x, ln2_g[l], mod, moe_router[jj], moe_w13, moe_w2, jj)

    y_sample = x[:T_LAT].reshape(DEC_BATCH, DEC_SEQ, D)
    y_prompt = x[T_LAT:].reshape(BATCH, SEQ, D)
    return (y_prompt, y_sample, new_k, new_v, jnp.stack(s_re, axis=1), jnp.stack(s_im, axis=1))
```

```python
import functools
import math

import numpy as np
import jax
import jax.numpy as jnp
from jax import lax
from jax.experimental import pallas as pl
from jax.experimental.pallas import tpu as pltpu

F32 = jnp.float32
BF16 = jnp.bfloat16

D = 1024
BATCH, SEQ = 16, 256
DEC_BATCH, DEC_SEQ = 4, 2048
GRID_W = 64
ROWS = DEC_SEQ // GRID_W
N_HEADS, HEAD_DIM = 16, 64
WIN_ROWS, WIN_COLS = 8, 16
SSM_GROUP, SSM_GROUPS, SSM_STATE = 16, 64, 64
D_FF = 2816
N_EXPERTS = 8
D_FF_EXPERT = 3584
EPS = 1e-6
NEG_INF = -1e30
SCALE = HEAD_DIM ** -0.5

T_LAT = DEC_BATCH * DEC_SEQ
T_CTX = BATCH * SEQ
T = T_LAT + T_CTX
N_COND = 8
CTX_COND = DEC_BATCH

LANES = 128
VMEM_LIMIT = 56 * 1024 * 1024

TM = 512
TM_MOE = 1024
TF_MOE = 512
MOE_ROW_SPLITS = 4
HP = N_HEADS // 2
QB_ROWS = 4
D_SLABS = D // LANES
S5_SLAB = 128
S5_SUB = S5_SLAB // LANES
S5_NSLAB = D // S5_SLAB
S5_LANES = (S5_SLAB // SSM_GROUP) * SSM_STATE
S5_T_LAT = 512
S5_T_CTX = SEQ
S5_ROW_PAD = 4
S5_UNROLL = 8


def _cparams(sem):
    return pltpu.CompilerParams(dimension_semantics=sem, vmem_limit_bytes=VMEM_LIMIT)


def _cond_of_row(row):
    return jnp.minimum(row // DEC_SEQ, CTX_COND)


def _modnorm(x, g, sc, sh):
    ms = jnp.mean(x * x, axis=-1, keepdims=True)
    y = x * lax.rsqrt(ms + EPS) * g
    return y * (1.0 + sc) + sh


def _sigmoid(x):
    return 1.0 / (1.0 + jnp.exp(-x))


def _silu(x):
    return x * _sigmoid(x)


def _gelu_tanh(x):
    c = math.sqrt(2.0 / math.pi)
    return 0.5 * x * (1.0 + jnp.tanh(c * (x + 0.044715 * (x * x * x))))


def _ada_kernel(c_ref, w_ref, b_ref, o_ref):
    s = _silu(c_ref[...]).astype(BF16)
    o_ref[0] = jnp.dot(s, w_ref[0].astype(BF16), preferred_element_type=F32) + b_ref[0]


def _ada_call(conds, ada_w, ada_b):
    depth = ada_w.shape[0]
    tn = 1536
    return pl.pallas_call(
        _ada_kernel,
        grid=(depth, 6 * D // tn),
        in_specs=[
            pl.BlockSpec((N_COND, D), lambda l, n: (0, 0)),
            pl.BlockSpec((1, D, tn), lambda l, n: (l, 0, n)),
            pl.BlockSpec((1, 1, tn), lambda l, n: (l, 0, n)),
        ],
        out_specs=pl.BlockSpec((1, N_COND, tn), lambda l, n: (l, 0, n)),
        out_shape=jax.ShapeDtypeStruct((depth, N_COND, 6 * D), F32),
        compiler_params=_cparams(("parallel", "parallel")),
        name="ada_mod",
    )(conds, ada_w, ada_b.reshape(depth, 1, 6 * D))


LAT_TILES = T_LAT // TM


def _lat_tile(i):
    return jnp.minimum(i, LAT_TILES - 1)


def _ctx_tile(i):
    return jnp.maximum(i - LAT_TILES, 0)


def _rows_specs(x):
    if isinstance(x, tuple):
        return [pl.BlockSpec((TM, D), lambda i: (_lat_tile(i), 0)),
                pl.BlockSpec((TM, D), lambda i: (_ctx_tile(i), 0))], list(x)
    return [pl.BlockSpec((TM, D), lambda i: (i, 0))], [x]


def _rows_value(refs):
    if len(refs) == 2:
        return jnp.where(pl.program_id(0) < LAT_TILES, refs[0][...], refs[1][...])
    return refs[0][...]


def _store_slabs(o_ref, val):
    for c in range(val.shape[1] // LANES):
        o_ref[c] = val[:, c * LANES:(c + 1) * LANES]


def _load_slabs(ref):
    return jnp.concatenate([ref[c] for c in range(ref.shape[0])], axis=1)


def _normproj_kernel(*refs, shift, scale, slab_out):
    *x_refs, g_ref, mod_ref, w_ref, o_ref = refs
    mod = mod_ref[0]
    h = _modnorm(_rows_value(x_refs), g_ref[...], mod[scale:scale + 1], mod[shift:shift + 1])
    res = jnp.dot(h.astype(BF16), w_ref[...], preferred_element_type=F32)
    if slab_out:
        _store_slabs(o_ref, res)
    else:
        o_ref[...] = res


def _normproj_call(x, ln_g, mod, w, *, shift, scale, name, slab_out=False):
    n = w.shape[1]
    x_specs, x_args = _rows_specs(x)
    if slab_out:
        out_spec = pl.BlockSpec((n // LANES, TM, LANES), lambda i: (0, i, 0))
        out_shape = jax.ShapeDtypeStruct((n // LANES, T, LANES), F32)
    else:
        out_spec = pl.BlockSpec((TM, n), lambda i: (i, 0))
        out_shape = jax.ShapeDtypeStruct((T, n), F32)
    return pl.pallas_call(
        functools.partial(_normproj_kernel, shift=shift, scale=scale, slab_out=slab_out),
        grid=(T // TM,),
        in_specs=x_specs + [
            pl.BlockSpec((1, D), lambda i: (0, 0)),
            pl.BlockSpec((1, 6, D), lambda i: (_cond_of_row(i * TM), 0, 0)),
            pl.BlockSpec((D, n), lambda i: (0, 0)),
        ],
        out_specs=out_spec,
        out_shape=out_shape,
        compiler_params=_cparams(("parallel",)),
        name=name,
    )(*x_args, ln_g.reshape(1, D), mod, w)


def _norm_kernel(x_ref, g_ref, mod_ref, o_ref, *, shift, scale):
    mod = mod_ref[0]
    _store_slabs(o_ref, _modnorm(x_ref[...], g_ref[...], mod[scale:scale + 1], mod[shift:shift + 1]))


def _norm_call(x, ln_g, mod, *, shift, scale, name):
    return pl.pallas_call(
        functools.partial(_norm_kernel, shift=shift, scale=scale),
        grid=(T // TM,),
        in_specs=[
            pl.BlockSpec((TM, D), lambda i: (i, 0)),
            pl.BlockSpec((1, D), lambda i: (0, 0)),
            pl.BlockSpec((1, 6, D), lambda i: (_cond_of_row(i * TM), 0, 0)),
        ],
        out_specs=pl.BlockSpec((D // LANES, TM, LANES), lambda i: (0, i, 0)),
        out_shape=jax.ShapeDtypeStruct((D // LANES, T, LANES), F32),
        compiler_params=_cparams(("parallel",)),
        name=name,
    )(x, ln_g.reshape(1, D), mod)


def _proj_res_kernel(*refs, gate, n_x):
    al_ref, ac_ref = refs[:2]
    x_refs = refs[2:2 + n_x]
    mod_ref, w_ref, o_ref = refs[2 + n_x:]
    a = jnp.where(pl.program_id(0) < LAT_TILES, _load_slabs(al_ref), _load_slabs(ac_ref))
    y = jnp.dot(a.astype(BF16), w_ref[...], preferred_element_type=F32)
    o_ref[...] = _rows_value(x_refs) + mod_ref[0][gate:gate + 1] * y


def _proj_res_call(a_lat, a_ctx, x, mod, w, *, gate, name):
    ns = D // LANES
    a_specs = [pl.BlockSpec((ns, TM, LANES), lambda i: (0, _lat_tile(i), 0)),
               pl.BlockSpec((ns, TM, LANES), lambda i: (0, _ctx_tile(i), 0))]
    a_args = [a_lat, a_ctx]
    x_specs, x_args = _rows_specs(x)
    return pl.pallas_call(
        functools.partial(_proj_res_kernel, gate=gate, n_x=len(x_args)),
        grid=(T // TM,),
        in_specs=a_specs + x_specs + [
            pl.BlockSpec((1, 6, D), lambda i: (_cond_of_row(i * TM), 0, 0)),
            pl.BlockSpec((D, D), lambda i: (0, 0)),
        ],
        out_specs=pl.BlockSpec((TM, D), lambda i: (i, 0)),
        out_shape=jax.ShapeDtypeStruct((T, D), F32),
        compiler_params=_cparams(("parallel",)),
        name=name,
    )(*a_args, *x_args, mod, w)


def _head_norm(x, g, head0):
    x2 = x * x
    s0 = jnp.sum(jnp.where(head0, x2, 0.0), axis=-1, keepdims=True)
    s1 = jnp.sum(jnp.where(head0, 0.0, x2), axis=-1, keepdims=True)
    ms = jnp.where(head0, s0, s1) * (1.0 / HEAD_DIM)
    return x * lax.rsqrt(ms + EPS) * g


def _nt_dot(a, b):
    return lax.dot_general(a, b, (((1,), (1,)), ((), ())), preferred_element_type=F32)


def _ctx_attn_kernel(*refs, n_prev):
    if n_prev:
        q_ref, k_ref, v_ref, qg_ref, kg_ref, pk_ref, pv_ref, o_ref, ko_ref, vo_ref = refs
        ko_ref[0, :n_prev] = pk_ref[0]
        vo_ref[0, :n_prev] = pv_ref[0]
    else:
        q_ref, k_ref, v_ref, qg_ref, kg_ref, o_ref, ko_ref, vo_ref = refs
    head0 = lax.broadcasted_iota(jnp.int32, (SEQ, LANES), 1) < HEAD_DIM
    for hp in range(HP):
        qn = _head_norm(q_ref[hp], qg_ref[...], head0) * SCALE
        kn = _head_norm(k_ref[hp], kg_ref[...], head0)
        v = v_ref[hp]
        ko_ref[0, n_prev, 2 * hp] = kn[:, :HEAD_DIM]
        ko_ref[0, n_prev, 2 * hp + 1] = kn[:, HEAD_DIM:]
        vo_ref[0, n_prev, 2 * hp] = v[:, :HEAD_DIM]
        vo_ref[0, n_prev, 2 * hp + 1] = v[:, HEAD_DIM:]
        knb = kn.astype(BF16)
        vb = v.astype(BF16)
        outs = []
        for hh in range(2):
            qm = jnp.where(head0 if hh == 0 else jnp.logical_not(head0), qn, 0.0).astype(BF16)
            s = _nt_dot(qm, knb)
            p = jnp.exp(s - jnp.max(s, axis=-1, keepdims=True))
            l = jnp.sum(p, axis=-1, keepdims=True)
            outs.append(jnp.dot(p.astype(BF16), vb, preferred_element_type=F32) / l)
        o_ref[hp] = jnp.where(head0, outs[0], outs[1])


def _ctx_attn_call(qkv, qg2, kg2, prev_k, prev_v):
    row0 = T_LAT // SEQ
    n_prev = 0 if prev_k is None else prev_k.shape[1]
    blk = lambda c: pl.BlockSpec((HP, SEQ, LANES), lambda b, c=c: (c, row0 + b, 0))
    kv_blk = lambda n: pl.BlockSpec((1, n, N_HEADS, SEQ, HEAD_DIM), lambda b: (b, 0, 0, 0, 0))
    kv_shape = jax.ShapeDtypeStruct((BATCH, n_prev + 1, N_HEADS, SEQ, HEAD_DIM), F32)
    prev_specs = [kv_blk(n_prev)] * 2 if n_prev else []
    prev_args = [prev_k, prev_v] if n_prev else []
    return pl.pallas_call(
        functools.partial(_ctx_attn_kernel, n_prev=n_prev),
        grid=(BATCH,),
        in_specs=[blk(0), blk(1), blk(2),
                  pl.BlockSpec((1, LANES), lambda b: (0, 0)),
                  pl.BlockSpec((1, LANES), lambda b: (0, 0))] + prev_specs,
        out_specs=[pl.BlockSpec((HP, SEQ, LANES), lambda b: (0, b, 0)), kv_blk(n_prev + 1), kv_blk(n_prev + 1)],
        out_shape=[jax.ShapeDtypeStruct((HP, T_CTX, LANES), F32), kv_shape, kv_shape],
        compiler_params=_cparams(("parallel",)),
        name="ctx_attn",
    )(qkv, qkv, qkv, qg2, kg2, *prev_args)


def _band_of_block(p):
    r_lo, r_hi = QB_ROWS * p, QB_ROWS * p + QB_ROWS - 1
    kr = min(WIN_ROWS, ROWS)
    lo = min(max(r_lo - kr // 2, 0), ROWS - kr)
    hi = min(max(r_hi - kr // 2, 0), ROWS - kr) + kr
    lo -= lo % 2
    hi += hi % 2
    return lo, hi - lo


def _lat_attn_kernel(q_ref, k_ref, v_ref, ck_ref, cv_ref, qg_ref, kg_ref, bp_ref, o_ref,
                     qm0_s, qm1_s, kt_s, v_s):
    head0 = lax.broadcasted_iota(jnp.int32, (DEC_SEQ, LANES), 1) < HEAD_DIM
    qn = _head_norm(q_ref[...], qg_ref[...], head0)
    qm0_s[...] = jnp.where(head0, qn * SCALE, 0.0).astype(BF16)
    qm1_s[...] = jnp.where(head0, 0.0, qn * SCALE).astype(BF16)
    kt_s[...] = _head_norm(k_ref[...], kg_ref[...], head0).T.astype(BF16)
    v_s[...] = v_ref[...].astype(BF16)
    ckt = ck_ref[...].T.astype(BF16)
    cvb = cv_ref[...].astype(BF16)

    qc = lax.broadcasted_iota(jnp.int32, (GRID_W, LANES), 0)
    kl = lax.broadcasted_iota(jnp.int32, (GRID_W, LANES), 1)
    kc = jnp.where(kl < GRID_W, kl, kl - GRID_W)
    cs = jnp.clip(qc - WIN_COLS // 2, 0, GRID_W - WIN_COLS)
    col_ok = jnp.logical_and(kc >= cs, kc < cs + WIN_COLS)
    left = kl < GRID_W
    mask_of = {
        (True, True): col_ok,
        (True, False): jnp.logical_and(col_ok, left),
        (False, True): jnp.logical_and(col_ok, jnp.logical_not(left)),
    }
    h0q = lax.broadcasted_iota(jnp.int32, (QB_ROWS * GRID_W, LANES), 1) < HEAD_DIM
    kr_win = min(WIN_ROWS, ROWS)

    for p in range(ROWS // QB_ROWS):
        u0, nrows = _band_of_block(p)
        q_lo = p * QB_ROWS * GRID_W
        ktb = kt_s[:, u0 * GRID_W:(u0 + nrows) * GRID_W]
        vb = v_s[u0 * GRID_W:(u0 + nrows) * GRID_W, :]
        nq = QB_ROWS * GRID_W
        qm = jnp.concatenate([qm0_s[q_lo:q_lo + nq, :], qm1_s[q_lo:q_lo + nq, :]], axis=0)
        n_loc = nrows * GRID_W
        s_all = jnp.dot(qm, jnp.concatenate([ktb, ckt], axis=1), preferred_element_type=F32)
        s_loc, s_ctx = s_all[:, :n_loc], s_all[:, n_loc:]
        p_rows, l_rows = [], []
        for hh in range(2):
            for i in range(QB_ROWS):
                r = p * QB_ROWS + i
                r0 = min(max(r - kr_win // 2, 0), ROWS - kr_win)
                rows = slice(hh * nq + i * GRID_W, hh * nq + (i + 1) * GRID_W)
                sbs = {}
                for m in range(nrows // 2):
                    kr = u0 + 2 * m
                    ok_l = r0 <= kr < r0 + kr_win
                    ok_r = r0 <= kr + 1 < r0 + kr_win
                    if ok_l or ok_r:
                        sb = s_loc[rows, m * LANES:(m + 1) * LANES] + bp_ref[hh, kr - r + WIN_ROWS]
                        sbs[m] = jnp.where(mask_of[(ok_l, ok_r)], sb, NEG_INF)
                sc = s_ctx[rows, :]
                mx = jnp.maximum(
                    jnp.max(functools.reduce(jnp.maximum, sbs.values()), axis=-1, keepdims=True),
                    jnp.max(sc, axis=-1, keepdims=True))
                pbs = {m: jnp.exp(sb - mx) for m, sb in sbs.items()}
                pc = jnp.exp(sc - mx)
                l_rows.append(jnp.sum(functools.reduce(jnp.add, pbs.values()), axis=-1, keepdims=True)
                              + jnp.sum(pc, axis=-1, keepdims=True))
                zero = jnp.zeros((GRID_W, LANES), BF16)
                p_rows.append(jnp.concatenate(
                    [pbs[m].astype(BF16) if m in pbs else zero for m in range(nrows // 2)]
                    + [pc.astype(BF16)], axis=1))
        o = jnp.dot(jnp.concatenate(p_rows, axis=0), jnp.concatenate([vb, cvb], axis=0),
                    preferred_element_type=F32)
        o = o / jnp.concatenate(l_rows, axis=0)
        o_ref[q_lo:q_lo + nq, :] = jnp.where(h0q, o[:nq], o[nq:])


def _lat_attn_call(qkv, ctx_k, ctx_v, qg2, kg2, bias_pairs):
    blk = lambda c: pl.BlockSpec((None, DEC_SEQ, LANES), lambda hp, b, c=c: (c * HP + hp, b, 0))
    cblk = pl.BlockSpec((None, None, ctx_k.shape[2], LANES), lambda hp, b: (b, hp, 0, 0))
    return pl.pallas_call(
        _lat_attn_kernel,
        grid=(HP, DEC_BATCH),
        in_specs=[blk(0), blk(1), blk(2), cblk, cblk,
                  pl.BlockSpec((1, LANES), lambda hp, b: (0, 0)),
                  pl.BlockSpec((1, LANES), lambda hp, b: (0, 0)),
                  pl.BlockSpec((2, 2 * WIN_ROWS, GRID_W, LANES), lambda hp, b: (hp, 0, 0, 0))],
        out_specs=pl.BlockSpec((None, DEC_SEQ, LANES), lambda hp, b: (hp, b, 0)),
        out_shape=jax.ShapeDtypeStruct((HP, T_LAT, LANES), F32),
        scratch_shapes=[pltpu.VMEM((DEC_SEQ, LANES), BF16), pltpu.VMEM((DEC_SEQ, LANES), BF16),
                        pltpu.VMEM((LANES, DEC_SEQ), BF16), pltpu.VMEM((DEC_SEQ, LANES), BF16)],
        compiler_params=_cparams(("parallel", "parallel")),
        name="lat_attn",
    )(qkv, qkv, qkv, ctx_k, ctx_v, qg2, kg2, bias_pairs)


def _bias_pairs(rpb):
    cols = np.arange(GRID_W)
    col_idx = np.clip(cols[None, :] - cols[:, None], -(WIN_COLS - 1), WIN_COLS - 1) + WIN_COLS - 1
    rc = rpb[:, :, col_idx]
    left = jnp.pad(rc, ((0, 0), (1, 0), (0, 0), (0, 0)))
    right = jnp.pad(rc, ((0, 0), (0, 1), (0, 0), (0, 0)))
    return jnp.concatenate([left, right], axis=-1)


def _ffn_kernel(x_ref, g_ref, mod_ref, w1_ref, w3_ref, w2_ref, o_ref):
    mod = mod_ref[0]
    x = x_ref[...]
    h = _modnorm(x, g_ref[...], mod[4:5], mod[3:4]).astype(BF16)
    a = _silu(jnp.dot(h, w1_ref[...], preferred_element_type=F32)) \
        * jnp.dot(h, w3_ref[...], preferred_element_type=F32)
    y = jnp.dot(a.astype(BF16), w2_ref[...], preferred_element_type=F32)
    o_ref[...] = x + mod[5:6] * y


def _ffn_call(x, ln_g, mod, w13, w2, layer):
    resident = dict(pipeline_mode=pl.Buffered(1))
    return pl.pallas_call(
        _ffn_kernel,
        grid=(T // TM,),
        in_specs=[
            pl.BlockSpec((TM, D), lambda i: (i, 0)),
            pl.BlockSpec((1, D), lambda i: (0, 0)),
            pl.BlockSpec((1, 6, D), lambda i: (_cond_of_row(i * TM), 0, 0)),
            pl.BlockSpec((None, D, D_FF), lambda i: (layer, 0, 0), **resident),
            pl.BlockSpec((None, D, D_FF), lambda i: (layer, 0, 1), **resident),
            pl.BlockSpec((None, D_FF, D), lambda i: (layer, 0, 0), **resident),
        ],
        out_specs=pl.BlockSpec((TM, D), lambda i: (i, 0)),
        out_shape=jax.ShapeDtypeStruct((T, D), F32),
        compiler_params=_cparams(("parallel",)),
        name="ffn_dense",
    )(x, ln_g.reshape(1, D), mod, w13, w13, w2)


def _s5_kernel(u_ref, bm_ref, cm_ref, ar_ref, ai_ref, hr_ref, hi_ref,
               y_ref, fr_ref, fi_ref, bu_s, st_s, cr_s, ci_s, *, nb, steps):
    d = pl.program_id(0)
    tb = pl.program_id(3)
    rows = nb * steps
    pitch = steps + S5_ROW_PAD
    nch = S5_LANES // LANES
    u = jnp.concatenate([u_ref[s].reshape(rows, LANES) for s in range(S5_SUB)], axis=1).astype(BF16)
    bu = jnp.dot(u, bm_ref[0, 0], preferred_element_type=F32)
    for c in range(2 * nch):
        for k in range(nb):
            bu_s[c, k * pitch:k * pitch + steps, :] = bu[k * steps:(k + 1) * steps, c * LANES:(c + 1) * LANES]

    @pl.when(tb == 0)
    def _():
        cr_s[...] = hr_ref[0, 0]
        ci_s[...] = hi_ref[0, 0]

    ar = [jnp.broadcast_to(ar_ref[0, 0][:, c * LANES:(c + 1) * LANES], (nb, LANES)) for c in range(nch)]
    ai = [jnp.broadcast_to(ai_ref[0, 0][:, c * LANES:(c + 1) * LANES], (nb, LANES)) for c in range(nch)]

    def step(i, carry):
        l = jnp.where(d == 0, i, steps - 1 - i)
        idx = pl.ds(l, nb, stride=pitch)
        out = []
        for c in range(nch):
            sr, si = carry[2 * c], carry[2 * c + 1]
            nr = ar[c] * sr - ai[c] * si + bu_s[c, idx, :]
            ni = ar[c] * si + ai[c] * sr + bu_s[nch + c, idx, :]
            st_s[c, idx, :] = nr
            st_s[nch + c, idx, :] = ni
            out += [nr, ni]
        return tuple(out)

    init = []
    for c in range(nch):
        init += [cr_s[:, c * LANES:(c + 1) * LANES], ci_s[:, c * LANES:(c + 1) * LANES]]
    fin = lax.fori_loop(0, steps, step, tuple(init), unroll=S5_UNROLL)
    sr = jnp.concatenate([fin[2 * c] for c in range(nch)], axis=1)
    si = jnp.concatenate([fin[2 * c + 1] for c in range(nch)], axis=1)
    cr_s[...] = sr
    ci_s[...] = si
    fr_ref[0, 0] = sr
    fi_ref[0, 0] = si
    states = jnp.concatenate(
        [jnp.concatenate([st_s[c, k * pitch:k * pitch + steps, :] for k in range(nb)], axis=0)
         for c in range(2 * nch)], axis=1).astype(BF16)
    y = jnp.dot(states, cm_ref[0, 0], preferred_element_type=F32)
    for s in range(S5_SUB):
        y_ref[s] = y[:, s * LANES:(s + 1) * LANES].reshape(nb, steps, LANES)


def _s5_call(u, bmat, cmat, a_re, a_im, h_re, h_im, *, row0, n_seq, seq_len, nb, steps, name):
    n_sg, n_tb = n_seq // nb, seq_len // steps
    sg0 = row0 // (nb * seq_len)

    def tbi(d, tb):
        return jnp.where(d == 0, tb, n_tb - 1 - tb)

    par = lambda last: pl.BlockSpec((1, 1) + last, lambda d, j, sg, tb: (d, j, 0, 0))
    st = pl.BlockSpec((1, 1, nb, S5_LANES), lambda d, j, sg, tb: (d, j, sg, 0))
    y, f_re, f_im = pl.pallas_call(
        functools.partial(_s5_kernel, nb=nb, steps=steps),
        grid=(2, S5_NSLAB, n_sg, n_tb),
        in_specs=[
            pl.BlockSpec((S5_SUB, nb, steps, LANES), lambda d, j, sg, tb: (j, sg0 + sg, tbi(d, tb), 0)),
            par((S5_SLAB, 2 * S5_LANES)),
            par((2 * S5_LANES, S5_SLAB)),
            par((1, S5_LANES)), par((1, S5_LANES)),
            st, st,
        ],
        out_specs=[
            pl.BlockSpec((None, S5_SUB, nb, steps, LANES), lambda d, j, sg, tb: (d, j, sg, tbi(d, tb), 0)),
            st, st,
        ],
        out_shape=[
            jax.ShapeDtypeStruct((2, D_SLABS, n_seq, seq_len, LANES), F32),
            jax.ShapeDtypeStruct((2, S5_NSLAB, n_seq, S5_LANES), F32),
            jax.ShapeDtypeStruct((2, S5_NSLAB, n_seq, S5_LANES), F32),
        ],
        scratch_shapes=[pltpu.VMEM((2 * S5_LANES // LANES, nb * (steps + S5_ROW_PAD), LANES), F32)] * 2
                       + [pltpu.VMEM((nb, S5_LANES), F32)] * 2,
        compiler_params=_cparams(("parallel", "parallel", "parallel", "arbitrary")),
        name=name,
    )(u.reshape(D_SLABS, T // seq_len, seq_len, LANES), bmat, cmat, a_re, a_im, h_re, h_im)
    return y.reshape(2, D_SLABS, n_seq * seq_len, LANES), f_re, f_im


def _s5_params(lam_re, lam_im, log_step, b_re, b_im, c_re, c_im):
    step = jnp.exp(log_step)[..., None]
    zr, zi = lam_re * step, lam_im * step
    mag = jnp.exp(zr)
    a_re, a_im = mag * jnp.cos(zi), mag * jnp.sin(zi)
    nr, ni = a_re - 1.0, a_im
    den = lam_re * lam_re + lam_im * lam_im
    k_re = (nr * lam_re + ni * lam_im) / den
    k_im = (ni * lam_re - nr * lam_im) / den
    bb_re = k_re[..., None] * b_re - k_im[..., None] * b_im
    bb_im = k_re[..., None] * b_im + k_im[..., None] * b_re
    gl = S5_SLAB // SSM_GROUP
    eye = jnp.eye(gl, dtype=F32)

    def bdiag_in(w):
        w = w.reshape(2, S5_NSLAB, gl, SSM_STATE, SSM_GROUP)
        return jnp.einsum('dsgnp,gh->dsgphn', w, eye).reshape(2, S5_NSLAB, S5_SLAB, S5_LANES)

    def bdiag_out(w):
        w = w.reshape(2, S5_NSLAB, gl, SSM_GROUP, SSM_STATE)
        return jnp.einsum('dsgpn,gh->dsgnhp', w, eye).reshape(2, S5_NSLAB, S5_LANES, S5_SLAB)

    bmat = jnp.concatenate([bdiag_in(bb_re), bdiag_in(bb_im)], axis=-1).astype(BF16)
    cmat = jnp.concatenate([bdiag_out(c_re), -bdiag_out(c_im)], axis=-2).astype(BF16)
    slab = lambda a: a.reshape(2, S5_NSLAB, 1, S5_LANES)
    return bmat, cmat, slab(a_re), slab(a_im)


def _state_to_slabs(h):
    b = h.shape[0]
    return jnp.transpose(h.reshape(b, 2, S5_NSLAB, S5_LANES), (1, 2, 0, 3))


def _slabs_to_state(f):
    b = f.shape[2]
    return jnp.transpose(f, (2, 0, 1, 3)).reshape(b, 2, SSM_GROUPS, SSM_STATE)


def _glu_res_kernel(u_ref, yl_ref, yc_ref, x_ref, mod_ref, d_ref, w_ref, b_ref, o_ref):
    is_lat = pl.program_id(0) < LAT_TILES
    y = jnp.where(is_lat, _load_slabs(yl_ref.at[0]) + _load_slabs(yl_ref.at[1]),
                  _load_slabs(yc_ref.at[0]) + _load_slabs(yc_ref.at[1]))
    yt = _load_slabs(u_ref) * d_ref[...] + y
    z = _gelu_tanh(yt)
    gate = _sigmoid(jnp.dot(z.astype(BF16), w_ref[...], preferred_element_type=F32) + b_ref[...])
    o_ref[...] = x_ref[...] + mod_ref[0][2:3] * (z * gate)


def _glu_res_call(u, y_lat, y_ctx, x, mod, d_skip, w_glu, b_glu):
    return pl.pallas_call(
        _glu_res_kernel,
        grid=(T // TM,),
        in_specs=[
            pl.BlockSpec((D_SLABS, TM, LANES), lambda i: (0, i, 0)),
            pl.BlockSpec((2, D_SLABS, TM, LANES), lambda i: (0, 0, _lat_tile(i), 0)),
            pl.BlockSpec((2, D_SLABS, TM, LANES), lambda i: (0, 0, _ctx_tile(i), 0)),
            pl.BlockSpec((TM, D), lambda i: (i, 0)),
            pl.BlockSpec((1, 6, D), lambda i: (_cond_of_row(i * TM), 0, 0)),
            pl.BlockSpec((1, D), lambda i: (0, 0)),
            pl.BlockSpec((D, D), lambda i: (0, 0)),
            pl.BlockSpec((1, D), lambda i: (0, 0)),
        ],
        out_specs=pl.BlockSpec((TM, D), lambda i: (i, 0)),
        out_shape=jax.ShapeDtypeStruct((T, D), F32),
        compiler_params=_cparams(("parallel",)),
        name="s5_glu_res",
    )(u, y_lat, y_ctx, x, mod, d_skip.reshape(1, D), w_glu, b_glu.reshape(1, D))


HALO = 8


def _conv_kernel(x_ref, xp_ref, xn_ref, g_ref, mod_ref, win_ref, cw_ref, cb_ref, wout_ref, o_ref):
    i = pl.program_id(0)
    mod = mod_ref[0]
    x = x_ref[...]
    x_ext = jnp.concatenate([xp_ref[...], x, xn_ref[...]], axis=0)
    h = _modnorm(x_ext, g_ref[...], mod[1:2], mod[0:1]).astype(BF16)
    proj = jnp.dot(h, win_ref[...], preferred_element_type=F32)
    z = proj[:, D:2 * D] * proj[:, 2 * D:]
    seq_mask = jnp.where(i < LAT_TILES, DEC_SEQ - 1, SEQ - 1)
    pos = (lax.broadcasted_iota(jnp.int32, (TM, 1), 0) + i * TM) & seq_mask
    zp = jnp.where(pos == 0, 0.0, z[HALO - 1:HALO - 1 + TM])
    zn = jnp.where(pos == seq_mask, 0.0, z[HALO + 1:HALO + 1 + TM])
    cw = cw_ref[...]
    zc = cw[0:1] * zp + cw[1:2] * z[HALO:HALO + TM] + cw[2:3] * zn + cb_ref[...]
    a = (proj[HALO:HALO + TM, :D] * zc).astype(BF16)
    y = jnp.dot(a, wout_ref[...], preferred_element_type=F32)
    o_ref[...] = x + mod[2:3] * y


def _conv_call(x, ln_g, mod, w_in, conv_w, conv_b, w_out):
    rh = TM // HALO
    nh = T // HALO
    resident = dict(pipeline_mode=pl.Buffered(1))
    return pl.pallas_call(
        _conv_kernel,
        grid=(T // TM,),
        in_specs=[
            pl.BlockSpec((TM, D), lambda i: (i, 0)),
            pl.BlockSpec((HALO, D), lambda i: (jnp.maximum(i * rh - 1, 0), 0)),
            pl.BlockSpec((HALO, D), lambda i: (jnp.minimum((i + 1) * rh, nh - 1), 0)),
            pl.BlockSpec((1, D), lambda i: (0, 0)),
            pl.BlockSpec((1, 6, D), lambda i: (_cond_of_row(i * TM), 0, 0)),
            pl.BlockSpec((D, 3 * D), lambda i: (0, 0), **resident),
            pl.BlockSpec((3, D), lambda i: (0, 0)),
            pl.BlockSpec((1, D), lambda i: (0, 0)),
            pl.BlockSpec((D, D), lambda i: (0, 0), **resident),
        ],
        out_specs=pl.BlockSpec((TM, D), lambda i: (i, 0)),
        out_shape=jax.ShapeDtypeStruct((T, D), F32),
        compiler_params=_cparams(("parallel",)),
        name="conv_mix",
    )(x, x, x, ln_g.reshape(1, D), mod, w_in, conv_w, conv_b.reshape(1, D), w_out)


def _router_kernel(x_ref, g_ref, mod_ref, r_ref, h_ref, rt_ref):
    mod = mod_ref[0]
    h = _modnorm(x_ref[...], g_ref[...], mod[4:5], mod[3:4])
    h_ref[...] = h
    logits = jnp.dot(h, r_ref[...], preferred_element_type=F32, precision=lax.Precision.HIGHEST)
    lane = lax.broadcasted_iota(jnp.int32, logits.shape, 1)
    logits = jnp.where(lane < N_EXPERTS, logits, -jnp.inf)
    m1 = jnp.max(logits, axis=-1, keepdims=True)
    i1 = jnp.min(jnp.where(logits == m1, lane, LANES), axis=-1, keepdims=True)
    rest = jnp.where(lane == i1, -jnp.inf, logits)
    m2 = jnp.max(rest, axis=-1, keepdims=True)
    i2 = jnp.min(jnp.where(rest == m2, lane, LANES), axis=-1, keepdims=True)
    e2 = jnp.exp(m2 - m1)
    w1 = 1.0 / (1.0 + e2)
    w2 = e2 / (1.0 + e2)
    rt_ref[...] = jnp.where(lane == 0, i1.astype(F32),
                            jnp.where(lane == 1, i2.astype(F32),
                                      jnp.where(lane == 2, w1, jnp.where(lane == 3, w2, 0.0))))


def _router_call(x, ln_g, mod, router_pad):
    return pl.pallas_call(
        _router_kernel,
        grid=(T // TM,),
        in_specs=[
            pl.BlockSpec((TM, D), lambda i: (i, 0)),
            pl.BlockSpec((1, D), lambda i: (0, 0)),
            pl.BlockSpec((1, 6, D), lambda i: (_cond_of_row(i * TM), 0, 0)),
            pl.BlockSpec((D, LANES), lambda i: (0, 0)),
        ],
        out_specs=[pl.BlockSpec((TM, D), lambda i: (i, 0)),
                   pl.BlockSpec((TM, LANES), lambda i: (i, 0))],
        out_shape=[jax.ShapeDtypeStruct((T, D), F32), jax.ShapeDtypeStruct((T, LANES), F32)],
        compiler_params=_cparams(("parallel",)),
        name="moe_router",
    )(x, ln_g.reshape(1, D), mod, router_pad)


R_MAX = 2 * T + N_EXPERTS * TM_MOE
N_TILES = R_MAX // TM_MOE
DMA_UNROLL = 8
GATHER_SHIFT = 3
GATHER_SPREAD = 1 << GATHER_SHIFT


def _row_copy(src_hbm, src_row, dst_vmem, dst_row, sem):
    return pltpu.make_async_copy(src_hbm.at[pl.ds(src_row, 1), :], dst_vmem.at[pl.ds(dst_row, 1), :], sem)


def _gather_kernel(nv_ref, src_ref, h_hbm, o_ref, buf, sem):
    i = pl.program_id(0)
    nv = nv_ref[0]

    def issue_tile(t):
        slot = t % 2
        base = t * TM_MOE

        def issue(g, c):
            for k in range(GATHER_SPREAD):
                r = k * (TM_MOE // GATHER_SPREAD) + g
                _row_copy(h_hbm, src_ref[base + r], buf.at[slot], r, sem.at[slot]).start(priority=k % 2)
            return c

        lax.fori_loop(0, TM_MOE // GATHER_SPREAD, issue, 0)

    @pl.when(jnp.logical_and(i == 0, nv > 0))
    def _():
        issue_tile(0)

    @pl.when(i + 1 < nv)
    def _():
        issue_tile(i + 1)

    @pl.when(i < nv)
    def _():
        slot = i % 2
        pltpu.make_async_copy(h_hbm.at[pl.ds(0, TM_MOE), :], buf.at[slot], sem.at[slot]).wait()
        o_ref[...] = buf[slot].astype(BF16)

    @pl.when(i >= nv)
    def _():
        o_ref[...] = jnp.zeros_like(o_ref)


def _gather_call(n_valid, src_tok, h):
    return pl.pallas_call(
        _gather_kernel,
        grid_spec=pltpu.PrefetchScalarGridSpec(
            num_scalar_prefetch=2,
            grid=(N_TILES,),
            in_specs=[pl.BlockSpec(memory_space=pl.ANY)],
            out_specs=pl.BlockSpec((TM_MOE, D), lambda i, nv, src: (i, 0)),
            scratch_shapes=[pltpu.VMEM((2, TM_MOE, D), F32), pltpu.SemaphoreType.DMA((2,))],
        ),
        out_shape=jax.ShapeDtypeStruct((R_MAX, D), BF16),
        compiler_params=_cparams(("arbitrary",)),
        name="moe_gather",
    )(n_valid, src_tok, h)


def _moe_kernel(nv_ref, te_ref, tr_ref, xs_ref, w1_ref, w3_ref, w2_ref, o_ref, acc_s):
    i = pl.program_id(0)
    f = pl.program_id(1)
    nf = pl.num_programs(1)
    valid = i < nv_ref[0]
    quarter = TM_MOE // MOE_ROW_SPLITS
    n_quarters = (tr_ref[i] + quarter - 1) // quarter

    @pl.when(jnp.logical_and(valid, f == 0))
    def _():
        acc_s[...] = jnp.zeros_like(acc_s)

    def swiglu_rows(rows):
        xs = xs_ref[:rows, :]
        w13 = jnp.concatenate([w1_ref[...].astype(BF16), w3_ref[...].astype(BF16)], axis=1)
        gu = jnp.dot(xs, w13, preferred_element_type=F32)
        a = _silu(gu[:, :TF_MOE]) * gu[:, TF_MOE:]
        acc_s[:rows, :] += jnp.dot(a.astype(BF16), w2_ref[...].astype(BF16), preferred_element_type=F32)

    for q in range(1, MOE_ROW_SPLITS + 1):
        @pl.when(jnp.logical_and(valid, n_quarters == q))
        def _(q=q):
            swiglu_rows(q * quarter)

    @pl.when(f == nf - 1)
    def _():
        o_ref[...] = jnp.where(valid, acc_s[...], 0.0)


def _moe_call(n_valid, tile_expert, tile_rows, xs, w13, w2, layer):
    nf = D_FF_EXPERT // TF_MOE

    def fe(i, f, nv):
        return jnp.where(i < nv[0], f, nf - 1)

    return pl.pallas_call(
        _moe_kernel,
        grid_spec=pltpu.PrefetchScalarGridSpec(
            num_scalar_prefetch=3,
            grid=(N_TILES, nf),
            in_specs=[
                pl.BlockSpec((TM_MOE, D), lambda i, f, nv, te, tr: (i, 0)),
                pl.BlockSpec((None, None, D, TF_MOE),
                             lambda i, f, nv, te, tr: (layer, te[i], 0, fe(i, f, nv))),
                pl.BlockSpec((None, None, D, TF_MOE),
                             lambda i, f, nv, te, tr: (layer, te[i], 0, nf + fe(i, f, nv))),
                pl.BlockSpec((None, None, TF_MOE, D),
                             lambda i, f, nv, te, tr: (layer, te[i], fe(i, f, nv), 0)),
            ],
            out_specs=pl.BlockSpec((TM_MOE, D), lambda i, f, nv, te, tr: (i, 0)),
            scratch_shapes=[pltpu.VMEM((TM_MOE, D), F32)],
        ),
        out_shape=jax.ShapeDtypeStruct((R_MAX, D), F32),
        compiler_params=_cparams(("arbitrary", "arbitrary")),
        name="moe_experts",
    )(n_valid, tile_expert, tile_rows, xs, w13, w13, w2)


TM_COMB = 512


def _combine_kernel(pa_ref, pb_ref, y_hbm, x_ref, rt_ref, mod_ref, o_ref, buf_a, buf_b, sem):
    i = pl.program_id(0)

    def issue_tile(t):
        slot = t % 2
        base = t * TM_COMB

        def issue(r, c):
            _row_copy(y_hbm, pa_ref[base + r], buf_a.at[slot], r, sem.at[slot]).start(priority=0)
            _row_copy(y_hbm, pb_ref[base + r], buf_b.at[slot], r, sem.at[slot]).start(priority=1)
            return c

        lax.fori_loop(0, TM_COMB, issue, 0, unroll=DMA_UNROLL)

    @pl.when(i == 0)
    def _():
        issue_tile(0)

    @pl.when(i + 1 < pl.num_programs(0))
    def _():
        issue_tile(i + 1)

    slot = i % 2
    for buf in (buf_a, buf_b):
        pltpu.make_async_copy(y_hbm.at[pl.ds(0, TM_COMB), :], buf.at[slot], sem.at[slot]).wait()
    rt = rt_ref[...]
    ffn = rt[:, 2:3] * buf_a[slot] + rt[:, 3:4] * buf_b[slot]
    o_ref[...] = x_ref[...] + mod_ref[0][5:6] * ffn


def _combine_call(pos_a, pos_b, y, x, rt, mod):
    return pl.pallas_call(
        _combine_kernel,
        grid_spec=pltpu.PrefetchScalarGridSpec(
            num_scalar_prefetch=2,
            grid=(T // TM_COMB,),
            in_specs=[
                pl.BlockSpec(memory_space=pl.ANY),
                pl.BlockSpec((TM_COMB, D), lambda i, pa, pb: (i, 0)),
                pl.BlockSpec((TM_COMB, LANES), lambda i, pa, pb: (i, 0)),
                pl.BlockSpec((1, 6, D), lambda i, pa, pb: (_cond_of_row(i * TM_COMB), 0, 0)),
            ],
            out_specs=pl.BlockSpec((TM_COMB, D), lambda i, pa, pb: (i, 0)),
            scratch_shapes=[pltpu.VMEM((2, TM_COMB, D), F32), pltpu.VMEM((2, TM_COMB, D), F32),
                            pltpu.SemaphoreType.DMA((2,))],
        ),
        out_shape=jax.ShapeDtypeStruct((T, D), F32),
        compiler_params=_cparams(("arbitrary",)),
        name="moe_combine",
    )(pos_a, pos_b, y, x, rt, mod)


def _routing_tables(rt):
    ea = rt[:, 0:2].astype(jnp.int32).reshape(-1)
    onehot = (ea[:, None] == jnp.arange(N_EXPERTS, dtype=jnp.int32)[None, :]).astype(jnp.int32)
    csum = jnp.cumsum(onehot, axis=0)
    rank = jnp.take_along_axis(csum, ea[:, None], axis=1)[:, 0] - 1
    counts = csum[-1]
    padded = ((counts + TM_MOE - 1) // TM_MOE) * TM_MOE
    ends = jnp.cumsum(padded)
    pos = (ends - padded)[ea] + rank
    n_valid = (ends[-1] // TM_MOE).astype(jnp.int32).reshape(1)
    tile_start = jnp.arange(N_TILES, dtype=jnp.int32) * TM_MOE
    expert_of_row = lambda r: jnp.sum((r[:, None] >= ends[None, :]).astype(jnp.int32), axis=1)
    tile_expert = jnp.minimum(expert_of_row(tile_start), expert_of_row(ends[-1:] - 1))
    real_end = (ends - padded + counts)[tile_expert]
    tile_rows = jnp.clip(real_end - tile_start, 0, TM_MOE).astype(jnp.int32)
    tok = jnp.arange(2 * T, dtype=jnp.int32) // 2
    src_tok = (jnp.arange(R_MAX, dtype=jnp.int32) % T).at[pos].set(tok)
    pos2 = pos.reshape(T, 2).astype(jnp.int32)
    return n_valid, tile_expert, tile_rows, src_tok, pos2[:, 0], pos2[:, 1]


def _moe_layer(x, ln_g, mod, router, w13, w2, layer):
    router_pad = jnp.pad(router, ((0, 0), (0, LANES - N_EXPERTS)))
    h, rt = _router_call(x, ln_g, mod, router_pad)
    n_valid, tile_expert, tile_rows, src_tok, pos_a, pos_b = _routing_tables(rt)
    xs = _gather_call(n_valid, src_tok, h)
    y = _moe_call(n_valid, tile_expert, tile_rows, xs, w13, w2, layer)
    return _combine_call(pos_a, pos_b, y, x, rt, mod)


def kernel(x_prompt, x_sample, c, cache_na_k, cache_na_v, state_ssm_re, state_ssm_im, c_ctx,
           ln1_g, ln2_g, ada_w, ada_b, na_w_qkv, na_w_o, na_q_g, na_k_g, na_rpb,
           ssm_lam_re, ssm_lam_im, ssm_log_step, ssm_b_re, ssm_b_im, ssm_c_re, ssm_c_im,
           ssm_d, ssm_w_glu, ssm_b_glu, cv_w_in, cv_conv_w, cv_conv_b, cv_w_out,
           ffn_w13, ffn_w2, moe_router, moe_w13, moe_w2):
    depth = ada_w.shape[0]
    x = (x_sample.reshape(T_LAT, D), x_prompt.reshape(T_CTX, D))
    conds = jnp.concatenate([c, c_ctx[None, :], jnp.zeros((N_COND - DEC_BATCH - 1, D), F32)], axis=0)
    mods = _ada_call(conds, ada_w, ada_b).reshape(depth, N_COND, 6, D)
    ffn_w13_b, ffn_w2_b = ffn_w13.astype(BF16), ffn_w2.astype(BF16)

    new_k = new_v = None
    s_re, s_im = [], []
    for l in range(depth):
        kind, j = l % 3, l // 3
        mod = mods[l]
        if kind == 0:
            qkv = _normproj_call(x, ln1_g[l], mod, na_w_qkv[j].astype(BF16), shift=0, scale=1,
                                 name="qkv_proj", slab_out=True)
            qg2 = jnp.tile(na_q_g[j], 2).reshape(1, LANES)
            kg2 = jnp.tile(na_k_g[j], 2).reshape(1, LANES)
            past = cache_na_k.shape[3]
            pair = lambda a: jnp.transpose(a[:, j].reshape(DEC_BATCH, HP, 2, past, HEAD_DIM),
                                           (0, 1, 3, 2, 4)).reshape(DEC_BATCH, HP, past, LANES)
            ctx_k, ctx_v = pair(cache_na_k), pair(cache_na_v)
            o_lat = _lat_attn_call(qkv, ctx_k, ctx_v, qg2, kg2, _bias_pairs(na_rpb[j]))
            o_ctx, new_k, new_v = _ctx_attn_call(qkv, qg2, kg2, new_k, new_v)
            x = _proj_res_call(o_lat, o_ctx, x, mod, na_w_o[j].astype(BF16), gate=2, name="attn_out")
        elif kind == 1:
            u = _norm_call(x, ln1_g[l], mod, shift=0, scale=1, name="s5_norm")
            bmat, cmat, a_re, a_im = _s5_params(ssm_lam_re[j], ssm_lam_im[j], ssm_log_step[j],
                                                ssm_b_re[j], ssm_b_im[j], ssm_c_re[j], ssm_c_im[j])
            y_lat, _, _ = _s5_call(u, bmat, cmat, a_re, a_im,
                                   _state_to_slabs(state_ssm_re[:, j]), _state_to_slabs(state_ssm_im[:, j]),
                                   row0=0, n_seq=DEC_BATCH, seq_len=DEC_SEQ,
                                   nb=DEC_BATCH, steps=S5_T_LAT, name="s5_lat")
            zero = jnp.zeros((2, S5_NSLAB, BATCH, S5_LANES), F32)
            y_ctx, f_re, f_im = _s5_call(u, bmat, cmat, a_re, a_im, zero, zero,
                                         row0=T_LAT, n_seq=BATCH, seq_len=SEQ, nb=8, steps=S5_T_CTX,
                                         name="s5_ctx")
            s_re.append(_slabs_to_state(f_re))
            s_im.append(_slabs_to_state(f_im))
            x = _glu_res_call(u, y_lat, y_ctx, x, mod, ssm_d[j], ssm_w_glu[j].astype(BF16), ssm_b_glu[j])
        else:
            x = _conv_call(x, ln1_g[l], mod, cv_w_in[j].astype(BF16), cv_conv_w[j], cv_conv_b[j],
                           cv_w_out[j].astype(BF16))
        jj = l // 2
        if l % 2 == 0:
            x = _ffn_call(x, ln2_g[l], mod, ffn_w13_b, ffn_w2_b, jj)
        else:
            x = _moe_layer(x, ln2_g[l], mod, moe_router[jj], moe_w13, moe_w2, jj)

    y_sample = x[:T_LAT].reshape(DEC_BATCH, DEC_SEQ, D)
    y_prompt = x[T_LAT:].reshape(BATCH, SEQ, D)
    return (y_prompt, y_sample, new_k, new_v, jnp.stack(s_re, axis=1), jnp.stack(s_im, axis=1))
```

```python
import functools
import math

import numpy as np
import jax
import jax.numpy as jnp
from jax import lax
from jax.experimental import pallas as pl
from jax.experimental.pallas import tpu as pltpu

F32 = jnp.float32
BF16 = jnp.bfloat16

D = 1024
BATCH, SEQ = 16, 256
DEC_BATCH, DEC_SEQ = 4, 2048
GRID_W = 64
ROWS = DEC_SEQ // GRID_W
N_HEADS, HEAD_DIM = 16, 64
WIN_ROWS, WIN_COLS = 8, 16
SSM_GROUP, SSM_GROUPS, SSM_STATE = 16, 64, 64
D_FF = 2816
N_EXPERTS = 8
D_FF_EXPERT = 3584
EPS = 1e-6
NEG_INF = -1e30
SCALE = HEAD_DIM ** -0.5

T_LAT = DEC_BATCH * DEC_SEQ
T_CTX = BATCH * SEQ
T = T_LAT + T_CTX
N_COND = 8
CTX_COND = DEC_BATCH

LANES = 128
VMEM_LIMIT = 56 * 1024 * 1024

TM = 512
TM_MOE = 512
TF_MOE = 512
MOE_ROW_SPLITS = 4
HP = N_HEADS // 2
QB_ROWS = 4
D_SLABS = D // LANES
S5_SLAB = 128
S5_SUB = S5_SLAB // LANES
S5_NSLAB = D // S5_SLAB
S5_LANES = (S5_SLAB // SSM_GROUP) * SSM_STATE
S5_T_LAT = 512
S5_T_CTX = SEQ
S5_ROW_PAD = 4
S5_UNROLL = 8


def _cparams(sem):
    return pltpu.CompilerParams(dimension_semantics=sem, vmem_limit_bytes=VMEM_LIMIT)


def _cond_of_row(row):
    return jnp.minimum(row // DEC_SEQ, CTX_COND)


def _modnorm(x, g, sc, sh):
    ms = jnp.mean(x * x, axis=-1, keepdims=True)
    y = x * lax.rsqrt(ms + EPS) * g
    return y * (1.0 + sc) + sh


def _sigmoid(x):
    return 1.0 / (1.0 + jnp.exp(-x))


def _silu(x):
    return x * _sigmoid(x)


def _gelu_tanh(x):
    c = math.sqrt(2.0 / math.pi)
    return 0.5 * x * (1.0 + jnp.tanh(c * (x + 0.044715 * (x * x * x))))


def _ada_kernel(c_ref, w_ref, b_ref, o_ref):
    s = _silu(c_ref[...]).astype(BF16)
    o_ref[0] = jnp.dot(s, w_ref[0].astype(BF16), preferred_element_type=F32) + b_ref[0]


def _ada_call(conds, ada_w, ada_b):
    depth = ada_w.shape[0]
    tn = 1536
    return pl.pallas_call(
        _ada_kernel,
        grid=(depth, 6 * D // tn),
        in_specs=[
            pl.BlockSpec((N_COND, D), lambda l, n: (0, 0)),
            pl.BlockSpec((1, D, tn), lambda l, n: (l, 0, n)),
            pl.BlockSpec((1, 1, tn), lambda l, n: (l, 0, n)),
        ],
        out_specs=pl.BlockSpec((1, N_COND, tn), lambda l, n: (l, 0, n)),
        out_shape=jax.ShapeDtypeStruct((depth, N_COND, 6 * D), F32),
        compiler_params=_cparams(("parallel", "parallel")),
        name="ada_mod",
    )(conds, ada_w, ada_b.reshape(depth, 1, 6 * D))


LAT_TILES = T_LAT // TM


def _lat_tile(i):
    return jnp.minimum(i, LAT_TILES - 1)


def _ctx_tile(i):
    return jnp.maximum(i - LAT_TILES, 0)


def _rows_specs(x):
    if isinstance(x, tuple):
        return [pl.BlockSpec((TM, D), lambda i: (_lat_tile(i), 0)),
                pl.BlockSpec((TM, D), lambda i: (_ctx_tile(i), 0))], list(x)
    return [pl.BlockSpec((TM, D), lambda i: (i, 0))], [x]


def _rows_value(refs):
    if len(refs) == 2:
        return jnp.where(pl.program_id(0) < LAT_TILES, refs[0][...], refs[1][...])
    return refs[0][...]


def _store_slabs(o_ref, val):
    for c in range(val.shape[1] // LANES):
        o_ref[c] = val[:, c * LANES:(c + 1) * LANES]


def _load_slabs(ref):
    return jnp.concatenate([ref[c] for c in range(ref.shape[0])], axis=1)


def _normproj_kernel(*refs, shift, scale, slab_out):
    *x_refs, g_ref, mod_ref, w_ref, o_ref = refs
    mod = mod_ref[0]
    h = _modnorm(_rows_value(x_refs), g_ref[...], mod[scale:scale + 1], mod[shift:shift + 1])
    res = jnp.dot(h.astype(BF16), w_ref[...], preferred_element_type=F32)
    if slab_out:
        _store_slabs(o_ref, res)
    else:
        o_ref[...] = res


def _normproj_call(x, ln_g, mod, w, *, shift, scale, name, slab_out=False):
    n = w.shape[1]
    x_specs, x_args = _rows_specs(x)
    if slab_out:
        out_spec = pl.BlockSpec((n // LANES, TM, LANES), lambda i: (0, i, 0))
        out_shape = jax.ShapeDtypeStruct((n // LANES, T, LANES), F32)
    else:
        out_spec = pl.BlockSpec((TM, n), lambda i: (i, 0))
        out_shape = jax.ShapeDtypeStruct((T, n), F32)
    return pl.pallas_call(
        functools.partial(_normproj_kernel, shift=shift, scale=scale, slab_out=slab_out),
        grid=(T // TM,),
        in_specs=x_specs + [
            pl.BlockSpec((1, D), lambda i: (0, 0)),
            pl.BlockSpec((1, 6, D), lambda i: (_cond_of_row(i * TM), 0, 0)),
            pl.BlockSpec((D, n), lambda i: (0, 0)),
        ],
        out_specs=out_spec,
        out_shape=out_shape,
        compiler_params=_cparams(("parallel",)),
        name=name,
    )(*x_args, ln_g.reshape(1, D), mod, w)


def _norm_kernel(x_ref, g_ref, mod_ref, o_ref, *, shift, scale):
    mod = mod_ref[0]
    _store_slabs(o_ref, _modnorm(x_ref[...], g_ref[...], mod[scale:scale + 1], mod[shift:shift + 1]))


def _norm_call(x, ln_g, mod, *, shift, scale, name):
    return pl.pallas_call(
        functools.partial(_norm_kernel, shift=shift, scale=scale),
        grid=(T // TM,),
        in_specs=[
            pl.BlockSpec((TM, D), lambda i: (i, 0)),
            pl.BlockSpec((1, D), lambda i: (0, 0)),
            pl.BlockSpec((1, 6, D), lambda i: (_cond_of_row(i * TM), 0, 0)),
        ],
        out_specs=pl.BlockSpec((D // LANES, TM, LANES), lambda i: (0, i, 0)),
        out_shape=jax.ShapeDtypeStruct((D // LANES, T, LANES), F32),
        compiler_params=_cparams(("parallel",)),
        name=name,
    )(x, ln_g.reshape(1, D), mod)


def _proj_res_kernel(*refs, gate, n_x):
    al_ref, ac_ref = refs[:2]
    x_refs = refs[2:2 + n_x]
    mod_ref, w_ref, o_ref = refs[2 + n_x:]
    a = jnp.where(pl.program_id(0) < LAT_TILES, _load_slabs(al_ref), _load_slabs(ac_ref))
    y = jnp.dot(a.astype(BF16), w_ref[...], preferred_element_type=F32)
    o_ref[...] = _rows_value(x_refs) + mod_ref[0][gate:gate + 1] * y


def _proj_res_call(a_lat, a_ctx, x, mod, w, *, gate, name):
    ns = D // LANES
    a_specs = [pl.BlockSpec((ns, TM, LANES), lambda i: (0, _lat_tile(i), 0)),
               pl.BlockSpec((ns, TM, LANES), lambda i: (0, _ctx_tile(i), 0))]
    a_args = [a_lat, a_ctx]
    x_specs, x_args = _rows_specs(x)
    return pl.pallas_call(
        functools.partial(_proj_res_kernel, gate=gate, n_x=len(x_args)),
        grid=(T // TM,),
        in_specs=a_specs + x_specs + [
            pl.BlockSpec((1, 6, D), lambda i: (_cond_of_row(i * TM), 0, 0)),
            pl.BlockSpec((D, D), lambda i: (0, 0)),
        ],
        out_specs=pl.BlockSpec((TM, D), lambda i: (i, 0)),
        out_shape=jax.ShapeDtypeStruct((T, D), F32),
        compiler_params=_cparams(("parallel",)),
        name=name,
    )(*a_args, *x_args, mod, w)


def _head_norm(x, g, head0):
    x2 = x * x
    s0 = jnp.sum(jnp.where(head0, x2, 0.0), axis=-1, keepdims=True)
    s1 = jnp.sum(jnp.where(head0, 0.0, x2), axis=-1, keepdims=True)
    ms = jnp.where(head0, s0, s1) * (1.0 / HEAD_DIM)
    return x * lax.rsqrt(ms + EPS) * g


def _nt_dot(a, b):
    return lax.dot_general(a, b, (((1,), (1,)), ((), ())), preferred_element_type=F32)


def _ctx_attn_kernel(*refs, n_prev):
    if n_prev:
        q_ref, k_ref, v_ref, qg_ref, kg_ref, pk_ref, pv_ref, o_ref, ko_ref, vo_ref = refs
        ko_ref[0, :n_prev] = pk_ref[0]
        vo_ref[0, :n_prev] = pv_ref[0]
    else:
        q_ref, k_ref, v_ref, qg_ref, kg_ref, o_ref, ko_ref, vo_ref = refs
    head0 = lax.broadcasted_iota(jnp.int32, (SEQ, LANES), 1) < HEAD_DIM
    for hp in range(HP):
        qn = _head_norm(q_ref[hp], qg_ref[...], head0) * SCALE
        kn = _head_norm(k_ref[hp], kg_ref[...], head0)
        v = v_ref[hp]
        ko_ref[0, n_prev, 2 * hp] = kn[:, :HEAD_DIM]
        ko_ref[0, n_prev, 2 * hp + 1] = kn[:, HEAD_DIM:]
        vo_ref[0, n_prev, 2 * hp] = v[:, :HEAD_DIM]
        vo_ref[0, n_prev, 2 * hp + 1] = v[:, HEAD_DIM:]
        knb = kn.astype(BF16)
        vb = v.astype(BF16)
        outs = []
        for hh in range(2):
            qm = jnp.where(head0 if hh == 0 else jnp.logical_not(head0), qn, 0.0).astype(BF16)
            s = _nt_dot(qm, knb)
            p = jnp.exp(s - jnp.max(s, axis=-1, keepdims=True))
            l = jnp.sum(p, axis=-1, keepdims=True)
            outs.append(jnp.dot(p.astype(BF16), vb, preferred_element_type=F32) / l)
        o_ref[hp] = jnp.where(head0, outs[0], outs[1])


def _ctx_attn_call(qkv, qg2, kg2, prev_k, prev_v):
    row0 = T_LAT // SEQ
    n_prev = 0 if prev_k is None else prev_k.shape[1]
    blk = lambda c: pl.BlockSpec((HP, SEQ, LANES), lambda b, c=c: (c, row0 + b, 0))
    kv_blk = lambda n: pl.BlockSpec((1, n, N_HEADS, SEQ, HEAD_DIM), lambda b: (b, 0, 0, 0, 0))
    kv_shape = jax.ShapeDtypeStruct((BATCH, n_prev + 1, N_HEADS, SEQ, HEAD_DIM), F32)
    prev_specs = [kv_blk(n_prev)] * 2 if n_prev else []
    prev_args = [prev_k, prev_v] if n_prev else []
    return pl.pallas_call(
        functools.partial(_ctx_attn_kernel, n_prev=n_prev),
        grid=(BATCH,),
        in_specs=[blk(0), blk(1), blk(2),
                  pl.BlockSpec((1, LANES), lambda b: (0, 0)),
                  pl.BlockSpec((1, LANES), lambda b: (0, 0))] + prev_specs,
        out_specs=[pl.BlockSpec((HP, SEQ, LANES), lambda b: (0, b, 0)), kv_blk(n_prev + 1), kv_blk(n_prev + 1)],
        out_shape=[jax.ShapeDtypeStruct((HP, T_CTX, LANES), F32), kv_shape, kv_shape],
        compiler_params=_cparams(("parallel",)),
        name="ctx_attn",
    )(qkv, qkv, qkv, qg2, kg2, *prev_args)


def _band_of_block(p):
    r_lo, r_hi = QB_ROWS * p, QB_ROWS * p + QB_ROWS - 1
    kr = min(WIN_ROWS, ROWS)
    lo = min(max(r_lo - kr // 2, 0), ROWS - kr)
    hi = min(max(r_hi - kr // 2, 0), ROWS - kr) + kr
    lo -= lo % 2
    hi += hi % 2
    return lo, hi - lo


def _lat_attn_kernel(q_ref, k_ref, v_ref, ck_ref, cv_ref, qg_ref, kg_ref, bp_ref, o_ref,
                     qm0_s, qm1_s, kt_s, v_s):
    head0 = lax.broadcasted_iota(jnp.int32, (DEC_SEQ, LANES), 1) < HEAD_DIM
    qn = _head_norm(q_ref[...], qg_ref[...], head0)
    qm0_s[...] = jnp.where(head0, qn * SCALE, 0.0).astype(BF16)
    qm1_s[...] = jnp.where(head0, 0.0, qn * SCALE).astype(BF16)
    kt_s[...] = _head_norm(k_ref[...], kg_ref[...], head0).T.astype(BF16)
    v_s[...] = v_ref[...].astype(BF16)
    ckt = ck_ref[...].T.astype(BF16)
    cvb = cv_ref[...].astype(BF16)

    qc = lax.broadcasted_iota(jnp.int32, (GRID_W, LANES), 0)
    kl = lax.broadcasted_iota(jnp.int32, (GRID_W, LANES), 1)
    kc = jnp.where(kl < GRID_W, kl, kl - GRID_W)
    cs = jnp.clip(qc - WIN_COLS // 2, 0, GRID_W - WIN_COLS)
    col_ok = jnp.logical_and(kc >= cs, kc < cs + WIN_COLS)
    left = kl < GRID_W
    mask_of = {
        (True, True): col_ok,
        (True, False): jnp.logical_and(col_ok, left),
        (False, True): jnp.logical_and(col_ok, jnp.logical_not(left)),
    }
    h0q = lax.broadcasted_iota(jnp.int32, (QB_ROWS * GRID_W, LANES), 1) < HEAD_DIM
    kr_win = min(WIN_ROWS, ROWS)

    for p in range(ROWS // QB_ROWS):
        u0, nrows = _band_of_block(p)
        q_lo = p * QB_ROWS * GRID_W
        ktb = kt_s[:, u0 * GRID_W:(u0 + nrows) * GRID_W]
        vb = v_s[u0 * GRID_W:(u0 + nrows) * GRID_W, :]
        nq = QB_ROWS * GRID_W
        qm = jnp.concatenate([qm0_s[q_lo:q_lo + nq, :], qm1_s[q_lo:q_lo + nq, :]], axis=0)
        n_loc = nrows * GRID_W
        s_all = jnp.dot(qm, jnp.concatenate([ktb, ckt], axis=1), preferred_element_type=F32)
        s_loc, s_ctx = s_all[:, :n_loc], s_all[:, n_loc:]
        p_rows, l_rows = [], []
        for hh in range(2):
            for i in range(QB_ROWS):
                r = p * QB_ROWS + i
                r0 = min(max(r - kr_win // 2, 0), ROWS - kr_win)
                rows = slice(hh * nq + i * GRID_W, hh * nq + (i + 1) * GRID_W)
                sbs = {}
                for m in range(nrows // 2):
                    kr = u0 + 2 * m
                    ok_l = r0 <= kr < r0 + kr_win
                    ok_r = r0 <= kr + 1 < r0 + kr_win
                    if ok_l or ok_r:
                        sb = s_loc[rows, m * LANES:(m + 1) * LANES] + bp_ref[hh, kr - r + WIN_ROWS]
                        sbs[m] = jnp.where(mask_of[(ok_l, ok_r)], sb, NEG_INF)
                sc = s_ctx[rows, :]
                mx = jnp.maximum(
                    jnp.max(functools.reduce(jnp.maximum, sbs.values()), axis=-1, keepdims=True),
                    jnp.max(sc, axis=-1, keepdims=True))
                pbs = {m: jnp.exp(sb - mx) for m, sb in sbs.items()}
                pc = jnp.exp(sc - mx)
                l_rows.append(jnp.sum(functools.reduce(jnp.add, pbs.values()), axis=-1, keepdims=True)
                              + jnp.sum(pc, axis=-1, keepdims=True))
                zero = jnp.zeros((GRID_W, LANES), BF16)
                p_rows.append(jnp.concatenate(
                    [pbs[m].astype(BF16) if m in pbs else zero for m in range(nrows // 2)]
                    + [pc.astype(BF16)], axis=1))
        o = jnp.dot(jnp.concatenate(p_rows, axis=0), jnp.concatenate([vb, cvb], axis=0),
                    preferred_element_type=F32)
        o = o / jnp.concatenate(l_rows, axis=0)
        o_ref[q_lo:q_lo + nq, :] = jnp.where(h0q, o[:nq], o[nq:])


def _lat_attn_call(qkv, ctx_k, ctx_v, qg2, kg2, bias_pairs):
    blk = lambda c: pl.BlockSpec((None, DEC_SEQ, LANES), lambda hp, b, c=c: (c * HP + hp, b, 0))
    cblk = pl.BlockSpec((None, None, ctx_k.shape[2], LANES), lambda hp, b: (b, hp, 0, 0))
    return pl.pallas_call(
        _lat_attn_kernel,
        grid=(HP, DEC_BATCH),
        in_specs=[blk(0), blk(1), blk(2), cblk, cblk,
                  pl.BlockSpec((1, LANES), lambda hp, b: (0, 0)),
                  pl.BlockSpec((1, LANES), lambda hp, b: (0, 0)),
                  pl.BlockSpec((2, 2 * WIN_ROWS, GRID_W, LANES), lambda hp, b: (hp, 0, 0, 0))],
        out_specs=pl.BlockSpec((None, DEC_SEQ, LANES), lambda hp, b: (hp, b, 0)),
        out_shape=jax.ShapeDtypeStruct((HP, T_LAT, LANES), F32),
        scratch_shapes=[pltpu.VMEM((DEC_SEQ, LANES), BF16), pltpu.VMEM((DEC_SEQ, LANES), BF16),
                        pltpu.VMEM((LANES, DEC_SEQ), BF16), pltpu.VMEM((DEC_SEQ, LANES), BF16)],
        compiler_params=_cparams(("parallel", "parallel")),
        name="lat_attn",
    )(qkv, qkv, qkv, ctx_k, ctx_v, qg2, kg2, bias_pairs)


def _bias_pairs(rpb):
    cols = np.arange(GRID_W)
    col_idx = np.clip(cols[None, :] - cols[:, None], -(WIN_COLS - 1), WIN_COLS - 1) + WIN_COLS - 1
    rc = rpb[:, :, col_idx]
    left = jnp.pad(rc, ((0, 0), (1, 0), (0, 0), (0, 0)))
    right = jnp.pad(rc, ((0, 0), (0, 1), (0, 0), (0, 0)))
    return jnp.concatenate([left, right], axis=-1)


def _ffn_kernel(x_ref, g_ref, mod_ref, w1_ref, w3_ref, w2_ref, o_ref):
    mod = mod_ref[0]
    x = x_ref[...]
    h = _modnorm(x, g_ref[...], mod[4:5], mod[3:4]).astype(BF16)
    a = _silu(jnp.dot(h, w1_ref[...], preferred_element_type=F32)) \
        * jnp.dot(h, w3_ref[...], preferred_element_type=F32)
    y = jnp.dot(a.astype(BF16), w2_ref[...], preferred_element_type=F32)
    o_ref[...] = x + mod[5:6] * y


def _ffn_call(x, ln_g, mod, w13, w2, layer):
    resident = dict(pipeline_mode=pl.Buffered(1))
    return pl.pallas_call(
        _ffn_kernel,
        grid=(T // TM,),
        in_specs=[
            pl.BlockSpec((TM, D), lambda i: (i, 0)),
            pl.BlockSpec((1, D), lambda i: (0, 0)),
            pl.BlockSpec((1, 6, D), lambda i: (_cond_of_row(i * TM), 0, 0)),
            pl.BlockSpec((None, D, D_FF), lambda i: (layer, 0, 0), **resident),
            pl.BlockSpec((None, D, D_FF), lambda i: (layer, 0, 1), **resident),
            pl.BlockSpec((None, D_FF, D), lambda i: (layer, 0, 0), **resident),
        ],
        out_specs=pl.BlockSpec((TM, D), lambda i: (i, 0)),
        out_shape=jax.ShapeDtypeStruct((T, D), F32),
        compiler_params=_cparams(("parallel",)),
        name="ffn_dense",
    )(x, ln_g.reshape(1, D), mod, w13, w13, w2)


def _s5_kernel(u_ref, bm_ref, cm_ref, ar_ref, ai_ref, hr_ref, hi_ref,
               y_ref, fr_ref, fi_ref, bu_s, st_s, cr_s, ci_s, *, nb, steps):
    d = pl.program_id(0)
    tb = pl.program_id(3)
    rows = nb * steps
    pitch = steps + S5_ROW_PAD
    nch = S5_LANES // LANES
    u = jnp.concatenate([u_ref[s].reshape(rows, LANES) for s in range(S5_SUB)], axis=1).astype(BF16)
    bu = jnp.dot(u, bm_ref[0, 0], preferred_element_type=F32)
    for c in range(2 * nch):
        for k in range(nb):
            bu_s[c, k * pitch:k * pitch + steps, :] = bu[k * steps:(k + 1) * steps, c * LANES:(c + 1) * LANES]

    @pl.when(tb == 0)
    def _():
        cr_s[...] = hr_ref[0, 0]
        ci_s[...] = hi_ref[0, 0]

    ar = [jnp.broadcast_to(ar_ref[0, 0][:, c * LANES:(c + 1) * LANES], (nb, LANES)) for c in range(nch)]
    ai = [jnp.broadcast_to(ai_ref[0, 0][:, c * LANES:(c + 1) * LANES], (nb, LANES)) for c in range(nch)]

    def step(i, carry):
        l = jnp.where(d == 0, i, steps - 1 - i)
        idx = pl.ds(l, nb, stride=pitch)
        out = []
        for c in range(nch):
            sr, si = carry[2 * c], carry[2 * c + 1]
            nr = ar[c] * sr - ai[c] * si + bu_s[c, idx, :]
            ni = ar[c] * si + ai[c] * sr + bu_s[nch + c, idx, :]
            st_s[c, idx, :] = nr
            st_s[nch + c, idx, :] = ni
            out += [nr, ni]
        return tuple(out)

    init = []
    for c in range(nch):
        init += [cr_s[:, c * LANES:(c + 1) * LANES], ci_s[:, c * LANES:(c + 1) * LANES]]
    fin = lax.fori_loop(0, steps, step, tuple(init), unroll=S5_UNROLL)
    sr = jnp.concatenate([fin[2 * c] for c in range(nch)], axis=1)
    si = jnp.concatenate([fin[2 * c + 1] for c in range(nch)], axis=1)
    cr_s[...] = sr
    ci_s[...] = si
    fr_ref[0, 0] = sr
    fi_ref[0, 0] = si
    states = jnp.concatenate(
        [jnp.concatenate([st_s[c, k * pitch:k * pitch + steps, :] for k in range(nb)], axis=0)
         for c in range(2 * nch)], axis=1).astype(BF16)
    y = jnp.dot(states, cm_ref[0, 0], preferred_element_type=F32)
    for s in range(S5_SUB):
        y_ref[s] = y[:, s * LANES:(s + 1) * LANES].reshape(nb, steps, LANES)


def _s5_call(u, bmat, cmat, a_re, a_im, h_re, h_im, *, row0, n_seq, seq_len, nb, steps, name):
    n_sg, n_tb = n_seq // nb, seq_len // steps
    sg0 = row0 // (nb * seq_len)

    def tbi(d, tb):
        return jnp.where(d == 0, tb, n_tb - 1 - tb)

    par = lambda last: pl.BlockSpec((1, 1) + last, lambda d, j, sg, tb: (d, j, 0, 0))
    st = pl.BlockSpec((1, 1, nb, S5_LANES), lambda d, j, sg, tb: (d, j, sg, 0))
    y, f_re, f_im = pl.pallas_call(
        functools.partial(_s5_kernel, nb=nb, steps=steps),
        grid=(2, S5_NSLAB, n_sg, n_tb),
        in_specs=[
            pl.BlockSpec((S5_SUB, nb, steps, LANES), lambda d, j, sg, tb: (j, sg0 + sg, tbi(d, tb), 0)),
            par((S5_SLAB, 2 * S5_LANES)),
            par((2 * S5_LANES, S5_SLAB)),
            par((1, S5_LANES)), par((1, S5_LANES)),
            st, st,
        ],
        out_specs=[
            pl.BlockSpec((None, S5_SUB, nb, steps, LANES), lambda d, j, sg, tb: (d, j, sg, tbi(d, tb), 0)),
            st, st,
        ],
        out_shape=[
            jax.ShapeDtypeStruct((2, D_SLABS, n_seq, seq_len, LANES), F32),
            jax.ShapeDtypeStruct((2, S5_NSLAB, n_seq, S5_LANES), F32),
            jax.ShapeDtypeStruct((2, S5_NSLAB, n_seq, S5_LANES), F32),
        ],
        scratch_shapes=[pltpu.VMEM((2 * S5_LANES // LANES, nb * (steps + S5_ROW_PAD), LANES), F32)] * 2
                       + [pltpu.VMEM((nb, S5_LANES), F32)] * 2,
        compiler_params=_cparams(("parallel", "parallel", "parallel", "arbitrary")),
        name=name,
    )(u.reshape(D_SLABS, T // seq_len, seq_len, LANES), bmat, cmat, a_re, a_im, h_re, h_im)
    return y.reshape(2, D_SLABS, n_seq * seq_len, LANES), f_re, f_im


def _s5_params(lam_re, lam_im, log_step, b_re, b_im, c_re, c_im):
    step = jnp.exp(log_step)[..., None]
    zr, zi = lam_re * step, lam_im * step
    mag = jnp.exp(zr)
    a_re, a_im = mag * jnp.cos(zi), mag * jnp.sin(zi)
    nr, ni = a_re - 1.0, a_im
    den = lam_re * lam_re + lam_im * lam_im
    k_re = (nr * lam_re + ni * lam_im) / den
    k_im = (ni * lam_re - nr * lam_im) / den
    bb_re = k_re[..., None] * b_re - k_im[..., None] * b_im
    bb_im = k_re[..., None] * b_im + k_im[..., None] * b_re
    gl = S5_SLAB // SSM_GROUP
    eye = jnp.eye(gl, dtype=F32)

    def bdiag_in(w):
        w = w.reshape(2, S5_NSLAB, gl, SSM_STATE, SSM_GROUP)
        return jnp.einsum('dsgnp,gh->dsgphn', w, eye).reshape(2, S5_NSLAB, S5_SLAB, S5_LANES)

    def bdiag_out(w):
        w = w.reshape(2, S5_NSLAB, gl, SSM_GROUP, SSM_STATE)
        return jnp.einsum('dsgpn,gh->dsgnhp', w, eye).reshape(2, S5_NSLAB, S5_LANES, S5_SLAB)

    bmat = jnp.concatenate([bdiag_in(bb_re), bdiag_in(bb_im)], axis=-1).astype(BF16)
    cmat = jnp.concatenate([bdiag_out(c_re), -bdiag_out(c_im)], axis=-2).astype(BF16)
    slab = lambda a: a.reshape(2, S5_NSLAB, 1, S5_LANES)
    return bmat, cmat, slab(a_re), slab(a_im)


def _state_to_slabs(h):
    b = h.shape[0]
    return jnp.transpose(h.reshape(b, 2, S5_NSLAB, S5_LANES), (1, 2, 0, 3))


def _slabs_to_state(f):
    b = f.shape[2]
    return jnp.transpose(f, (2, 0, 1, 3)).reshape(b, 2, SSM_GROUPS, SSM_STATE)


def _glu_res_kernel(u_ref, yl_ref, yc_ref, x_ref, mod_ref, d_ref, w_ref, b_ref, o_ref):
    is_lat = pl.program_id(0) < LAT_TILES
    y = jnp.where(is_lat, _load_slabs(yl_ref.at[0]) + _load_slabs(yl_ref.at[1]),
                  _load_slabs(yc_ref.at[0]) + _load_slabs(yc_ref.at[1]))
    yt = _load_slabs(u_ref) * d_ref[...] + y
    z = _gelu_tanh(yt)
    gate = _sigmoid(jnp.dot(z.astype(BF16), w_ref[...], preferred_element_type=F32) + b_ref[...])
    o_ref[...] = x_ref[...] + mod_ref[0][2:3] * (z * gate)


def _glu_res_call(u, y_lat, y_ctx, x, mod, d_skip, w_glu, b_glu):
    return pl.pallas_call(
        _glu_res_kernel,
        grid=(T // TM,),
        in_specs=[
            pl.BlockSpec((D_SLABS, TM, LANES), lambda i: (0, i, 0)),
            pl.BlockSpec((2, D_SLABS, TM, LANES), lambda i: (0, 0, _lat_tile(i), 0)),
            pl.BlockSpec((2, D_SLABS, TM, LANES), lambda i: (0, 0, _ctx_tile(i), 0)),
            pl.BlockSpec((TM, D), lambda i: (i, 0)),
            pl.BlockSpec((1, 6, D), lambda i: (_cond_of_row(i * TM), 0, 0)),
            pl.BlockSpec((1, D), lambda i: (0, 0)),
            pl.BlockSpec((D, D), lambda i: (0, 0)),
            pl.BlockSpec((1, D), lambda i: (0, 0)),
        ],
        out_specs=pl.BlockSpec((TM, D), lambda i: (i, 0)),
        out_shape=jax.ShapeDtypeStruct((T, D), F32),
        compiler_params=_cparams(("parallel",)),
        name="s5_glu_res",
    )(u, y_lat, y_ctx, x, mod, d_skip.reshape(1, D), w_glu, b_glu.reshape(1, D))


HALO = 8


def _conv_kernel(x_ref, xp_ref, xn_ref, g_ref, mod_ref, win_ref, cw_ref, cb_ref, wout_ref, o_ref):
    i = pl.program_id(0)
    mod = mod_ref[0]
    x = x_ref[...]
    x_ext = jnp.concatenate([xp_ref[...], x, xn_ref[...]], axis=0)
    h = _modnorm(x_ext, g_ref[...], mod[1:2], mod[0:1]).astype(BF16)
    proj = jnp.dot(h, win_ref[...], preferred_element_type=F32)
    z = proj[:, D:2 * D] * proj[:, 2 * D:]
    seq_mask = jnp.where(i < LAT_TILES, DEC_SEQ - 1, SEQ - 1)
    pos = (lax.broadcasted_iota(jnp.int32, (TM, 1), 0) + i * TM) & seq_mask
    zp = jnp.where(pos == 0, 0.0, z[HALO - 1:HALO - 1 + TM])
    zn = jnp.where(pos == seq_mask, 0.0, z[HALO + 1:HALO + 1 + TM])
    cw = cw_ref[...]
    zc = cw[0:1] * zp + cw[1:2] * z[HALO:HALO + TM] + cw[2:3] * zn + cb_ref[...]
    a = (proj[HALO:HALO + TM, :D] * zc).astype(BF16)
    y = jnp.dot(a, wout_ref[...], preferred_element_type=F32)
    o_ref[...] = x + mod[2:3] * y


def _conv_call(x, ln_g, mod, w_in, conv_w, conv_b, w_out):
    rh = TM // HALO
    nh = T // HALO
    resident = dict(pipeline_mode=pl.Buffered(1))
    return pl.pallas_call(
        _conv_kernel,
        grid=(T // TM,),
        in_specs=[
            pl.BlockSpec((TM, D), lambda i: (i, 0)),
            pl.BlockSpec((HALO, D), lambda i: (jnp.maximum(i * rh - 1, 0), 0)),
            pl.BlockSpec((HALO, D), lambda i: (jnp.minimum((i + 1) * rh, nh - 1), 0)),
            pl.BlockSpec((1, D), lambda i: (0, 0)),
            pl.BlockSpec((1, 6, D), lambda i: (_cond_of_row(i * TM), 0, 0)),
            pl.BlockSpec((D, 3 * D), lambda i: (0, 0), **resident),
            pl.BlockSpec((3, D), lambda i: (0, 0)),
            pl.BlockSpec((1, D), lambda i: (0, 0)),
            pl.BlockSpec((D, D), lambda i: (0, 0), **resident),
        ],
        out_specs=pl.BlockSpec((TM, D), lambda i: (i, 0)),
        out_shape=jax.ShapeDtypeStruct((T, D), F32),
        compiler_params=_cparams(("parallel",)),
        name="conv_mix",
    )(x, x, x, ln_g.reshape(1, D), mod, w_in, conv_w, conv_b.reshape(1, D), w_out)


def _router_kernel(x_ref, g_ref, mod_ref, r_ref, h_ref, rt_ref):
    mod = mod_ref[0]
    h = _modnorm(x_ref[...], g_ref[...], mod[4:5], mod[3:4])
    h_ref[...] = h
    logits = jnp.dot(h, r_ref[...], preferred_element_type=F32, precision=lax.Precision.HIGHEST)
    lane = lax.broadcasted_iota(jnp.int32, logits.shape, 1)
    logits = jnp.where(lane < N_EXPERTS, logits, -jnp.inf)
    m1 = jnp.max(logits, axis=-1, keepdims=True)
    i1 = jnp.min(jnp.where(logits == m1, lane, LANES), axis=-1, keepdims=True)
    rest = jnp.where(lane == i1, -jnp.inf, logits)
    m2 = jnp.max(rest, axis=-1, keepdims=True)
    i2 = jnp.min(jnp.where(rest == m2, lane, LANES), axis=-1, keepdims=True)
    e2 = jnp.exp(m2 - m1)
    w1 = 1.0 / (1.0 + e2)
    w2 = e2 / (1.0 + e2)
    rt_ref[...] = jnp.where(lane == 0, i1.astype(F32),
                            jnp.where(lane == 1, i2.astype(F32),
                                      jnp.where(lane == 2, w1, jnp.where(lane == 3, w2, 0.0))))


def _router_call(x, ln_g, mod, router_pad):
    return pl.pallas_call(
        _router_kernel,
        grid=(T // TM,),
        in_specs=[
            pl.BlockSpec((TM, D), lambda i: (i, 0)),
            pl.BlockSpec((1, D), lambda i: (0, 0)),
            pl.BlockSpec((1, 6, D), lambda i: (_cond_of_row(i * TM), 0, 0)),
            pl.BlockSpec((D, LANES), lambda i: (0, 0)),
        ],
        out_specs=[pl.BlockSpec((TM, D), lambda i: (i, 0)),
                   pl.BlockSpec((TM, LANES), lambda i: (i, 0))],
        out_shape=[jax.ShapeDtypeStruct((T, D), F32), jax.ShapeDtypeStruct((T, LANES), F32)],
        compiler_params=_cparams(("parallel",)),
        name="moe_router",
    )(x, ln_g.reshape(1, D), mod, router_pad)


R_MAX = 2 * T + N_EXPERTS * TM_MOE
N_TILES = R_MAX // TM_MOE
DMA_UNROLL = 8
GATHER_SHIFT = 3
GATHER_SPREAD = 1 << GATHER_SHIFT


def _row_copy(src_hbm, src_row, dst_vmem, dst_row, sem):
    return pltpu.make_async_copy(src_hbm.at[pl.ds(src_row, 1), :], dst_vmem.at[pl.ds(dst_row, 1), :], sem)


def _gather_kernel(nv_ref, src_ref, h_hbm, o_ref, buf, sem):
    i = pl.program_id(0)
    nv = nv_ref[0]

    def issue_tile(t):
        slot = t % 2
        base = t * TM_MOE

        def issue(g, c):
            for k in range(GATHER_SPREAD):
                r = k * (TM_MOE // GATHER_SPREAD) + g
                _row_copy(h_hbm, src_ref[base + r], buf.at[slot], r, sem.at[slot]).start(priority=k % 2)
            return c

        lax.fori_loop(0, TM_MOE // GATHER_SPREAD, issue, 0)

    @pl.when(jnp.logical_and(i == 0, nv > 0))
    def _():
        issue_tile(0)

    @pl.when(i + 1 < nv)
    def _():
        issue_tile(i + 1)

    @pl.when(i < nv)
    def _():
        slot = i % 2
        pltpu.make_async_copy(h_hbm.at[pl.ds(0, TM_MOE), :], buf.at[slot], sem.at[slot]).wait()
        o_ref[...] = buf[slot].astype(BF16)

    @pl.when(i >= nv)
    def _():
        o_ref[...] = jnp.zeros_like(o_ref)


def _gather_call(n_valid, src_tok, h):
    return pl.pallas_call(
        _gather_kernel,
        grid_spec=pltpu.PrefetchScalarGridSpec(
            num_scalar_prefetch=2,
            grid=(N_TILES,),
            in_specs=[pl.BlockSpec(memory_space=pl.ANY)],
            out_specs=pl.BlockSpec((TM_MOE, D), lambda i, nv, src: (i, 0)),
            scratch_shapes=[pltpu.VMEM((2, TM_MOE, D), F32), pltpu.SemaphoreType.DMA((2,))],
        ),
        out_shape=jax.ShapeDtypeStruct((R_MAX, D), BF16),
        compiler_params=_cparams(("arbitrary",)),
        name="moe_gather",
    )(n_valid, src_tok, h)


def _moe_kernel(nv_ref, te_ref, tr_ref, tf_ref, xs_ref, w1_ref, w3_ref, w2_ref, o_ref,
                acc_s, wc13_s, wc2_s):
    i = pl.program_id(0)
    f = pl.program_id(1)
    nf = pl.num_programs(1)
    valid = i < nv_ref[0]

    @pl.when(jnp.logical_and(valid, tf_ref[i] == 1))
    def _():
        wc13_s[f] = jnp.concatenate([w1_ref[...].astype(BF16), w3_ref[...].astype(BF16)], axis=1)
        wc2_s[f] = w2_ref[...].astype(BF16)

    quarter = TM_MOE // MOE_ROW_SPLITS
    n_quarters = (tr_ref[i] + quarter - 1) // quarter

    @pl.when(jnp.logical_and(valid, f == 0))
    def _():
        acc_s[...] = jnp.zeros_like(acc_s)

    def swiglu_rows(rows):
        xs = xs_ref[:rows, :]
        gu = jnp.dot(xs, wc13_s[f], preferred_element_type=F32)
        a = _silu(gu[:, :TF_MOE]) * gu[:, TF_MOE:]
        acc_s[:rows, :] += jnp.dot(a.astype(BF16), wc2_s[f], preferred_element_type=F32)

    for q in range(1, MOE_ROW_SPLITS + 1):
        @pl.when(jnp.logical_and(valid, n_quarters == q))
        def _(q=q):
            swiglu_rows(q * quarter)

    @pl.when(f == nf - 1)
    def _():
        o_ref[...] = jnp.where(valid, acc_s[...], 0.0)


def _moe_call(n_valid, tile_expert, tile_rows, tile_first, xs, w13, w2, layer):
    nf = D_FF_EXPERT // TF_MOE

    def fe(i, f, nv, tf):
        return jnp.where(jnp.logical_and(i < nv[0], tf[i] == 1), f, nf - 1)

    return pl.pallas_call(
        _moe_kernel,
        grid_spec=pltpu.PrefetchScalarGridSpec(
            num_scalar_prefetch=4,
            grid=(N_TILES, nf),
            in_specs=[
                pl.BlockSpec((TM_MOE, D), lambda i, f, nv, te, tr, tf: (i, 0)),
                pl.BlockSpec((None, None, D, TF_MOE),
                             lambda i, f, nv, te, tr, tf: (layer, te[i], 0, fe(i, f, nv, tf))),
                pl.BlockSpec((None, None, D, TF_MOE),
                             lambda i, f, nv, te, tr, tf: (layer, te[i], 0, nf + fe(i, f, nv, tf))),
                pl.BlockSpec((None, None, TF_MOE, D),
                             lambda i, f, nv, te, tr, tf: (layer, te[i], fe(i, f, nv, tf), 0)),
            ],
            out_specs=pl.BlockSpec((TM_MOE, D), lambda i, f, nv, te, tr, tf: (i, 0)),
            scratch_shapes=[pltpu.VMEM((TM_MOE, D), F32),
                            pltpu.VMEM((nf, D, 2 * TF_MOE), BF16),
                            pltpu.VMEM((nf, TF_MOE, D), BF16)],
        ),
        out_shape=jax.ShapeDtypeStruct((R_MAX, D), F32),
        compiler_params=_cparams(("arbitrary", "arbitrary")),
        name="moe_experts",
    )(n_valid, tile_expert, tile_rows, tile_first, xs, w13, w13, w2)


TM_COMB = 512


def _combine_kernel(pa_ref, pb_ref, y_hbm, x_ref, rt_ref, mod_ref, o_ref, buf_a, buf_b, sem):
    i = pl.program_id(0)

    def issue_tile(t):
        slot = t % 2
        base = t * TM_COMB

        def issue(r, c):
            _row_copy(y_hbm, pa_ref[base + r], buf_a.at[slot], r, sem.at[slot]).start(priority=0)
            _row_copy(y_hbm, pb_ref[base + r], buf_b.at[slot], r, sem.at[slot]).start(priority=1)
            return c

        lax.fori_loop(0, TM_COMB, issue, 0, unroll=DMA_UNROLL)

    @pl.when(i == 0)
    def _():
        issue_tile(0)

    @pl.when(i + 1 < pl.num_programs(0))
    def _():
        issue_tile(i + 1)

    slot = i % 2
    for buf in (buf_a, buf_b):
        pltpu.make_async_copy(y_hbm.at[pl.ds(0, TM_COMB), :], buf.at[slot], sem.at[slot]).wait()
    rt = rt_ref[...]
    ffn = rt[:, 2:3] * buf_a[slot] + rt[:, 3:4] * buf_b[slot]
    o_ref[...] = x_ref[...] + mod_ref[0][5:6] * ffn


def _combine_call(pos_a, pos_b, y, x, rt, mod):
    return pl.pallas_call(
        _combine_kernel,
        grid_spec=pltpu.PrefetchScalarGridSpec(
            num_scalar_prefetch=2,
            grid=(T // TM_COMB,),
            in_specs=[
                pl.BlockSpec(memory_space=pl.ANY),
                pl.BlockSpec((TM_COMB, D), lambda i, pa, pb: (i, 0)),
                pl.BlockSpec((TM_COMB, LANES), lambda i, pa, pb: (i, 0)),
                pl.BlockSpec((1, 6, D), lambda i, pa, pb: (_cond_of_row(i * TM_COMB), 0, 0)),
            ],
            out_specs=pl.BlockSpec((TM_COMB, D), lambda i, pa, pb: (i, 0)),
            scratch_shapes=[pltpu.VMEM((2, TM_COMB, D), F32), pltpu.VMEM((2, TM_COMB, D), F32),
                            pltpu.SemaphoreType.DMA((2,))],
        ),
        out_shape=jax.ShapeDtypeStruct((T, D), F32),
        compiler_params=_cparams(("arbitrary",)),
        name="moe_combine",
    )(pos_a, pos_b, y, x, rt, mod)


def _routing_tables(rt):
    ea = rt[:, 0:2].astype(jnp.int32).reshape(-1)
    onehot = (ea[:, None] == jnp.arange(N_EXPERTS, dtype=jnp.int32)[None, :]).astype(jnp.int32)
    csum = jnp.cumsum(onehot, axis=0)
    rank = jnp.take_along_axis(csum, ea[:, None], axis=1)[:, 0] - 1
    counts = csum[-1]
    padded = ((counts + TM_MOE - 1) // TM_MOE) * TM_MOE
    ends = jnp.cumsum(padded)
    pos = (ends - padded)[ea] + rank
    n_valid = (ends[-1] // TM_MOE).astype(jnp.int32).reshape(1)
    tile_start = jnp.arange(N_TILES, dtype=jnp.int32) * TM_MOE
    expert_of_row = lambda r: jnp.sum((r[:, None] >= ends[None, :]).astype(jnp.int32), axis=1)
    tile_expert = jnp.minimum(expert_of_row(tile_start), expert_of_row(ends[-1:] - 1))
    real_end = (ends - padded + counts)[tile_expert]
    tile_rows = jnp.clip(real_end - tile_start, 0, TM_MOE).astype(jnp.int32)
    tile_idx = jnp.arange(N_TILES, dtype=jnp.int32)
    prev_expert = jnp.concatenate([jnp.full((1,), -1, jnp.int32), tile_expert[:-1]])
    tile_first = jnp.logical_and(tile_expert != prev_expert, tile_idx < n_valid[0]).astype(jnp.int32)
    tok = jnp.arange(2 * T, dtype=jnp.int32) // 2
    src_tok = (jnp.arange(R_MAX, dtype=jnp.int32) % T).at[pos].set(tok)
    pos2 = pos.reshape(T, 2).astype(jnp.int32)
    return n_valid, tile_expert, tile_rows, tile_first, src_tok, pos2[:, 0], pos2[:, 1]


def _moe_layer(x, ln_g, mod, router, w13, w2, layer):
    router_pad = jnp.pad(router, ((0, 0), (0, LANES - N_EXPERTS)))
    h, rt = _router_call(x, ln_g, mod, router_pad)
    n_valid, tile_expert, tile_rows, tile_first, src_tok, pos_a, pos_b = _routing_tables(rt)
    xs = _gather_call(n_valid, src_tok, h)
    y = _moe_call(n_valid, tile_expert, tile_rows, tile_first, xs, w13, w2, layer)
    return _combine_call(pos_a, pos_b, y, x, rt, mod)


def kernel(x_prompt, x_sample, c, cache_na_k, cache_na_v, state_ssm_re, state_ssm_im, c_ctx,
           ln1_g, ln2_g, ada_w, ada_b, na_w_qkv, na_w_o, na_q_g, na_k_g, na_rpb,
           ssm_lam_re, ssm_lam_im, ssm_log_step, ssm_b_re, ssm_b_im, ssm_c_re, ssm_c_im,
           ssm_d, ssm_w_glu, ssm_b_glu, cv_w_in, cv_conv_w, cv_conv_b, cv_w_out,
           ffn_w13, ffn_w2, moe_router, moe_w13, moe_w2):
    depth = ada_w.shape[0]
    x = (x_sample.reshape(T_LAT, D), x_prompt.reshape(T_CTX, D))
    conds = jnp.concatenate([c, c_ctx[None, :], jnp.zeros((N_COND - DEC_BATCH - 1, D), F32)], axis=0)
    mods = _ada_call(conds, ada_w, ada_b).reshape(depth, N_COND, 6, D)
    ffn_w13_b, ffn_w2_b = ffn_w13.astype(BF16), ffn_w2.astype(BF16)

    new_k = new_v = None
    s_re, s_im = [], []
    for l in range(depth):
        kind, j = l % 3, l // 3
        mod = mods[l]
        if kind == 0:
            qkv = _normproj_call(x, ln1_g[l], mod, na_w_qkv[j].astype(BF16), shift=0, scale=1,
                                 name="qkv_proj", slab_out=True)
            qg2 = jnp.tile(na_q_g[j], 2).reshape(1, LANES)
            kg2 = jnp.tile(na_k_g[j], 2).reshape(1, LANES)
            past = cache_na_k.shape[3]
            pair = lambda a: jnp.transpose(a[:, j].reshape(DEC_BATCH, HP, 2, past, HEAD_DIM),
                                           (0, 1, 3, 2, 4)).reshape(DEC_BATCH, HP, past, LANES)
            ctx_k, ctx_v = pair(cache_na_k), pair(cache_na_v)
            o_lat = _lat_attn_call(qkv, ctx_k, ctx_v, qg2, kg2, _bias_pairs(na_rpb[j]))
            o_ctx, new_k, new_v = _ctx_attn_call(qkv, qg2, kg2, new_k, new_v)
            x = _proj_res_call(o_lat, o_ctx, x, mod, na_w_o[j].astype(BF16), gate=2, name="attn_out")
        elif kind == 1:
            u = _norm_call(x, ln1_g[l], mod, shift=0, scale=1, name="s5_norm")
            bmat, cmat, a_re, a_im = _s5_params(ssm_lam_re[j], ssm_lam_im[j], ssm_log_step[j],
                                                ssm_b_re[j], ssm_b_im[j], ssm_c_re[j], ssm_c_im[j])
            y_lat, _, _ = _s5_call(u, bmat, cmat, a_re, a_im,
                                   _state_to_slabs(state_ssm_re[:, j]), _state_to_slabs(state_ssm_im[:, j]),
                                   row0=0, n_seq=DEC_BATCH, seq_len=DEC_SEQ,
                                   nb=DEC_BATCH, steps=S5_T_LAT, name="s5_lat")
            zero = jnp.zeros((2, S5_NSLAB, BATCH, S5_LANES), F32)
            y_ctx, f_re, f_im = _s5_call(u, bmat, cmat, a_re, a_im, zero, zero,
                                         row0=T_LAT, n_seq=BATCH, seq_len=SEQ, nb=8, steps=S5_T_CTX,
                                         name="s5_ctx")
            s_re.append(_slabs_to_state(f_re))
            s_im.append(_slabs_to_state(f_im))
            x = _glu_res_call(u, y_lat, y_ctx, x, mod, ssm_d[j], ssm_w_glu[j].astype(BF16), ssm_b_glu[j])
        else:
            x = _conv_call(x, ln1_g[l], mod, cv_w_in[j].astype(BF16), cv_conv_w[j], cv_conv_b[j],
                           cv_w_out[j].astype(BF16))
        jj = l // 2
        if l % 2 == 0:
            x = _ffn_call(x, ln2_g[l], mod, ffn_w13_b, ffn_w2_b, jj)
        else:
            x = _moe_layer(x, ln2_g[l], mod, moe_router[jj], moe_w13, moe_w2, jj)

    y_sample = x[:T_LAT].reshape(DEC_BATCH, DEC_SEQ, D)
    y_prompt = x[T_LAT:].reshape(BATCH, SEQ, D)
    return (y_prompt, y_sample, new_k, new_v, jnp.stack(s_re, axis=1), jnp.stack(s_im, axis=1))
```

```python
import functools
import math

import numpy as np
import jax
import jax.numpy as jnp
from jax import lax
from jax.experimental import pallas as pl
from jax.experimental.pallas import tpu as pltpu

F32 = jnp.float32
BF16 = jnp.bfloat16

D = 1024
BATCH, SEQ = 16, 256
DEC_BATCH, DEC_SEQ = 4, 2048
GRID_W = 64
ROWS = DEC_SEQ // GRID_W
N_HEADS, HEAD_DIM = 16, 64
WIN_ROWS, WIN_COLS = 8, 16
SSM_GROUP, SSM_GROUPS, SSM_STATE = 16, 64, 64
D_FF = 2816
N_EXPERTS = 8
D_FF_EXPERT = 3584
EPS = 1e-6
NEG_INF = -1e30
SCALE = HEAD_DIM ** -0.5

T_LAT = DEC_BATCH * DEC_SEQ
T_CTX = BATCH * SEQ
T = T_LAT + T_CTX
N_COND = 8
CTX_COND = DEC_BATCH

LANES = 128
VMEM_LIMIT = 56 * 1024 * 1024

TM = 512
TM_MOE = 1024
TF_MOE = 512
MOE_ROW_SPLITS = 8
HP = N_HEADS // 2
QB_ROWS = 4
D_SLABS = D // LANES
S5_SLAB = 128
S5_SUB = S5_SLAB // LANES
S5_NSLAB = D // S5_SLAB
S5_LANES = (S5_SLAB // SSM_GROUP) * SSM_STATE
S5_T_LAT = 512
S5_T_CTX = SEQ
S5_ROW_PAD = 4
S5_UNROLL = 8


def _cparams(sem):
    return pltpu.CompilerParams(dimension_semantics=sem, vmem_limit_bytes=VMEM_LIMIT)


def _cond_of_row(row):
    return jnp.minimum(row // DEC_SEQ, CTX_COND)


def _modnorm(x, g, sc, sh):
    ms = jnp.mean(x * x, axis=-1, keepdims=True)
    y = x * lax.rsqrt(ms + EPS) * g
    return y * (1.0 + sc) + sh


def _sigmoid(x):
    return 1.0 / (1.0 + jnp.exp(-x))


def _silu(x):
    return x * _sigmoid(x)


def _gelu_tanh(x):
    c = math.sqrt(2.0 / math.pi)
    return 0.5 * x * (1.0 + jnp.tanh(c * (x + 0.044715 * (x * x * x))))


def _ada_kernel(c_ref, w_ref, b_ref, o_ref):
    s = _silu(c_ref[...]).astype(BF16)
    o_ref[0] = jnp.dot(s, w_ref[0].astype(BF16), preferred_element_type=F32) + b_ref[0]


def _ada_call(conds, ada_w, ada_b):
    depth = ada_w.shape[0]
    tn = 1536
    return pl.pallas_call(
        _ada_kernel,
        grid=(depth, 6 * D // tn),
        in_specs=[
            pl.BlockSpec((N_COND, D), lambda l, n: (0, 0)),
            pl.BlockSpec((1, D, tn), lambda l, n: (l, 0, n)),
            pl.BlockSpec((1, 1, tn), lambda l, n: (l, 0, n)),
        ],
        out_specs=pl.BlockSpec((1, N_COND, tn), lambda l, n: (l, 0, n)),
        out_shape=jax.ShapeDtypeStruct((depth, N_COND, 6 * D), F32),
        compiler_params=_cparams(("parallel", "parallel")),
        name="ada_mod",
    )(conds, ada_w, ada_b.reshape(depth, 1, 6 * D))


LAT_TILES = T_LAT // TM


def _lat_tile(i):
    return jnp.minimum(i, LAT_TILES - 1)


def _ctx_tile(i):
    return jnp.maximum(i - LAT_TILES, 0)


def _rows_specs(x):
    if isinstance(x, tuple):
        return [pl.BlockSpec((TM, D), lambda i: (_lat_tile(i), 0)),
                pl.BlockSpec((TM, D), lambda i: (_ctx_tile(i), 0))], list(x)
    return [pl.BlockSpec((TM, D), lambda i: (i, 0))], [x]


def _rows_value(refs):
    if len(refs) == 2:
        return jnp.where(pl.program_id(0) < LAT_TILES, refs[0][...], refs[1][...])
    return refs[0][...]


def _store_slabs(o_ref, val):
    for c in range(val.shape[1] // LANES):
        o_ref[c] = val[:, c * LANES:(c + 1) * LANES]


def _load_slabs(ref):
    return jnp.concatenate([ref[c] for c in range(ref.shape[0])], axis=1)


def _normproj_kernel(*refs, shift, scale, slab_out):
    *x_refs, g_ref, mod_ref, w_ref, o_ref = refs
    mod = mod_ref[0]
    h = _modnorm(_rows_value(x_refs), g_ref[...], mod[scale:scale + 1], mod[shift:shift + 1])
    res = jnp.dot(h.astype(BF16), w_ref[...], preferred_element_type=F32)
    if slab_out:
        _store_slabs(o_ref, res)
    else:
        o_ref[...] = res


def _normproj_call(x, ln_g, mod, w, *, shift, scale, name, slab_out=False):
    n = w.shape[1]
    x_specs, x_args = _rows_specs(x)
    if slab_out:
        out_spec = pl.BlockSpec((n // LANES, TM, LANES), lambda i: (0, i, 0))
        out_shape = jax.ShapeDtypeStruct((n // LANES, T, LANES), F32)
    else:
        out_spec = pl.BlockSpec((TM, n), lambda i: (i, 0))
        out_shape = jax.ShapeDtypeStruct((T, n), F32)
    return pl.pallas_call(
        functools.partial(_normproj_kernel, shift=shift, scale=scale, slab_out=slab_out),
        grid=(T // TM,),
        in_specs=x_specs + [
            pl.BlockSpec((1, D), lambda i: (0, 0)),
            pl.BlockSpec((1, 6, D), lambda i: (_cond_of_row(i * TM), 0, 0)),
            pl.BlockSpec((D, n), lambda i: (0, 0)),
        ],
        out_specs=out_spec,
        out_shape=out_shape,
        compiler_params=_cparams(("parallel",)),
        name=name,
    )(*x_args, ln_g.reshape(1, D), mod, w)


def _norm_kernel(x_ref, g_ref, mod_ref, o_ref, *, shift, scale):
    mod = mod_ref[0]
    _store_slabs(o_ref, _modnorm(x_ref[...], g_ref[...], mod[scale:scale + 1], mod[shift:shift + 1]))


def _norm_call(x, ln_g, mod, *, shift, scale, name):
    return pl.pallas_call(
        functools.partial(_norm_kernel, shift=shift, scale=scale),
        grid=(T // TM,),
        in_specs=[
            pl.BlockSpec((TM, D), lambda i: (i, 0)),
            pl.BlockSpec((1, D), lambda i: (0, 0)),
            pl.BlockSpec((1, 6, D), lambda i: (_cond_of_row(i * TM), 0, 0)),
        ],
        out_specs=pl.BlockSpec((D // LANES, TM, LANES), lambda i: (0, i, 0)),
        out_shape=jax.ShapeDtypeStruct((D // LANES, T, LANES), F32),
        compiler_params=_cparams(("parallel",)),
        name=name,
    )(x, ln_g.reshape(1, D), mod)


def _proj_res_kernel(*refs, gate, n_x):
    al_ref, ac_ref = refs[:2]
    x_refs = refs[2:2 + n_x]
    mod_ref, w_ref, o_ref = refs[2 + n_x:]
    a = jnp.where(pl.program_id(0) < LAT_TILES, _load_slabs(al_ref), _load_slabs(ac_ref))
    y = jnp.dot(a.astype(BF16), w_ref[...], preferred_element_type=F32)
    o_ref[...] = _rows_value(x_refs) + mod_ref[0][gate:gate + 1] * y


def _proj_res_call(a_lat, a_ctx, x, mod, w, *, gate, name):
    ns = D // LANES
    a_specs = [pl.BlockSpec((ns, TM, LANES), lambda i: (0, _lat_tile(i), 0)),
               pl.BlockSpec((ns, TM, LANES), lambda i: (0, _ctx_tile(i), 0))]
    a_args = [a_lat, a_ctx]
    x_specs, x_args = _rows_specs(x)
    return pl.pallas_call(
        functools.partial(_proj_res_kernel, gate=gate, n_x=len(x_args)),
        grid=(T // TM,),
        in_specs=a_specs + x_specs + [
            pl.BlockSpec((1, 6, D), lambda i: (_cond_of_row(i * TM), 0, 0)),
            pl.BlockSpec((D, D), lambda i: (0, 0)),
        ],
        out_specs=pl.BlockSpec((TM, D), lambda i: (i, 0)),
        out_shape=jax.ShapeDtypeStruct((T, D), F32),
        compiler_params=_cparams(("parallel",)),
        name=name,
    )(*a_args, *x_args, mod, w)


def _head_norm(x, g, head0):
    x2 = x * x
    s0 = jnp.sum(jnp.where(head0, x2, 0.0), axis=-1, keepdims=True)
    s1 = jnp.sum(jnp.where(head0, 0.0, x2), axis=-1, keepdims=True)
    ms = jnp.where(head0, s0, s1) * (1.0 / HEAD_DIM)
    return x * lax.rsqrt(ms + EPS) * g


def _nt_dot(a, b):
    return lax.dot_general(a, b, (((1,), (1,)), ((), ())), preferred_element_type=F32)


def _ctx_attn_kernel(*refs, n_prev):
    if n_prev:
        q_ref, k_ref, v_ref, qg_ref, kg_ref, pk_ref, pv_ref, o_ref, ko_ref, vo_ref = refs
        ko_ref[0, :n_prev] = pk_ref[0]
        vo_ref[0, :n_prev] = pv_ref[0]
    else:
        q_ref, k_ref, v_ref, qg_ref, kg_ref, o_ref, ko_ref, vo_ref = refs
    head0 = lax.broadcasted_iota(jnp.int32, (SEQ, LANES), 1) < HEAD_DIM
    for hp in range(HP):
        qn = _head_norm(q_ref[hp], qg_ref[...], head0) * SCALE
        kn = _head_norm(k_ref[hp], kg_ref[...], head0)
        v = v_ref[hp]
        ko_ref[0, n_prev, 2 * hp] = kn[:, :HEAD_DIM]
        ko_ref[0, n_prev, 2 * hp + 1] = kn[:, HEAD_DIM:]
        vo_ref[0, n_prev, 2 * hp] = v[:, :HEAD_DIM]
        vo_ref[0, n_prev, 2 * hp + 1] = v[:, HEAD_DIM:]
        knb = kn.astype(BF16)
        vb = v.astype(BF16)
        outs = []
        for hh in range(2):
            qm = jnp.where(head0 if hh == 0 else jnp.logical_not(head0), qn, 0.0).astype(BF16)
            s = _nt_dot(qm, knb)
            p = jnp.exp(s - jnp.max(s, axis=-1, keepdims=True))
            l = jnp.sum(p, axis=-1, keepdims=True)
            outs.append(jnp.dot(p.astype(BF16), vb, preferred_element_type=F32) / l)
        o_ref[hp] = jnp.where(head0, outs[0], outs[1])


def _ctx_attn_call(qkv, qg2, kg2, prev_k, prev_v):
    row0 = T_LAT // SEQ
    n_prev = 0 if prev_k is None else prev_k.shape[1]
    blk = lambda c: pl.BlockSpec((HP, SEQ, LANES), lambda b, c=c: (c, row0 + b, 0))
    kv_blk = lambda n: pl.BlockSpec((1, n, N_HEADS, SEQ, HEAD_DIM), lambda b: (b, 0, 0, 0, 0))
    kv_shape = jax.ShapeDtypeStruct((BATCH, n_prev + 1, N_HEADS, SEQ, HEAD_DIM), F32)
    prev_specs = [kv_blk(n_prev)] * 2 if n_prev else []
    prev_args = [prev_k, prev_v] if n_prev else []
    return pl.pallas_call(
        functools.partial(_ctx_attn_kernel, n_prev=n_prev),
        grid=(BATCH,),
        in_specs=[blk(0), blk(1), blk(2),
                  pl.BlockSpec((1, LANES), lambda b: (0, 0)),
                  pl.BlockSpec((1, LANES), lambda b: (0, 0))] + prev_specs,
        out_specs=[pl.BlockSpec((HP, SEQ, LANES), lambda b: (0, b, 0)), kv_blk(n_prev + 1), kv_blk(n_prev + 1)],
        out_shape=[jax.ShapeDtypeStruct((HP, T_CTX, LANES), F32), kv_shape, kv_shape],
        compiler_params=_cparams(("parallel",)),
        name="ctx_attn",
    )(qkv, qkv, qkv, qg2, kg2, *prev_args)


def _band_of_block(p):
    r_lo, r_hi = QB_ROWS * p, QB_ROWS * p + QB_ROWS - 1
    kr = min(WIN_ROWS, ROWS)
    lo = min(max(r_lo - kr // 2, 0), ROWS - kr)
    hi = min(max(r_hi - kr // 2, 0), ROWS - kr) + kr
    lo -= lo % 2
    hi += hi % 2
    return lo, hi - lo


def _lat_attn_kernel(q_ref, k_ref, v_ref, ck_ref, cv_ref, qg_ref, kg_ref, bp_ref, o_ref,
                     qm0_s, qm1_s, kt_s, v_s):
    head0 = lax.broadcasted_iota(jnp.int32, (DEC_SEQ, LANES), 1) < HEAD_DIM
    qn = _head_norm(q_ref[...], qg_ref[...], head0)
    qm0_s[...] = jnp.where(head0, qn * SCALE, 0.0).astype(BF16)
    qm1_s[...] = jnp.where(head0, 0.0, qn * SCALE).astype(BF16)
    kt_s[...] = _head_norm(k_ref[...], kg_ref[...], head0).T.astype(BF16)
    v_s[...] = v_ref[...].astype(BF16)
    ckt = ck_ref[...].T.astype(BF16)
    cvb = cv_ref[...].astype(BF16)

    qc = lax.broadcasted_iota(jnp.int32, (GRID_W, LANES), 0)
    kl = lax.broadcasted_iota(jnp.int32, (GRID_W, LANES), 1)
    kc = jnp.where(kl < GRID_W, kl, kl - GRID_W)
    cs = jnp.clip(qc - WIN_COLS // 2, 0, GRID_W - WIN_COLS)
    col_ok = jnp.logical_and(kc >= cs, kc < cs + WIN_COLS)
    left = kl < GRID_W
    mask_of = {
        (True, True): col_ok,
        (True, False): jnp.logical_and(col_ok, left),
        (False, True): jnp.logical_and(col_ok, jnp.logical_not(left)),
    }
    h0q = lax.broadcasted_iota(jnp.int32, (QB_ROWS * GRID_W, LANES), 1) < HEAD_DIM
    kr_win = min(WIN_ROWS, ROWS)

    for p in range(ROWS // QB_ROWS):
        u0, nrows = _band_of_block(p)
        q_lo = p * QB_ROWS * GRID_W
        ktb = kt_s[:, u0 * GRID_W:(u0 + nrows) * GRID_W]
        vb = v_s[u0 * GRID_W:(u0 + nrows) * GRID_W, :]
        nq = QB_ROWS * GRID_W
        qm = jnp.concatenate([qm0_s[q_lo:q_lo + nq, :], qm1_s[q_lo:q_lo + nq, :]], axis=0)
        n_loc = nrows * GRID_W
        s_all = jnp.dot(qm, jnp.concatenate([ktb, ckt], axis=1), preferred_element_type=F32)
        s_loc, s_ctx = s_all[:, :n_loc], s_all[:, n_loc:]
        p_rows, l_rows = [], []
        for hh in range(2):
            for i in range(QB_ROWS):
                r = p * QB_ROWS + i
                r0 = min(max(r - kr_win // 2, 0), ROWS - kr_win)
                rows = slice(hh * nq + i * GRID_W, hh * nq + (i + 1) * GRID_W)
                sbs = {}
                for m in range(nrows // 2):
                    kr = u0 + 2 * m
                    ok_l = r0 <= kr < r0 + kr_win
                    ok_r = r0 <= kr + 1 < r0 + kr_win
                    if ok_l or ok_r:
                        sb = s_loc[rows, m * LANES:(m + 1) * LANES] + bp_ref[hh, kr - r + WIN_ROWS]
                        sbs[m] = jnp.where(mask_of[(ok_l, ok_r)], sb, NEG_INF)
                sc = s_ctx[rows, :]
                mx = jnp.maximum(
                    jnp.max(functools.reduce(jnp.maximum, sbs.values()), axis=-1, keepdims=True),
                    jnp.max(sc, axis=-1, keepdims=True))
                pbs = {m: jnp.exp(sb - mx) for m, sb in sbs.items()}
                pc = jnp.exp(sc - mx)
                l_rows.append(jnp.sum(functools.reduce(jnp.add, pbs.values()), axis=-1, keepdims=True)
                              + jnp.sum(pc, axis=-1, keepdims=True))
                zero = jnp.zeros((GRID_W, LANES), BF16)
                p_rows.append(jnp.concatenate(
                    [pbs[m].astype(BF16) if m in pbs else zero for m in range(nrows // 2)]
                    + [pc.astype(BF16)], axis=1))
        o = jnp.dot(jnp.concatenate(p_rows, axis=0), jnp.concatenate([vb, cvb], axis=0),
                    preferred_element_type=F32)
        o = o / jnp.concatenate(l_rows, axis=0)
        o_ref[q_lo:q_lo + nq, :] = jnp.where(h0q, o[:nq], o[nq:])


def _lat_attn_call(qkv, ctx_k, ctx_v, qg2, kg2, bias_pairs):
    blk = lambda c: pl.BlockSpec((None, DEC_SEQ, LANES), lambda hp, b, c=c: (c * HP + hp, b, 0))
    cblk = pl.BlockSpec((None, None, ctx_k.shape[2], LANES), lambda hp, b: (b, hp, 0, 0))
    return pl.pallas_call(
        _lat_attn_kernel,
        grid=(HP, DEC_BATCH),
        in_specs=[blk(0), blk(1), blk(2), cblk, cblk,
                  pl.BlockSpec((1, LANES), lambda hp, b: (0, 0)),
                  pl.BlockSpec((1, LANES), lambda hp, b: (0, 0)),
                  pl.BlockSpec((2, 2 * WIN_ROWS, GRID_W, LANES), lambda hp, b: (hp, 0, 0, 0))],
        out_specs=pl.BlockSpec((None, DEC_SEQ, LANES), lambda hp, b: (hp, b, 0)),
        out_shape=jax.ShapeDtypeStruct((HP, T_LAT, LANES), F32),
        scratch_shapes=[pltpu.VMEM((DEC_SEQ, LANES), BF16), pltpu.VMEM((DEC_SEQ, LANES), BF16),
                        pltpu.VMEM((LANES, DEC_SEQ), BF16), pltpu.VMEM((DEC_SEQ, LANES), BF16)],
        compiler_params=_cparams(("parallel", "parallel")),
        name="lat_attn",
    )(qkv, qkv, qkv, ctx_k, ctx_v, qg2, kg2, bias_pairs)


def _bias_pairs(rpb):
    cols = np.arange(GRID_W)
    col_idx = np.clip(cols[None, :] - cols[:, None], -(WIN_COLS - 1), WIN_COLS - 1) + WIN_COLS - 1
    rc = rpb[:, :, col_idx]
    left = jnp.pad(rc, ((0, 0), (1, 0), (0, 0), (0, 0)))
    right = jnp.pad(rc, ((0, 0), (0, 1), (0, 0), (0, 0)))
    return jnp.concatenate([left, right], axis=-1)


def _ffn_kernel(x_ref, g_ref, mod_ref, w1_ref, w3_ref, w2_ref, o_ref):
    mod = mod_ref[0]
    x = x_ref[...]
    h = _modnorm(x, g_ref[...], mod[4:5], mod[3:4]).astype(BF16)
    a = _silu(jnp.dot(h, w1_ref[...], preferred_element_type=F32)) \
        * jnp.dot(h, w3_ref[...], preferred_element_type=F32)
    y = jnp.dot(a.astype(BF16), w2_ref[...], preferred_element_type=F32)
    o_ref[...] = x + mod[5:6] * y


def _ffn_call(x, ln_g, mod, w13, w2, layer):
    resident = dict(pipeline_mode=pl.Buffered(1))
    return pl.pallas_call(
        _ffn_kernel,
        grid=(T // TM,),
        in_specs=[
            pl.BlockSpec((TM, D), lambda i: (i, 0)),
            pl.BlockSpec((1, D), lambda i: (0, 0)),
            pl.BlockSpec((1, 6, D), lambda i: (_cond_of_row(i * TM), 0, 0)),
            pl.BlockSpec((None, D, D_FF), lambda i: (layer, 0, 0), **resident),
            pl.BlockSpec((None, D, D_FF), lambda i: (layer, 0, 1), **resident),
            pl.BlockSpec((None, D_FF, D), lambda i: (layer, 0, 0), **resident),
        ],
        out_specs=pl.BlockSpec((TM, D), lambda i: (i, 0)),
        out_shape=jax.ShapeDtypeStruct((T, D), F32),
        compiler_params=_cparams(("parallel",)),
        name="ffn_dense",
    )(x, ln_g.reshape(1, D), mod, w13, w13, w2)


def _s5_kernel(u_ref, bm_ref, cm_ref, ar_ref, ai_ref, hr_ref, hi_ref,
               y_ref, fr_ref, fi_ref, bu_s, st_s, cr_s, ci_s, *, nb, steps):
    d = pl.program_id(0)
    tb = pl.program_id(3)
    rows = nb * steps
    pitch = steps + S5_ROW_PAD
    nch = S5_LANES // LANES
    u = jnp.concatenate([u_ref[s].reshape(rows, LANES) for s in range(S5_SUB)], axis=1).astype(BF16)
    bu = jnp.dot(u, bm_ref[0, 0], preferred_element_type=F32)
    for c in range(2 * nch):
        for k in range(nb):
            bu_s[c, k * pitch:k * pitch + steps, :] = bu[k * steps:(k + 1) * steps, c * LANES:(c + 1) * LANES]

    @pl.when(tb == 0)
    def _():
        cr_s[...] = hr_ref[0, 0]
        ci_s[...] = hi_ref[0, 0]

    ar = [jnp.broadcast_to(ar_ref[0, 0][:, c * LANES:(c + 1) * LANES], (nb, LANES)) for c in range(nch)]
    ai = [jnp.broadcast_to(ai_ref[0, 0][:, c * LANES:(c + 1) * LANES], (nb, LANES)) for c in range(nch)]

    def step(i, carry):
        l = jnp.where(d == 0, i, steps - 1 - i)
        idx = pl.ds(l, nb, stride=pitch)
        out = []
        for c in range(nch):
            sr, si = carry[2 * c], carry[2 * c + 1]
            nr = ar[c] * sr - ai[c] * si + bu_s[c, idx, :]
            ni = ar[c] * si + ai[c] * sr + bu_s[nch + c, idx, :]
            st_s[c, idx, :] = nr
            st_s[nch + c, idx, :] = ni
            out += [nr, ni]
        return tuple(out)

    init = []
    for c in range(nch):
        init += [cr_s[:, c * LANES:(c + 1) * LANES], ci_s[:, c * LANES:(c + 1) * LANES]]
    fin = lax.fori_loop(0, steps, step, tuple(init), unroll=S5_UNROLL)
    sr = jnp.concatenate([fin[2 * c] for c in range(nch)], axis=1)
    si = jnp.concatenate([fin[2 * c + 1] for c in range(nch)], axis=1)
    cr_s[...] = sr
    ci_s[...] = si
    fr_ref[0, 0] = sr
    fi_ref[0, 0] = si
    states = jnp.concatenate(
        [jnp.concatenate([st_s[c, k * pitch:k * pitch + steps, :] for k in range(nb)], axis=0)
         for c in range(2 * nch)], axis=1).astype(BF16)
    y = jnp.dot(states, cm_ref[0, 0], preferred_element_type=F32)
    for s in range(S5_SUB):
        y_ref[s] = y[:, s * LANES:(s + 1) * LANES].reshape(nb, steps, LANES)


def _s5_call(u, bmat, cmat, a_re, a_im, h_re, h_im, *, row0, n_seq, seq_len, nb, steps, name):
    n_sg, n_tb = n_seq // nb, seq_len // steps
    sg0 = row0 // (nb * seq_len)

    def tbi(d, tb):
        return jnp.where(d == 0, tb, n_tb - 1 - tb)

    par = lambda last: pl.BlockSpec((1, 1) + last, lambda d, j, sg, tb: (d, j, 0, 0))
    st = pl.BlockSpec((1, 1, nb, S5_LANES), lambda d, j, sg, tb: (d, j, sg, 0))
    y, f_re, f_im = pl.pallas_call(
        functools.partial(_s5_kernel, nb=nb, steps=steps),
        grid=(2, S5_NSLAB, n_sg, n_tb),
        in_specs=[
            pl.BlockSpec((S5_SUB, nb, steps, LANES), lambda d, j, sg, tb: (j, sg0 + sg, tbi(d, tb), 0)),
            par((S5_SLAB, 2 * S5_LANES)),
            par((2 * S5_LANES, S5_SLAB)),
            par((1, S5_LANES)), par((1, S5_LANES)),
            st, st,
        ],
        out_specs=[
            pl.BlockSpec((None, S5_SUB, nb, steps, LANES), lambda d, j, sg, tb: (d, j, sg, tbi(d, tb), 0)),
            st, st,
        ],
        out_shape=[
            jax.ShapeDtypeStruct((2, D_SLABS, n_seq, seq_len, LANES), F32),
            jax.ShapeDtypeStruct((2, S5_NSLAB, n_seq, S5_LANES), F32),
            jax.ShapeDtypeStruct((2, S5_NSLAB, n_seq, S5_LANES), F32),
        ],
        scratch_shapes=[pltpu.VMEM((2 * S5_LANES // LANES, nb * (steps + S5_ROW_PAD), LANES), F32)] * 2
                       + [pltpu.VMEM((nb, S5_LANES), F32)] * 2,
        compiler_params=_cparams(("parallel", "parallel", "parallel", "arbitrary")),
        name=name,
    )(u.reshape(D_SLABS, T // seq_len, seq_len, LANES), bmat, cmat, a_re, a_im, h_re, h_im)
    return y.reshape(2, D_SLABS, n_seq * seq_len, LANES), f_re, f_im


def _s5_params(lam_re, lam_im, log_step, b_re, b_im, c_re, c_im):
    step = jnp.exp(log_step)[..., None]
    zr, zi = lam_re * step, lam_im * step
    mag = jnp.exp(zr)
    a_re, a_im = mag * jnp.cos(zi), mag * jnp.sin(zi)
    nr, ni = a_re - 1.0, a_im
    den = lam_re * lam_re + lam_im * lam_im
    k_re = (nr * lam_re + ni * lam_im) / den
    k_im = (ni * lam_re - nr * lam_im) / den
    bb_re = k_re[..., None] * b_re - k_im[..., None] * b_im
    bb_im = k_re[..., None] * b_im + k_im[..., None] * b_re
    gl = S5_SLAB // SSM_GROUP
    eye = jnp.eye(gl, dtype=F32)

    def bdiag_in(w):
        w = w.reshape(2, S5_NSLAB, gl, SSM_STATE, SSM_GROUP)
        return jnp.einsum('dsgnp,gh->dsgphn', w, eye).reshape(2, S5_NSLAB, S5_SLAB, S5_LANES)

    def bdiag_out(w):
        w = w.reshape(2, S5_NSLAB, gl, SSM_GROUP, SSM_STATE)
        return jnp.einsum('dsgpn,gh->dsgnhp', w, eye).reshape(2, S5_NSLAB, S5_LANES, S5_SLAB)

    bmat = jnp.concatenate([bdiag_in(bb_re), bdiag_in(bb_im)], axis=-1).astype(BF16)
    cmat = jnp.concatenate([bdiag_out(c_re), -bdiag_out(c_im)], axis=-2).astype(BF16)
    slab = lambda a: a.reshape(2, S5_NSLAB, 1, S5_LANES)
    return bmat, cmat, slab(a_re), slab(a_im)


def _state_to_slabs(h):
    b = h.shape[0]
    return jnp.transpose(h.reshape(b, 2, S5_NSLAB, S5_LANES), (1, 2, 0, 3))


def _slabs_to_state(f):
    b = f.shape[2]
    return jnp.transpose(f, (2, 0, 1, 3)).reshape(b, 2, SSM_GROUPS, SSM_STATE)


def _glu_res_kernel(u_ref, yl_ref, yc_ref, x_ref, mod_ref, d_ref, w_ref, b_ref, o_ref):
    is_lat = pl.program_id(0) < LAT_TILES
    y = jnp.where(is_lat, _load_slabs(yl_ref.at[0]) + _load_slabs(yl_ref.at[1]),
                  _load_slabs(yc_ref.at[0]) + _load_slabs(yc_ref.at[1]))
    yt = _load_slabs(u_ref) * d_ref[...] + y
    z = _gelu_tanh(yt)
    gate = _sigmoid(jnp.dot(z.astype(BF16), w_ref[...], preferred_element_type=F32) + b_ref[...])
    o_ref[...] = x_ref[...] + mod_ref[0][2:3] * (z * gate)


def _glu_res_call(u, y_lat, y_ctx, x, mod, d_skip, w_glu, b_glu):
    return pl.pallas_call(
        _glu_res_kernel,
        grid=(T // TM,),
        in_specs=[
            pl.BlockSpec((D_SLABS, TM, LANES), lambda i: (0, i, 0)),
            pl.BlockSpec((2, D_SLABS, TM, LANES), lambda i: (0, 0, _lat_tile(i), 0)),
            pl.BlockSpec((2, D_SLABS, TM, LANES), lambda i: (0, 0, _ctx_tile(i), 0)),
            pl.BlockSpec((TM, D), lambda i: (i, 0)),
            pl.BlockSpec((1, 6, D), lambda i: (_cond_of_row(i * TM), 0, 0)),
            pl.BlockSpec((1, D), lambda i: (0, 0)),
            pl.BlockSpec((D, D), lambda i: (0, 0)),
            pl.BlockSpec((1, D), lambda i: (0, 0)),
        ],
        out_specs=pl.BlockSpec((TM, D), lambda i: (i, 0)),
        out_shape=jax.ShapeDtypeStruct((T, D), F32),
        compiler_params=_cparams(("parallel",)),
        name="s5_glu_res",
    )(u, y_lat, y_ctx, x, mod, d_skip.reshape(1, D), w_glu, b_glu.reshape(1, D))


HALO = 8


def _conv_kernel(x_ref, xp_ref, xn_ref, g_ref, mod_ref, win_ref, cw_ref, cb_ref, wout_ref, o_ref):
    i = pl.program_id(0)
    mod = mod_ref[0]
    x = x_ref[...]
    x_ext = jnp.concatenate([xp_ref[...], x, xn_ref[...]], axis=0)
    h = _modnorm(x_ext, g_ref[...], mod[1:2], mod[0:1]).astype(BF16)
    proj = jnp.dot(h, win_ref[...], preferred_element_type=F32)
    z = proj[:, D:2 * D] * proj[:, 2 * D:]
    seq_mask = jnp.where(i < LAT_TILES, DEC_SEQ - 1, SEQ - 1)
    pos = (lax.broadcasted_iota(jnp.int32, (TM, 1), 0) + i * TM) & seq_mask
    zp = jnp.where(pos == 0, 0.0, z[HALO - 1:HALO - 1 + TM])
    zn = jnp.where(pos == seq_mask, 0.0, z[HALO + 1:HALO + 1 + TM])
    cw = cw_ref[...]
    zc = cw[0:1] * zp + cw[1:2] * z[HALO:HALO + TM] + cw[2:3] * zn + cb_ref[...]
    a = (proj[HALO:HALO + TM, :D] * zc).astype(BF16)
    y = jnp.dot(a, wout_ref[...], preferred_element_type=F32)
    o_ref[...] = x + mod[2:3] * y


def _conv_call(x, ln_g, mod, w_in, conv_w, conv_b, w_out):
    rh = TM // HALO
    nh = T // HALO
    resident = dict(pipeline_mode=pl.Buffered(1))
    return pl.pallas_call(
        _conv_kernel,
        grid=(T // TM,),
        in_specs=[
            pl.BlockSpec((TM, D), lambda i: (i, 0)),
            pl.BlockSpec((HALO, D), lambda i: (jnp.maximum(i * rh - 1, 0), 0)),
            pl.BlockSpec((HALO, D), lambda i: (jnp.minimum((i + 1) * rh, nh - 1), 0)),
            pl.BlockSpec((1, D), lambda i: (0, 0)),
            pl.BlockSpec((1, 6, D), lambda i: (_cond_of_row(i * TM), 0, 0)),
            pl.BlockSpec((D, 3 * D), lambda i: (0, 0), **resident),
            pl.BlockSpec((3, D), lambda i: (0, 0)),
            pl.BlockSpec((1, D), lambda i: (0, 0)),
            pl.BlockSpec((D, D), lambda i: (0, 0), **resident),
        ],
        out_specs=pl.BlockSpec((TM, D), lambda i: (i, 0)),
        out_shape=jax.ShapeDtypeStruct((T, D), F32),
        compiler_params=_cparams(("parallel",)),
        name="conv_mix",
    )(x, x, x, ln_g.reshape(1, D), mod, w_in, conv_w, conv_b.reshape(1, D), w_out)


def _router_kernel(x_ref, g_ref, mod_ref, r_ref, h_ref, rt_ref):
    mod = mod_ref[0]
    h = _modnorm(x_ref[...], g_ref[...], mod[4:5], mod[3:4])
    h_ref[...] = h
    logits = jnp.dot(h, r_ref[...], preferred_element_type=F32, precision=lax.Precision.HIGHEST)
    lane = lax.broadcasted_iota(jnp.int32, logits.shape, 1)
    logits = jnp.where(lane < N_EXPERTS, logits, -jnp.inf)
    m1 = jnp.max(logits, axis=-1, keepdims=True)
    i1 = jnp.min(jnp.where(logits == m1, lane, LANES), axis=-1, keepdims=True)
    rest = jnp.where(lane == i1, -jnp.inf, logits)
    m2 = jnp.max(rest, axis=-1, keepdims=True)
    i2 = jnp.min(jnp.where(rest == m2, lane, LANES), axis=-1, keepdims=True)
    e2 = jnp.exp(m2 - m1)
    w1 = 1.0 / (1.0 + e2)
    w2 = e2 / (1.0 + e2)
    rt_ref[...] = jnp.where(lane == 0, i1.astype(F32),
                            jnp.where(lane == 1, i2.astype(F32),
                                      jnp.where(lane == 2, w1, jnp.where(lane == 3, w2, 0.0))))


def _router_call(x, ln_g, mod, router_pad):
    return pl.pallas_call(
        _router_kernel,
        grid=(T // TM,),
        in_specs=[
            pl.BlockSpec((TM, D), lambda i: (i, 0)),
            pl.BlockSpec((1, D), lambda i: (0, 0)),
            pl.BlockSpec((1, 6, D), lambda i: (_cond_of_row(i * TM), 0, 0)),
            pl.BlockSpec((D, LANES), lambda i: (0, 0)),
        ],
        out_specs=[pl.BlockSpec((TM, D), lambda i: (i, 0)),
                   pl.BlockSpec((TM, LANES), lambda i: (i, 0))],
        out_shape=[jax.ShapeDtypeStruct((T, D), F32), jax.ShapeDtypeStruct((T, LANES), F32)],
        compiler_params=_cparams(("parallel",)),
        name="moe_router",
    )(x, ln_g.reshape(1, D), mod, router_pad)


R_MAX = 2 * T + N_EXPERTS * TM_MOE
N_TILES = R_MAX // TM_MOE
DMA_UNROLL = 8
GATHER_SHIFT = 3
GATHER_SPREAD = 1 << GATHER_SHIFT


def _row_copy(src_hbm, src_row, dst_vmem, dst_row, sem):
    return pltpu.make_async_copy(src_hbm.at[pl.ds(src_row, 1), :], dst_vmem.at[pl.ds(dst_row, 1), :], sem)


def _gather_kernel(nv_ref, src_ref, h_hbm, o_ref, buf, sem):
    i = pl.program_id(0)
    nv = nv_ref[0]

    def issue_tile(t):
        slot = t % 2
        base = t * TM_MOE

        def issue(g, c):
            for k in range(GATHER_SPREAD):
                r = k * (TM_MOE // GATHER_SPREAD) + g
                _row_copy(h_hbm, src_ref[base + r], buf.at[slot], r, sem.at[slot]).start(priority=k % 2)
            return c

        lax.fori_loop(0, TM_MOE // GATHER_SPREAD, issue, 0)

    @pl.when(jnp.logical_and(i == 0, nv > 0))
    def _():
        issue_tile(0)

    @pl.when(i + 1 < nv)
    def _():
        issue_tile(i + 1)

    @pl.when(i < nv)
    def _():
        slot = i % 2
        pltpu.make_async_copy(h_hbm.at[pl.ds(0, TM_MOE), :], buf.at[slot], sem.at[slot]).wait()
        o_ref[...] = buf[slot].astype(BF16)

    @pl.when(i >= nv)
    def _():
        o_ref[...] = jnp.zeros_like(o_ref)


def _gather_call(n_valid, src_tok, h):
    return pl.pallas_call(
        _gather_kernel,
        grid_spec=pltpu.PrefetchScalarGridSpec(
            num_scalar_prefetch=2,
            grid=(N_TILES,),
            in_specs=[pl.BlockSpec(memory_space=pl.ANY)],
            out_specs=pl.BlockSpec((TM_MOE, D), lambda i, nv, src: (i, 0)),
            scratch_shapes=[pltpu.VMEM((2, TM_MOE, D), F32), pltpu.SemaphoreType.DMA((2,))],
        ),
        out_shape=jax.ShapeDtypeStruct((R_MAX, D), BF16),
        compiler_params=_cparams(("arbitrary",)),
        name="moe_gather",
    )(n_valid, src_tok, h)


def _moe_kernel(nv_ref, te_ref, tr_ref, xs_ref, w1_ref, w3_ref, w2_ref, o_ref, acc_s):
    i = pl.program_id(0)
    f = pl.program_id(1)
    nf = pl.num_programs(1)
    valid = i < nv_ref[0]
    quarter = TM_MOE // MOE_ROW_SPLITS
    n_quarters = (tr_ref[i] + quarter - 1) // quarter

    @pl.when(jnp.logical_and(valid, f == 0))
    def _():
        acc_s[...] = jnp.zeros_like(acc_s)

    def swiglu_rows(rows):
        xs = xs_ref[:rows, :]
        w13 = jnp.concatenate([w1_ref[...].astype(BF16), w3_ref[...].astype(BF16)], axis=1)
        gu = jnp.dot(xs, w13, preferred_element_type=F32)
        a = _silu(gu[:, :TF_MOE]) * gu[:, TF_MOE:]
        acc_s[:rows, :] += jnp.dot(a.astype(BF16), w2_ref[...].astype(BF16), preferred_element_type=F32)

    for q in range(1, MOE_ROW_SPLITS + 1):
        @pl.when(jnp.logical_and(valid, n_quarters == q))
        def _(q=q):
            swiglu_rows(q * quarter)

    @pl.when(f == nf - 1)
    def _():
        o_ref[...] = jnp.where(valid, acc_s[...], 0.0)


def _moe_call(n_valid, tile_expert, tile_rows, xs, w13, w2, layer):
    nf = D_FF_EXPERT // TF_MOE

    def fe(i, f, nv):
        return jnp.where(i < nv[0], f, nf - 1)

    return pl.pallas_call(
        _moe_kernel,
        grid_spec=pltpu.PrefetchScalarGridSpec(
            num_scalar_prefetch=3,
            grid=(N_TILES, nf),
            in_specs=[
                pl.BlockSpec((TM_MOE, D), lambda i, f, nv, te, tr: (i, 0)),
                pl.BlockSpec((None, None, D, TF_MOE),
                             lambda i, f, nv, te, tr: (layer, te[i], 0, fe(i, f, nv))),
                pl.BlockSpec((None, None, D, TF_MOE),
                             lambda i, f, nv, te, tr: (layer, te[i], 0, nf + fe(i, f, nv))),
                pl.BlockSpec((None, None, TF_MOE, D),
                             lambda i, f, nv, te, tr: (layer, te[i], fe(i, f, nv), 0)),
            ],
            out_specs=pl.BlockSpec((TM_MOE, D), lambda i, f, nv, te, tr: (i, 0)),
            scratch_shapes=[pltpu.VMEM((TM_MOE, D), F32)],
        ),
        out_shape=jax.ShapeDtypeStruct((R_MAX, D), F32),
        compiler_params=_cparams(("arbitrary", "arbitrary")),
        name="moe_experts",
    )(n_valid, tile_expert, tile_rows, xs, w13, w13, w2)


TM_COMB = 512


def _combine_kernel(pa_ref, pb_ref, y_hbm, x_ref, rt_ref, mod_ref, o_ref, buf_a, buf_b, sem):
    i = pl.program_id(0)

    def issue_tile(t):
        slot = t % 2
        base = t * TM_COMB

        def issue(r, c):
            _row_copy(y_hbm, pa_ref[base + r], buf_a.at[slot], r, sem.at[slot]).start(priority=0)
            _row_copy(y_hbm, pb_ref[base + r], buf_b.at[slot], r, sem.at[slot]).start(priority=1)
            return c

        lax.fori_loop(0, TM_COMB, issue, 0, unroll=DMA_UNROLL)

    @pl.when(i == 0)
    def _():
        issue_tile(0)

    @pl.when(i + 1 < pl.num_programs(0))
    def _():
        issue_tile(i + 1)

    slot = i % 2
    for buf in (buf_a, buf_b):
        pltpu.make_async_copy(y_hbm.at[pl.ds(0, TM_COMB), :], buf.at[slot], sem.at[slot]).wait()
    rt = rt_ref[...]
    ffn = rt[:, 2:3] * buf_a[slot] + rt[:, 3:4] * buf_b[slot]
    o_ref[...] = x_ref[...] + mod_ref[0][5:6] * ffn


def _combine_call(pos_a, pos_b, y, x, rt, mod):
    return pl.pallas_call(
        _combine_kernel,
        grid_spec=pltpu.PrefetchScalarGridSpec(
            num_scalar_prefetch=2,
            grid=(T // TM_COMB,),
            in_specs=[
                pl.BlockSpec(memory_space=pl.ANY),
                pl.BlockSpec((TM_COMB, D), lambda i, pa, pb: (i, 0)),
                pl.BlockSpec((TM_COMB, LANES), lambda i, pa, pb: (i, 0)),
                pl.BlockSpec((1, 6, D), lambda i, pa, pb: (_cond_of_row(i * TM_COMB), 0, 0)),
            ],
            out_specs=pl.BlockSpec((TM_COMB, D), lambda i, pa, pb: (i, 0)),
            scratch_shapes=[pltpu.VMEM((2, TM_COMB, D), F32), pltpu.VMEM((2, TM_COMB, D), F32),
                            pltpu.SemaphoreType.DMA((2,))],
        ),
        out_shape=jax.ShapeDtypeStruct((T, D), F32),
        compiler_params=_cparams(("arbitrary",)),
        name="moe_combine",
    )(pos_a, pos_b, y, x, rt, mod)


def _routing_tables(rt):
    ea = rt[:, 0:2].astype(jnp.int32).reshape(-1)
    onehot = (ea[:, None] == jnp.arange(N_EXPERTS, dtype=jnp.int32)[None, :]).astype(jnp.int32)
    csum = jnp.cumsum(onehot, axis=0)
    rank = jnp.take_along_axis(csum, ea[:, None], axis=1)[:, 0] - 1
    counts = csum[-1]
    padded = ((counts + TM_MOE - 1) // TM_MOE) * TM_MOE
    ends = jnp.cumsum(padded)
    pos = (ends - padded)[ea] + rank
    n_valid = (ends[-1] // TM_MOE).astype(jnp.int32).reshape(1)
    tile_start = jnp.arange(N_TILES, dtype=jnp.int32) * TM_MOE
    expert_of_row = lambda r: jnp.sum((r[:, None] >= ends[None, :]).astype(jnp.int32), axis=1)
    tile_expert = jnp.minimum(expert_of_row(tile_start), expert_of_row(ends[-1:] - 1))
    real_end = (ends - padded + counts)[tile_expert]
    tile_rows = jnp.clip(real_end - tile_start, 0, TM_MOE).astype(jnp.int32)
    tok = jnp.arange(2 * T, dtype=jnp.int32) // 2
    src_tok = (jnp.arange(R_MAX, dtype=jnp.int32) % T).at[pos].set(tok)
    pos2 = pos.reshape(T, 2).astype(jnp.int32)
    return n_valid, tile_expert, tile_rows, src_tok, pos2[:, 0], pos2[:, 1]


def _moe_layer(x, ln_g, mod, router, w13, w2, layer):
    router_pad = jnp.pad(router, ((0, 0), (0, LANES - N_EXPERTS)))
    h, rt = _router_call(x, ln_g, mod, router_pad)
    n_valid, tile_expert, tile_rows, src_tok, pos_a, pos_b = _routing_tables(rt)
    xs = _gather_call(n_valid, src_tok, h)
    y = _moe_call(n_valid, tile_expert, tile_rows, xs, w13, w2, layer)
    return _combine_call(pos_a, pos_b, y, x, rt, mod)


def kernel(x_prompt, x_sample, c, cache_na_k, cache_na_v, state_ssm_re, state_ssm_im, c_ctx,
           ln1_g, ln2_g, ada_w, ada_b, na_w_qkv, na_w_o, na_q_g, na_k_g, na_rpb,
           ssm_lam_re, ssm_lam_im, ssm_log_step, ssm_b_re, ssm_b_im, ssm_c_re, ssm_c_im,
           ssm_d, ssm_w_glu, ssm_b_glu, cv_w_in, cv_conv_w, cv_conv_b, cv_w_out,
           ffn_w13, ffn_w2, moe_router, moe_w13, moe_w2):
    depth = ada_w.shape[0]
    x = (x_sample.reshape(T_LAT, D), x_prompt.reshape(T_CTX, D))
    conds = jnp.concatenate([c, c_ctx[None, :], jnp.zeros((N_COND - DEC_BATCH - 1, D), F32)], axis=0)
    mods = _ada_call(conds, ada_w, ada_b).reshape(depth, N_COND, 6, D)
    ffn_w13_b, ffn_w2_b = ffn_w13.astype(BF16), ffn_w2.astype(BF16)

    new_k = new_v = None
    s_re, s_im = [], []
    for l in range(depth):
        kind, j = l % 3, l // 3
        mod = mods[l]
        if kind == 0:
            qkv = _normproj_call(x, ln1_g[l], mod, na_w_qkv[j].astype(BF16), shift=0, scale=1,
                                 name="qkv_proj", slab_out=True)
            qg2 = jnp.tile(na_q_g[j], 2).reshape(1, LANES)
            kg2 = jnp.tile(na_k_g[j], 2).reshape(1, LANES)
            past = cache_na_k.shape[3]
            pair = lambda a: jnp.transpose(a[:, j].reshape(DEC_BATCH, HP, 2, past, HEAD_DIM),
                                           (0, 1, 3, 2, 4)).reshape(DEC_BATCH, HP, past, LANES)
            ctx_k, ctx_v = pair(cache_na_k), pair(cache_na_v)
            o_lat = _lat_attn_call(qkv, ctx_k, ctx_v, qg2, kg2, _bias_pairs(na_rpb[j]))
            o_ctx, new_k, new_v = _ctx_attn_call(qkv, qg2, kg2, new_k, new_v)
            x = _proj_res_call(o_lat, o_ctx, x, mod, na_w_o[j].astype(BF16), gate=2, name="attn_out")
        elif kind == 1:
            u = _norm_call(x, ln1_g[l], mod, shift=0, scale=1, name="s5_norm")
            bmat, cmat, a_re, a_im = _s5_params(ssm_lam_re[j], ssm_lam_im[j], ssm_log_step[j],
                                                ssm_b_re[j], ssm_b_im[j], ssm_c_re[j], ssm_c_im[j])
            y_lat, _, _ = _s5_call(u, bmat, cmat, a_re, a_im,
                                   _state_to_slabs(state_ssm_re[:, j]), _state_to_slabs(state_ssm_im[:, j]),
                                   row0=0, n_seq=DEC_BATCH, seq_len=DEC_SEQ,
                                   nb=DEC_BATCH, steps=S5_T_LAT, name="s5_lat")
            zero = jnp.zeros((2, S5_NSLAB, BATCH, S5_LANES), F32)
            y_ctx, f_re, f_im = _s5_call(u, bmat, cmat, a_re, a_im, zero, zero,
                                         row0=T_LAT, n_seq=BATCH, seq_len=SEQ, nb=8, steps=S5_T_CTX,
                                         name="s5_ctx")
            s_re.append(_slabs_to_state(f_re))
            s_im.append(_slabs_to_state(f_im))
            x = _glu_res_call(u, y_lat, y_ctx, x, mod, ssm_d[j], ssm_w_glu[j].astype(BF16), ssm_b_glu[j])
        else:
            x = _conv_call(x, ln1_g[l], mod, cv_w_in[j].astype(BF16), cv_conv_w[j], cv_conv_b[j],
                           cv_w_out[j].astype(BF16))
        jj = l // 2
        if l % 2 == 0:
            x = _ffn_call(x, ln2_g[l], mod, ffn_w13_b, ffn_w2_b, jj)
        else:
            x = _moe_layer(x, ln2_g[l], mod, moe_router[jj], moe_w13, moe_w2, jj)

    y_sample = x[:T_LAT].reshape(DEC_BATCH, DEC_SEQ, D)
    y_prompt = x[T_LAT:].reshape(BATCH, SEQ, D)
    return (y_prompt, y_sample, new_k, new_v, jnp.stack(s_re, axis=1), jnp.stack(s_im, axis=1))
```
